```python
import jax, jax.numpy as jnp
from jax import lax
import numpy as np

D_MODEL = 2048
BATCH = 8
SEQ = 4096
DEPTH = 4

N_MEM = 256
GRID_W = 64
HEAD_DIM = 128
MIX_WIDTH = D_MODEL
MEM_HEADS = 4
MEM_WIDTH = MEM_HEADS * HEAD_DIM
TOK_WIDTH = MIX_WIDTH - MEM_WIDTH
CHUNK = 128
A_GROUPS = TOK_WIDTH // HEAD_DIM
A_GROUP_DIM = HEAD_DIM
Q_HEADS = TOK_WIDTH // HEAD_DIM
KV_HEADS = 4
Q_PER_KV = Q_HEADS // KV_HEADS
KV_WIDTH = KV_HEADS * HEAD_DIM
Q_BLOCK = 128
ROPE_THETA = 10000.0
ROPE_PAIRS = HEAD_DIM // 4
D_FF = ((8 * D_MODEL // 3 + 255) // 256) * 256
N_MIXERS = 2
N_A = (DEPTH + 1) // 2
N_B = DEPTH // 2
EPS = 1e-6

kernel_name = "hybrid_gmlp_axial_gqa_memory_encoder"


def rms_norm(x, g):
    xf = x.astype(jnp.float32)
    y = xf * lax.rsqrt(jnp.mean(xf * xf, axis=-1, keepdims=True) + EPS)
    return (y * g.astype(jnp.float32)).astype(x.dtype)


def axial_rope_tables(seq_len):
    n_rows = seq_len // GRID_W
    rows = jnp.broadcast_to(jnp.arange(n_rows)[:, None], (n_rows, GRID_W)).reshape(seq_len)
    cols = jnp.broadcast_to(jnp.arange(GRID_W)[None, :], (n_rows, GRID_W)).reshape(seq_len)
    freqs = ROPE_THETA ** (-jnp.arange(ROPE_PAIRS, dtype=jnp.float32) / ROPE_PAIRS)
    ang_r = rows.astype(jnp.float32)[:, None] * freqs
    ang_c = cols.astype(jnp.float32)[:, None] * freqs
    ang = jnp.concatenate([ang_r, ang_r, ang_c, ang_c], axis=-1)
    return jnp.cos(ang), jnp.sin(ang)


def apply_axial_rope(x, cos, sin):
    shape = (1, cos.shape[0]) + (1,) * (x.ndim - 3) + (HEAD_DIM,)
    c, s = cos.reshape(shape), sin.reshape(shape)
    xf = x.astype(jnp.float32)
    xs = xf.reshape(x.shape[:-1] + (2, 2, ROPE_PAIRS))
    rot = jnp.stack([-xs[..., 1, :], xs[..., 0, :]], axis=-2).reshape(x.shape)
    return (xf * c + rot * s).astype(x.dtype)


def chunked_spatial_gating(uv, g_v, w_s, b_s):
    B, S, _ = uv.shape
    uv = jax.nn.gelu(uv, approximate=False)
    u, v = jnp.split(uv, 2, axis=-1)
    v = rms_norm(v, g_v)
    v = v.reshape(B, S // CHUNK, CHUNK, A_GROUPS, A_GROUP_DIM)
    s = jnp.einsum('gts,bcsgd->bctgd', w_s, v) + b_s.T[None, None, :, :, None]
    return u * s.reshape(B, S, TOK_WIDTH)


def gqa_axial_attention(qkv, g_q, g_k, cos, sin):
    B, S, _ = qkv.shape
    q, k, v = jnp.split(qkv, [TOK_WIDTH, TOK_WIDTH + KV_WIDTH], axis=-1)
    q = q.reshape(B, S, KV_HEADS, Q_PER_KV, HEAD_DIM)
    k = k.reshape(B, S, KV_HEADS, HEAD_DIM)
    v = v.reshape(B, S, KV_HEADS, HEAD_DIM)
    q = apply_axial_rope(rms_norm(q, g_q), cos, sin)
    k = apply_axial_rope(rms_norm(k, g_k), cos, sin)
    scale = HEAD_DIM ** -0.5
    n_blk = S // Q_BLOCK
    qb = q.reshape(B, n_blk, Q_BLOCK, KV_HEADS, Q_PER_KV, HEAD_DIM).transpose(1, 0, 2, 3, 4, 5)

    def block(qi):
        s = jnp.einsum('bqhgd,bkhd->bhgqk', qi, k).astype(jnp.float32) * scale
        p = jax.nn.softmax(s, axis=-1).astype(v.dtype)
        return jnp.einsum('bhgqk,bkhd->bqhgd', p, v)

    o = lax.map(block, qb)
    return o.transpose(1, 0, 2, 3, 4, 5).reshape(B, S, TOK_WIDTH)


def memory_attention(q_mem, mem, g_mem, w_kv, g_mq, g_mk):
    B, S, _ = q_mem.shape
    kv = rms_norm(mem, g_mem) @ w_kv
    k, v = jnp.split(kv, 2, axis=-1)
    k = rms_norm(k.reshape(B, -1, MEM_HEADS, HEAD_DIM), g_mk)
    v = v.reshape(B, -1, MEM_HEADS, HEAD_DIM)
    q = rms_norm(q_mem.reshape(B, S, MEM_HEADS, HEAD_DIM), g_mq)
    s = jnp.einsum('bshd,bmhd->bhsm', q, k).astype(jnp.float32) * (HEAD_DIM ** -0.5)
    p = jax.nn.softmax(s, axis=-1).astype(v.dtype)
    return jnp.einsum('bhsm,bmhd->bshd', p, v).reshape(B, S, MEM_WIDTH)


def swiglu(h, w_gate_up, w_down):
    gate, up = jnp.split(h @ w_gate_up, 2, axis=-1)
    return (jax.nn.silu(gate) * up) @ w_down


def _fwd_setup_inputs(seed: int = 0) -> dict:
    key = jax.random.key(seed)
    ks = jax.random.split(key, 20)
    f32 = jnp.float32

    def nrm(k, shape, fan_in):
        return jax.random.normal(k, shape, f32) * (fan_in ** -0.5)

    def gain(k, shape):
        return 1.0 + 0.02 * jax.random.normal(k, shape, f32)

    return {
        "x": jax.random.normal(ks[0], (BATCH, SEQ, D_MODEL), f32),
        "mem": jax.random.normal(ks[1], (BATCH, N_MEM, D_MODEL), f32),
        "g_mix": gain(ks[2], (DEPTH, D_MODEL)),
        "g_ffn": gain(ks[3], (DEPTH, D_MODEL)),
        "w_in_a": nrm(ks[4], (N_A, D_MODEL, 2 * TOK_WIDTH + MEM_WIDTH), D_MODEL),
        "g_v_a": gain(ks[5], (N_A, TOK_WIDTH)),
        "w_spatial": nrm(ks[6], (N_A, A_GROUPS, CHUNK, CHUNK), CHUNK),
        "b_spatial": 0.02 * jax.random.normal(ks[7], (N_A, A_GROUPS, CHUNK), f32),
        "w_in_b": nrm(ks[8], (N_B, D_MODEL, TOK_WIDTH + 2 * KV_WIDTH + MEM_WIDTH), D_MODEL),
        "g_q_b": gain(ks[9], (N_B, HEAD_DIM)),
        "g_k_b": gain(ks[10], (N_B, HEAD_DIM)),
        "g_mem": gain(ks[11], (DEPTH, D_MODEL)),
        "w_mem_kv": nrm(ks[12], (DEPTH, D_MODEL, 2 * MEM_WIDTH), D_MODEL),
        "g_mq": gain(ks[13], (DEPTH, HEAD_DIM)),
        "g_mk": gain(ks[14], (DEPTH, HEAD_DIM)),
        "w_out": nrm(ks[15], (DEPTH, MIX_WIDTH, D_MODEL), MIX_WIDTH),
        "w_gate_up": nrm(ks[16], (DEPTH, D_MODEL, 2 * D_FF), D_MODEL),
        "w_down": nrm(ks[17], (DEPTH, D_FF, D_MODEL), D_FF),
    }


def _fwd_reference(x, mem, g_mix, g_ffn, w_in_a, g_v_a, w_spatial, b_spatial, w_in_b, g_q_b, g_k_b,
              g_mem, w_mem_kv, g_mq, g_mk, w_out, w_gate_up, w_down):
    S = x.shape[1]
    cos, sin = axial_rope_tables(S)
    for l in range(DEPTH):
        h = rms_norm(x, g_mix[l])
        if l % N_MIXERS == 0:
            ia = l // N_MIXERS
            z = h @ w_in_a[ia]
            tok_in, q_mem = jnp.split(z, [2 * TOK_WIDTH], axis=-1)
            tok_out = chunked_spatial_gating(tok_in, g_v_a[ia], w_spatial[ia], b_spatial[ia])
        else:
            ib = l // N_MIXERS
            z = h @ w_in_b[ib]
            tok_in, q_mem = jnp.split(z, [TOK_WIDTH + 2 * KV_WIDTH], axis=-1)
            tok_out = gqa_axial_attention(tok_in, g_q_b[ib], g_k_b[ib], cos, sin)
        mem_out = memory_attention(q_mem, mem, g_mem[l], w_mem_kv[l], g_mq[l], g_mk[l])
        x = x + jnp.concatenate([tok_out, mem_out], axis=-1) @ w_out[l]
        x = x + swiglu(rms_norm(x, g_ffn[l]), w_gate_up[l], w_down[l])
    return x


import jax as _jax
import jax.numpy as _jnp

TWIN_FORMAT = 'train_step'
FWD_PARAMS = ['x', 'mem', 'g_mix', 'g_ffn', 'w_in_a', 'g_v_a', 'w_spatial', 'b_spatial', 'w_in_b', 'g_q_b', 'g_k_b', 'g_mem', 'w_mem_kv', 'g_mq', 'g_mk', 'w_out', 'w_gate_up', 'w_down']
TWIN_WEIGHTS = ['g_mix', 'g_ffn', 'w_in_a', 'g_v_a', 'w_spatial', 'b_spatial', 'w_in_b', 'g_q_b', 'g_k_b', 'g_mem', 'w_mem_kv', 'g_mq', 'g_mk', 'w_out', 'w_gate_up', 'w_down']
TWIN_DIFF_INPUT = 'x'
TWIN_INPUTS = ['x', 'mem', 'g_mix', 'g_ffn', 'w_in_a', 'g_v_a', 'w_spatial', 'b_spatial', 'w_in_b', 'g_q_b', 'g_k_b', 'g_mem', 'w_mem_kv', 'g_mq', 'g_mk', 'w_out', 'w_gate_up', 'w_down', 'loss_target', 'm_g_mix', 'm_g_ffn', 'm_w_in_a', 'm_g_v_a', 'm_w_spatial', 'm_b_spatial', 'm_w_in_b', 'm_g_q_b', 'm_g_k_b', 'm_g_mem', 'm_w_mem_kv', 'm_g_mq', 'm_g_mk', 'm_w_out', 'm_w_gate_up', 'm_w_down', 'v_g_mix', 'v_g_ffn', 'v_w_in_a', 'v_g_v_a', 'v_w_spatial', 'v_b_spatial', 'v_w_in_b', 'v_g_q_b', 'v_g_k_b', 'v_g_mem', 'v_w_mem_kv', 'v_g_mq', 'v_g_mk', 'v_w_out', 'v_w_gate_up', 'v_w_down']
TWIN_OUTPUTS = ['loss', 'grad_x', 'grad_g_mix', 'grad_g_ffn', 'grad_w_in_a', 'grad_g_v_a', 'grad_w_spatial', 'grad_b_spatial', 'grad_w_in_b', 'grad_g_q_b', 'grad_g_k_b', 'grad_g_mem', 'grad_w_mem_kv', 'grad_g_mq', 'grad_g_mk', 'grad_w_out', 'grad_w_gate_up', 'grad_w_down', 'delta_g_mix', 'delta_g_ffn', 'delta_w_in_a', 'delta_g_v_a', 'delta_w_spatial', 'delta_b_spatial', 'delta_w_in_b', 'delta_g_q_b', 'delta_g_k_b', 'delta_g_mem', 'delta_w_mem_kv', 'delta_g_mq', 'delta_g_mk', 'delta_w_out', 'delta_w_gate_up', 'delta_w_down', 'new_m_g_mix', 'new_m_g_ffn', 'new_m_w_in_a', 'new_m_g_v_a', 'new_m_w_spatial', 'new_m_b_spatial', 'new_m_w_in_b', 'new_m_g_q_b', 'new_m_g_k_b', 'new_m_g_mem', 'new_m_w_mem_kv', 'new_m_g_mq', 'new_m_g_mk', 'new_m_w_out', 'new_m_w_gate_up', 'new_m_w_down', 'new_v_g_mix', 'new_v_g_ffn', 'new_v_w_in_a', 'new_v_g_v_a', 'new_v_w_spatial', 'new_v_b_spatial', 'new_v_w_in_b', 'new_v_g_q_b', 'new_v_g_k_b', 'new_v_g_mem', 'new_v_w_mem_kv', 'new_v_g_mq', 'new_v_g_mk', 'new_v_w_out', 'new_v_w_gate_up', 'new_v_w_down']
TWIN_LEAF_KINDS = {'loss': 'loss', 'grad_x': 'grad_x', 'grad_g_mix': 'grad_w', 'grad_g_ffn': 'grad_w', 'grad_w_in_a': 'grad_w', 'grad_g_v_a': 'grad_w', 'grad_w_spatial': 'grad_w', 'grad_b_spatial': 'grad_w', 'grad_w_in_b': 'grad_w', 'grad_g_q_b': 'grad_w', 'grad_g_k_b': 'grad_w', 'grad_g_mem': 'grad_w', 'grad_w_mem_kv': 'grad_w', 'grad_g_mq': 'grad_w', 'grad_g_mk': 'grad_w', 'grad_w_out': 'grad_w', 'grad_w_gate_up': 'grad_w', 'grad_w_down': 'grad_w', 'delta_g_mix': 'delta_w', 'delta_g_ffn': 'delta_w', 'delta_w_in_a': 'delta_w', 'delta_g_v_a': 'delta_w', 'delta_w_spatial': 'delta_w', 'delta_b_spatial': 'delta_w', 'delta_w_in_b': 'delta_w', 'delta_g_q_b': 'delta_w', 'delta_g_k_b': 'delta_w', 'delta_g_mem': 'delta_w', 'delta_w_mem_kv': 'delta_w', 'delta_g_mq': 'delta_w', 'delta_g_mk': 'delta_w', 'delta_w_out': 'delta_w', 'delta_w_gate_up': 'delta_w', 'delta_w_down': 'delta_w', 'new_m_g_mix': 'new_m', 'new_m_g_ffn': 'new_m', 'new_m_w_in_a': 'new_m', 'new_m_g_v_a': 'new_m', 'new_m_w_spatial': 'new_m', 'new_m_b_spatial': 'new_m', 'new_m_w_in_b': 'new_m', 'new_m_g_q_b': 'new_m', 'new_m_g_k_b': 'new_m', 'new_m_g_mem': 'new_m', 'new_m_w_mem_kv': 'new_m', 'new_m_g_mq': 'new_m', 'new_m_g_mk': 'new_m', 'new_m_w_out': 'new_m', 'new_m_w_gate_up': 'new_m', 'new_m_w_down': 'new_m', 'new_v_g_mix': 'new_v', 'new_v_g_ffn': 'new_v', 'new_v_w_in_a': 'new_v', 'new_v_g_v_a': 'new_v', 'new_v_w_spatial': 'new_v', 'new_v_b_spatial': 'new_v', 'new_v_w_in_b': 'new_v', 'new_v_g_q_b': 'new_v', 'new_v_g_k_b': 'new_v', 'new_v_g_mem': 'new_v', 'new_v_w_mem_kv': 'new_v', 'new_v_g_mq': 'new_v', 'new_v_g_mk': 'new_v', 'new_v_w_out': 'new_v', 'new_v_w_gate_up': 'new_v', 'new_v_w_down': 'new_v'}


def _forward(args):
    return _fwd_reference(*[args[k] for k in FWD_PARAMS])


def _output_shape():
    def fwd():
        inp = _fwd_setup_inputs(0)
        return _fwd_reference(*[inp[k] for k in FWD_PARAMS])
    out = _jax.eval_shape(fwd)
    return out.shape, out.dtype

N_MICROBATCH = 1
ADAM_LR = 0.001
ADAM_B1 = 0.9
ADAM_B2 = 0.999
ADAM_EPS = 1e-08
ADAM_WD = 0.01
ADAM_STEP = 10
PER_EXAMPLE_BATCH_AXIS = {'x': 0, 'mem': 0, 'loss_target': 0}
SHARED_INPUTS = []
_WEIGHT_DTYPES = {'g_mix': _jnp.float32, 'g_ffn': _jnp.float32, 'w_in_a': _jnp.float32, 'g_v_a': _jnp.float32, 'w_spatial': _jnp.float32, 'b_spatial': _jnp.float32, 'w_in_b': _jnp.float32, 'g_q_b': _jnp.float32, 'g_k_b': _jnp.float32, 'g_mem': _jnp.float32, 'w_mem_kv': _jnp.float32, 'g_mq': _jnp.float32, 'g_mk': _jnp.float32, 'w_out': _jnp.float32, 'w_gate_up': _jnp.float32, 'w_down': _jnp.float32}
MOMENT_SCALE = {'g_mix': 4.035416e+00, 'g_ffn': 1.237249e+01, 'w_in_a': 2.604247e-01, 'g_v_a': 6.697008e+00, 'w_spatial': 1.259605e+00, 'b_spatial': 2.808879e+00, 'w_in_b': 4.537780e-02, 'g_q_b': 4.646739e-01, 'g_k_b': 4.655337e-01, 'g_mem': 6.400526e-02, 'w_mem_kv': 6.888430e-02, 'g_mq': 5.833005e-01, 'g_mk': 5.824130e-01, 'w_out': 1.696689e-01, 'w_gate_up': 1.167387e-01, 'w_down': 1.892123e-01}


def _to_microbatches(a, axis):
    t = _jnp.moveaxis(a, axis, 0)
    t = t.reshape((N_MICROBATCH, t.shape[0] // N_MICROBATCH) + t.shape[1:])
    return _jnp.moveaxis(t, 1, axis + 1)


def setup_inputs(seed: int = 0) -> dict:
    inp = _fwd_setup_inputs(seed)
    key = _jax.random.fold_in(_jax.random.key(seed), 7919)
    shape, _ = _output_shape()
    out = dict(inp)
    out["loss_target"] = _jax.random.normal(_jax.random.fold_in(key, 0), shape, _jnp.float32)
    for i, name in enumerate(TWIN_WEIGHTS):
        w = inp[name].astype(_jnp.float32)
        if MOMENT_SCALE is None:
            s = _jnp.sqrt(_jnp.mean(_jnp.square(w)) + 1e-30)
        else:
            s = MOMENT_SCALE[name]
        km, kv = _jax.random.split(_jax.random.fold_in(key, i + 1))
        out[name] = w
        out["m_" + name] = s * _jax.random.normal(km, w.shape, _jnp.float32)
        out["v_" + name] = (s * s) * _jax.random.uniform(kv, w.shape, _jnp.float32, 0.5, 1.5)
    if N_MICROBATCH > 1:
        for name, axis in PER_EXAMPLE_BATCH_AXIS.items():
            out[name] = _to_microbatches(out[name], axis)
    return {'x': out['x'], 'mem': out['mem'], 'g_mix': out['g_mix'], 'g_ffn': out['g_ffn'], 'w_in_a': out['w_in_a'], 'g_v_a': out['g_v_a'], 'w_spatial': out['w_spatial'], 'b_spatial': out['b_spatial'], 'w_in_b': out['w_in_b'], 'g_q_b': out['g_q_b'], 'g_k_b': out['g_k_b'], 'g_mem': out['g_mem'], 'w_mem_kv': out['w_mem_kv'], 'g_mq': out['g_mq'], 'g_mk': out['g_mk'], 'w_out': out['w_out'], 'w_gate_up': out['w_gate_up'], 'w_down': out['w_down'], 'loss_target': out['loss_target'], 'm_g_mix': out['m_g_mix'], 'm_g_ffn': out['m_g_ffn'], 'm_w_in_a': out['m_w_in_a'], 'm_g_v_a': out['m_g_v_a'], 'm_w_spatial': out['m_w_spatial'], 'm_b_spatial': out['m_b_spatial'], 'm_w_in_b': out['m_w_in_b'], 'm_g_q_b': out['m_g_q_b'], 'm_g_k_b': out['m_g_k_b'], 'm_g_mem': out['m_g_mem'], 'm_w_mem_kv': out['m_w_mem_kv'], 'm_g_mq': out['m_g_mq'], 'm_g_mk': out['m_g_mk'], 'm_w_out': out['m_w_out'], 'm_w_gate_up': out['m_w_gate_up'], 'm_w_down': out['m_w_down'], 'v_g_mix': out['v_g_mix'], 'v_g_ffn': out['v_g_ffn'], 'v_w_in_a': out['v_w_in_a'], 'v_g_v_a': out['v_g_v_a'], 'v_w_spatial': out['v_w_spatial'], 'v_b_spatial': out['v_b_spatial'], 'v_w_in_b': out['v_w_in_b'], 'v_g_q_b': out['v_g_q_b'], 'v_g_k_b': out['v_g_k_b'], 'v_g_mem': out['v_g_mem'], 'v_w_mem_kv': out['v_w_mem_kv'], 'v_g_mq': out['v_g_mq'], 'v_g_mk': out['v_g_mk'], 'v_w_out': out['v_w_out'], 'v_w_gate_up': out['v_w_gate_up'], 'v_w_down': out['v_w_down']}


def _loss(weights, diff, rest, loss_target):
    with _jax.named_scope("forward"):
        args = {**rest, TWIN_DIFF_INPUT: diff, **{k: w.astype(_WEIGHT_DTYPES[k]) for k, w in weights.items()}}
        y = _forward(args)
    with _jax.named_scope("loss_head"):
        err = _jnp.square(y.astype(_jnp.float32) - loss_target)
        return 0.5 * _jnp.sum(_jnp.mean(err, axis=-1)) if err.ndim else 0.5 * err


def _adamw(w, g, m, v):
    m = ADAM_B1 * m + (1.0 - ADAM_B1) * g
    v = ADAM_B2 * v + (1.0 - ADAM_B2) * _jnp.square(g)
    m_hat = m / (1.0 - ADAM_B1 ** ADAM_STEP)
    v_hat = v / (1.0 - ADAM_B2 ** ADAM_STEP)
    delta = -ADAM_LR * (m_hat / (_jnp.sqrt(v_hat) + ADAM_EPS) + ADAM_WD * w)
    return delta, m, v


def reference(x, mem, g_mix, g_ffn, w_in_a, g_v_a, w_spatial, b_spatial, w_in_b, g_q_b, g_k_b, g_mem, w_mem_kv, g_mq, g_mk, w_out, w_gate_up, w_down, loss_target, m_g_mix, m_g_ffn, m_w_in_a, m_g_v_a, m_w_spatial, m_b_spatial, m_w_in_b, m_g_q_b, m_g_k_b, m_g_mem, m_w_mem_kv, m_g_mq, m_g_mk, m_w_out, m_w_gate_up, m_w_down, v_g_mix, v_g_ffn, v_w_in_a, v_g_v_a, v_w_spatial, v_b_spatial, v_w_in_b, v_g_q_b, v_g_k_b, v_g_mem, v_w_mem_kv, v_g_mq, v_g_mk, v_w_out, v_w_gate_up, v_w_down):
    given = dict(x=x, mem=mem, g_mix=g_mix, g_ffn=g_ffn, w_in_a=w_in_a, g_v_a=g_v_a, w_spatial=w_spatial, b_spatial=b_spatial, w_in_b=w_in_b, g_q_b=g_q_b, g_k_b=g_k_b, g_mem=g_mem, w_mem_kv=w_mem_kv, g_mq=g_mq, g_mk=g_mk, w_out=w_out, w_gate_up=w_gate_up, w_down=w_down, loss_target=loss_target, m_g_mix=m_g_mix, m_g_ffn=m_g_ffn, m_w_in_a=m_w_in_a, m_g_v_a=m_g_v_a, m_w_spatial=m_w_spatial, m_b_spatial=m_b_spatial, m_w_in_b=m_w_in_b, m_g_q_b=m_g_q_b, m_g_k_b=m_g_k_b, m_g_mem=m_g_mem, m_w_mem_kv=m_w_mem_kv, m_g_mq=m_g_mq, m_g_mk=m_g_mk, m_w_out=m_w_out, m_w_gate_up=m_w_gate_up, m_w_down=m_w_down, v_g_mix=v_g_mix, v_g_ffn=v_g_ffn, v_w_in_a=v_w_in_a, v_g_v_a=v_g_v_a, v_w_spatial=v_w_spatial, v_b_spatial=v_b_spatial, v_w_in_b=v_w_in_b, v_g_q_b=v_g_q_b, v_g_k_b=v_g_k_b, v_g_mem=v_g_mem, v_w_mem_kv=v_w_mem_kv, v_g_mq=v_g_mq, v_g_mk=v_g_mk, v_w_out=v_w_out, v_w_gate_up=v_w_gate_up, v_w_down=v_w_down)
    weights = {n: given[n] for n in TWIN_WEIGHTS}
    shared = {n: given[n] for n in SHARED_INPUTS}
    per_example = {n: given[n] for n in ['x', 'mem']}
    grad_fn = _jax.value_and_grad(_loss, argnums=(0, 1))

    def one_microbatch(ex, loss_target):
        ex = dict(ex)
        diff = ex.pop(TWIN_DIFF_INPUT)
        return grad_fn(weights, diff, {**shared, **ex}, loss_target)

    if N_MICROBATCH == 1:
        loss, (grad_w, grad_x) = one_microbatch(per_example, given["loss_target"])
    else:
        def body(carry, xs):
            loss_sum, grad_sum = carry
            l_k, (gw_k, gx_k) = one_microbatch(xs[0], xs[1])
            with _jax.named_scope("update"):
                return (loss_sum + l_k, _jax.tree.map(_jnp.add, grad_sum, gw_k)), gx_k

        init = (_jnp.zeros((), _jnp.float32), _jax.tree.map(_jnp.zeros_like, weights))
        (loss, grad_w), grad_x = _jax.lax.scan(body, init, (per_example, given["loss_target"]))
    with _jax.named_scope("update"):
        delta_w, new_m, new_v = {}, {}, {}
        for n in TWIN_WEIGHTS:
            delta_w[n], new_m[n], new_v[n] = _adamw(weights[n], grad_w[n], given["m_" + n], given["v_" + n])
    return (loss, grad_x, *[grad_w[n] for n in TWIN_WEIGHTS], *[delta_w[n] for n in TWIN_WEIGHTS],
            *[new_m[n] for n in TWIN_WEIGHTS], *[new_v[n] for n in TWIN_WEIGHTS])
```

```python
import functools

import jax
import jax.numpy as jnp
from jax import lax
from jax.experimental import pallas as pl
from jax.experimental.pallas import tpu as pltpu

f32 = jnp.float32
bf16 = jnp.bfloat16

HEAD = 128
CHUNK = 128
GRID_W = 64
MEM_HEADS = 4
KV_HEADS = 4
MEM_WIDTH = MEM_HEADS * HEAD
KV_WIDTH = KV_HEADS * HEAD
ROPE_THETA = 10000.0
ROPE_PAIRS = HEAD // 4
EPS = 1e-6
SCALE = HEAD ** -0.5
N_DEV = 8
LANES = 1024
AG_CHUNKS = 4
ROW_QUANTUM = 16 * AG_CHUNKS
VMEM_LIMIT = 56 * 1024 * 1024

ADAM_LR, ADAM_B1, ADAM_B2, ADAM_EPS, ADAM_WD, ADAM_STEP = 0.001, 0.9, 0.999, 1e-08, 0.01, 10

MESH = pl.DeviceIdType.MESH
_pallas_call = pl.pallas_call


def _pick(dim, cands):
    for c in cands:
        if dim % c == 0:
            return c
    return dim


def _cp(sem):
    return pltpu.CompilerParams(dimension_semantics=sem, vmem_limit_bytes=VMEM_LIMIT)


def _sds(shape, dtype):
    return jax.ShapeDtypeStruct(shape, dtype)


def _dot(a, b, ca, cb):
    return lax.dot_general(a, b, (((ca,), (cb,)), ((), ())), preferred_element_type=f32)


def _gelu(z):
    return 0.5 * z * (1.0 + lax.erf(z * 0.7071067811865476))


def _gelu_grad(z):
    return 0.5 * (1.0 + lax.erf(z * 0.7071067811865476)) + z * jnp.exp(-0.5 * z * z) * 0.3989422804014327


def _rot(x, sin_a, sin_b):
    return pltpu.roll(x, 96, 1) * sin_a + pltpu.roll(x, 32, 1) * sin_b


def _matmul(a, b, *, ta=False, tb=False, out_dtype=f32, res=None, tm=None, tn=None, tk=None):
    assert a.dtype == bf16 and b.dtype == bf16
    kdim, m = a.shape if ta else a.shape[::-1]
    n, k2 = b.shape if tb else b.shape[::-1]
    assert kdim == k2, (a.shape, b.shape, ta, tb)
    tm = tm or _pick(m, (1024, 512, 256, 128))
    tn = tn or _pick(n, (1024, 512, 256, 128))
    if tk is None:
        tk = kdim if kdim <= 2048 else _pick(kdim, (1408, 1024, 512, 256, 128))
    nk = kdim // tk
    ca, cb = (0 if ta else 1), (1 if tb else 0)
    has_res = res is not None

    def body(*refs):
        a_ref, b_ref = refs[0], refs[1]
        r_ref = refs[2] if has_res else None
        o_ref = refs[3] if has_res else refs[2]
        prod = _dot(a_ref[...], b_ref[...], ca, cb)
        if nk == 1:
            if has_res:
                prod = prod + r_ref[...]
            o_ref[...] = prod.astype(o_ref.dtype)
        else:
            acc = refs[-1]
            k = pl.program_id(2)

            @pl.when(k == 0)
            def _():
                acc[...] = prod

            @pl.when(k > 0)
            def _():
                acc[...] += prod

            @pl.when(k == nk - 1)
            def _():
                out = acc[...]
                if has_res:
                    out = out + r_ref[...]
                o_ref[...] = out.astype(o_ref.dtype)

    a_spec = pl.BlockSpec((tk, tm), lambda i, j, k: (k, i)) if ta else pl.BlockSpec((tm, tk), lambda i, j, k: (i, k))
    b_spec = pl.BlockSpec((tn, tk), lambda i, j, k: (j, k)) if tb else pl.BlockSpec((tk, tn), lambda i, j, k: (k, j))
    o_spec = pl.BlockSpec((tm, tn), lambda i, j, k: (i, j))
    in_specs = [a_spec, b_spec] + ([o_spec] if has_res else [])
    args = (a, b) + ((res,) if has_res else ())
    mode = ("t" if ta else "n") + ("t" if tb else "n")
    return _pallas_call(
        body, out_shape=_sds((m, n), out_dtype), grid=(m // tm, n // tn, nk),
        in_specs=in_specs, out_specs=o_spec,
        scratch_shapes=([pltpu.VMEM((tm, tn), f32)] if nk > 1 else []),
        compiler_params=_cp(("parallel", "parallel", "arbitrary")),
        name=f"mm_{mode}_{m}x{kdim}x{n}{'_res' if has_res else ''}_{jnp.dtype(out_dtype).name}",
    )(*args)


def _shards_per_step(n):
    p = 1 if n % 128 == 0 else 2
    assert (p * n) % 128 == 0 and N_DEV % p == 0
    return p


def _lane_pieces(v, p, n):
    return [v] if p == 1 else [v[:, q * n:(q + 1) * n] for q in range(p)]


def _mm_cols_fwd(a, g, out_dtype):
    m, kdim = a.shape
    nd, k2, n = g.shape
    assert kdim == k2 and a.dtype == bf16 and g.dtype == bf16
    p = _shards_per_step(n)
    tm = _pick(m, (1024, 512, 256, 128))

    def body(a_ref, g_ref, o_ref):
        av = a_ref[...]
        parts = [_dot(av, g_ref[q], 1, 0) for q in range(p)]
        out = parts[0] if p == 1 else jnp.concatenate(parts, axis=1)
        o_ref[...] = out.astype(o_ref.dtype)

    return _pallas_call(
        body, out_shape=_sds((m, nd * n), out_dtype), grid=(m // tm, nd // p),
        in_specs=[pl.BlockSpec((tm, kdim), lambda i, j: (i, 0)), pl.BlockSpec((p, kdim, n), lambda i, j: (j, 0, 0))],
        out_specs=pl.BlockSpec((tm, p * n), lambda i, j: (i, j)),
        compiler_params=_cp(("parallel", "arbitrary")), name=f"mm_cols_fwd_{m}x{kdim}x{nd * n}_{jnp.dtype(out_dtype).name}",
    )(a, g)


def _mm_cols_dgrad(dz, g):
    m, nn = dz.shape
    nd, kdim, n = g.shape
    assert nn == nd * n and dz.dtype == bf16 and g.dtype == bf16
    p = _shards_per_step(n)
    nj = nd // p
    tm = _pick(m, (512, 256, 128))

    def body(dz_ref, g_ref, o_ref, acc):
        j = pl.program_id(1)
        tot = None
        for q, piece in enumerate(_lane_pieces(dz_ref[...], p, n)):
            dd = _dot(piece, g_ref[q], 1, 1)
            tot = dd if tot is None else tot + dd

        @pl.when(j == 0)
        def _():
            acc[...] = tot

        @pl.when(j > 0)
        def _():
            acc[...] += tot

        @pl.when(j == nj - 1)
        def _():
            o_ref[...] = acc[...]

    return _pallas_call(
        body, out_shape=_sds((m, kdim), f32), grid=(m // tm, nj),
        in_specs=[pl.BlockSpec((tm, p * n), lambda i, j: (i, j)), pl.BlockSpec((p, kdim, n), lambda i, j: (j, 0, 0))],
        out_specs=pl.BlockSpec((tm, kdim), lambda i, j: (i, 0)),
        scratch_shapes=[pltpu.VMEM((tm, kdim), f32)],
        compiler_params=_cp(("parallel", "arbitrary")), name=f"mm_cols_dgrad_{m}x{nn}x{kdim}",
    )(dz, g)


def _mm_cols_wgrad(a, dz, n):
    s, kdim = a.shape
    nd = dz.shape[1] // n
    assert a.dtype == bf16 and dz.dtype == bf16
    p = _shards_per_step(n)
    tkw = _pick(kdim, (1024, 512, 256, 128))
    ts = _pick(s, (1024, 512, 256, 128))
    ns = s // ts

    def body(a_ref, dz_ref, o_ref, acc):
        si = pl.program_id(2)
        av = a_ref[...]
        prods = [_dot(av, piece, 0, 0) for piece in _lane_pieces(dz_ref[...], p, n)]

        @pl.when(si == 0)
        def _():
            for q in range(p):
                acc[q] = prods[q]

        @pl.when(si > 0)
        def _():
            for q in range(p):
                acc[q] += prods[q]

        @pl.when(si == ns - 1)
        def _():
            o_ref[...] = acc[...].astype(bf16)

    return _pallas_call(
        body, out_shape=_sds((nd, kdim, n), bf16), grid=(kdim // tkw, nd // p, ns),
        in_specs=[pl.BlockSpec((ts, tkw), lambda i, j, k: (k, i)), pl.BlockSpec((ts, p * n), lambda i, j, k: (k, j))],
        out_specs=pl.BlockSpec((p, tkw, n), lambda i, j, k: (j, i, 0)),
        scratch_shapes=[pltpu.VMEM((p, tkw, n), f32)],
        compiler_params=_cp(("parallel", "parallel", "arbitrary")), name=f"mm_cols_wgrad_{kdim}x{s}x{nd * n}",
    )(a, dz)


def _rms_fwd(x, g_row):
    s, d = x.shape
    tr = _pick(s, (512, 256, 128))

    def body(x_ref, g_ref, o_ref):
        xv = x_ref[...]
        r = lax.rsqrt(jnp.mean(xv * xv, axis=-1, keepdims=True) + EPS)
        o_ref[...] = (xv * r * g_ref[...]).astype(bf16)

    return _pallas_call(
        body, out_shape=_sds((s, d), bf16), grid=(s // tr,),
        in_specs=[pl.BlockSpec((tr, d), lambda i: (i, 0)), pl.BlockSpec((1, d), lambda i: (0, 0))],
        out_specs=pl.BlockSpec((tr, d), lambda i: (i, 0)),
        compiler_params=_cp(("parallel",)), name=f"rms_fwd_{s}x{d}",
    )(x, g_row)


def _rms_bwd(x, g_row, dh, dres):
    s, d = x.shape
    tr = _pick(s, (512, 256, 128))
    with_dx = dres is not None

    def body(*refs):
        if with_dx:
            x_ref, g_ref, dh_ref, dres_ref, dx_ref, dxb_ref, dg_ref = refs
        else:
            x_ref, g_ref, dh_ref, dg_ref = refs

        @pl.when(pl.program_id(0) == 0)
        def _():
            dg_ref[...] = jnp.zeros_like(dg_ref)

        xv = x_ref[...]
        r = lax.rsqrt(jnp.mean(xv * xv, axis=-1, keepdims=True) + EPS)
        xh = xv * r
        dy = dh_ref[...].astype(f32)
        dg_ref[...] += jnp.sum(dy * xh, axis=0, keepdims=True)
        if with_dx:
            gy = dy * g_ref[...]
            dx = dres_ref[...] + r * (gy - xh * jnp.mean(gy * xh, axis=-1, keepdims=True))
            dx_ref[...] = dx
            dxb_ref[...] = dx.astype(bf16)

    row = pl.BlockSpec((tr, d), lambda i: (i, 0))
    vec = pl.BlockSpec((1, d), lambda i: (0, 0))
    if with_dx:
        return _pallas_call(
            body, out_shape=(_sds((s, d), f32), _sds((s, d), bf16), _sds((1, d), f32)), grid=(s // tr,),
            in_specs=[row, vec, row, row], out_specs=(row, row, vec),
            compiler_params=_cp(("arbitrary",)), name=f"rms_bwd_{s}x{d}",
        )(x, g_row, dh, dres)
    return _pallas_call(
        body, out_shape=_sds((1, d), f32), grid=(s // tr,),
        in_specs=[row, vec, row], out_specs=vec,
        compiler_params=_cp(("arbitrary",)), name=f"rms_bwd_gain_{s}x{d}",
    )(x, g_row, dh)


def _gmlp_rows(s):
    return CHUNK * (2 if (s // CHUNK) % 2 == 0 else 1)


def _gmlp_fwd(z, g_v, w_s, b_t):
    s = z.shape[0]
    t = g_v.shape[1]
    ng = t // HEAD
    rb = _gmlp_rows(s)

    def body(z_ref, gv_ref, ws_ref, bt_ref, o_ref):
        for ci in range(rb // CHUNK):
            lo = ci * CHUNK
            a = _gelu(z_ref[lo:lo + CHUNK, :])
            u, vv = a[:, :t], a[:, t:]
            r = lax.rsqrt(jnp.mean(vv * vv, axis=-1, keepdims=True) + EPS)
            vn = (vv * r * gv_ref[...]).astype(bf16)
            for g in range(ng):
                cs = slice(g * HEAD, (g + 1) * HEAD)
                sg = _dot(ws_ref[g].astype(bf16), vn[:, cs], 1, 0) + bt_ref[:, g:g + 1]
                o_ref[lo:lo + CHUNK, cs] = (u[:, cs] * sg).astype(bf16)

    return _pallas_call(
        body, out_shape=_sds((s, t + MEM_WIDTH), bf16), grid=(s // rb,),
        in_specs=[pl.BlockSpec((rb, 2 * t), lambda i: (i, 0)), pl.BlockSpec((1, t), lambda i: (0, 0)),
                  pl.BlockSpec((ng, CHUNK, CHUNK), lambda i: (0, 0, 0)), pl.BlockSpec((CHUNK, ng), lambda i: (0, 0))],
        out_specs=pl.BlockSpec((rb, t), lambda i: (i, 0)),
        compiler_params=_cp(("parallel",)), name=f"gmlp_fwd_{s}",
    )(z, g_v, w_s, b_t)


def _gmlp_bwd(z, g_v, w_s, b_t, dtok):
    s = z.shape[0]
    t = g_v.shape[1]
    ng = t // HEAD
    rb = _gmlp_rows(s)
    nsteps = s // rb

    def body(z_ref, gv_ref, ws_ref, bt_ref, dt_ref, dz_ref, dws_ref, dbt_ref, dgv_ref, ds_acc):
        step = pl.program_id(0)

        @pl.when(step == 0)
        def _():
            dws_ref[...] = jnp.zeros_like(dws_ref)
            dgv_ref[...] = jnp.zeros_like(dgv_ref)
            ds_acc[...] = jnp.zeros_like(ds_acc)

        for ci in range(rb // CHUNK):
            lo = ci * CHUNK
            zz = z_ref[lo:lo + CHUNK, :]
            a = _gelu(zz)
            u, vv = a[:, :t], a[:, t:]
            r = lax.rsqrt(jnp.mean(vv * vv, axis=-1, keepdims=True) + EPS)
            vh = vv * r
            vn = (vh * gv_ref[...]).astype(bf16)
            dtok = dt_ref[lo:lo + CHUNK, :].astype(f32)
            ds = dtok * u
            ds_acc[...] += ds
            dsb = ds.astype(bf16)
            du_parts, dvn_parts = [], []
            for g in range(ng):
                cs = slice(g * HEAD, (g + 1) * HEAD)
                wg = ws_ref[g].astype(bf16)
                sg = _dot(wg, vn[:, cs], 1, 0) + bt_ref[:, g:g + 1]
                du_parts.append(dtok[:, cs] * sg)
                dws_ref[g] += _dot(dsb[:, cs], vn[:, cs], 1, 1)
                dvn_parts.append(_dot(wg, dsb[:, cs], 0, 0))
            dvn = jnp.concatenate(dvn_parts, axis=1)
            dgv_ref[...] += jnp.sum(dvn * vh, axis=0, keepdims=True)
            gy = dvn * gv_ref[...]
            dvv = r * (gy - vh * jnp.mean(gy * vh, axis=-1, keepdims=True))
            da = jnp.concatenate(du_parts + [dvv], axis=1)
            dz_ref[lo:lo + CHUNK, :] = (da * _gelu_grad(zz)).astype(bf16)

        @pl.when(step == nsteps - 1)
        def _():
            for g in range(ng):
                dbt_ref[:, g:g + 1] = jnp.sum(ds_acc[:, g * HEAD:(g + 1) * HEAD], axis=1, keepdims=True)

    return _pallas_call(
        body,
        out_shape=(_sds((s, z.shape[1]), bf16), _sds((ng, CHUNK, CHUNK), f32), _sds((CHUNK, ng), f32), _sds((1, t), f32)),
        grid=(nsteps,),
        in_specs=[pl.BlockSpec((rb, 2 * t), lambda i: (i, 0)), pl.BlockSpec((1, t), lambda i: (0, 0)),
                  pl.BlockSpec((ng, CHUNK, CHUNK), lambda i: (0, 0, 0)), pl.BlockSpec((CHUNK, ng), lambda i: (0, 0)),
                  pl.BlockSpec((rb, t), lambda i: (i, 0))],
        out_specs=(pl.BlockSpec((rb, 2 * t), lambda i: (i, 0)), pl.BlockSpec((ng, CHUNK, CHUNK), lambda i: (0, 0, 0)),
                   pl.BlockSpec((CHUNK, ng), lambda i: (0, 0)), pl.BlockSpec((1, t), lambda i: (0, 0))),
        scratch_shapes=[pltpu.VMEM((CHUNK, t), f32)],
        compiler_params=_cp(("arbitrary",)), name=f"gmlp_bwd_{s}",
    )(z, g_v, w_s, b_t, dtok)


def _rope_tables(s):
    n_rows = s // GRID_W
    rows = jnp.broadcast_to(jnp.arange(n_rows)[:, None], (n_rows, GRID_W)).reshape(s)
    cols = jnp.broadcast_to(jnp.arange(GRID_W)[None, :], (n_rows, GRID_W)).reshape(s)
    freqs = ROPE_THETA ** (-jnp.arange(ROPE_PAIRS, dtype=f32) / ROPE_PAIRS)
    ang_r = rows.astype(f32)[:, None] * freqs
    ang_c = cols.astype(f32)[:, None] * freqs
    ang = jnp.concatenate([ang_r, ang_r, ang_c, ang_c], axis=-1)
    cos, sin = jnp.cos(ang), jnp.sin(ang)
    first = (jnp.arange(HEAD) % (2 * ROPE_PAIRS)) < ROPE_PAIRS
    return cos, jnp.where(first, -sin, 0.0), jnp.where(first, 0.0, sin)


def _attn_prep_fwd(z, g_q, g_k, tables, t):
    s = z.shape[0]
    tr = _pick(s, (256, 128))
    nq = t // HEAD
    width = t + 2 * KV_WIDTH

    def body(z_ref, gq_ref, gk_ref, cos_ref, sa_ref, sb_ref, q_ref, k_ref, v_ref):
        cos, sa, sb = cos_ref[...], sa_ref[...], sb_ref[...]
        for h in range(nq + KV_HEADS):
            cs = slice(h * HEAD, (h + 1) * HEAD)
            xv = z_ref[:, cs]
            r = lax.rsqrt(jnp.mean(xv * xv, axis=-1, keepdims=True) + EPS)
            xn = xv * r * (gq_ref[...] if h < nq else gk_ref[...])
            y = (xn * cos + _rot(xn, sa, sb)).astype(bf16)
            if h < nq:
                q_ref[:, cs] = y
            else:
                k_ref[:, (h - nq) * HEAD:(h - nq + 1) * HEAD] = y
        v_ref[...] = z_ref[:, t + KV_WIDTH:width].astype(bf16)

    row = lambda w: pl.BlockSpec((tr, w), lambda i: (i, 0))
    vec = pl.BlockSpec((1, HEAD), lambda i: (0, 0))
    return _pallas_call(
        body, out_shape=(_sds((s, t), bf16), _sds((s, KV_WIDTH), bf16), _sds((s, KV_WIDTH), bf16)), grid=(s // tr,),
        in_specs=[row(width), vec, vec, row(HEAD), row(HEAD), row(HEAD)],
        out_specs=(row(t), row(KV_WIDTH), row(KV_WIDTH)),
        compiler_params=_cp(("parallel",)), name=f"attn_prep_fwd_{s}",
    )(z, g_q, g_k, *tables)


def _attn_prep_bwd(z, g_q, g_k, tables, dq, dk, dv, t):
    s = z.shape[0]
    tr = _pick(s, (256, 128))
    nq = t // HEAD
    width = t + 2 * KV_WIDTH

    def body(z_ref, gq_ref, gk_ref, cos_ref, sa_ref, sb_ref, dq_ref, dk_ref, dv_ref, dz_ref, dgq_ref, dgk_ref):
        @pl.when(pl.program_id(0) == 0)
        def _():
            dgq_ref[...] = jnp.zeros_like(dgq_ref)
            dgk_ref[...] = jnp.zeros_like(dgk_ref)

        cos, sa, sb = cos_ref[...], sa_ref[...], sb_ref[...]
        for h in range(nq + KV_HEADS):
            cs = slice(h * HEAD, (h + 1) * HEAD)
            xv = z_ref[:, cs]
            r = lax.rsqrt(jnp.mean(xv * xv, axis=-1, keepdims=True) + EPS)
            xh = xv * r
            if h < nq:
                dy, g_ref, dg_ref = dq_ref[:, cs], gq_ref, dgq_ref
            else:
                dy, g_ref, dg_ref = dk_ref[:, (h - nq) * HEAD:(h - nq + 1) * HEAD], gk_ref, dgk_ref
            dy = dy.astype(f32)
            dxn = dy * cos - _rot(dy, sa, sb)
            dg_ref[...] += jnp.sum(dxn * xh, axis=0, keepdims=True)
            gy = dxn * g_ref[...]
            dz_ref[:, cs] = (r * (gy - xh * jnp.mean(gy * xh, axis=-1, keepdims=True))).astype(bf16)
        dz_ref[:, t + KV_WIDTH:width] = dv_ref[...].astype(bf16)

    row = lambda w: pl.BlockSpec((tr, w), lambda i: (i, 0))
    vec = pl.BlockSpec((1, HEAD), lambda i: (0, 0))
    return _pallas_call(
        body, out_shape=(_sds((s, z.shape[1]), bf16), _sds((1, HEAD), f32), _sds((1, HEAD), f32)), grid=(s // tr,),
        in_specs=[row(width), vec, vec, row(HEAD), row(HEAD), row(HEAD), row(t), row(KV_WIDTH), row(KV_WIDTH)],
        out_specs=(row(width), vec, vec),
        compiler_params=_cp(("arbitrary",)), name=f"attn_prep_bwd_{s}",
    )(z, g_q, g_k, *tables, dq, dk, dv)


def _flash_tiles(s):
    return _pick(s, (512, 256, 128)), _pick(s, (512, 256, 128))


def _stack_heads(ref, grp):
    return jnp.concatenate([ref[:, g * HEAD:(g + 1) * HEAD] for g in range(grp)], axis=0)


def _flash_fwd(q, k, v):
    s, t = q.shape
    grp = t // KV_WIDTH
    tq, tk = _flash_tiles(s)
    nkv = s // tk
    rows = grp * tq

    def body(q_ref, k_ref, v_ref, o_ref, lse_ref, m_sc, l_sc, acc_sc):
        ki = pl.program_id(2)

        @pl.when(ki == 0)
        def _():
            m_sc[...] = jnp.full(m_sc.shape, -jnp.inf, f32)
            l_sc[...] = jnp.zeros_like(l_sc)
            acc_sc[...] = jnp.zeros_like(acc_sc)

        qs = _stack_heads(q_ref, grp)
        sc = _dot(qs, k_ref[...], 1, 1) * SCALE
        m_prev = m_sc[...]
        m_new = jnp.maximum(m_prev, jnp.max(sc, axis=-1, keepdims=True))
        alpha = jnp.exp(m_prev - m_new)
        p = jnp.exp(sc - m_new)
        l_sc[...] = alpha * l_sc[...] + jnp.sum(p, axis=-1, keepdims=True)
        acc_sc[...] = alpha * acc_sc[...] + _dot(p.astype(bf16), v_ref[...], 1, 0)
        m_sc[...] = m_new

        @pl.when(ki == nkv - 1)
        def _():
            o = acc_sc[...] / l_sc[...]
            for g in range(grp):
                o_ref[:, g * HEAD:(g + 1) * HEAD] = o[g * tq:(g + 1) * tq].astype(bf16)
            lse_ref[0] = jnp.broadcast_to(m_sc[...] + jnp.log(l_sc[...]), (rows, HEAD))

    return _pallas_call(
        body, out_shape=(_sds((s, t + MEM_WIDTH), bf16), _sds((KV_HEADS, grp * s, HEAD), f32)), grid=(KV_HEADS, s // tq, nkv),
        in_specs=[pl.BlockSpec((tq, grp * HEAD), lambda h, i, j: (i, h)), pl.BlockSpec((tk, HEAD), lambda h, i, j: (j, h)),
                  pl.BlockSpec((tk, HEAD), lambda h, i, j: (j, h))],
        out_specs=(pl.BlockSpec((tq, grp * HEAD), lambda h, i, j: (i, h)), pl.BlockSpec((1, rows, HEAD), lambda h, i, j: (h, i, 0))),
        scratch_shapes=[pltpu.VMEM((rows, 1), f32), pltpu.VMEM((rows, 1), f32), pltpu.VMEM((rows, HEAD), f32)],
        compiler_params=_cp(("parallel", "parallel", "arbitrary")), name=f"flash_fwd_{s}",
    )(q, k, v)


def _flash_probs(q_ref, k_ref, v_ref, o_ref, do_ref, lse_ref, grp):
    qs = _stack_heads(q_ref, grp)
    dos = _stack_heads(do_ref, grp)
    os_ = _stack_heads(o_ref, grp)
    delta = jnp.sum(dos.astype(f32) * os_.astype(f32), axis=-1, keepdims=True)
    sc = _dot(qs, k_ref[...], 1, 1) * SCALE
    p = jnp.exp(sc - lse_ref[0, :, 0:1])
    dp = _dot(dos, v_ref[...], 1, 1)
    ds = (p * (dp - delta) * SCALE).astype(bf16)
    return qs, dos, p.astype(bf16), ds


def _flash_bwd(q, k, v, o, do, lse):
    s, t = q.shape
    grp = t // KV_WIDTH
    tq, tk = _flash_tiles(s)
    nq, nkv = s // tq, s // tk
    rows = grp * tq

    def dkv_body(q_ref, k_ref, v_ref, o_ref, do_ref, lse_ref, dk_ref, dv_ref, dk_acc, dv_acc):
        qi = pl.program_id(2)

        @pl.when(qi == 0)
        def _():
            dk_acc[...] = jnp.zeros_like(dk_acc)
            dv_acc[...] = jnp.zeros_like(dv_acc)

        qs, dos, p, ds = _flash_probs(q_ref, k_ref, v_ref, o_ref, do_ref, lse_ref, grp)
        dv_acc[...] += _dot(p, dos, 0, 0)
        dk_acc[...] += _dot(ds, qs, 0, 0)

        @pl.when(qi == nq - 1)
        def _():
            dk_ref[...] = dk_acc[...]
            dv_ref[...] = dv_acc[...]

    def dq_body(q_ref, k_ref, v_ref, o_ref, do_ref, lse_ref, dq_ref, dq_acc):
        kj = pl.program_id(2)

        @pl.when(kj == 0)
        def _():
            dq_acc[...] = jnp.zeros_like(dq_acc)

        _, _, _, ds = _flash_probs(q_ref, k_ref, v_ref, o_ref, do_ref, lse_ref, grp)
        dq_acc[...] += _dot(ds, k_ref[...], 1, 0)

        @pl.when(kj == nkv - 1)
        def _():
            for g in range(grp):
                dq_ref[:, g * HEAD:(g + 1) * HEAD] = dq_acc[g * tq:(g + 1) * tq]

    def specs(qmap, kmap):
        qb = pl.BlockSpec((tq, grp * HEAD), lambda h, a, b: (qmap(a, b), h))
        kb = pl.BlockSpec((tk, HEAD), lambda h, a, b: (kmap(a, b), h))
        lb = pl.BlockSpec((1, rows, HEAD), lambda h, a, b: (h, qmap(a, b), 0))
        return qb, kb, lb

    qb, kb, lb = specs(lambda a, b: b, lambda a, b: a)
    dk, dv = _pallas_call(
        dkv_body, out_shape=(_sds((s, KV_WIDTH), f32), _sds((s, KV_WIDTH), f32)), grid=(KV_HEADS, nkv, nq),
        in_specs=[qb, kb, kb, qb, qb, lb], out_specs=(kb, kb),
        scratch_shapes=[pltpu.VMEM((tk, HEAD), f32), pltpu.VMEM((tk, HEAD), f32)],
        compiler_params=_cp(("parallel", "parallel", "arbitrary")), name=f"flash_bwd_dkv_{s}",
    )(q, k, v, o, do, lse)
    qb, kb, lb = specs(lambda a, b: a, lambda a, b: b)
    dq = _pallas_call(
        dq_body, out_shape=_sds((s, t), f32), grid=(KV_HEADS, nq, nkv),
        in_specs=[qb, kb, kb, qb, qb, lb], out_specs=qb,
        scratch_shapes=[pltpu.VMEM((rows, HEAD), f32)],
        compiler_params=_cp(("parallel", "parallel", "arbitrary")), name=f"flash_bwd_dq_{s}",
    )(q, k, v, o, do, lse)
    return dq, dk, dv


def _mem_heads(z_ref, kv_ref, gq_ref, gk_ref, h):
    cs = slice(h * HEAD, (h + 1) * HEAD)
    xv = z_ref[:, cs]
    r = lax.rsqrt(jnp.mean(xv * xv, axis=-1, keepdims=True) + EPS)
    xh = xv * r
    kx = kv_ref[:, cs]
    rk = lax.rsqrt(jnp.mean(kx * kx, axis=-1, keepdims=True) + EPS)
    kn = (kx * rk * gk_ref[...]).astype(bf16)
    vv = kv_ref[:, MEM_WIDTH + h * HEAD:MEM_WIDTH + (h + 1) * HEAD].astype(bf16)
    qn = (xh * gq_ref[...]).astype(bf16)
    sc = _dot(qn, kn, 1, 1) * SCALE
    e = jnp.exp(sc - jnp.max(sc, axis=-1, keepdims=True))
    p = e / jnp.sum(e, axis=-1, keepdims=True)
    return cs, r, xh, qn, kn, vv, p


def _mem_fwd(z, qblk, kv, g_mq, g_mk, cat):
    s = z.shape[0]
    nm = kv.shape[0]
    tr = _pick(s, (512, 256, 128))
    oblk = cat.shape[1] // MEM_WIDTH - 1

    def body(z_ref, kv_ref, gq_ref, gk_ref, cat_ref, o_ref):
        for h in range(MEM_HEADS):
            cs, _, _, _, _, vv, p = _mem_heads(z_ref, kv_ref, gq_ref, gk_ref, h)
            o_ref[:, cs] = _dot(p.astype(bf16), vv, 1, 0).astype(bf16)

    vec = pl.BlockSpec((1, HEAD), lambda i: (0, 0))
    return _pallas_call(
        body, out_shape=_sds(cat.shape, bf16), grid=(s // tr,),
        in_specs=[pl.BlockSpec((tr, MEM_WIDTH), lambda i: (i, qblk)), pl.BlockSpec((nm, 2 * MEM_WIDTH), lambda i: (0, 0)), vec, vec,
                  pl.BlockSpec(memory_space=pl.ANY)],
        out_specs=pl.BlockSpec((tr, MEM_WIDTH), lambda i: (i, oblk)),
        input_output_aliases={4: 0},
        compiler_params=_cp(("parallel",)), name=f"mem_fwd_{s}_{qblk}",
    )(z, kv, g_mq, g_mk, cat)


def _mem_bwd(z, qblk, kv, g_mq, g_mk, dcat, dz):
    s = z.shape[0]
    nm = kv.shape[0]
    tr = _pick(s, (512, 256, 128))
    dblk = dcat.shape[1] // MEM_WIDTH - 1

    def body(z_ref, kv_ref, gq_ref, gk_ref, dm_ref, dzin_ref, dz_ref, dkn_ref, dv_ref, dgq_ref):
        @pl.when(pl.program_id(0) == 0)
        def _():
            dkn_ref[...] = jnp.zeros_like(dkn_ref)
            dv_ref[...] = jnp.zeros_like(dv_ref)
            dgq_ref[...] = jnp.zeros_like(dgq_ref)

        for h in range(MEM_HEADS):
            cs, r, xh, qn, kn, vv, p = _mem_heads(z_ref, kv_ref, gq_ref, gk_ref, h)
            dm = dm_ref[:, cs]
            dv_ref[:, cs] += _dot(p.astype(bf16), dm, 0, 0)
            dp = _dot(dm, vv, 1, 1)
            ds = (p * (dp - jnp.sum(dp * p, axis=-1, keepdims=True)) * SCALE).astype(bf16)
            dqn = _dot(ds, kn, 1, 0)
            dkn_ref[:, cs] += _dot(ds, qn, 0, 0)
            dgq_ref[...] += jnp.sum(dqn * xh, axis=0, keepdims=True)
            gy = dqn * gq_ref[...]
            dz_ref[:, cs] = (r * (gy - xh * jnp.mean(gy * xh, axis=-1, keepdims=True))).astype(bf16)

    vec = pl.BlockSpec((1, HEAD), lambda i: (0, 0))
    acc = pl.BlockSpec((nm, MEM_WIDTH), lambda i: (0, 0))
    return _pallas_call(
        body, out_shape=(_sds(dz.shape, bf16), _sds((nm, MEM_WIDTH), f32), _sds((nm, MEM_WIDTH), f32), _sds((1, HEAD), f32)),
        grid=(s // tr,),
        in_specs=[pl.BlockSpec((tr, MEM_WIDTH), lambda i: (i, qblk)), pl.BlockSpec((nm, 2 * MEM_WIDTH), lambda i: (0, 0)), vec, vec,
                  pl.BlockSpec((tr, MEM_WIDTH), lambda i: (i, dblk)), pl.BlockSpec(memory_space=pl.ANY)],
        out_specs=(pl.BlockSpec((tr, MEM_WIDTH), lambda i: (i, qblk)), acc, acc, vec),
        input_output_aliases={5: 0},
        compiler_params=_cp(("arbitrary",)), name=f"mem_bwd_{s}_{qblk}",
    )(z, kv, g_mq, g_mk, dcat, dz)


def _mem_kv_bwd(kv, g_mk, dkn, dv):
    nm = kv.shape[0]

    def body(kv_ref, gk_ref, dkn_ref, dv_ref, dkv_ref, dgk_ref):
        dgk = jnp.zeros((1, HEAD), f32)
        for h in range(MEM_HEADS):
            cs = slice(h * HEAD, (h + 1) * HEAD)
            kx = kv_ref[:, cs]
            rk = lax.rsqrt(jnp.mean(kx * kx, axis=-1, keepdims=True) + EPS)
            kh = kx * rk
            dkn_h = dkn_ref[:, cs]
            dgk = dgk + jnp.sum(dkn_h * kh, axis=0, keepdims=True)
            gy = dkn_h * gk_ref[...]
            dkv_ref[:, cs] = (rk * (gy - kh * jnp.mean(gy * kh, axis=-1, keepdims=True))).astype(bf16)
        dkv_ref[:, MEM_WIDTH:] = dv_ref[...].astype(bf16)
        dgk_ref[...] = dgk

    return _pallas_call(
        body, out_shape=(_sds((nm, 2 * MEM_WIDTH), bf16), _sds((1, HEAD), f32)),
        compiler_params=pltpu.CompilerParams(vmem_limit_bytes=VMEM_LIMIT), name=f"mem_kv_bwd_{nm}",
    )(kv, g_mk, dkn, dv)


def _ffn_tiles(s, ff):
    return _pick(s, (512, 256, 128)), _pick(ff, (1408, 1024, 512, 256, 128))


def _swiglu_fwd(gu):
    s, ff2 = gu.shape
    ff = ff2 // 2
    tr, tf = _ffn_tiles(s, ff)
    nf = ff // tf

    def body(g_ref, u_ref, o_ref):
        g = g_ref[...].astype(f32)
        o_ref[...] = (g * jax.nn.sigmoid(g) * u_ref[...].astype(f32)).astype(bf16)

    return _pallas_call(
        body, out_shape=_sds((s, ff), bf16), grid=(s // tr, nf),
        in_specs=[pl.BlockSpec((tr, tf), lambda i, j: (i, j)), pl.BlockSpec((tr, tf), lambda i, j: (i, j + nf))],
        out_specs=pl.BlockSpec((tr, tf), lambda i, j: (i, j)),
        compiler_params=_cp(("parallel", "parallel")), name=f"swiglu_fwd_{s}x{ff}",
    )(gu, gu)


def _swiglu_bwd(gu, dact):
    s, ff2 = gu.shape
    ff = ff2 // 2
    tr, tf = _ffn_tiles(s, ff)
    nf = ff // tf

    def body(g_ref, u_ref, da_ref, o_ref):
        g = g_ref[...].astype(f32)
        u = u_ref[...].astype(f32)
        da = da_ref[...].astype(f32)
        sg = jax.nn.sigmoid(g)
        dgate = da * u * sg * (1.0 + g * (1.0 - sg))
        dup = da * g * sg
        o_ref[...] = jnp.where(pl.program_id(1) < nf, dgate, dup).astype(bf16)

    return _pallas_call(
        body, out_shape=_sds((s, ff2), bf16), grid=(s // tr, 2 * nf),
        in_specs=[pl.BlockSpec((tr, tf), lambda i, j: (i, j % nf)), pl.BlockSpec((tr, tf), lambda i, j: (i, j % nf + nf)),
                  pl.BlockSpec((tr, tf), lambda i, j: (i, j % nf))],
        out_specs=pl.BlockSpec((tr, tf), lambda i, j: (i, j)),
        compiler_params=_cp(("parallel", "parallel")), name=f"swiglu_bwd_{s}x{ff}",
    )(gu, gu, dact)


def _loss_head(y, target):
    s, d = y.shape
    tr = _pick(s, (512, 256, 128))

    def body(y_ref, t_ref, l_ref, dy_ref, dyb_ref):
        @pl.when(pl.program_id(0) == 0)
        def _():
            l_ref[...] = jnp.zeros_like(l_ref)

        err = y_ref[...] - t_ref[...]
        l_ref[...] += 0.5 * jnp.sum(jnp.mean(err * err, axis=-1, keepdims=True), axis=0, keepdims=True)
        dy = err * (1.0 / d)
        dy_ref[...] = dy
        dyb_ref[...] = dy.astype(bf16)

    row = pl.BlockSpec((tr, d), lambda i: (i, 0))
    return _pallas_call(
        body, out_shape=(_sds((1, HEAD), f32), _sds((s, d), f32), _sds((s, d), bf16)), grid=(s // tr,),
        in_specs=[row, row], out_specs=(pl.BlockSpec((1, HEAD), lambda i: (0, 0)), row, row),
        compiler_params=_cp(("arbitrary",)), name=f"loss_{s}x{d}",
    )(y, target)


def _place():
    return lax.axis_index("x"), lax.axis_index("y"), lax.axis_index("c")


def _tag(arrays):
    return "_".join("x".join(str(dd) for dd in a.shape) for a in arrays)


def _all_gather(shards):
    nw = len(shards)
    hbm = pl.BlockSpec(memory_space=pl.ANY)

    def body(*refs):
        x_refs, out_refs = refs[:nw], refs[nw:2 * nw]
        send_sems, recv_sems, local_sems = refs[2 * nw:]
        x, y, c = _place()
        me, sibling = (x, y, c), (x, y, 1 - c)
        chips = [(1 - x, y), (x, 1 - y), (1 - x, 1 - y)]

        def slot(w, place):
            px, py, pc = place
            return out_refs[w].at[4 * px + 2 * py + pc]

        def copy(k, w, block_of, to, from_input=False):
            return pltpu.make_async_remote_copy(
                src_ref=x_refs[w] if from_input else slot(w, block_of), dst_ref=slot(w, block_of),
                send_sem=send_sems.at[k, w], recv_sem=recv_sems.at[k, w], device_id=to, device_id_type=MESH)

        mine = [pltpu.make_async_copy(x_refs[w], slot(w, me), local_sems.at[w]) for w in range(nw)]
        for cp in mine:
            cp.start()
        first = []
        for w in range(nw):
            first.append(copy(0, w, me, sibling, from_input=True))
            first += [copy(1 + j, w, me, (*chip, c), from_input=True) for j, chip in enumerate(chips)]
        for cp in first:
            cp.start()
        passed = []
        for w in range(nw):
            for j, chip in enumerate(chips):
                copy(1 + j, w, (*chip, c), me).wait_recv()
                fwd = copy(4 + j, w, (*chip, c), sibling)
                fwd.start()
                passed.append(fwd)
        for w in range(nw):
            copy(0, w, sibling, me).wait_recv()
            for j, chip in enumerate(chips):
                copy(4 + j, w, (*chip, 1 - c), me).wait_recv()
        for cp in first + passed:
            cp.wait_send()
        for cp in mine:
            cp.wait()

    return _pallas_call(
        body, out_shape=tuple(_sds((N_DEV,) + a.shape, a.dtype) for a in shards), in_specs=[hbm] * nw, out_specs=tuple([hbm] * nw),
        scratch_shapes=[pltpu.SemaphoreType.DMA((7, nw)), pltpu.SemaphoreType.DMA((7, nw)), pltpu.SemaphoreType.DMA((nw,))],
        name=f"all_gather_{_tag(shards)}_{jnp.dtype(shards[0].dtype).name}",
    )(*shards)


def _swap_with_sibling(grads):
    nw = len(grads)
    nchip = N_DEV // 2
    hbm = pl.BlockSpec(memory_space=pl.ANY)

    def body(*refs):
        g_refs, got_refs = refs[:nw], refs[nw:2 * nw]
        send_sems, recv_sems = refs[2 * nw:]
        x, y, c = _place()
        copies = [pltpu.make_async_remote_copy(
            src_ref=g_refs[w].at[2 * k + (1 - c)], dst_ref=got_refs[w].at[k],
            send_sem=send_sems.at[w, k], recv_sem=recv_sems.at[w, k], device_id=(x, y, 1 - c), device_id_type=MESH)
            for w in range(nw) for k in range(nchip)]
        for cp in copies:
            cp.start()
        for cp in copies:
            cp.wait()

    return _pallas_call(
        body, out_shape=tuple(_sds((nchip,) + g.shape[1:], g.dtype) for g in grads), in_specs=[hbm] * nw, out_specs=tuple([hbm] * nw),
        scratch_shapes=[pltpu.SemaphoreType.DMA((nw, nchip)), pltpu.SemaphoreType.DMA((nw, nchip))],
        name=f"swap_sibling_{_tag(grads)}",
    )(*grads)


def _pair_sum(grad, got):
    nd, a, b = grad.shape
    ta = _pick(a, (1024, 704, 512, 352, 256, 128, 64, 32, 16))

    def body(c_ref, a_ref, b_ref, o_ref):
        o_ref[...] = (a_ref[...].astype(f32) + b_ref[...].astype(f32)).astype(o_ref.dtype)

    c_idx = lax.axis_index("c").astype(jnp.int32).reshape(1)
    return _pallas_call(
        body, out_shape=_sds(got.shape, grad.dtype),
        grid_spec=pltpu.PrefetchScalarGridSpec(
            num_scalar_prefetch=1, grid=(nd // 2, a // ta),
            in_specs=[pl.BlockSpec((None, ta, b), lambda k, i, c_ref: (2 * k + c_ref[0], i, 0)),
                      pl.BlockSpec((None, ta, b), lambda k, i, c_ref: (k, i, 0))],
            out_specs=pl.BlockSpec((None, ta, b), lambda k, i, c_ref: (k, i, 0))),
        compiler_params=_cp(("parallel", "parallel")), name=f"pair_sum_{a}x{b}",
    )(c_idx, grad, got)


def _scatter_to_chips(psums):
    nw = len(psums)
    hbm = pl.BlockSpec(memory_space=pl.ANY)

    def body(*refs):
        p_refs, got_refs = refs[:nw], refs[nw:2 * nw]
        send_sems, recv_sems, local_sems = refs[2 * nw:]
        x, y, c = _place()
        my_chip = 2 * x + y
        mine = [pltpu.make_async_copy(p_refs[w].at[my_chip], got_refs[w].at[my_chip], local_sems.at[w]) for w in range(nw)]
        for cp in mine:
            cp.start()
        copies = [pltpu.make_async_remote_copy(
            src_ref=p_refs[w].at[2 * px + py], dst_ref=got_refs[w].at[my_chip],
            send_sem=send_sems.at[w, j], recv_sem=recv_sems.at[w, j], device_id=(px, py, c), device_id_type=MESH)
            for w in range(nw) for j, (px, py) in enumerate([(1 - x, y), (x, 1 - y), (1 - x, 1 - y)])]
        for cp in copies:
            cp.start()
        for cp in copies:
            cp.wait()
        for cp in mine:
            cp.wait()

    return _pallas_call(
        body, out_shape=tuple(_sds(p.shape, p.dtype) for p in psums), in_specs=[hbm] * nw, out_specs=tuple([hbm] * nw),
        scratch_shapes=[pltpu.SemaphoreType.DMA((nw, 3)), pltpu.SemaphoreType.DMA((nw, 3)), pltpu.SemaphoreType.DMA((nw,))],
        name=f"scatter_chips_{_tag(psums)}",
    )(*psums)


def _adamw(parts, w_all, m_all, v_all, l, carried):
    nparts, a, b = parts.shape
    nl = w_all.shape[0]
    ta = next(cc for cc in (1024, 704, 512, 352, 256, 128, 64, 32, 16, 8) if a % cc == 0 and (cc * b * 4 <= 2 ** 20 or cc == 8))
    c1 = 1.0 / (1.0 - ADAM_B1 ** ADAM_STEP)
    c2 = 1.0 / (1.0 - ADAM_B2 ** ADAM_STEP)

    def body(p_ref, w_ref, m_ref, v_ref, *rest):
        g_out, d_out, m_out, v_out = rest[-4:]
        g = p_ref[0].astype(f32)
        for k in range(1, nparts):
            g = g + p_ref[k].astype(f32)
        m_new = ADAM_B1 * m_ref[...] + (1.0 - ADAM_B1) * g
        v_new = ADAM_B2 * v_ref[...] + (1.0 - ADAM_B2) * (g * g)
        m_hat = m_new * c1
        v_hat = v_new * c2
        g_out[...] = g
        d_out[...] = -ADAM_LR * (m_hat / (jnp.sqrt(v_hat) + ADAM_EPS) + ADAM_WD * w_ref[...])
        m_out[...] = m_new
        v_out[...] = v_new

    one = pl.BlockSpec((None, ta, b), lambda i: (l, i, 0))
    keep = [] if carried is None else [pl.BlockSpec(memory_space=pl.ANY)] * 4
    return _pallas_call(
        body, out_shape=tuple(_sds((nl, a, b), f32) for _ in range(4)), grid=(a // ta,),
        in_specs=[pl.BlockSpec((nparts, ta, b), lambda i: (0, i, 0)), one, one, one] + keep, out_specs=(one, one, one, one),
        input_output_aliases=({} if carried is None else {4 + q: q for q in range(4)}),
        compiler_params=_cp(("parallel",)), name=f"adamw_{nparts}x{nl}x{a}x{b}_{l}{'' if carried is None else '_carried'}",
    )(parts, w_all, m_all, v_all, *(carried or ()))


def _to_flat(arrays):
    flat = jnp.concatenate([a.reshape(-1).astype(f32) for a in arrays])
    rows = -(-flat.shape[0] // (8 * LANES)) * 8
    return jnp.pad(flat, (0, rows * LANES - flat.shape[0])).reshape(rows, LANES)


def _from_flat(flat, shapes):
    flat = flat.reshape(-1)
    out, off = [], 0
    for shp in shapes:
        n = 1
        for dd in shp:
            n *= dd
        out.append(flat[off:off + n].reshape(shp))
        off += n
    return out


def kernel(x, mem, g_mix, g_ffn, w_in_a, g_v_a, w_spatial, b_spatial, w_in_b, g_q_b, g_k_b, g_mem, w_mem_kv, g_mq, g_mk, w_out, w_gate_up, w_down, loss_target, m_g_mix, m_g_ffn, m_w_in_a, m_g_v_a, m_w_spatial, m_b_spatial, m_w_in_b, m_g_q_b, m_g_k_b, m_g_mem, m_w_mem_kv, m_g_mq, m_g_mk, m_w_out, m_w_gate_up, m_w_down, v_g_mix, v_g_ffn, v_w_in_a, v_g_v_a, v_w_spatial, v_b_spatial, v_w_in_b, v_g_q_b, v_g_k_b, v_g_mem, v_w_mem_kv, v_g_mq, v_g_mk, v_w_out, v_w_gate_up, v_w_down):
    given = dict(locals())
    depth = g_mix.shape[0]
    s, d = x.shape[1], x.shape[2]
    nm = mem.shape[1]
    t = d - MEM_WIDTH
    ff = w_down.shape[1] * N_DEV
    x0 = x.reshape(s, d)
    mem0 = mem.reshape(nm, d)
    target = loss_target.reshape(s, d)
    tables = _rope_tables(s)

    big_names = ("w_in", "w_mem_kv", "w_out", "w_gate_up", "w_down")

    def stacked_key(name, l):
        if name == "w_in":
            return ("w_in_a" if l % 2 == 0 else "w_in_b"), l // 2
        return name, l

    saved = []
    xc = x0
    for l in range(depth):
        is_a = l % 2 == 0
        shards = []
        for name in big_names:
            key, idx = stacked_key(name, l)
            shards.append(given[key][idx].astype(bf16))
        g_in, g_kv, g_out, g_gu, g_dn = _all_gather(shards)
        w_kv, w_o, w_dn = (g.reshape(-1, g.shape[2]) for g in (g_kv, g_out, g_dn))
        qblk = (N_DEV * g_in.shape[2] - MEM_WIDTH) // MEM_WIDTH

        gm_row, gf_row, gmem_row = g_mix[l].reshape(1, d), g_ffn[l].reshape(1, d), g_mem[l].reshape(1, d)
        gmq_row, gmk_row = g_mq[l].reshape(1, HEAD), g_mk[l].reshape(1, HEAD)
        h = _rms_fwd(xc, gm_row)
        z = _mm_cols_fwd(h, g_in, f32)
        if is_a:
            ia = l // 2
            mix = dict(g_v=g_v_a[ia].reshape(1, t), w_s=w_spatial[ia], b_t=b_spatial[ia].T)
            cat = _gmlp_fwd(z, mix["g_v"], mix["w_s"], mix["b_t"])
        else:
            ib = l // 2
            mix = dict(g_q=g_q_b[ib].reshape(1, HEAD), g_k=g_k_b[ib].reshape(1, HEAD))
            q, k, v = _attn_prep_fwd(z, mix["g_q"], mix["g_k"], tables, t)
            cat, lse = _flash_fwd(q, k, v)
            mix.update(q=q, k=k, v=v, lse=lse)
        hm = _rms_fwd(mem0, gmem_row)
        kv = _matmul(hm, w_kv)
        cat = _mem_fwd(z, qblk, kv, gmq_row, gmk_row, cat)
        x1 = _matmul(cat, w_o, res=xc)
        h2 = _rms_fwd(x1, gf_row)
        gu = _mm_cols_fwd(h2, g_gu, bf16)
        act = _swiglu_fwd(gu)
        x2 = _matmul(act, w_dn, res=x1)
        saved.append(dict(x=xc, h=h, z=z, mix=mix, cat=cat, hm=hm, kv=kv, x1=x1, h2=h2, gu=gu, act=act, qblk=qblk,
                          w=(g_in, w_kv, w_o, g_gu, w_dn), rows=(gm_row, gf_row, gmem_row, gmq_row, gmk_row)))
        xc = x2

    loss_row, dy, dy_b = _loss_head(xc, target)
    loss = lax.psum(loss_row[0, 0], ("x", "y", "c"))

    small = {n: [None] * given[n].shape[0] for n in ("g_mix", "g_ffn", "g_v_a", "w_spatial", "b_spatial", "g_q_b", "g_k_b", "g_mem", "g_mq", "g_mk")}
    big_out = {}
    dx, dx_b = dy, dy_b
    for l in reversed(range(depth)):
        sv = saved[l]
        is_a = l % 2 == 0
        g_in, w_kv, w_o, g_gu, w_dn = sv["w"]
        gm_row, gf_row, gmem_row, gmq_row, gmk_row = sv["rows"]
        mix = sv["mix"]
        dw_dn = _matmul(sv["act"], dx_b, ta=True, out_dtype=bf16)
        dact = _matmul(dx_b, w_dn, tb=True, out_dtype=bf16)
        dgu = _swiglu_bwd(sv["gu"], dact)
        dw_gu = _mm_cols_wgrad(sv["h2"], dgu, g_gu.shape[2])
        dh2 = _mm_cols_dgrad(dgu, g_gu)
        dx1, dx1_b, dgf = _rms_bwd(sv["x1"], gf_row, dh2, dx)
        small["g_ffn"][l] = dgf.reshape(d)
        dw_o = _matmul(sv["cat"], dx1_b, ta=True, out_dtype=bf16)
        dcat = _matmul(dx1_b, w_o, tb=True, out_dtype=bf16)
        if is_a:
            dz, dws, dbt, dgv = _gmlp_bwd(sv["z"], mix["g_v"], mix["w_s"], mix["b_t"], dcat)
            small["w_spatial"][l // 2], small["b_spatial"][l // 2], small["g_v_a"][l // 2] = dws, dbt.T, dgv.reshape(t)
        else:
            dq, dk, dv = _flash_bwd(mix["q"], mix["k"], mix["v"], sv["cat"], dcat, mix["lse"])
            dz, dgq, dgk = _attn_prep_bwd(sv["z"], mix["g_q"], mix["g_k"], tables, dq, dk, dv, t)
            small["g_q_b"][l // 2], small["g_k_b"][l // 2] = dgq.reshape(HEAD), dgk.reshape(HEAD)
        dz, dkn, dvm, dgmq = _mem_bwd(sv["z"], sv["qblk"], sv["kv"], gmq_row, gmk_row, dcat, dz)
        dkv, dgmk = _mem_kv_bwd(sv["kv"], gmk_row, dkn, dvm)
        dw_kv = _matmul(sv["hm"], dkv, ta=True, out_dtype=bf16)
        dhm = _matmul(dkv, w_kv, tb=True)
        small["g_mem"][l] = _rms_bwd(mem0, gmem_row, dhm, None).reshape(d)
        small["g_mq"][l] = dgmq.reshape(HEAD)
        small["g_mk"][l] = dgmk.reshape(HEAD)
        dw_in = _mm_cols_wgrad(sv["h"], dz, g_in.shape[2])
        dh = _mm_cols_dgrad(dz, g_in)
        dx, dx_b, dgm = _rms_bwd(sv["x"], gm_row, dh, dx1)
        small["g_mix"][l] = dgm.reshape(d)

        grads = [dw_in] + [dw.reshape(N_DEV, -1, dw.shape[1]) for dw in (dw_kv, dw_o)] + [dw_gu, dw_dn.reshape(N_DEV, -1, d)]
        got = _swap_with_sibling(grads)
        arrived = _scatter_to_chips([_pair_sum(g, r) for g, r in zip(grads, got)])
        for name, parts in zip(big_names, arrived):
            key, idx = stacked_key(name, l)
            big_out[key] = _adamw(parts, given[key], given["m_" + key], given["v_" + key], idx, big_out.get(key))

    small_names = tuple(small)
    small_grads = [jnp.stack(small[n]) for n in small_names]
    small_shapes = [g.shape for g in small_grads]
    (all_parts,) = _all_gather([_to_flat(small_grads)])
    souts = _adamw(all_parts, *[_to_flat([given[p + n] for n in small_names])[None] for p in ("", "m_", "v_")], 0, None)
    small_out = dict(zip(small_names, zip(*[_from_flat(flat, small_shapes) for flat in souts])))

    weights = ("g_mix", "g_ffn", "w_in_a", "g_v_a", "w_spatial", "b_spatial", "w_in_b", "g_q_b", "g_k_b", "g_mem", "w_mem_kv",
               "g_mq", "g_mk", "w_out", "w_gate_up", "w_down")
    results = {n: (small_out[n] if n in small_out else big_out[n]) for n in weights}
    grad_x = dx.reshape(1, s, d)
    return (loss, grad_x, *[results[n][kind] for kind in range(4) for n in weights])
```

```python
import functools

import jax
import jax.numpy as jnp
from jax import lax
from jax.experimental import pallas as pl
from jax.experimental.pallas import tpu as pltpu

f32 = jnp.float32
bf16 = jnp.bfloat16

HEAD = 128
CHUNK = 128
GRID_W = 64
MEM_HEADS = 4
KV_HEADS = 4
MEM_WIDTH = MEM_HEADS * HEAD
KV_WIDTH = KV_HEADS * HEAD
ROPE_THETA = 10000.0
ROPE_PAIRS = HEAD // 4
EPS = 1e-6
SCALE = HEAD ** -0.5
N_DEV = 8
LANES = 1024
VMEM_LIMIT = 56 * 1024 * 1024

ADAM_LR, ADAM_B1, ADAM_B2, ADAM_EPS, ADAM_WD, ADAM_STEP = 0.001, 0.9, 0.999, 1e-08, 0.01, 10

MESH = pl.DeviceIdType.MESH
_pallas_call = pl.pallas_call


def _pick(dim, cands):
    for c in cands:
        if dim % c == 0:
            return c
    return dim


def _cp(sem):
    return pltpu.CompilerParams(dimension_semantics=sem, vmem_limit_bytes=VMEM_LIMIT)


def _sds(shape, dtype):
    return jax.ShapeDtypeStruct(shape, dtype)


def _dot(a, b, ca, cb):
    return lax.dot_general(a, b, (((ca,), (cb,)), ((), ())), preferred_element_type=f32)


def _gelu(z):
    return 0.5 * z * (1.0 + lax.erf(z * 0.7071067811865476))


def _gelu_grad(z):
    return 0.5 * (1.0 + lax.erf(z * 0.7071067811865476)) + z * jnp.exp(-0.5 * z * z) * 0.3989422804014327


def _rot(x, sin_a, sin_b):
    return pltpu.roll(x, 96, 1) * sin_a + pltpu.roll(x, 32, 1) * sin_b


def _matmul(a, b, *, ta=False, tb=False, out_dtype=f32, res=None, tm=None, tn=None, tk=None):
    assert a.dtype == bf16 and b.dtype == bf16
    kdim, m = a.shape if ta else a.shape[::-1]
    n, k2 = b.shape if tb else b.shape[::-1]
    assert kdim == k2, (a.shape, b.shape, ta, tb)
    tm = tm or _pick(m, (1024, 512, 256, 128))
    tn = tn or _pick(n, (1024, 512, 256, 128))
    if tk is None:
        tk = kdim if kdim <= 2048 else _pick(kdim, (1408, 1024, 512, 256, 128))
    nk = kdim // tk
    ca, cb = (0 if ta else 1), (1 if tb else 0)
    has_res = res is not None

    def body(*refs):
        a_ref, b_ref = refs[0], refs[1]
        r_ref = refs[2] if has_res else None
        o_ref = refs[3] if has_res else refs[2]
        prod = _dot(a_ref[...], b_ref[...], ca, cb)
        if nk == 1:
            if has_res:
                prod = prod + r_ref[...]
            o_ref[...] = prod.astype(o_ref.dtype)
        else:
            acc = refs[-1]
            k = pl.program_id(2)

            @pl.when(k == 0)
            def _():
                acc[...] = prod

            @pl.when(k > 0)
            def _():
                acc[...] += prod

            @pl.when(k == nk - 1)
            def _():
                out = acc[...]
                if has_res:
                    out = out + r_ref[...]
                o_ref[...] = out.astype(o_ref.dtype)

    a_spec = pl.BlockSpec((tk, tm), lambda i, j, k: (k, i)) if ta else pl.BlockSpec((tm, tk), lambda i, j, k: (i, k))
    b_spec = pl.BlockSpec((tn, tk), lambda i, j, k: (j, k)) if tb else pl.BlockSpec((tk, tn), lambda i, j, k: (k, j))
    o_spec = pl.BlockSpec((tm, tn), lambda i, j, k: (i, j))
    in_specs = [a_spec, b_spec] + ([o_spec] if has_res else [])
    args = (a, b) + ((res,) if has_res else ())
    mode = ("t" if ta else "n") + ("t" if tb else "n")
    return _pallas_call(
        body, out_shape=_sds((m, n), out_dtype), grid=(m // tm, n // tn, nk),
        in_specs=in_specs, out_specs=o_spec,
        scratch_shapes=([pltpu.VMEM((tm, tn), f32)] if nk > 1 else []),
        compiler_params=_cp(("parallel", "parallel", "arbitrary")),
        name=f"mm_{mode}_{m}x{kdim}x{n}{'_res' if has_res else ''}_{jnp.dtype(out_dtype).name}",
    )(*args)


def _shards_per_step(n):
    p = 1 if n % 128 == 0 else 2
    assert (p * n) % 128 == 0 and N_DEV % p == 0
    return p


def _lane_pieces(v, p, n):
    return [v] if p == 1 else [v[:, q * n:(q + 1) * n] for q in range(p)]


def _mm_cols_fwd(a, g, out_dtype):
    m, kdim = a.shape
    nd, k2, n = g.shape
    assert kdim == k2 and a.dtype == bf16 and g.dtype == bf16
    p = _shards_per_step(n)
    tm = _pick(m, (1024, 512, 256, 128))

    def body(a_ref, g_ref, o_ref):
        av = a_ref[...]
        parts = [_dot(av, g_ref[q], 1, 0) for q in range(p)]
        out = parts[0] if p == 1 else jnp.concatenate(parts, axis=1)
        o_ref[...] = out.astype(o_ref.dtype)

    return _pallas_call(
        body, out_shape=_sds((m, nd * n), out_dtype), grid=(m // tm, nd // p),
        in_specs=[pl.BlockSpec((tm, kdim), lambda i, j: (i, 0)), pl.BlockSpec((p, kdim, n), lambda i, j: (j, 0, 0))],
        out_specs=pl.BlockSpec((tm, p * n), lambda i, j: (i, j)),
        compiler_params=_cp(("parallel", "arbitrary")), name=f"mm_cols_fwd_{m}x{kdim}x{nd * n}_{jnp.dtype(out_dtype).name}",
    )(a, g)


def _mm_cols_dgrad(dz, g):
    m, nn = dz.shape
    nd, kdim, n = g.shape
    assert nn == nd * n and dz.dtype == bf16 and g.dtype == bf16
    p = _shards_per_step(n)
    nj = nd // p
    tm = _pick(m, (512, 256, 128))

    def body(dz_ref, g_ref, o_ref, acc):
        j = pl.program_id(1)
        tot = None
        for q, piece in enumerate(_lane_pieces(dz_ref[...], p, n)):
            dd = _dot(piece, g_ref[q], 1, 1)
            tot = dd if tot is None else tot + dd

        @pl.when(j == 0)
        def _():
            acc[...] = tot

        @pl.when(j > 0)
        def _():
            acc[...] += tot

        @pl.when(j == nj - 1)
        def _():
            o_ref[...] = acc[...]

    return _pallas_call(
        body, out_shape=_sds((m, kdim), f32), grid=(m // tm, nj),
        in_specs=[pl.BlockSpec((tm, p * n), lambda i, j: (i, j)), pl.BlockSpec((p, kdim, n), lambda i, j: (j, 0, 0))],
        out_specs=pl.BlockSpec((tm, kdim), lambda i, j: (i, 0)),
        scratch_shapes=[pltpu.VMEM((tm, kdim), f32)],
        compiler_params=_cp(("parallel", "arbitrary")), name=f"mm_cols_dgrad_{m}x{nn}x{kdim}",
    )(dz, g)


def _mm_cols_wgrad(a, dz, n):
    s, kdim = a.shape
    nd = dz.shape[1] // n
    assert a.dtype == bf16 and dz.dtype == bf16
    p = _shards_per_step(n)
    tkw = _pick(kdim, (1024, 512, 256, 128))
    ts = _pick(s, (1024, 512, 256, 128))
    ns = s // ts

    def body(a_ref, dz_ref, o_ref, acc):
        si = pl.program_id(2)
        av = a_ref[...]
        prods = [_dot(av, piece, 0, 0) for piece in _lane_pieces(dz_ref[...], p, n)]

        @pl.when(si == 0)
        def _():
            for q in range(p):
                acc[q] = prods[q]

        @pl.when(si > 0)
        def _():
            for q in range(p):
                acc[q] += prods[q]

        @pl.when(si == ns - 1)
        def _():
            o_ref[...] = acc[...].astype(bf16)

    return _pallas_call(
        body, out_shape=_sds((nd, kdim, n), bf16), grid=(kdim // tkw, nd // p, ns),
        in_specs=[pl.BlockSpec((ts, tkw), lambda i, j, k: (k, i)), pl.BlockSpec((ts, p * n), lambda i, j, k: (k, j))],
        out_specs=pl.BlockSpec((p, tkw, n), lambda i, j, k: (j, i, 0)),
        scratch_shapes=[pltpu.VMEM((p, tkw, n), f32)],
        compiler_params=_cp(("parallel", "parallel", "arbitrary")), name=f"mm_cols_wgrad_{kdim}x{s}x{nd * n}",
    )(a, dz)


def _rms_fwd(x, g_row):
    s, d = x.shape
    tr = _pick(s, (512, 256, 128))

    def body(x_ref, g_ref, o_ref):
        xv = x_ref[...]
        r = lax.rsqrt(jnp.mean(xv * xv, axis=-1, keepdims=True) + EPS)
        o_ref[...] = (xv * r * g_ref[...]).astype(bf16)

    return _pallas_call(
        body, out_shape=_sds((s, d), bf16), grid=(s // tr,),
        in_specs=[pl.BlockSpec((tr, d), lambda i: (i, 0)), pl.BlockSpec((1, d), lambda i: (0, 0))],
        out_specs=pl.BlockSpec((tr, d), lambda i: (i, 0)),
        compiler_params=_cp(("parallel",)), name=f"rms_fwd_{s}x{d}",
    )(x, g_row)


def _rms_bwd(x, g_row, dh, dres):
    s, d = x.shape
    tr = _pick(s, (512, 256, 128))
    with_dx = dres is not None

    def body(*refs):
        if with_dx:
            x_ref, g_ref, dh_ref, dres_ref, dx_ref, dxb_ref, dg_ref = refs
        else:
            x_ref, g_ref, dh_ref, dg_ref = refs

        @pl.when(pl.program_id(0) == 0)
        def _():
            dg_ref[...] = jnp.zeros_like(dg_ref)

        xv = x_ref[...]
        r = lax.rsqrt(jnp.mean(xv * xv, axis=-1, keepdims=True) + EPS)
        xh = xv * r
        dy = dh_ref[...].astype(f32)
        dg_ref[...] += jnp.sum(dy * xh, axis=0, keepdims=True)
        if with_dx:
            gy = dy * g_ref[...]
            dx = dres_ref[...] + r * (gy - xh * jnp.mean(gy * xh, axis=-1, keepdims=True))
            dx_ref[...] = dx
            dxb_ref[...] = dx.astype(bf16)

    row = pl.BlockSpec((tr, d), lambda i: (i, 0))
    vec = pl.BlockSpec((1, d), lambda i: (0, 0))
    if with_dx:
        return _pallas_call(
            body, out_shape=(_sds((s, d), f32), _sds((s, d), bf16), _sds((1, d), f32)), grid=(s // tr,),
            in_specs=[row, vec, row, row], out_specs=(row, row, vec),
            compiler_params=_cp(("arbitrary",)), name=f"rms_bwd_{s}x{d}",
        )(x, g_row, dh, dres)
    return _pallas_call(
        body, out_shape=_sds((1, d), f32), grid=(s // tr,),
        in_specs=[row, vec, row], out_specs=vec,
        compiler_params=_cp(("arbitrary",)), name=f"rms_bwd_gain_{s}x{d}",
    )(x, g_row, dh)


def _gmlp_rows(s):
    return CHUNK * (2 if (s // CHUNK) % 2 == 0 else 1)


def _gmlp_fwd(z, g_v, w_s, b_t):
    s = z.shape[0]
    t = g_v.shape[1]
    ng = t // HEAD
    rb = _gmlp_rows(s)

    def body(z_ref, gv_ref, ws_ref, bt_ref, o_ref):
        for ci in range(rb // CHUNK):
            lo = ci * CHUNK
            a = _gelu(z_ref[lo:lo + CHUNK, :])
            u, vv = a[:, :t], a[:, t:]
            r = lax.rsqrt(jnp.mean(vv * vv, axis=-1, keepdims=True) + EPS)
            vn = (vv * r * gv_ref[...]).astype(bf16)
            for g in range(ng):
                cs = slice(g * HEAD, (g + 1) * HEAD)
                sg = _dot(ws_ref[g].astype(bf16), vn[:, cs], 1, 0) + bt_ref[:, g:g + 1]
                o_ref[lo:lo + CHUNK, cs] = (u[:, cs] * sg).astype(bf16)

    return _pallas_call(
        body, out_shape=_sds((s, t + MEM_WIDTH), bf16), grid=(s // rb,),
        in_specs=[pl.BlockSpec((rb, 2 * t), lambda i: (i, 0)), pl.BlockSpec((1, t), lambda i: (0, 0)),
                  pl.BlockSpec((ng, CHUNK, CHUNK), lambda i: (0, 0, 0)), pl.BlockSpec((CHUNK, ng), lambda i: (0, 0))],
        out_specs=pl.BlockSpec((rb, t), lambda i: (i, 0)),
        compiler_params=_cp(("parallel",)), name=f"gmlp_fwd_{s}",
    )(z, g_v, w_s, b_t)


def _gmlp_bwd(z, g_v, w_s, b_t, dtok):
    s = z.shape[0]
    t = g_v.shape[1]
    ng = t // HEAD
    rb = _gmlp_rows(s)
    nsteps = s // rb

    def body(z_ref, gv_ref, ws_ref, bt_ref, dt_ref, dz_ref, dws_ref, dbt_ref, dgv_ref, ds_acc):
        step = pl.program_id(0)

        @pl.when(step == 0)
        def _():
            dws_ref[...] = jnp.zeros_like(dws_ref)
            dgv_ref[...] = jnp.zeros_like(dgv_ref)
            ds_acc[...] = jnp.zeros_like(ds_acc)

        for ci in range(rb // CHUNK):
            lo = ci * CHUNK
            zz = z_ref[lo:lo + CHUNK, :]
            a = _gelu(zz)
            u, vv = a[:, :t], a[:, t:]
            r = lax.rsqrt(jnp.mean(vv * vv, axis=-1, keepdims=True) + EPS)
            vh = vv * r
            vn = (vh * gv_ref[...]).astype(bf16)
            dtok = dt_ref[lo:lo + CHUNK, :].astype(f32)
            ds = dtok * u
            ds_acc[...] += ds
            dsb = ds.astype(bf16)
            du_parts, dvn_parts = [], []
            for g in range(ng):
                cs = slice(g * HEAD, (g + 1) * HEAD)
                wg = ws_ref[g].astype(bf16)
                sg = _dot(wg, vn[:, cs], 1, 0) + bt_ref[:, g:g + 1]
                du_parts.append(dtok[:, cs] * sg)
                dws_ref[g] += _dot(dsb[:, cs], vn[:, cs], 1, 1)
                dvn_parts.append(_dot(wg, dsb[:, cs], 0, 0))
            dvn = jnp.concatenate(dvn_parts, axis=1)
            dgv_ref[...] += jnp.sum(dvn * vh, axis=0, keepdims=True)
            gy = dvn * gv_ref[...]
            dvv = r * (gy - vh * jnp.mean(gy * vh, axis=-1, keepdims=True))
            da = jnp.concatenate(du_parts + [dvv], axis=1)
            dz_ref[lo:lo + CHUNK, :] = (da * _gelu_grad(zz)).astype(bf16)

        @pl.when(step == nsteps - 1)
        def _():
            for g in range(ng):
                dbt_ref[:, g:g + 1] = jnp.sum(ds_acc[:, g * HEAD:(g + 1) * HEAD], axis=1, keepdims=True)

    return _pallas_call(
        body,
        out_shape=(_sds((s, z.shape[1]), bf16), _sds((ng, CHUNK, CHUNK), f32), _sds((CHUNK, ng), f32), _sds((1, t), f32)),
        grid=(nsteps,),
        in_specs=[pl.BlockSpec((rb, 2 * t), lambda i: (i, 0)), pl.BlockSpec((1, t), lambda i: (0, 0)),
                  pl.BlockSpec((ng, CHUNK, CHUNK), lambda i: (0, 0, 0)), pl.BlockSpec((CHUNK, ng), lambda i: (0, 0)),
                  pl.BlockSpec((rb, t), lambda i: (i, 0))],
        out_specs=(pl.BlockSpec((rb, 2 * t), lambda i: (i, 0)), pl.BlockSpec((ng, CHUNK, CHUNK), lambda i: (0, 0, 0)),
                   pl.BlockSpec((CHUNK, ng), lambda i: (0, 0)), pl.BlockSpec((1, t), lambda i: (0, 0))),
        scratch_shapes=[pltpu.VMEM((CHUNK, t), f32)],
        compiler_params=_cp(("arbitrary",)), name=f"gmlp_bwd_{s}",
    )(z, g_v, w_s, b_t, dtok)


def _rope_tables(s):
    n_rows = s // GRID_W
    rows = jnp.broadcast_to(jnp.arange(n_rows)[:, None], (n_rows, GRID_W)).reshape(s)
    cols = jnp.broadcast_to(jnp.arange(GRID_W)[None, :], (n_rows, GRID_W)).reshape(s)
    freqs = ROPE_THETA ** (-jnp.arange(ROPE_PAIRS, dtype=f32) / ROPE_PAIRS)
    ang_r = rows.astype(f32)[:, None] * freqs
    ang_c = cols.astype(f32)[:, None] * freqs
    ang = jnp.concatenate([ang_r, ang_r, ang_c, ang_c], axis=-1)
    cos, sin = jnp.cos(ang), jnp.sin(ang)
    first = (jnp.arange(HEAD) % (2 * ROPE_PAIRS)) < ROPE_PAIRS
    return cos, jnp.where(first, -sin, 0.0), jnp.where(first, 0.0, sin)


def _attn_prep_fwd(z, g_q, g_k, tables, t):
    s = z.shape[0]
    tr = _pick(s, (256, 128))
    nq = t // HEAD
    width = t + 2 * KV_WIDTH

    def body(z_ref, gq_ref, gk_ref, cos_ref, sa_ref, sb_ref, q_ref, k_ref, v_ref):
        cos, sa, sb = cos_ref[...], sa_ref[...], sb_ref[...]
        for h in range(nq + KV_HEADS):
            cs = slice(h * HEAD, (h + 1) * HEAD)
            xv = z_ref[:, cs]
            r = lax.rsqrt(jnp.mean(xv * xv, axis=-1, keepdims=True) + EPS)
            xn = xv * r * (gq_ref[...] if h < nq else gk_ref[...])
            y = (xn * cos + _rot(xn, sa, sb)).astype(bf16)
            if h < nq:
                q_ref[:, cs] = y
            else:
                k_ref[:, (h - nq) * HEAD:(h - nq + 1) * HEAD] = y
        v_ref[...] = z_ref[:, t + KV_WIDTH:width].astype(bf16)

    row = lambda w: pl.BlockSpec((tr, w), lambda i: (i, 0))
    vec = pl.BlockSpec((1, HEAD), lambda i: (0, 0))
    return _pallas_call(
        body, out_shape=(_sds((s, t), bf16), _sds((s, KV_WIDTH), bf16), _sds((s, KV_WIDTH), bf16)), grid=(s // tr,),
        in_specs=[row(width), vec, vec, row(HEAD), row(HEAD), row(HEAD)],
        out_specs=(row(t), row(KV_WIDTH), row(KV_WIDTH)),
        compiler_params=_cp(("parallel",)), name=f"attn_prep_fwd_{s}",
    )(z, g_q, g_k, *tables)


def _attn_prep_bwd(z, g_q, g_k, tables, dq, dk, dv, t):
    s = z.shape[0]
    tr = _pick(s, (256, 128))
    nq = t // HEAD
    width = t + 2 * KV_WIDTH

    def body(z_ref, gq_ref, gk_ref, cos_ref, sa_ref, sb_ref, dq_ref, dk_ref, dv_ref, dz_ref, dgq_ref, dgk_ref):
        @pl.when(pl.program_id(0) == 0)
        def _():
            dgq_ref[...] = jnp.zeros_like(dgq_ref)
            dgk_ref[...] = jnp.zeros_like(dgk_ref)

        cos, sa, sb = cos_ref[...], sa_ref[...], sb_ref[...]
        for h in range(nq + KV_HEADS):
            cs = slice(h * HEAD, (h + 1) * HEAD)
            xv = z_ref[:, cs]
            r = lax.rsqrt(jnp.mean(xv * xv, axis=-1, keepdims=True) + EPS)
            xh = xv * r
            if h < nq:
                dy, g_ref, dg_ref = dq_ref[:, cs], gq_ref, dgq_ref
            else:
                dy, g_ref, dg_ref = dk_ref[:, (h - nq) * HEAD:(h - nq + 1) * HEAD], gk_ref, dgk_ref
            dy = dy.astype(f32)
            dxn = dy * cos - _rot(dy, sa, sb)
            dg_ref[...] += jnp.sum(dxn * xh, axis=0, keepdims=True)
            gy = dxn * g_ref[...]
            dz_ref[:, cs] = (r * (gy - xh * jnp.mean(gy * xh, axis=-1, keepdims=True))).astype(bf16)
        dz_ref[:, t + KV_WIDTH:width] = dv_ref[...].astype(bf16)

    row = lambda w: pl.BlockSpec((tr, w), lambda i: (i, 0))
    vec = pl.BlockSpec((1, HEAD), lambda i: (0, 0))
    return _pallas_call(
        body, out_shape=(_sds((s, z.shape[1]), bf16), _sds((1, HEAD), f32), _sds((1, HEAD), f32)), grid=(s // tr,),
        in_specs=[row(width), vec, vec, row(HEAD), row(HEAD), row(HEAD), row(t), row(KV_WIDTH), row(KV_WIDTH)],
        out_specs=(row(width), vec, vec),
        compiler_params=_cp(("arbitrary",)), name=f"attn_prep_bwd_{s}",
    )(z, g_q, g_k, *tables, dq, dk, dv)


def _flash_tiles(s):
    return _pick(s, (512, 256, 128)), _pick(s, (512, 256, 128))


def _stack_heads(ref, grp):
    return jnp.concatenate([ref[:, g * HEAD:(g + 1) * HEAD] for g in range(grp)], axis=0)


def _flash_fwd(q, k, v):
    s, t = q.shape
    grp = t // KV_WIDTH
    tq, tk = _flash_tiles(s)
    nkv = s // tk
    rows = grp * tq

    def body(q_ref, k_ref, v_ref, o_ref, lse_ref, m_sc, l_sc, acc_sc):
        ki = pl.program_id(2)

        @pl.when(ki == 0)
        def _():
            m_sc[...] = jnp.full(m_sc.shape, -jnp.inf, f32)
            l_sc[...] = jnp.zeros_like(l_sc)
            acc_sc[...] = jnp.zeros_like(acc_sc)

        qs = _stack_heads(q_ref, grp)
        sc = _dot(qs, k_ref[...], 1, 1) * SCALE
        m_prev = m_sc[...]
        m_new = jnp.maximum(m_prev, jnp.max(sc, axis=-1, keepdims=True))
        alpha = jnp.exp(m_prev - m_new)
        p = jnp.exp(sc - m_new)
        l_sc[...] = alpha * l_sc[...] + jnp.sum(p, axis=-1, keepdims=True)
        acc_sc[...] = alpha * acc_sc[...] + _dot(p.astype(bf16), v_ref[...], 1, 0)
        m_sc[...] = m_new

        @pl.when(ki == nkv - 1)
        def _():
            o = acc_sc[...] / l_sc[...]
            for g in range(grp):
                o_ref[:, g * HEAD:(g + 1) * HEAD] = o[g * tq:(g + 1) * tq].astype(bf16)
            lse_ref[0] = jnp.broadcast_to(m_sc[...] + jnp.log(l_sc[...]), (rows, HEAD))

    return _pallas_call(
        body, out_shape=(_sds((s, t + MEM_WIDTH), bf16), _sds((KV_HEADS, grp * s, HEAD), f32)), grid=(KV_HEADS, s // tq, nkv),
        in_specs=[pl.BlockSpec((tq, grp * HEAD), lambda h, i, j: (i, h)), pl.BlockSpec((tk, HEAD), lambda h, i, j: (j, h)),
                  pl.BlockSpec((tk, HEAD), lambda h, i, j: (j, h))],
        out_specs=(pl.BlockSpec((tq, grp * HEAD), lambda h, i, j: (i, h)), pl.BlockSpec((1, rows, HEAD), lambda h, i, j: (h, i, 0))),
        scratch_shapes=[pltpu.VMEM((rows, 1), f32), pltpu.VMEM((rows, 1), f32), pltpu.VMEM((rows, HEAD), f32)],
        compiler_params=_cp(("parallel", "parallel", "arbitrary")), name=f"flash_fwd_{s}",
    )(q, k, v)


def _flash_probs(q_ref, k_ref, v_ref, o_ref, do_ref, lse_ref, grp):
    qs = _stack_heads(q_ref, grp)
    dos = _stack_heads(do_ref, grp)
    os_ = _stack_heads(o_ref, grp)
    delta = jnp.sum(dos.astype(f32) * os_.astype(f32), axis=-1, keepdims=True)
    sc = _dot(qs, k_ref[...], 1, 1) * SCALE
    p = jnp.exp(sc - lse_ref[0, :, 0:1])
    dp = _dot(dos, v_ref[...], 1, 1)
    ds = (p * (dp - delta) * SCALE).astype(bf16)
    return qs, dos, p.astype(bf16), ds


def _flash_bwd(q, k, v, o, do, lse):
    s, t = q.shape
    grp = t // KV_WIDTH
    tq, tk = _flash_tiles(s)
    nq, nkv = s // tq, s // tk
    rows = grp * tq

    def dkv_body(q_ref, k_ref, v_ref, o_ref, do_ref, lse_ref, dk_ref, dv_ref, dk_acc, dv_acc):
        qi = pl.program_id(2)

        @pl.when(qi == 0)
        def _():
            dk_acc[...] = jnp.zeros_like(dk_acc)
            dv_acc[...] = jnp.zeros_like(dv_acc)

        qs, dos, p, ds = _flash_probs(q_ref, k_ref, v_ref, o_ref, do_ref, lse_ref, grp)
        dv_acc[...] += _dot(p, dos, 0, 0)
        dk_acc[...] += _dot(ds, qs, 0, 0)

        @pl.when(qi == nq - 1)
        def _():
            dk_ref[...] = dk_acc[...]
            dv_ref[...] = dv_acc[...]

    def dq_body(q_ref, k_ref, v_ref, o_ref, do_ref, lse_ref, dq_ref, dq_acc):
        kj = pl.program_id(2)

        @pl.when(kj == 0)
        def _():
            dq_acc[...] = jnp.zeros_like(dq_acc)

        _, _, _, ds = _flash_probs(q_ref, k_ref, v_ref, o_ref, do_ref, lse_ref, grp)
        dq_acc[...] += _dot(ds, k_ref[...], 1, 0)

        @pl.when(kj == nkv - 1)
        def _():
            for g in range(grp):
                dq_ref[:, g * HEAD:(g + 1) * HEAD] = dq_acc[g * tq:(g + 1) * tq]

    def specs(qmap, kmap):
        qb = pl.BlockSpec((tq, grp * HEAD), lambda h, a, b: (qmap(a, b), h))
        kb = pl.BlockSpec((tk, HEAD), lambda h, a, b: (kmap(a, b), h))
        lb = pl.BlockSpec((1, rows, HEAD), lambda h, a, b: (h, qmap(a, b), 0))
        return qb, kb, lb

    qb, kb, lb = specs(lambda a, b: b, lambda a, b: a)
    dk, dv = _pallas_call(
        dkv_body, out_shape=(_sds((s, KV_WIDTH), f32), _sds((s, KV_WIDTH), f32)), grid=(KV_HEADS, nkv, nq),
        in_specs=[qb, kb, kb, qb, qb, lb], out_specs=(kb, kb),
        scratch_shapes=[pltpu.VMEM((tk, HEAD), f32), pltpu.VMEM((tk, HEAD), f32)],
        compiler_params=_cp(("parallel", "parallel", "arbitrary")), name=f"flash_bwd_dkv_{s}",
    )(q, k, v, o, do, lse)
    qb, kb, lb = specs(lambda a, b: a, lambda a, b: b)
    dq = _pallas_call(
        dq_body, out_shape=_sds((s, t), f32), grid=(KV_HEADS, nq, nkv),
        in_specs=[qb, kb, kb, qb, qb, lb], out_specs=qb,
        scratch_shapes=[pltpu.VMEM((rows, HEAD), f32)],
        compiler_params=_cp(("parallel", "parallel", "arbitrary")), name=f"flash_bwd_dq_{s}",
    )(q, k, v, o, do, lse)
    return dq, dk, dv


def _mem_heads(z_ref, kv_ref, gq_ref, gk_ref, h):
    cs = slice(h * HEAD, (h + 1) * HEAD)
    xv = z_ref[:, cs]
    r = lax.rsqrt(jnp.mean(xv * xv, axis=-1, keepdims=True) + EPS)
    xh = xv * r
    kx = kv_ref[:, cs]
    rk = lax.rsqrt(jnp.mean(kx * kx, axis=-1, keepdims=True) + EPS)
    kn = (kx * rk * gk_ref[...]).astype(bf16)
    vv = kv_ref[:, MEM_WIDTH + h * HEAD:MEM_WIDTH + (h + 1) * HEAD].astype(bf16)
    qn = (xh * gq_ref[...]).astype(bf16)
    sc = _dot(qn, kn, 1, 1) * SCALE
    e = jnp.exp(sc - jnp.max(sc, axis=-1, keepdims=True))
    p = e / jnp.sum(e, axis=-1, keepdims=True)
    return cs, r, xh, qn, kn, vv, p


def _mem_fwd(z, qblk, kv, g_mq, g_mk, cat):
    s = z.shape[0]
    nm = kv.shape[0]
    tr = _pick(s, (512, 256, 128))
    oblk = cat.shape[1] // MEM_WIDTH - 1

    def body(z_ref, kv_ref, gq_ref, gk_ref, cat_ref, o_ref):
        for h in range(MEM_HEADS):
            cs, _, _, _, _, vv, p = _mem_heads(z_ref, kv_ref, gq_ref, gk_ref, h)
            o_ref[:, cs] = _dot(p.astype(bf16), vv, 1, 0).astype(bf16)

    vec = pl.BlockSpec((1, HEAD), lambda i: (0, 0))
    return _pallas_call(
        body, out_shape=_sds(cat.shape, bf16), grid=(s // tr,),
        in_specs=[pl.BlockSpec((tr, MEM_WIDTH), lambda i: (i, qblk)), pl.BlockSpec((nm, 2 * MEM_WIDTH), lambda i: (0, 0)), vec, vec,
                  pl.BlockSpec(memory_space=pl.ANY)],
        out_specs=pl.BlockSpec((tr, MEM_WIDTH), lambda i: (i, oblk)),
        input_output_aliases={4: 0},
        compiler_params=_cp(("parallel",)), name=f"mem_fwd_{s}_{qblk}",
    )(z, kv, g_mq, g_mk, cat)


def _mem_bwd(z, qblk, kv, g_mq, g_mk, dcat, dz):
    s = z.shape[0]
    nm = kv.shape[0]
    tr = _pick(s, (512, 256, 128))
    dblk = dcat.shape[1] // MEM_WIDTH - 1

    def body(z_ref, kv_ref, gq_ref, gk_ref, dm_ref, dzin_ref, dz_ref, dkn_ref, dv_ref, dgq_ref):
        @pl.when(pl.program_id(0) == 0)
        def _():
            dkn_ref[...] = jnp.zeros_like(dkn_ref)
            dv_ref[...] = jnp.zeros_like(dv_ref)
            dgq_ref[...] = jnp.zeros_like(dgq_ref)

        for h in range(MEM_HEADS):
            cs, r, xh, qn, kn, vv, p = _mem_heads(z_ref, kv_ref, gq_ref, gk_ref, h)
            dm = dm_ref[:, cs]
            dv_ref[:, cs] += _dot(p.astype(bf16), dm, 0, 0)
            dp = _dot(dm, vv, 1, 1)
            ds = (p * (dp - jnp.sum(dp * p, axis=-1, keepdims=True)) * SCALE).astype(bf16)
            dqn = _dot(ds, kn, 1, 0)
            dkn_ref[:, cs] += _dot(ds, qn, 0, 0)
            dgq_ref[...] += jnp.sum(dqn * xh, axis=0, keepdims=True)
            gy = dqn * gq_ref[...]
            dz_ref[:, cs] = (r * (gy - xh * jnp.mean(gy * xh, axis=-1, keepdims=True))).astype(bf16)

    vec = pl.BlockSpec((1, HEAD), lambda i: (0, 0))
    acc = pl.BlockSpec((nm, MEM_WIDTH), lambda i: (0, 0))
    return _pallas_call(
        body, out_shape=(_sds(dz.shape, bf16), _sds((nm, MEM_WIDTH), f32), _sds((nm, MEM_WIDTH), f32), _sds((1, HEAD), f32)),
        grid=(s // tr,),
        in_specs=[pl.BlockSpec((tr, MEM_WIDTH), lambda i: (i, qblk)), pl.BlockSpec((nm, 2 * MEM_WIDTH), lambda i: (0, 0)), vec, vec,
                  pl.BlockSpec((tr, MEM_WIDTH), lambda i: (i, dblk)), pl.BlockSpec(memory_space=pl.ANY)],
        out_specs=(pl.BlockSpec((tr, MEM_WIDTH), lambda i: (i, qblk)), acc, acc, vec),
        input_output_aliases={5: 0},
        compiler_params=_cp(("arbitrary",)), name=f"mem_bwd_{s}_{qblk}",
    )(z, kv, g_mq, g_mk, dcat, dz)


def _mem_kv_bwd(kv, g_mk, dkn, dv):
    nm = kv.shape[0]

    def body(kv_ref, gk_ref, dkn_ref, dv_ref, dkv_ref, dgk_ref):
        dgk = jnp.zeros((1, HEAD), f32)
        for h in range(MEM_HEADS):
            cs = slice(h * HEAD, (h + 1) * HEAD)
            kx = kv_ref[:, cs]
            rk = lax.rsqrt(jnp.mean(kx * kx, axis=-1, keepdims=True) + EPS)
            kh = kx * rk
            dkn_h = dkn_ref[:, cs]
            dgk = dgk + jnp.sum(dkn_h * kh, axis=0, keepdims=True)
            gy = dkn_h * gk_ref[...]
            dkv_ref[:, cs] = (rk * (gy - kh * jnp.mean(gy * kh, axis=-1, keepdims=True))).astype(bf16)
        dkv_ref[:, MEM_WIDTH:] = dv_ref[...].astype(bf16)
        dgk_ref[...] = dgk

    return _pallas_call(
        body, out_shape=(_sds((nm, 2 * MEM_WIDTH), bf16), _sds((1, HEAD), f32)),
        compiler_params=pltpu.CompilerParams(vmem_limit_bytes=VMEM_LIMIT), name=f"mem_kv_bwd_{nm}",
    )(kv, g_mk, dkn, dv)


def _ffn_tiles(s, ff):
    return _pick(s, (512, 256, 128)), _pick(ff, (1408, 1024, 512, 256, 128))


def _swiglu_fwd(gu):
    s, ff2 = gu.shape
    ff = ff2 // 2
    tr, tf = _ffn_tiles(s, ff)
    nf = ff // tf

    def body(g_ref, u_ref, o_ref):
        g = g_ref[...].astype(f32)
        o_ref[...] = (g * jax.nn.sigmoid(g) * u_ref[...].astype(f32)).astype(bf16)

    return _pallas_call(
        body, out_shape=_sds((s, ff), bf16), grid=(s // tr, nf),
        in_specs=[pl.BlockSpec((tr, tf), lambda i, j: (i, j)), pl.BlockSpec((tr, tf), lambda i, j: (i, j + nf))],
        out_specs=pl.BlockSpec((tr, tf), lambda i, j: (i, j)),
        compiler_params=_cp(("parallel", "parallel")), name=f"swiglu_fwd_{s}x{ff}",
    )(gu, gu)


def _swiglu_bwd(gu, dact):
    s, ff2 = gu.shape
    ff = ff2 // 2
    tr, tf = _ffn_tiles(s, ff)
    nf = ff // tf

    def body(g_ref, u_ref, da_ref, o_ref):
        g = g_ref[...].astype(f32)
        u = u_ref[...].astype(f32)
        da = da_ref[...].astype(f32)
        sg = jax.nn.sigmoid(g)
        dgate = da * u * sg * (1.0 + g * (1.0 - sg))
        dup = da * g * sg
        o_ref[...] = jnp.where(pl.program_id(1) < nf, dgate, dup).astype(bf16)

    return _pallas_call(
        body, out_shape=_sds((s, ff2), bf16), grid=(s // tr, 2 * nf),
        in_specs=[pl.BlockSpec((tr, tf), lambda i, j: (i, j % nf)), pl.BlockSpec((tr, tf), lambda i, j: (i, j % nf + nf)),
                  pl.BlockSpec((tr, tf), lambda i, j: (i, j % nf))],
        out_specs=pl.BlockSpec((tr, tf), lambda i, j: (i, j)),
        compiler_params=_cp(("parallel", "parallel")), name=f"swiglu_bwd_{s}x{ff}",
    )(gu, gu, dact)


def _loss_head(y, target):
    s, d = y.shape
    tr = _pick(s, (512, 256, 128))

    def body(y_ref, t_ref, l_ref, dy_ref, dyb_ref):
        @pl.when(pl.program_id(0) == 0)
        def _():
            l_ref[...] = jnp.zeros_like(l_ref)

        err = y_ref[...] - t_ref[...]
        l_ref[...] += 0.5 * jnp.sum(jnp.mean(err * err, axis=-1, keepdims=True), axis=0, keepdims=True)
        dy = err * (1.0 / d)
        dy_ref[...] = dy
        dyb_ref[...] = dy.astype(bf16)

    row = pl.BlockSpec((tr, d), lambda i: (i, 0))
    return _pallas_call(
        body, out_shape=(_sds((1, HEAD), f32), _sds((s, d), f32), _sds((s, d), bf16)), grid=(s // tr,),
        in_specs=[row, row], out_specs=(pl.BlockSpec((1, HEAD), lambda i: (0, 0)), row, row),
        compiler_params=_cp(("arbitrary",)), name=f"loss_{s}x{d}",
    )(y, target)


def _place():
    return lax.axis_index("x"), lax.axis_index("y"), lax.axis_index("c")


def _tag(arrays):
    return "_".join("x".join(str(dd) for dd in a.shape) for a in arrays)


def _all_gather(shards):
    nw = len(shards)
    hbm = pl.BlockSpec(memory_space=pl.ANY)

    def body(*refs):
        x_refs, out_refs = refs[:nw], refs[nw:2 * nw]
        send_sems, recv_sems, local_sems = refs[2 * nw:]
        x, y, c = _place()
        me, sibling = (x, y, c), (x, y, 1 - c)
        chips = [(1 - x, y), (x, 1 - y), (1 - x, 1 - y)]

        def slot(w, place):
            px, py, pc = place
            return out_refs[w].at[4 * px + 2 * py + pc]

        def copy(k, w, block_of, to, from_input=False):
            return pltpu.make_async_remote_copy(
                src_ref=x_refs[w] if from_input else slot(w, block_of), dst_ref=slot(w, block_of),
                send_sem=send_sems.at[k, w], recv_sem=recv_sems.at[k, w], device_id=to, device_id_type=MESH)

        mine = [pltpu.make_async_copy(x_refs[w], slot(w, me), local_sems.at[w]) for w in range(nw)]
        for cp in mine:
            cp.start()
        first = []
        for w in range(nw):
            first.append(copy(0, w, me, sibling, from_input=True))
            first += [copy(1 + j, w, me, (*chip, c), from_input=True) for j, chip in enumerate(chips)]
        for cp in first:
            cp.start()
        passed = []
        for w in range(nw):
            for j, chip in enumerate(chips):
                copy(1 + j, w, (*chip, c), me).wait_recv()
                fwd = copy(4 + j, w, (*chip, c), sibling)
                fwd.start()
                passed.append(fwd)
        for w in range(nw):
            copy(0, w, sibling, me).wait_recv()
            for j, chip in enumerate(chips):
                copy(4 + j, w, (*chip, 1 - c), me).wait_recv()
        for cp in first + passed:
            cp.wait_send()
        for cp in mine:
            cp.wait()

    return _pallas_call(
        body, out_shape=tuple(_sds((N_DEV,) + a.shape, a.dtype) for a in shards), in_specs=[hbm] * nw, out_specs=tuple([hbm] * nw),
        scratch_shapes=[pltpu.SemaphoreType.DMA((7, nw)), pltpu.SemaphoreType.DMA((7, nw)), pltpu.SemaphoreType.DMA((nw,))],
        name=f"all_gather_{_tag(shards)}_{jnp.dtype(shards[0].dtype).name}",
    )(*shards)


def _swap_with_sibling(grads):
    nw = len(grads)
    nchip = N_DEV // 2
    hbm = pl.BlockSpec(memory_space=pl.ANY)

    def body(*refs):
        g_refs, got_refs = refs[:nw], refs[nw:2 * nw]
        send_sems, recv_sems = refs[2 * nw:]
        x, y, c = _place()
        copies = [pltpu.make_async_remote_copy(
            src_ref=g_refs[w].at[2 * k + (1 - c)], dst_ref=got_refs[w].at[k],
            send_sem=send_sems.at[w, k], recv_sem=recv_sems.at[w, k], device_id=(x, y, 1 - c), device_id_type=MESH)
            for w in range(nw) for k in range(nchip)]
        for cp in copies:
            cp.start()
        for cp in copies:
            cp.wait()

    return _pallas_call(
        body, out_shape=tuple(_sds((nchip,) + g.shape[1:], g.dtype) for g in grads), in_specs=[hbm] * nw, out_specs=tuple([hbm] * nw),
        scratch_shapes=[pltpu.SemaphoreType.DMA((nw, nchip)), pltpu.SemaphoreType.DMA((nw, nchip))],
        name=f"swap_sibling_{_tag(grads)}",
    )(*grads)


def _pair_sum(grad, got):
    nd, a, b = grad.shape
    nchip = nd // 2
    ta = _pick(a, (1024, 704, 512, 352, 256, 128, 64, 32, 16))

    def my_chip():
        return 2 * lax.axis_index("x") + lax.axis_index("y")

    def body(a_ref, b_ref, o_ref, land_ref):
        tot = (a_ref[...].astype(f32) + b_ref[...].astype(f32)).astype(o_ref.dtype)
        o_ref[...] = tot

        @pl.when(pl.program_id(1) == my_chip())
        def _():
            land_ref[...] = tot

    return _pallas_call(
        body, out_shape=(_sds(got.shape, grad.dtype), _sds(got.shape, grad.dtype)), grid=(a // ta, nchip),
        in_specs=[pl.BlockSpec((None, ta, b), lambda i, k: (2 * k + lax.axis_index("c"), i, 0)),
                  pl.BlockSpec((None, ta, b), lambda i, k: (k, i, 0))],
        out_specs=(pl.BlockSpec((None, ta, b), lambda i, k: (k, i, 0)),
                   pl.BlockSpec((None, ta, b), lambda i, k: (my_chip(), i, 0))),
        compiler_params=_cp(("parallel", "arbitrary")), name=f"pair_sum_{a}x{b}",
    )(grad, got)


_HBM = pl.BlockSpec(memory_space=pltpu.HBM)
_SEM = pl.BlockSpec(memory_space=pltpu.SEMAPHORE)
_ANY = pl.BlockSpec(memory_space=pl.ANY)
_DATAFLOW = pltpu.SideEffectType.DATAFLOW_SIDE_EFFECTING


def _in_hbm(a):
    return pltpu.with_memory_space_constraint(a, pltpu.HBM)


def _exchange_begin(srcs, lands, route, after, copies_of, n_copies, name):
    nw = len(srcs)

    def body(*refs):
        s_refs, l_refs = refs[:nw], refs[nw:2 * nw]
        send_sems, recv_sems = refs[2 * nw + 2], refs[2 * nw + 3]
        for w in range(nw):
            for k, (src, dst, to) in enumerate(copies_of(w, s_refs[w], l_refs[w])):
                pltpu.make_async_remote_copy(src_ref=src, dst_ref=dst, send_sem=send_sems.at[k * nw + w],
                                             recv_sem=recv_sems.at[k * nw + w], device_id=to, device_id_type=MESH).start()

    out = _pallas_call(
        body, name=name,
        out_shape=(pltpu.SemaphoreType.DMA((n_copies * nw,)), pltpu.SemaphoreType.DMA((n_copies * nw,)),
                   *[pltpu.HBM(a.shape, a.dtype) for a in srcs], *[pltpu.HBM(a.shape, a.dtype) for a in lands],
                   pltpu.HBM(route.shape, route.dtype)),
        in_specs=[_HBM] * (2 * nw + 1) + [_ANY], out_specs=(_SEM, _SEM, *[_HBM] * (2 * nw + 1)),
        input_output_aliases={i: 2 + i for i in range(2 * nw + 1)},
        compiler_params=pltpu.CompilerParams(has_side_effects=_DATAFLOW),
    )(*[_in_hbm(a) for a in srcs], *[_in_hbm(a) for a in lands], _in_hbm(route), after)
    return (out[0], out[1], out[2:2 + 2 * nw]), out[2 + 2 * nw]


def _exchange_end(handle, after, copies_of, n_copies, name):
    send_sems, recv_sems, thru = handle
    nw = len(thru) // 2

    def body(*refs):
        s_refs, l_refs = refs[:nw], refs[nw:2 * nw]
        send_sems, recv_sems = refs[2 * nw], refs[2 * nw + 1]
        for w in range(nw):
            for k, (src, dst, to) in enumerate(copies_of(w, s_refs[w], l_refs[w])):
                cp = pltpu.make_async_remote_copy(src_ref=src, dst_ref=dst, send_sem=send_sems.at[k * nw + w],
                                                  recv_sem=recv_sems.at[k * nw + w], device_id=to, device_id_type=MESH)
                cp.wait_send()
                cp.wait_recv()

    out = _pallas_call(
        body, name=name, out_shape=tuple(pltpu.HBM(a.shape, a.dtype) for a in thru),
        in_specs=[_HBM] * (2 * nw) + [_SEM, _SEM, _ANY], out_specs=tuple([_HBM] * (2 * nw)),
        input_output_aliases={i: i for i in range(2 * nw)},
        compiler_params=pltpu.CompilerParams(has_side_effects=_DATAFLOW),
    )(*thru, send_sems, recv_sems, after)
    return list(out[:nw]), list(out[nw:])


def _gather_copies(w, src_ref, land_ref):
    x, y, c = _place()
    dst = land_ref.at[4 * x + 2 * y + c]
    return [(src_ref, dst, to) for to in ((x, y, 1 - c), (1 - x, y, c), (x, 1 - y, c), (1 - x, 1 - y, c))]


def _gather_begin(shards, route, after, tag):
    lands = [lax.empty((N_DEV,) + a.shape, a.dtype) for a in shards]
    return _exchange_begin(shards, lands, route, after, _gather_copies, 4, f"gather_begin_{tag}")


def _gather_end(handle, after, tag):
    return _exchange_end(handle, after, _gather_copies, 4, f"gather_end_{tag}")


def _gather_pass_on(shards, lands):
    nw = len(shards)

    def body(*refs):
        x_refs, l_refs = refs[:nw], refs[2 * nw:3 * nw]
        send_sems, recv_sems, local_sems = refs[3 * nw:]
        x, y, c = _place()
        mine = [pltpu.make_async_copy(x_refs[w], l_refs[w].at[4 * x + 2 * y + c], local_sems.at[w]) for w in range(nw)]
        for cp in mine:
            cp.start()
        copies = []
        for w in range(nw):
            for j, (px, py) in enumerate([(1 - x, y), (x, 1 - y), (1 - x, 1 - y)]):
                blk = l_refs[w].at[4 * px + 2 * py + c]
                copies.append(pltpu.make_async_remote_copy(
                    src_ref=blk, dst_ref=blk, send_sem=send_sems.at[j, w], recv_sem=recv_sems.at[j, w],
                    device_id=(x, y, 1 - c), device_id_type=MESH))
        for cp in copies:
            cp.start()
        for cp in copies:
            cp.wait_send()
        for w in range(nw):
            for j, (px, py) in enumerate([(1 - x, y), (x, 1 - y), (1 - x, 1 - y)]):
                blk = l_refs[w].at[4 * px + 2 * py + (1 - c)]
                pltpu.make_async_remote_copy(src_ref=blk, dst_ref=blk, send_sem=send_sems.at[j, w], recv_sem=recv_sems.at[j, w],
                                             device_id=(x, y, 1 - c), device_id_type=MESH).wait_recv()
        for cp in mine:
            cp.wait()

    return _pallas_call(
        body, out_shape=tuple(_sds(a.shape, a.dtype) for a in lands), in_specs=[_ANY] * (2 * nw), out_specs=tuple([_ANY] * nw),
        input_output_aliases={nw + w: w for w in range(nw)},
        scratch_shapes=[pltpu.SemaphoreType.DMA((3, nw)), pltpu.SemaphoreType.DMA((3, nw)), pltpu.SemaphoreType.DMA((nw,))],
        name=f"gather_pass_on_{_tag(shards)}",
    )(*shards, *lands)


def _scatter_copies(w, src_ref, land_ref):
    x, y, c = _place()
    dst = land_ref.at[2 * x + y]
    return [(src_ref.at[2 * px + py], dst, (px, py, c)) for px, py in ((1 - x, y), (x, 1 - y), (1 - x, 1 - y))]


def _scatter_begin(psums, lands, route, after, tag):
    return _exchange_begin(psums, lands, route, after, _scatter_copies, 3, f"scatter_begin_{tag}")


def _scatter_end(handle, after, tag):
    return _exchange_end(handle, after, _scatter_copies, 3, f"scatter_end_{tag}")


def _adamw(parts, w_all, m_all, v_all, l, carried):
    nparts, a, b = parts.shape
    nl = w_all.shape[0]
    ta = next(cc for cc in (1024, 704, 512, 352, 256, 128, 64, 32, 16, 8) if a % cc == 0 and (cc * b * 4 <= 2 ** 20 or cc == 8))
    c1 = 1.0 / (1.0 - ADAM_B1 ** ADAM_STEP)
    c2 = 1.0 / (1.0 - ADAM_B2 ** ADAM_STEP)

    def body(p_ref, w_ref, m_ref, v_ref, *rest):
        g_out, d_out, m_out, v_out = rest[-4:]
        g = p_ref[0].astype(f32)
        for k in range(1, nparts):
            g = g + p_ref[k].astype(f32)
        m_new = ADAM_B1 * m_ref[...] + (1.0 - ADAM_B1) * g
        v_new = ADAM_B2 * v_ref[...] + (1.0 - ADAM_B2) * (g * g)
        m_hat = m_new * c1
        v_hat = v_new * c2
        g_out[...] = g
        d_out[...] = -ADAM_LR * (m_hat / (jnp.sqrt(v_hat) + ADAM_EPS) + ADAM_WD * w_ref[...])
        m_out[...] = m_new
        v_out[...] = v_new

    one = pl.BlockSpec((None, ta, b), lambda i: (l, i, 0))
    keep = [] if carried is None else [pl.BlockSpec(memory_space=pl.ANY)] * 4
    return _pallas_call(
        body, out_shape=tuple(_sds((nl, a, b), f32) for _ in range(4)), grid=(a // ta,),
        in_specs=[pl.BlockSpec((nparts, ta, b), lambda i: (0, i, 0)), one, one, one] + keep, out_specs=(one, one, one, one),
        input_output_aliases=({} if carried is None else {4 + q: q for q in range(4)}),
        compiler_params=_cp(("parallel",)), name=f"adamw_{nparts}x{nl}x{a}x{b}_{l}{'' if carried is None else '_carried'}",
    )(parts, w_all, m_all, v_all, *(carried or ()))


def _to_flat(arrays):
    flat = jnp.concatenate([a.reshape(-1).astype(f32) for a in arrays])
    rows = -(-flat.shape[0] // (8 * LANES)) * 8
    return jnp.pad(flat, (0, rows * LANES - flat.shape[0])).reshape(rows, LANES)


def _from_flat(flat, shapes):
    flat = flat.reshape(-1)
    out, off = [], 0
    for shp in shapes:
        n = 1
        for dd in shp:
            n *= dd
        out.append(flat[off:off + n].reshape(shp))
        off += n
    return out


def kernel(x, mem, g_mix, g_ffn, w_in_a, g_v_a, w_spatial, b_spatial, w_in_b, g_q_b, g_k_b, g_mem, w_mem_kv, g_mq, g_mk, w_out, w_gate_up, w_down, loss_target, m_g_mix, m_g_ffn, m_w_in_a, m_g_v_a, m_w_spatial, m_b_spatial, m_w_in_b, m_g_q_b, m_g_k_b, m_g_mem, m_w_mem_kv, m_g_mq, m_g_mk, m_w_out, m_w_gate_up, m_w_down, v_g_mix, v_g_ffn, v_w_in_a, v_g_v_a, v_w_spatial, v_b_spatial, v_w_in_b, v_g_q_b, v_g_k_b, v_g_mem, v_w_mem_kv, v_g_mq, v_g_mk, v_w_out, v_w_gate_up, v_w_down):
    given = dict(locals())
    depth = g_mix.shape[0]
    s, d = x.shape[1], x.shape[2]
    nm = mem.shape[1]
    t = d - MEM_WIDTH
    ff = w_down.shape[1] * N_DEV
    x0 = x.reshape(s, d)
    mem0 = mem.reshape(nm, d)
    target = loss_target.reshape(s, d)
    tables = _rope_tables(s)

    big_names = ("w_in", "w_mem_kv", "w_out", "w_gate_up", "w_down")

    def stacked_key(name, l):
        if name == "w_in":
            return ("w_in_a" if l % 2 == 0 else "w_in_b"), l // 2
        return name, l

    def layer_shards(l):
        return [given[key][idx].astype(bf16) for key, idx in (stacked_key(name, l) for name in big_names)]

    def gather_finish(handle, after, l):
        shards, lands = _gather_end(handle, after, f"l{l}")
        return _gather_pass_on(shards, lands)

    saved = []
    xc = x0
    handle, _ = _gather_begin(layer_shards(0), g_mix[0].reshape(1, d), mem0, "l0")
    gathered = gather_finish(handle, mem0, 0)
    for l in range(depth):
        is_a = l % 2 == 0
        g_in, g_kv, g_out, g_gu, g_dn = gathered
        w_kv, w_o, w_dn = (g.reshape(-1, g.shape[2]) for g in (g_kv, g_out, g_dn))
        qblk = (N_DEV * g_in.shape[2] - MEM_WIDTH) // MEM_WIDTH

        gm_row, gf_row, gmem_row = g_mix[l].reshape(1, d), g_ffn[l].reshape(1, d), g_mem[l].reshape(1, d)
        gmq_row, gmk_row = g_mq[l].reshape(1, HEAD), g_mk[l].reshape(1, HEAD)
        if l + 1 < depth:
            handle, gm_row = _gather_begin(layer_shards(l + 1), gm_row, g_in, f"l{l + 1}")
        h = _rms_fwd(xc, gm_row)
        z = _mm_cols_fwd(h, g_in, f32)
        if is_a:
            ia = l // 2
            mix = dict(g_v=g_v_a[ia].reshape(1, t), w_s=w_spatial[ia], b_t=b_spatial[ia].T)
            cat = _gmlp_fwd(z, mix["g_v"], mix["w_s"], mix["b_t"])
        else:
            ib = l // 2
            mix = dict(g_q=g_q_b[ib].reshape(1, HEAD), g_k=g_k_b[ib].reshape(1, HEAD))
            q, k, v = _attn_prep_fwd(z, mix["g_q"], mix["g_k"], tables, t)
            cat, lse = _flash_fwd(q, k, v)
            mix.update(q=q, k=k, v=v, lse=lse)
        hm = _rms_fwd(mem0, gmem_row)
        kv = _matmul(hm, w_kv)
        cat = _mem_fwd(z, qblk, kv, gmq_row, gmk_row, cat)
        x1 = _matmul(cat, w_o, res=xc)
        h2 = _rms_fwd(x1, gf_row)
        gu = _mm_cols_fwd(h2, g_gu, bf16)
        act = _swiglu_fwd(gu)
        x2 = _matmul(act, w_dn, res=x1)
        saved.append(dict(x=xc, h=h, z=z, mix=mix, cat=cat, hm=hm, kv=kv, x1=x1, h2=h2, gu=gu, act=act, qblk=qblk,
                          w=(g_in, w_kv, w_o, g_gu, w_dn), rows=(gm_row, gf_row, gmem_row, gmq_row, gmk_row)))
        if l + 1 < depth:
            gathered = gather_finish(handle, x2, l + 1)
        xc = x2

    loss_row, dy, dy_b = _loss_head(xc, target)
    loss = lax.psum(loss_row[0, 0], ("x", "y", "c"))

    small = {n: [None] * given[n].shape[0] for n in ("g_mix", "g_ffn", "g_v_a", "w_spatial", "b_spatial", "g_q_b", "g_k_b", "g_mem", "g_mq", "g_mk")}
    big_out = {}

    def scatter_finish(handle, l, after):
        _, arrived = _scatter_end(handle, after, f"l{l}")
        for name, parts in zip(big_names, arrived):
            key, idx = stacked_key(name, l)
            big_out[key] = _adamw(parts, given[key], given["m_" + key], given["v_" + key], idx, big_out.get(key))

    pending = None
    dx, dx_b = dy, dy_b
    for l in reversed(range(depth)):
        sv = saved[l]
        is_a = l % 2 == 0
        g_in, w_kv, w_o, g_gu, w_dn = sv["w"]
        gm_row, gf_row, gmem_row, gmq_row, gmk_row = sv["rows"]
        mix = sv["mix"]
        dw_dn = _matmul(sv["act"], dx_b, ta=True, out_dtype=bf16)
        dact = _matmul(dx_b, w_dn, tb=True, out_dtype=bf16)
        dgu = _swiglu_bwd(sv["gu"], dact)
        dw_gu = _mm_cols_wgrad(sv["h2"], dgu, g_gu.shape[2])
        dh2 = _mm_cols_dgrad(dgu, g_gu)
        dx1, dx1_b, dgf = _rms_bwd(sv["x1"], gf_row, dh2, dx)
        small["g_ffn"][l] = dgf.reshape(d)
        dw_o = _matmul(sv["cat"], dx1_b, ta=True, out_dtype=bf16)
        dcat = _matmul(dx1_b, w_o, tb=True, out_dtype=bf16)
        if is_a:
            dz, dws, dbt, dgv = _gmlp_bwd(sv["z"], mix["g_v"], mix["w_s"], mix["b_t"], dcat)
            small["w_spatial"][l // 2], small["b_spatial"][l // 2], small["g_v_a"][l // 2] = dws, dbt.T, dgv.reshape(t)
        else:
            dq, dk, dv = _flash_bwd(mix["q"], mix["k"], mix["v"], sv["cat"], dcat, mix["lse"])
            dz, dgq, dgk = _attn_prep_bwd(sv["z"], mix["g_q"], mix["g_k"], tables, dq, dk, dv, t)
            small["g_q_b"][l // 2], small["g_k_b"][l // 2] = dgq.reshape(HEAD), dgk.reshape(HEAD)
        dz, dkn, dvm, dgmq = _mem_bwd(sv["z"], sv["qblk"], sv["kv"], gmq_row, gmk_row, dcat, dz)
        dkv, dgmk = _mem_kv_bwd(sv["kv"], gmk_row, dkn, dvm)
        dw_kv = _matmul(sv["hm"], dkv, ta=True, out_dtype=bf16)
        dhm = _matmul(dkv, w_kv, tb=True)
        small["g_mem"][l] = _rms_bwd(mem0, gmem_row, dhm, None).reshape(d)
        small["g_mq"][l] = dgmq.reshape(HEAD)
        small["g_mk"][l] = dgmk.reshape(HEAD)
        dw_in = _mm_cols_wgrad(sv["h"], dz, g_in.shape[2])
        dh = _mm_cols_dgrad(dz, g_in)
        dx, dx_b, dgm = _rms_bwd(sv["x"], gm_row, dh, dx1)
        small["g_mix"][l] = dgm.reshape(d)

        grads = [dw_in] + [dw.reshape(N_DEV, -1, dw.shape[1]) for dw in (dw_kv, dw_o)] + [dw_gu, dw_dn.reshape(N_DEV, -1, d)]
        got = _swap_with_sibling(grads)
        sums = [_pair_sum(g, r) for g, r in zip(grads, got)]
        if pending is not None:
            scatter_finish(*pending, dx)
        handle, dx_b = _scatter_begin([p for p, _ in sums], [q for _, q in sums], dx_b, dx, f"l{l}")
        pending = (handle, l)
    scatter_finish(*pending, dx)

    small_names = tuple(small)
    small_grads = [jnp.stack(small[n]) for n in small_names]
    small_shapes = [g.shape for g in small_grads]
    (all_parts,) = _all_gather([_to_flat(small_grads)])
    souts = _adamw(all_parts, *[_to_flat([given[p + n] for n in small_names])[None] for p in ("", "m_", "v_")], 0, None)
    small_out = dict(zip(small_names, zip(*[_from_flat(flat, small_shapes) for flat in souts])))

    weights = ("g_mix", "g_ffn", "w_in_a", "g_v_a", "w_spatial", "b_spatial", "w_in_b", "g_q_b", "g_k_b", "g_mem", "w_mem_kv",
               "g_mq", "g_mk", "w_out", "w_gate_up", "w_down")
    results = {n: (small_out[n] if n in small_out else big_out[n]) for n in weights}
    grad_x = dx.reshape(1, s, d)
    return (loss, grad_x, *[results[n][kind] for kind in range(4) for n in weights])
```

```python
import functools

import jax
import jax.numpy as jnp
from jax import lax
from jax.experimental import pallas as pl
from jax.experimental.pallas import tpu as pltpu

f32 = jnp.float32
bf16 = jnp.bfloat16

HEAD = 128
CHUNK = 128
GRID_W = 64
MEM_HEADS = 4
KV_HEADS = 4
MEM_WIDTH = MEM_HEADS * HEAD
KV_WIDTH = KV_HEADS * HEAD
ROPE_THETA = 10000.0
ROPE_PAIRS = HEAD // 4
EPS = 1e-6
SCALE = HEAD ** -0.5
LOG2E = 1.4426950408889634
N_DEV = 8
LANES = 1024
VMEM_LIMIT = 56 * 1024 * 1024

ADAM_LR, ADAM_B1, ADAM_B2, ADAM_EPS, ADAM_WD, ADAM_STEP = 0.001, 0.9, 0.999, 1e-08, 0.01, 10

MESH = pl.DeviceIdType.MESH
_pallas_call = pl.pallas_call


def _pick(dim, cands):
    for c in cands:
        if dim % c == 0:
            return c
    return dim


def _cp(sem):
    return pltpu.CompilerParams(dimension_semantics=sem, vmem_limit_bytes=VMEM_LIMIT)


def _sds(shape, dtype):
    return jax.ShapeDtypeStruct(shape, dtype)


def _dot(a, b, ca, cb):
    return lax.dot_general(a, b, (((ca,), (cb,)), ((), ())), preferred_element_type=f32)


def _gelu(z):
    return 0.5 * z * (1.0 + lax.erf(z * 0.7071067811865476))


def _gelu_grad(z):
    return 0.5 * (1.0 + lax.erf(z * 0.7071067811865476)) + z * jnp.exp(-0.5 * z * z) * 0.3989422804014327


def _rot(x, sin_a, sin_b):
    return pltpu.roll(x, 96, 1) * sin_a + pltpu.roll(x, 32, 1) * sin_b


def _matmul(a, b, *, ta=False, tb=False, out_dtype=f32, res=None, tm=None, tn=None, tk=None):
    assert a.dtype == bf16 and b.dtype == bf16
    kdim, m = a.shape if ta else a.shape[::-1]
    n, k2 = b.shape if tb else b.shape[::-1]
    assert kdim == k2, (a.shape, b.shape, ta, tb)
    tm = tm or _pick(m, (1024, 512, 256, 128))
    tn = tn or _pick(n, (1024, 512, 256, 128))
    if tk is None:
        tk = kdim if kdim <= 2048 else _pick(kdim, (1408, 1024, 512, 256, 128))
    nk = kdim // tk
    ca, cb = (0 if ta else 1), (1 if tb else 0)
    has_res = res is not None

    def body(*refs):
        a_ref, b_ref = refs[0], refs[1]
        r_ref = refs[2] if has_res else None
        o_ref = refs[3] if has_res else refs[2]
        prod = _dot(a_ref[...], b_ref[...], ca, cb)
        if nk == 1:
            if has_res:
                prod = prod + r_ref[...]
            o_ref[...] = prod.astype(o_ref.dtype)
        else:
            acc = refs[-1]
            k = pl.program_id(2)

            @pl.when(k == 0)
            def _():
                acc[...] = prod

            @pl.when(k > 0)
            def _():
                acc[...] += prod

            @pl.when(k == nk - 1)
            def _():
                out = acc[...]
                if has_res:
                    out = out + r_ref[...]
                o_ref[...] = out.astype(o_ref.dtype)

    a_spec = pl.BlockSpec((tk, tm), lambda i, j, k: (k, i)) if ta else pl.BlockSpec((tm, tk), lambda i, j, k: (i, k))
    b_spec = pl.BlockSpec((tn, tk), lambda i, j, k: (j, k)) if tb else pl.BlockSpec((tk, tn), lambda i, j, k: (k, j))
    o_spec = pl.BlockSpec((tm, tn), lambda i, j, k: (i, j))
    in_specs = [a_spec, b_spec] + ([o_spec] if has_res else [])
    args = (a, b) + ((res,) if has_res else ())
    mode = ("t" if ta else "n") + ("t" if tb else "n")
    return _pallas_call(
        body, out_shape=_sds((m, n), out_dtype), grid=(m // tm, n // tn, nk),
        in_specs=in_specs, out_specs=o_spec,
        scratch_shapes=([pltpu.VMEM((tm, tn), f32)] if nk > 1 else []),
        compiler_params=_cp(("parallel", "parallel", "arbitrary")),
        name=f"mm_{mode}_{m}x{kdim}x{n}{'_res' if has_res else ''}_{jnp.dtype(out_dtype).name}",
    )(*args)


def _shards_per_step(n):
    p = 1 if n % 128 == 0 else 2
    assert (p * n) % 128 == 0 and N_DEV % p == 0
    return p


def _lane_pieces(v, p, n):
    return [v] if p == 1 else [v[:, q * n:(q + 1) * n] for q in range(p)]


def _mm_cols_fwd(a, g, out_dtype):
    m, kdim = a.shape
    nd, k2, n = g.shape
    assert kdim == k2 and a.dtype == bf16 and g.dtype == bf16
    p = _shards_per_step(n)
    tm = _pick(m, (1024, 512, 256, 128))

    def body(a_ref, g_ref, o_ref):
        av = a_ref[...]
        parts = [_dot(av, g_ref[q], 1, 0) for q in range(p)]
        out = parts[0] if p == 1 else jnp.concatenate(parts, axis=1)
        o_ref[...] = out.astype(o_ref.dtype)

    return _pallas_call(
        body, out_shape=_sds((m, nd * n), out_dtype), grid=(m // tm, nd // p),
        in_specs=[pl.BlockSpec((tm, kdim), lambda i, j: (i, 0)), pl.BlockSpec((p, kdim, n), lambda i, j: (j, 0, 0))],
        out_specs=pl.BlockSpec((tm, p * n), lambda i, j: (i, j)),
        compiler_params=_cp(("parallel", "arbitrary")), name=f"mm_cols_fwd_{m}x{kdim}x{nd * n}_{jnp.dtype(out_dtype).name}",
    )(a, g)


def _mm_cols_dgrad(dz, g):
    m, nn = dz.shape
    nd, kdim, n = g.shape
    assert nn == nd * n and dz.dtype == bf16 and g.dtype == bf16
    p = _shards_per_step(n)
    nj = nd // p
    tm = _pick(m, (512, 256, 128))

    def body(dz_ref, g_ref, o_ref, acc):
        j = pl.program_id(1)
        tot = None
        for q, piece in enumerate(_lane_pieces(dz_ref[...], p, n)):
            dd = _dot(piece, g_ref[q], 1, 1)
            tot = dd if tot is None else tot + dd

        @pl.when(j == 0)
        def _():
            acc[...] = tot

        @pl.when(j > 0)
        def _():
            acc[...] += tot

        @pl.when(j == nj - 1)
        def _():
            o_ref[...] = acc[...]

    return _pallas_call(
        body, out_shape=_sds((m, kdim), f32), grid=(m // tm, nj),
        in_specs=[pl.BlockSpec((tm, p * n), lambda i, j: (i, j)), pl.BlockSpec((p, kdim, n), lambda i, j: (j, 0, 0))],
        out_specs=pl.BlockSpec((tm, kdim), lambda i, j: (i, 0)),
        scratch_shapes=[pltpu.VMEM((tm, kdim), f32)],
        compiler_params=_cp(("parallel", "arbitrary")), name=f"mm_cols_dgrad_{m}x{nn}x{kdim}",
    )(dz, g)


def _mm_cols_wgrad(a, dz, n):
    s, kdim = a.shape
    nd = dz.shape[1] // n
    assert a.dtype == bf16 and dz.dtype == bf16
    p = _shards_per_step(n)
    tkw = _pick(kdim, (1024, 512, 256, 128))
    ts = _pick(s, (1024, 512, 256, 128))
    ns = s // ts

    def body(a_ref, dz_ref, o_ref, acc):
        si = pl.program_id(2)
        av = a_ref[...]
        prods = [_dot(av, piece, 0, 0) for piece in _lane_pieces(dz_ref[...], p, n)]

        @pl.when(si == 0)
        def _():
            for q in range(p):
                acc[q] = prods[q]

        @pl.when(si > 0)
        def _():
            for q in range(p):
                acc[q] += prods[q]

        @pl.when(si == ns - 1)
        def _():
            o_ref[...] = acc[...].astype(bf16)

    return _pallas_call(
        body, out_shape=_sds((nd, kdim, n), bf16), grid=(kdim // tkw, nd // p, ns),
        in_specs=[pl.BlockSpec((ts, tkw), lambda i, j, k: (k, i)), pl.BlockSpec((ts, p * n), lambda i, j, k: (k, j))],
        out_specs=pl.BlockSpec((p, tkw, n), lambda i, j, k: (j, i, 0)),
        scratch_shapes=[pltpu.VMEM((p, tkw, n), f32)],
        compiler_params=_cp(("parallel", "parallel", "arbitrary")), name=f"mm_cols_wgrad_{kdim}x{s}x{nd * n}",
    )(a, dz)


def _rms_fwd(x, g_row):
    s, d = x.shape
    tr = _pick(s, (512, 256, 128))

    def body(x_ref, g_ref, o_ref):
        xv = x_ref[...]
        r = lax.rsqrt(jnp.mean(xv * xv, axis=-1, keepdims=True) + EPS)
        o_ref[...] = (xv * r * g_ref[...]).astype(bf16)

    return _pallas_call(
        body, out_shape=_sds((s, d), bf16), grid=(s // tr,),
        in_specs=[pl.BlockSpec((tr, d), lambda i: (i, 0)), pl.BlockSpec((1, d), lambda i: (0, 0))],
        out_specs=pl.BlockSpec((tr, d), lambda i: (i, 0)),
        compiler_params=_cp(("parallel",)), name=f"rms_fwd_{s}x{d}",
    )(x, g_row)


def _rms_bwd(x, g_row, dh, dres):
    s, d = x.shape
    tr = _pick(s, (512, 256, 128))
    with_dx = dres is not None

    def body(*refs):
        if with_dx:
            x_ref, g_ref, dh_ref, dres_ref, dx_ref, dxb_ref, dg_ref = refs
        else:
            x_ref, g_ref, dh_ref, dg_ref = refs

        @pl.when(pl.program_id(0) == 0)
        def _():
            dg_ref[...] = jnp.zeros_like(dg_ref)

        xv = x_ref[...]
        r = lax.rsqrt(jnp.mean(xv * xv, axis=-1, keepdims=True) + EPS)
        xh = xv * r
        dy = dh_ref[...].astype(f32)
        dg_ref[...] += jnp.sum(dy * xh, axis=0, keepdims=True)
        if with_dx:
            gy = dy * g_ref[...]
            dx = dres_ref[...] + r * (gy - xh * jnp.mean(gy * xh, axis=-1, keepdims=True))
            dx_ref[...] = dx
            dxb_ref[...] = dx.astype(bf16)

    row = pl.BlockSpec((tr, d), lambda i: (i, 0))
    vec = pl.BlockSpec((1, d), lambda i: (0, 0))
    if with_dx:
        return _pallas_call(
            body, out_shape=(_sds((s, d), f32), _sds((s, d), bf16), _sds((1, d), f32)), grid=(s // tr,),
            in_specs=[row, vec, row, row], out_specs=(row, row, vec),
            compiler_params=_cp(("arbitrary",)), name=f"rms_bwd_{s}x{d}",
        )(x, g_row, dh, dres)
    return _pallas_call(
        body, out_shape=_sds((1, d), f32), grid=(s // tr,),
        in_specs=[row, vec, row], out_specs=vec,
        compiler_params=_cp(("arbitrary",)), name=f"rms_bwd_gain_{s}x{d}",
    )(x, g_row, dh)


def _gmlp_rows(s):
    return CHUNK * (2 if (s // CHUNK) % 2 == 0 else 1)


def _gmlp_fwd(z, g_v, w_s, b_t):
    s = z.shape[0]
    t = g_v.shape[1]
    ng = t // HEAD
    rb = _gmlp_rows(s)

    def body(z_ref, gv_ref, ws_ref, bt_ref, o_ref):
        for ci in range(rb // CHUNK):
            lo = ci * CHUNK
            a = _gelu(z_ref[lo:lo + CHUNK, :])
            u, vv = a[:, :t], a[:, t:]
            r = lax.rsqrt(jnp.mean(vv * vv, axis=-1, keepdims=True) + EPS)
            vn = (vv * r * gv_ref[...]).astype(bf16)
            for g in range(ng):
                cs = slice(g * HEAD, (g + 1) * HEAD)
                sg = _dot(ws_ref[g].astype(bf16), vn[:, cs], 1, 0) + bt_ref[:, g:g + 1]
                o_ref[lo:lo + CHUNK, cs] = (u[:, cs] * sg).astype(bf16)

    return _pallas_call(
        body, out_shape=_sds((s, t + MEM_WIDTH), bf16), grid=(s // rb,),
        in_specs=[pl.BlockSpec((rb, 2 * t), lambda i: (i, 0)), pl.BlockSpec((1, t), lambda i: (0, 0)),
                  pl.BlockSpec((ng, CHUNK, CHUNK), lambda i: (0, 0, 0)), pl.BlockSpec((CHUNK, ng), lambda i: (0, 0))],
        out_specs=pl.BlockSpec((rb, t), lambda i: (i, 0)),
        compiler_params=_cp(("parallel",)), name=f"gmlp_fwd_{s}",
    )(z, g_v, w_s, b_t)


def _gmlp_bwd(z, g_v, w_s, b_t, dtok):
    s = z.shape[0]
    t = g_v.shape[1]
    ng = t // HEAD
    rb = _gmlp_rows(s)
    nsteps = s // rb

    def body(z_ref, gv_ref, ws_ref, bt_ref, dt_ref, dz_ref, dws_ref, dbt_ref, dgv_ref, ds_acc):
        step = pl.program_id(0)

        @pl.when(step == 0)
        def _():
            dws_ref[...] = jnp.zeros_like(dws_ref)
            dgv_ref[...] = jnp.zeros_like(dgv_ref)
            ds_acc[...] = jnp.zeros_like(ds_acc)

        for ci in range(rb // CHUNK):
            lo = ci * CHUNK
            zz = z_ref[lo:lo + CHUNK, :]
            a = _gelu(zz)
            u, vv = a[:, :t], a[:, t:]
            r = lax.rsqrt(jnp.mean(vv * vv, axis=-1, keepdims=True) + EPS)
            vh = vv * r
            vn = (vh * gv_ref[...]).astype(bf16)
            dtok = dt_ref[lo:lo + CHUNK, :].astype(f32)
            ds = dtok * u
            ds_acc[...] += ds
            dsb = ds.astype(bf16)
            du_parts, dvn_parts = [], []
            for g in range(ng):
                cs = slice(g * HEAD, (g + 1) * HEAD)
                wg = ws_ref[g].astype(bf16)
                sg = _dot(wg, vn[:, cs], 1, 0) + bt_ref[:, g:g + 1]
                du_parts.append(dtok[:, cs] * sg)
                dws_ref[g] += _dot(dsb[:, cs], vn[:, cs], 1, 1)
                dvn_parts.append(_dot(wg, dsb[:, cs], 0, 0))
            dvn = jnp.concatenate(dvn_parts, axis=1)
            dgv_ref[...] += jnp.sum(dvn * vh, axis=0, keepdims=True)
            gy = dvn * gv_ref[...]
            dvv = r * (gy - vh * jnp.mean(gy * vh, axis=-1, keepdims=True))
            da = jnp.concatenate(du_parts + [dvv], axis=1)
            dz_ref[lo:lo + CHUNK, :] = (da * _gelu_grad(zz)).astype(bf16)

        @pl.when(step == nsteps - 1)
        def _():
            for g in range(ng):
                dbt_ref[:, g:g + 1] = jnp.sum(ds_acc[:, g * HEAD:(g + 1) * HEAD], axis=1, keepdims=True)

    return _pallas_call(
        body,
        out_shape=(_sds((s, z.shape[1]), bf16), _sds((ng, CHUNK, CHUNK), f32), _sds((CHUNK, ng), f32), _sds((1, t), f32)),
        grid=(nsteps,),
        in_specs=[pl.BlockSpec((rb, 2 * t), lambda i: (i, 0)), pl.BlockSpec((1, t), lambda i: (0, 0)),
                  pl.BlockSpec((ng, CHUNK, CHUNK), lambda i: (0, 0, 0)), pl.BlockSpec((CHUNK, ng), lambda i: (0, 0)),
                  pl.BlockSpec((rb, t), lambda i: (i, 0))],
        out_specs=(pl.BlockSpec((rb, 2 * t), lambda i: (i, 0)), pl.BlockSpec((ng, CHUNK, CHUNK), lambda i: (0, 0, 0)),
                   pl.BlockSpec((CHUNK, ng), lambda i: (0, 0)), pl.BlockSpec((1, t), lambda i: (0, 0))),
        scratch_shapes=[pltpu.VMEM((CHUNK, t), f32)],
        compiler_params=_cp(("arbitrary",)), name=f"gmlp_bwd_{s}",
    )(z, g_v, w_s, b_t, dtok)


def _rope_tables(s):
    n_rows = s // GRID_W
    rows = jnp.broadcast_to(jnp.arange(n_rows)[:, None], (n_rows, GRID_W)).reshape(s)
    cols = jnp.broadcast_to(jnp.arange(GRID_W)[None, :], (n_rows, GRID_W)).reshape(s)
    freqs = ROPE_THETA ** (-jnp.arange(ROPE_PAIRS, dtype=f32) / ROPE_PAIRS)
    ang_r = rows.astype(f32)[:, None] * freqs
    ang_c = cols.astype(f32)[:, None] * freqs
    ang = jnp.concatenate([ang_r, ang_r, ang_c, ang_c], axis=-1)
    cos, sin = jnp.cos(ang), jnp.sin(ang)
    first = (jnp.arange(HEAD) % (2 * ROPE_PAIRS)) < ROPE_PAIRS
    return cos, jnp.where(first, -sin, 0.0), jnp.where(first, 0.0, sin)


def _attn_prep_fwd(z, g_q, g_k, tables, t):
    s = z.shape[0]
    tr = _pick(s, (256, 128))
    nq = t // HEAD
    width = t + 2 * KV_WIDTH

    def body(z_ref, gq_ref, gk_ref, cos_ref, sa_ref, sb_ref, q_ref, k_ref, v_ref):
        cos, sa, sb = cos_ref[...], sa_ref[...], sb_ref[...]
        for h in range(nq + KV_HEADS):
            cs = slice(h * HEAD, (h + 1) * HEAD)
            xv = z_ref[:, cs]
            r = lax.rsqrt(jnp.mean(xv * xv, axis=-1, keepdims=True) + EPS)
            xn = xv * r * (gq_ref[...] if h < nq else gk_ref[...])
            y = xn * cos + _rot(xn, sa, sb)
            if h < nq:
                q_ref[:, cs] = (y * (SCALE * LOG2E)).astype(bf16)
            else:
                k_ref[:, (h - nq) * HEAD:(h - nq + 1) * HEAD] = y.astype(bf16)
        v_ref[...] = z_ref[:, t + KV_WIDTH:width].astype(bf16)

    row = lambda w: pl.BlockSpec((tr, w), lambda i: (i, 0))
    vec = pl.BlockSpec((1, HEAD), lambda i: (0, 0))
    return _pallas_call(
        body, out_shape=(_sds((s, t), bf16), _sds((s, KV_WIDTH), bf16), _sds((s, KV_WIDTH), bf16)), grid=(s // tr,),
        in_specs=[row(width), vec, vec, row(HEAD), row(HEAD), row(HEAD)],
        out_specs=(row(t), row(KV_WIDTH), row(KV_WIDTH)),
        compiler_params=_cp(("parallel",)), name=f"attn_prep_fwd_{s}",
    )(z, g_q, g_k, *tables)


def _attn_prep_bwd(z, g_q, g_k, tables, dq, dk, dv, t):
    s = z.shape[0]
    tr = _pick(s, (256, 128))
    nq = t // HEAD
    width = t + 2 * KV_WIDTH

    def body(z_ref, gq_ref, gk_ref, cos_ref, sa_ref, sb_ref, dq_ref, dk_ref, dv_ref, dz_ref, dgq_ref, dgk_ref):
        @pl.when(pl.program_id(0) == 0)
        def _():
            dgq_ref[...] = jnp.zeros_like(dgq_ref)
            dgk_ref[...] = jnp.zeros_like(dgk_ref)

        cos, sa, sb = cos_ref[...], sa_ref[...], sb_ref[...]
        for h in range(nq + KV_HEADS):
            cs = slice(h * HEAD, (h + 1) * HEAD)
            xv = z_ref[:, cs]
            r = lax.rsqrt(jnp.mean(xv * xv, axis=-1, keepdims=True) + EPS)
            xh = xv * r
            if h < nq:
                dy, g_ref, dg_ref = dq_ref[:, cs], gq_ref, dgq_ref
            else:
                dy, g_ref, dg_ref = dk_ref[:, (h - nq) * HEAD:(h - nq + 1) * HEAD], gk_ref, dgk_ref
            dy = dy.astype(f32)
            dxn = dy * cos - _rot(dy, sa, sb)
            dg_ref[...] += jnp.sum(dxn * xh, axis=0, keepdims=True)
            gy = dxn * g_ref[...]
            dz_ref[:, cs] = (r * (gy - xh * jnp.mean(gy * xh, axis=-1, keepdims=True))).astype(bf16)
        dz_ref[:, t + KV_WIDTH:width] = dv_ref[...].astype(bf16)

    row = lambda w: pl.BlockSpec((tr, w), lambda i: (i, 0))
    vec = pl.BlockSpec((1, HEAD), lambda i: (0, 0))
    return _pallas_call(
        body, out_shape=(_sds((s, z.shape[1]), bf16), _sds((1, HEAD), f32), _sds((1, HEAD), f32)), grid=(s // tr,),
        in_specs=[row(width), vec, vec, row(HEAD), row(HEAD), row(HEAD), row(t), row(KV_WIDTH), row(KV_WIDTH)],
        out_specs=(row(width), vec, vec),
        compiler_params=_cp(("arbitrary",)), name=f"attn_prep_bwd_{s}",
    )(z, g_q, g_k, *tables, dq, dk, dv)


def _flash_tiles(s):
    return _pick(s, (512, 256, 128)), _pick(s, (512, 256, 128))


def _stack_heads(ref, grp):
    return jnp.concatenate([ref[:, g * HEAD:(g + 1) * HEAD] for g in range(grp)], axis=0)


def _flash_fwd(q, k, v):
    s, t = q.shape
    grp = t // KV_WIDTH
    tq, tk = _flash_tiles(s)
    nkv = s // tk
    rows = grp * tq

    def body(q_ref, k_ref, v_ref, o_ref, lse_ref, m_sc, acc_sc):
        ki = pl.program_id(2)

        @pl.when(ki == 0)
        def _():
            m_sc[...] = jnp.full(m_sc.shape, -jnp.inf, f32)
            acc_sc[...] = jnp.zeros_like(acc_sc)

        qs = _stack_heads(q_ref, grp)
        sc = _dot(qs, k_ref[...], 1, 1)
        m_prev = m_sc[...]
        m_new = jnp.maximum(m_prev, jnp.max(sc, axis=-1, keepdims=True))
        alpha = jnp.exp2(m_prev - m_new)
        p = jnp.exp2((sc - m_new).astype(bf16))
        v1 = jnp.concatenate([v_ref[...], jnp.ones((tk, HEAD), bf16)], axis=1)
        acc_sc[...] = alpha * acc_sc[...] + _dot(p, v1, 1, 0)
        m_sc[...] = m_new

        @pl.when(ki == nkv - 1)
        def _():
            acc = acc_sc[...]
            l = acc[:, HEAD:HEAD + 1]
            o = acc[:, :HEAD] / l
            for g in range(grp):
                o_ref[:, g * HEAD:(g + 1) * HEAD] = o[g * tq:(g + 1) * tq].astype(bf16)
            lse_ref[0] = jnp.broadcast_to(m_sc[...] + jnp.log(l) * LOG2E, (rows, HEAD))

    return _pallas_call(
        body, out_shape=(_sds((s, t + MEM_WIDTH), bf16), _sds((KV_HEADS, grp * s, HEAD), f32)), grid=(KV_HEADS, s // tq, nkv),
        in_specs=[pl.BlockSpec((tq, grp * HEAD), lambda h, i, j: (i, h)), pl.BlockSpec((tk, HEAD), lambda h, i, j: (j, h)),
                  pl.BlockSpec((tk, HEAD), lambda h, i, j: (j, h))],
        out_specs=(pl.BlockSpec((tq, grp * HEAD), lambda h, i, j: (i, h)), pl.BlockSpec((1, rows, HEAD), lambda h, i, j: (h, i, 0))),
        scratch_shapes=[pltpu.VMEM((rows, 1), f32), pltpu.VMEM((rows, 2 * HEAD), f32)],
        compiler_params=_cp(("parallel", "parallel", "arbitrary")), name=f"flash_fwd_{s}",
    )(q, k, v)


def _flash_delta(o, do, t):
    s = o.shape[0]
    grp = t // KV_WIDTH
    tq, _ = _flash_tiles(s)
    rows = grp * tq

    def body(o_ref, do_ref, d_ref):
        for g in range(grp):
            cs = slice(g * HEAD, (g + 1) * HEAD)
            dd = jnp.sum(o_ref[:, cs].astype(f32) * do_ref[:, cs].astype(f32), axis=-1, keepdims=True)
            d_ref[0, g * tq:(g + 1) * tq, :] = jnp.broadcast_to(dd, (tq, HEAD))

    qb = pl.BlockSpec((tq, grp * HEAD), lambda h, i: (i, h))
    return _pallas_call(
        body, out_shape=_sds((KV_HEADS, grp * s, HEAD), f32), grid=(KV_HEADS, s // tq),
        in_specs=[qb, qb], out_specs=pl.BlockSpec((1, rows, HEAD), lambda h, i: (h, i, 0)),
        compiler_params=_cp(("parallel", "parallel")), name=f"flash_delta_{s}",
    )(o, do)


def _flash_probs(q_ref, k_ref, v_ref, do_ref, lse_ref, delta_ref, grp):
    qs = _stack_heads(q_ref, grp)
    dos = _stack_heads(do_ref, grp)
    sc = _dot(qs, k_ref[...], 1, 1)
    p = jnp.exp2((sc - lse_ref[0, :, 0:1]).astype(bf16))
    dp = _dot(dos, v_ref[...], 1, 1)
    ds = p * (dp - delta_ref[0, :, 0:1]).astype(bf16)
    return qs, dos, p, ds


def _flash_bwd(q, k, v, o, do, lse):
    s, t = q.shape
    grp = t // KV_WIDTH
    tq, tk = _flash_tiles(s)
    nq, nkv = s // tq, s // tk
    rows = grp * tq
    delta = _flash_delta(o, do, t)

    def dkv_body(q_ref, k_ref, v_ref, do_ref, lse_ref, delta_ref, dk_ref, dv_ref, dk_acc, dv_acc):
        qi = pl.program_id(2)

        @pl.when(qi == 0)
        def _():
            dk_acc[...] = jnp.zeros_like(dk_acc)
            dv_acc[...] = jnp.zeros_like(dv_acc)

        qs, dos, p, ds = _flash_probs(q_ref, k_ref, v_ref, do_ref, lse_ref, delta_ref, grp)
        dv_acc[...] += _dot(p, dos, 0, 0)
        dk_acc[...] += _dot(ds, qs, 0, 0)

        @pl.when(qi == nq - 1)
        def _():
            dk_ref[...] = dk_acc[...] * (1.0 / LOG2E)
            dv_ref[...] = dv_acc[...]

    def dq_body(q_ref, k_ref, v_ref, do_ref, lse_ref, delta_ref, dq_ref, dq_acc):
        kj = pl.program_id(2)

        @pl.when(kj == 0)
        def _():
            dq_acc[...] = jnp.zeros_like(dq_acc)

        _, _, _, ds = _flash_probs(q_ref, k_ref, v_ref, do_ref, lse_ref, delta_ref, grp)
        dq_acc[...] += _dot(ds, k_ref[...], 1, 0)

        @pl.when(kj == nkv - 1)
        def _():
            for g in range(grp):
                dq_ref[:, g * HEAD:(g + 1) * HEAD] = dq_acc[g * tq:(g + 1) * tq] * SCALE

    def specs(qmap, kmap):
        qb = pl.BlockSpec((tq, grp * HEAD), lambda h, a, b: (qmap(a, b), h))
        kb = pl.BlockSpec((tk, HEAD), lambda h, a, b: (kmap(a, b), h))
        lb = pl.BlockSpec((1, rows, HEAD), lambda h, a, b: (h, qmap(a, b), 0))
        return qb, kb, lb

    qb, kb, lb = specs(lambda a, b: b, lambda a, b: a)
    dk, dv = _pallas_call(
        dkv_body, out_shape=(_sds((s, KV_WIDTH), f32), _sds((s, KV_WIDTH), f32)), grid=(KV_HEADS, nkv, nq),
        in_specs=[qb, kb, kb, qb, lb, lb], out_specs=(kb, kb),
        scratch_shapes=[pltpu.VMEM((tk, HEAD), f32), pltpu.VMEM((tk, HEAD), f32)],
        compiler_params=_cp(("parallel", "parallel", "arbitrary")), name=f"flash_bwd_dkv_{s}",
    )(q, k, v, do, lse, delta)
    qb, kb, lb = specs(lambda a, b: a, lambda a, b: b)
    dq = _pallas_call(
        dq_body, out_shape=_sds((s, t), f32), grid=(KV_HEADS, nq, nkv),
        in_specs=[qb, kb, kb, qb, lb, lb], out_specs=qb,
        scratch_shapes=[pltpu.VMEM((rows, HEAD), f32)],
        compiler_params=_cp(("parallel", "parallel", "arbitrary")), name=f"flash_bwd_dq_{s}",
    )(q, k, v, do, lse, delta)
    return dq, dk, dv


def _mem_heads(z_ref, kv_ref, gq_ref, gk_ref, h):
    cs = slice(h * HEAD, (h + 1) * HEAD)
    xv = z_ref[:, cs]
    r = lax.rsqrt(jnp.mean(xv * xv, axis=-1, keepdims=True) + EPS)
    xh = xv * r
    kx = kv_ref[:, cs]
    rk = lax.rsqrt(jnp.mean(kx * kx, axis=-1, keepdims=True) + EPS)
    kn = (kx * rk * gk_ref[...]).astype(bf16)
    vv = kv_ref[:, MEM_WIDTH + h * HEAD:MEM_WIDTH + (h + 1) * HEAD].astype(bf16)
    qn = (xh * gq_ref[...]).astype(bf16)
    sc = _dot(qn, kn, 1, 1) * SCALE
    e = jnp.exp(sc - jnp.max(sc, axis=-1, keepdims=True))
    p = e / jnp.sum(e, axis=-1, keepdims=True)
    return cs, r, xh, qn, kn, vv, p


def _mem_fwd(z, qblk, kv, g_mq, g_mk, cat):
    s = z.shape[0]
    nm = kv.shape[0]
    tr = _pick(s, (512, 256, 128))
    oblk = cat.shape[1] // MEM_WIDTH - 1

    def body(z_ref, kv_ref, gq_ref, gk_ref, cat_ref, o_ref):
        for h in range(MEM_HEADS):
            cs, _, _, _, _, vv, p = _mem_heads(z_ref, kv_ref, gq_ref, gk_ref, h)
            o_ref[:, cs] = _dot(p.astype(bf16), vv, 1, 0).astype(bf16)

    vec = pl.BlockSpec((1, HEAD), lambda i: (0, 0))
    return _pallas_call(
        body, out_shape=_sds(cat.shape, bf16), grid=(s // tr,),
        in_specs=[pl.BlockSpec((tr, MEM_WIDTH), lambda i: (i, qblk)), pl.BlockSpec((nm, 2 * MEM_WIDTH), lambda i: (0, 0)), vec, vec,
                  pl.BlockSpec(memory_space=pl.ANY)],
        out_specs=pl.BlockSpec((tr, MEM_WIDTH), lambda i: (i, oblk)),
        input_output_aliases={4: 0},
        compiler_params=_cp(("parallel",)), name=f"mem_fwd_{s}_{qblk}",
    )(z, kv, g_mq, g_mk, cat)


def _mem_bwd(z, qblk, kv, g_mq, g_mk, dcat, dz):
    s = z.shape[0]
    nm = kv.shape[0]
    tr = _pick(s, (512, 256, 128))
    dblk = dcat.shape[1] // MEM_WIDTH - 1

    def body(z_ref, kv_ref, gq_ref, gk_ref, dm_ref, dzin_ref, dz_ref, dkn_ref, dv_ref, dgq_ref):
        @pl.when(pl.program_id(0) == 0)
        def _():
            dkn_ref[...] = jnp.zeros_like(dkn_ref)
            dv_ref[...] = jnp.zeros_like(dv_ref)
            dgq_ref[...] = jnp.zeros_like(dgq_ref)

        for h in range(MEM_HEADS):
            cs, r, xh, qn, kn, vv, p = _mem_heads(z_ref, kv_ref, gq_ref, gk_ref, h)
            dm = dm_ref[:, cs]
            dv_ref[:, cs] += _dot(p.astype(bf16), dm, 0, 0)
            dp = _dot(dm, vv, 1, 1)
            ds = (p * (dp - jnp.sum(dp * p, axis=-1, keepdims=True)) * SCALE).astype(bf16)
            dqn = _dot(ds, kn, 1, 0)
            dkn_ref[:, cs] += _dot(ds, qn, 0, 0)
            dgq_ref[...] += jnp.sum(dqn * xh, axis=0, keepdims=True)
            gy = dqn * gq_ref[...]
            dz_ref[:, cs] = (r * (gy - xh * jnp.mean(gy * xh, axis=-1, keepdims=True))).astype(bf16)

    vec = pl.BlockSpec((1, HEAD), lambda i: (0, 0))
    acc = pl.BlockSpec((nm, MEM_WIDTH), lambda i: (0, 0))
    return _pallas_call(
        body, out_shape=(_sds(dz.shape, bf16), _sds((nm, MEM_WIDTH), f32), _sds((nm, MEM_WIDTH), f32), _sds((1, HEAD), f32)),
        grid=(s // tr,),
        in_specs=[pl.BlockSpec((tr, MEM_WIDTH), lambda i: (i, qblk)), pl.BlockSpec((nm, 2 * MEM_WIDTH), lambda i: (0, 0)), vec, vec,
                  pl.BlockSpec((tr, MEM_WIDTH), lambda i: (i, dblk)), pl.BlockSpec(memory_space=pl.ANY)],
        out_specs=(pl.BlockSpec((tr, MEM_WIDTH), lambda i: (i, qblk)), acc, acc, vec),
        input_output_aliases={5: 0},
        compiler_params=_cp(("arbitrary",)), name=f"mem_bwd_{s}_{qblk}",
    )(z, kv, g_mq, g_mk, dcat, dz)


def _mem_kv_bwd(kv, g_mk, dkn, dv):
    nm = kv.shape[0]

    def body(kv_ref, gk_ref, dkn_ref, dv_ref, dkv_ref, dgk_ref):
        dgk = jnp.zeros((1, HEAD), f32)
        for h in range(MEM_HEADS):
            cs = slice(h * HEAD, (h + 1) * HEAD)
            kx = kv_ref[:, cs]
            rk = lax.rsqrt(jnp.mean(kx * kx, axis=-1, keepdims=True) + EPS)
            kh = kx * rk
            dkn_h = dkn_ref[:, cs]
            dgk = dgk + jnp.sum(dkn_h * kh, axis=0, keepdims=True)
            gy = dkn_h * gk_ref[...]
            dkv_ref[:, cs] = (rk * (gy - kh * jnp.mean(gy * kh, axis=-1, keepdims=True))).astype(bf16)
        dkv_ref[:, MEM_WIDTH:] = dv_ref[...].astype(bf16)
        dgk_ref[...] = dgk

    return _pallas_call(
        body, out_shape=(_sds((nm, 2 * MEM_WIDTH), bf16), _sds((1, HEAD), f32)),
        compiler_params=pltpu.CompilerParams(vmem_limit_bytes=VMEM_LIMIT), name=f"mem_kv_bwd_{nm}",
    )(kv, g_mk, dkn, dv)


def _ffn_tiles(s, ff):
    return _pick(s, (512, 256, 128)), _pick(ff, (1408, 1024, 512, 256, 128))


def _swiglu_fwd(gu):
    s, ff2 = gu.shape
    ff = ff2 // 2
    tr, tf = _ffn_tiles(s, ff)
    nf = ff // tf

    def body(g_ref, u_ref, o_ref):
        g = g_ref[...].astype(f32)
        o_ref[...] = (g * jax.nn.sigmoid(g) * u_ref[...].astype(f32)).astype(bf16)

    return _pallas_call(
        body, out_shape=_sds((s, ff), bf16), grid=(s // tr, nf),
        in_specs=[pl.BlockSpec((tr, tf), lambda i, j: (i, j)), pl.BlockSpec((tr, tf), lambda i, j: (i, j + nf))],
        out_specs=pl.BlockSpec((tr, tf), lambda i, j: (i, j)),
        compiler_params=_cp(("parallel", "parallel")), name=f"swiglu_fwd_{s}x{ff}",
    )(gu, gu)


def _swiglu_bwd(gu, dact):
    s, ff2 = gu.shape
    ff = ff2 // 2
    tr, tf = _ffn_tiles(s, ff)
    nf = ff // tf

    def body(g_ref, u_ref, da_ref, o_ref):
        g = g_ref[...].astype(f32)
        da = da_ref[...].astype(f32)
        sg = jax.nn.sigmoid(g)

        @pl.when(pl.program_id(1) < nf)
        def _():
            o_ref[...] = (da * u_ref[...].astype(f32) * sg * (1.0 + g * (1.0 - sg))).astype(bf16)

        @pl.when(pl.program_id(1) >= nf)
        def _():
            o_ref[...] = (da * g * sg).astype(bf16)

    return _pallas_call(
        body, out_shape=_sds((s, ff2), bf16), grid=(s // tr, 2 * nf),
        in_specs=[pl.BlockSpec((tr, tf), lambda i, j: (i, j % nf)), pl.BlockSpec((tr, tf), lambda i, j: (i, j % nf + nf)),
                  pl.BlockSpec((tr, tf), lambda i, j: (i, j % nf))],
        out_specs=pl.BlockSpec((tr, tf), lambda i, j: (i, j)),
        compiler_params=_cp(("parallel", "parallel")), name=f"swiglu_bwd_{s}x{ff}",
    )(gu, gu, dact)


def _loss_head(y, target):
    s, d = y.shape
    tr = _pick(s, (512, 256, 128))

    def body(y_ref, t_ref, l_ref, dy_ref, dyb_ref):
        @pl.when(pl.program_id(0) == 0)
        def _():
            l_ref[...] = jnp.zeros_like(l_ref)

        err = y_ref[...] - t_ref[...]
        l_ref[...] += 0.5 * jnp.sum(jnp.mean(err * err, axis=-1, keepdims=True), axis=0, keepdims=True)
        dy = err * (1.0 / d)
        dy_ref[...] = dy
        dyb_ref[...] = dy.astype(bf16)

    row = pl.BlockSpec((tr, d), lambda i: (i, 0))
    return _pallas_call(
        body, out_shape=(_sds((1, HEAD), f32), _sds((s, d), f32), _sds((s, d), bf16)), grid=(s // tr,),
        in_specs=[row, row], out_specs=(pl.BlockSpec((1, HEAD), lambda i: (0, 0)), row, row),
        compiler_params=_cp(("arbitrary",)), name=f"loss_{s}x{d}",
    )(y, target)


def _place():
    return lax.axis_index("x"), lax.axis_index("y"), lax.axis_index("c")


def _tag(arrays):
    return "_".join("x".join(str(dd) for dd in a.shape) for a in arrays)


def _all_gather(shards):
    nw = len(shards)
    hbm = pl.BlockSpec(memory_space=pl.ANY)

    def body(*refs):
        x_refs, out_refs = refs[:nw], refs[nw:2 * nw]
        send_sems, recv_sems, local_sems = refs[2 * nw:]
        x, y, c = _place()
        me, sibling = (x, y, c), (x, y, 1 - c)
        chips = [(1 - x, y), (x, 1 - y), (1 - x, 1 - y)]

        def slot(w, place):
            px, py, pc = place
            return out_refs[w].at[4 * px + 2 * py + pc]

        def copy(k, w, block_of, to, from_input=False):
            return pltpu.make_async_remote_copy(
                src_ref=x_refs[w] if from_input else slot(w, block_of), dst_ref=slot(w, block_of),
                send_sem=send_sems.at[k, w], recv_sem=recv_sems.at[k, w], device_id=to, device_id_type=MESH)

        mine = [pltpu.make_async_copy(x_refs[w], slot(w, me), local_sems.at[w]) for w in range(nw)]
        for cp in mine:
            cp.start()
        first = []
        for w in range(nw):
            first.append(copy(0, w, me, sibling, from_input=True))
            first += [copy(1 + j, w, me, (*chip, c), from_input=True) for j, chip in enumerate(chips)]
        for cp in first:
            cp.start()
        passed = []
        for w in range(nw):
            for j, chip in enumerate(chips):
                copy(1 + j, w, (*chip, c), me).wait_recv()
                fwd = copy(4 + j, w, (*chip, c), sibling)
                fwd.start()
                passed.append(fwd)
        for w in range(nw):
            copy(0, w, sibling, me).wait_recv()
            for j, chip in enumerate(chips):
                copy(4 + j, w, (*chip, 1 - c), me).wait_recv()
        for cp in first + passed:
            cp.wait_send()
        for cp in mine:
            cp.wait()

    return _pallas_call(
        body, out_shape=tuple(_sds((N_DEV,) + a.shape, a.dtype) for a in shards), in_specs=[hbm] * nw, out_specs=tuple([hbm] * nw),
        scratch_shapes=[pltpu.SemaphoreType.DMA((7, nw)), pltpu.SemaphoreType.DMA((7, nw)), pltpu.SemaphoreType.DMA((nw,))],
        name=f"all_gather_{_tag(shards)}_{jnp.dtype(shards[0].dtype).name}",
    )(*shards)


def _swap_with_sibling(grads):
    nw = len(grads)
    nchip = N_DEV // 2
    hbm = pl.BlockSpec(memory_space=pl.ANY)

    def body(*refs):
        g_refs, got_refs = refs[:nw], refs[nw:2 * nw]
        send_sems, recv_sems = refs[2 * nw:]
        x, y, c = _place()
        copies = [pltpu.make_async_remote_copy(
            src_ref=g_refs[w].at[2 * k + (1 - c)], dst_ref=got_refs[w].at[k],
            send_sem=send_sems.at[w, k], recv_sem=recv_sems.at[w, k], device_id=(x, y, 1 - c), device_id_type=MESH)
            for w in range(nw) for k in range(nchip)]
        for cp in copies:
            cp.start()
        for cp in copies:
            cp.wait()

    return _pallas_call(
        body, out_shape=tuple(_sds((nchip,) + g.shape[1:], g.dtype) for g in grads), in_specs=[hbm] * nw, out_specs=tuple([hbm] * nw),
        scratch_shapes=[pltpu.SemaphoreType.DMA((nw, nchip)), pltpu.SemaphoreType.DMA((nw, nchip))],
        name=f"swap_sibling_{_tag(grads)}",
    )(*grads)


def _pair_sum(grad, got):
    nd, a, b = grad.shape
    nchip = nd // 2
    ta = _pick(a, (1024, 704, 512, 352, 256, 128, 64, 32, 16))

    def my_chip():
        return 2 * lax.axis_index("x") + lax.axis_index("y")

    def body(a_ref, b_ref, o_ref, land_ref):
        tot = (a_ref[...].astype(f32) + b_ref[...].astype(f32)).astype(o_ref.dtype)
        o_ref[...] = tot

        @pl.when(pl.program_id(1) == my_chip())
        def _():
            land_ref[...] = tot

    return _pallas_call(
        body, out_shape=(_sds(got.shape, grad.dtype), _sds(got.shape, grad.dtype)), grid=(a // ta, nchip),
        in_specs=[pl.BlockSpec((None, ta, b), lambda i, k: (2 * k + lax.axis_index("c"), i, 0)),
                  pl.BlockSpec((None, ta, b), lambda i, k: (k, i, 0))],
        out_specs=(pl.BlockSpec((None, ta, b), lambda i, k: (k, i, 0)),
                   pl.BlockSpec((None, ta, b), lambda i, k: (my_chip(), i, 0))),
        compiler_params=_cp(("parallel", "arbitrary")), name=f"pair_sum_{a}x{b}",
    )(grad, got)


_HBM = pl.BlockSpec(memory_space=pltpu.HBM)
_SEM = pl.BlockSpec(memory_space=pltpu.SEMAPHORE)
_ANY = pl.BlockSpec(memory_space=pl.ANY)
_DATAFLOW = pltpu.SideEffectType.DATAFLOW_SIDE_EFFECTING


def _in_hbm(a):
    return pltpu.with_memory_space_constraint(a, pltpu.HBM)


def _exchange_begin(bufs, nw, route, after, copies_of, n_copies, name):
    nb = len(bufs)

    def body(*refs):
        send_sems, recv_sems = refs[nb + 2], refs[nb + 3]
        for w in range(nw):
            for k, (src, dst, to) in enumerate(copies_of(w, refs[:nb])):
                pltpu.make_async_remote_copy(src_ref=src, dst_ref=dst, send_sem=send_sems.at[k * nw + w],
                                             recv_sem=recv_sems.at[k * nw + w], device_id=to, device_id_type=MESH).start()

    out = _pallas_call(
        body, name=name,
        out_shape=(pltpu.SemaphoreType.DMA((n_copies * nw,)), pltpu.SemaphoreType.DMA((n_copies * nw,)),
                   *[pltpu.HBM(a.shape, a.dtype) for a in bufs], pltpu.HBM(route.shape, route.dtype)),
        in_specs=[_HBM] * (nb + 1) + [_ANY], out_specs=(_SEM, _SEM, *[_HBM] * (nb + 1)),
        input_output_aliases={i: 2 + i for i in range(nb + 1)},
        compiler_params=pltpu.CompilerParams(has_side_effects=_DATAFLOW),
    )(*[_in_hbm(a) for a in bufs], _in_hbm(route), after)
    return (out[0], out[1], out[2:2 + nb], nw), out[2 + nb]


def _exchange_end(handle, after, copies_of, n_copies, name):
    send_sems, recv_sems, thru, nw = handle
    nb = len(thru)

    def body(*refs):
        send_sems, recv_sems = refs[nb], refs[nb + 1]
        for w in range(nw):
            for k, (src, dst, to) in enumerate(copies_of(w, refs[:nb])):
                cp = pltpu.make_async_remote_copy(src_ref=src, dst_ref=dst, send_sem=send_sems.at[k * nw + w],
                                                  recv_sem=recv_sems.at[k * nw + w], device_id=to, device_id_type=MESH)
                cp.wait_send()
                cp.wait_recv()

    out = _pallas_call(
        body, name=name, out_shape=tuple(pltpu.HBM(a.shape, a.dtype) for a in thru),
        in_specs=[_HBM] * nb + [_SEM, _SEM, _ANY], out_specs=tuple([_HBM] * nb),
        input_output_aliases={i: i for i in range(nb)},
        compiler_params=pltpu.CompilerParams(has_side_effects=_DATAFLOW),
    )(*thru, send_sems, recv_sems, after)
    return list(out)


def _my_slot():
    return 4 * lax.axis_index("x") + 2 * lax.axis_index("y") + lax.axis_index("c")


def _shard_into_land(w_all, idx):
    _, a, b = w_all.shape
    ta = next(cc for cc in (1024, 704, 512, 352, 256, 128, 64, 32, 16) if a % cc == 0 and (cc * b * 4 <= 2 ** 21 or cc == 16))

    def body(w_ref, o_ref):
        o_ref[...] = w_ref[...].astype(bf16)

    return _pallas_call(
        body, out_shape=_sds((N_DEV, a, b), bf16), grid=(a // ta,),
        in_specs=[pl.BlockSpec((None, ta, b), lambda i: (idx, i, 0))],
        out_specs=pl.BlockSpec((None, ta, b), lambda i: (_my_slot(), i, 0)),
        compiler_params=_cp(("parallel",)), name=f"shard_into_land_{a}x{b}_{idx}",
    )(w_all)


def _gather_copies(w, land_refs):
    x, y, c = _place()
    blk = land_refs[w].at[4 * x + 2 * y + c]
    return [(blk, blk, to) for to in ((x, y, 1 - c), (1 - x, y, c), (x, 1 - y, c), (1 - x, 1 - y, c))]


def _gather_begin(lands, route, after, tag):
    return _exchange_begin(lands, len(lands), route, after, _gather_copies, 4, f"gather_begin_{tag}")


def _gather_end(handle, after, tag):
    return _exchange_end(handle, after, _gather_copies, 4, f"gather_end_{tag}")


def _gather_pass_on(lands):
    nw = len(lands)

    def body(*refs):
        l_refs = refs[nw:2 * nw]
        send_sems, recv_sems = refs[2 * nw:]
        x, y, c = _place()
        copies = []
        for w in range(nw):
            for j, (px, py) in enumerate([(1 - x, y), (x, 1 - y), (1 - x, 1 - y)]):
                blk = l_refs[w].at[4 * px + 2 * py + c]
                copies.append(pltpu.make_async_remote_copy(
                    src_ref=blk, dst_ref=blk, send_sem=send_sems.at[j, w], recv_sem=recv_sems.at[j, w],
                    device_id=(x, y, 1 - c), device_id_type=MESH))
        for cp in copies:
            cp.start()
        for cp in copies:
            cp.wait_send()
        for w in range(nw):
            for j, (px, py) in enumerate([(1 - x, y), (x, 1 - y), (1 - x, 1 - y)]):
                blk = l_refs[w].at[4 * px + 2 * py + (1 - c)]
                pltpu.make_async_remote_copy(src_ref=blk, dst_ref=blk, send_sem=send_sems.at[j, w], recv_sem=recv_sems.at[j, w],
                                             device_id=(x, y, 1 - c), device_id_type=MESH).wait_recv()

    return _pallas_call(
        body, out_shape=tuple(_sds(a.shape, a.dtype) for a in lands), in_specs=[_ANY] * nw, out_specs=tuple([_ANY] * nw),
        input_output_aliases={w: w for w in range(nw)},
        scratch_shapes=[pltpu.SemaphoreType.DMA((3, nw)), pltpu.SemaphoreType.DMA((3, nw))],
        name=f"gather_pass_on_{_tag(lands)}",
    )(*lands)


def _scatter_copies(w, refs):
    x, y, c = _place()
    nw = len(refs) // 2
    dst = refs[nw + w].at[2 * x + y]
    return [(refs[w].at[2 * px + py], dst, (px, py, c)) for px, py in ((1 - x, y), (x, 1 - y), (1 - x, 1 - y))]


def _scatter_begin(psums, lands, route, after, tag):
    return _exchange_begin(list(psums) + list(lands), len(psums), route, after, _scatter_copies, 3, f"scatter_begin_{tag}")


def _scatter_end(handle, after, tag):
    return _exchange_end(handle, after, _scatter_copies, 3, f"scatter_end_{tag}")


def _adamw(parts, w_all, m_all, v_all, l, carried):
    nparts, a, b = parts.shape
    nl = w_all.shape[0]
    ta = next(cc for cc in (1024, 704, 512, 352, 256, 128, 64, 32, 16, 8) if a % cc == 0 and (cc * b * 4 <= 2 ** 20 or cc == 8))
    c1 = 1.0 / (1.0 - ADAM_B1 ** ADAM_STEP)
    c2 = 1.0 / (1.0 - ADAM_B2 ** ADAM_STEP)

    def body(p_ref, w_ref, m_ref, v_ref, *rest):
        g_out, d_out, m_out, v_out = rest[-4:]
        g = p_ref[0].astype(f32)
        for k in range(1, nparts):
            g = g + p_ref[k].astype(f32)
        m_new = ADAM_B1 * m_ref[...] + (1.0 - ADAM_B1) * g
        v_new = ADAM_B2 * v_ref[...] + (1.0 - ADAM_B2) * (g * g)
        m_hat = m_new * c1
        v_hat = v_new * c2
        g_out[...] = g
        d_out[...] = -ADAM_LR * (m_hat / (jnp.sqrt(v_hat) + ADAM_EPS) + ADAM_WD * w_ref[...])
        m_out[...] = m_new
        v_out[...] = v_new

    one = pl.BlockSpec((None, ta, b), lambda i: (l, i, 0))
    keep = [] if carried is None else [pl.BlockSpec(memory_space=pl.ANY)] * 4
    return _pallas_call(
        body, out_shape=tuple(_sds((nl, a, b), f32) for _ in range(4)), grid=(a // ta,),
        in_specs=[pl.BlockSpec((nparts, ta, b), lambda i: (0, i, 0)), one, one, one] + keep, out_specs=(one, one, one, one),
        input_output_aliases=({} if carried is None else {4 + q: q for q in range(4)}),
        compiler_params=_cp(("parallel",)), name=f"adamw_{nparts}x{nl}x{a}x{b}_{l}{'' if carried is None else '_carried'}",
    )(parts, w_all, m_all, v_all, *(carried or ()))


def _to_flat(arrays):
    flat = jnp.concatenate([a.reshape(-1).astype(f32) for a in arrays])
    rows = -(-flat.shape[0] // (8 * LANES)) * 8
    return jnp.pad(flat, (0, rows * LANES - flat.shape[0])).reshape(rows, LANES)


def _from_flat(flat, shapes):
    flat = flat.reshape(-1)
    out, off = [], 0
    for shp in shapes:
        n = 1
        for dd in shp:
            n *= dd
        out.append(flat[off:off + n].reshape(shp))
        off += n
    return out


def kernel(x, mem, g_mix, g_ffn, w_in_a, g_v_a, w_spatial, b_spatial, w_in_b, g_q_b, g_k_b, g_mem, w_mem_kv, g_mq, g_mk, w_out, w_gate_up, w_down, loss_target, m_g_mix, m_g_ffn, m_w_in_a, m_g_v_a, m_w_spatial, m_b_spatial, m_w_in_b, m_g_q_b, m_g_k_b, m_g_mem, m_w_mem_kv, m_g_mq, m_g_mk, m_w_out, m_w_gate_up, m_w_down, v_g_mix, v_g_ffn, v_w_in_a, v_g_v_a, v_w_spatial, v_b_spatial, v_w_in_b, v_g_q_b, v_g_k_b, v_g_mem, v_w_mem_kv, v_g_mq, v_g_mk, v_w_out, v_w_gate_up, v_w_down):
    given = dict(locals())
    depth = g_mix.shape[0]
    s, d = x.shape[1], x.shape[2]
    nm = mem.shape[1]
    t = d - MEM_WIDTH
    ff = w_down.shape[1] * N_DEV
    x0 = x.reshape(s, d)
    mem0 = mem.reshape(nm, d)
    target = loss_target.reshape(s, d)
    tables = _rope_tables(s)

    big_names = ("w_in", "w_mem_kv", "w_out", "w_gate_up", "w_down")

    def stacked_key(name, l):
        if name == "w_in":
            return ("w_in_a" if l % 2 == 0 else "w_in_b"), l // 2
        return name, l

    n_mix = 3

    def gather_start(l, route, after):
        lands = [_shard_into_land(given[key], idx) for key, idx in (stacked_key(name, l) for name in big_names)]
        h_mix, route = _gather_begin(lands[:n_mix], route, after, f"mix{l}")
        h_ffn, route = _gather_begin(lands[n_mix:], route, after, f"ffn{l}")
        return h_mix, h_ffn, route

    def gather_finish(handle, after, tag):
        return _gather_pass_on(_gather_end(handle, after, tag))

    saved = []
    xc = x0
    h_mix, h_ffn, _ = gather_start(0, g_mix[0].reshape(1, d), mem0)
    w_mix = gather_finish(h_mix, mem0, "mix0")
    w_ffn = None
    for l in range(depth):
        is_a = l % 2 == 0
        g_in, g_kv, g_out = w_mix
        w_kv, w_o = (g.reshape(-1, g.shape[2]) for g in (g_kv, g_out))
        qblk = (N_DEV * g_in.shape[2] - MEM_WIDTH) // MEM_WIDTH

        gm_row, gf_row, gmem_row = g_mix[l].reshape(1, d), g_ffn[l].reshape(1, d), g_mem[l].reshape(1, d)
        gmq_row, gmk_row = g_mq[l].reshape(1, HEAD), g_mk[l].reshape(1, HEAD)
        if l + 1 < depth:
            next_mix, next_ffn, gm_row = gather_start(l + 1, gm_row, g_in)
        h = _rms_fwd(xc, gm_row)
        z = _mm_cols_fwd(h, g_in, f32)
        if is_a:
            ia = l // 2
            mix = dict(g_v=g_v_a[ia].reshape(1, t), w_s=w_spatial[ia], b_t=b_spatial[ia].T)
            cat = _gmlp_fwd(z, mix["g_v"], mix["w_s"], mix["b_t"])
        else:
            ib = l // 2
            mix = dict(g_q=g_q_b[ib].reshape(1, HEAD), g_k=g_k_b[ib].reshape(1, HEAD))
            q, k, v = _attn_prep_fwd(z, mix["g_q"], mix["g_k"], tables, t)
            cat, lse = _flash_fwd(q, k, v)
            mix.update(q=q, k=k, v=v, lse=lse)
        hm = _rms_fwd(mem0, gmem_row)
        kv = _matmul(hm, w_kv)
        cat = _mem_fwd(z, qblk, kv, gmq_row, gmk_row, cat)
        x1 = _matmul(cat, w_o, res=xc)
        if l == 0:
            w_ffn = gather_finish(h_ffn, x1, "ffn0")
        g_gu, g_dn = w_ffn
        w_dn = g_dn.reshape(-1, g_dn.shape[2])
        h2 = _rms_fwd(x1, gf_row)
        gu = _mm_cols_fwd(h2, g_gu, bf16)
        act = _swiglu_fwd(gu)
        x2 = _matmul(act, w_dn, res=x1)
        saved.append(dict(x=xc, h=h, z=z, mix=mix, cat=cat, hm=hm, kv=kv, x1=x1, h2=h2, gu=gu, act=act, qblk=qblk,
                          w=(g_in, w_kv, w_o, g_gu, w_dn), rows=(gm_row, gf_row, gmem_row, gmq_row, gmk_row)))
        if l + 1 < depth:
            w_mix = gather_finish(next_mix, x2, f"mix{l + 1}")
            w_ffn = gather_finish(next_ffn, x2, f"ffn{l + 1}")
        xc = x2

    loss_row, dy, dy_b = _loss_head(xc, target)
    loss = lax.psum(loss_row[0, 0], ("x", "y", "c"))

    small = {n: [None] * given[n].shape[0] for n in ("g_mix", "g_ffn", "g_v_a", "w_spatial", "b_spatial", "g_q_b", "g_k_b", "g_mem", "g_mq", "g_mk")}
    big_out = {}

    def scatter_start(grads, route, after, tag):
        got = _swap_with_sibling(grads)
        sums = [_pair_sum(g, r) for g, r in zip(grads, got)]
        return _scatter_begin([p for p, _ in sums], [q for _, q in sums], route, after, tag)

    def scatter_finish(handle, names, l, after, tag):
        arrived = _scatter_end(handle, after, tag)[len(names):]
        for name, parts in zip(names, arrived):
            key, idx = stacked_key(name, l)
            big_out[key] = _adamw(parts, given[key], given["m_" + key], given["v_" + key], idx, big_out.get(key))

    pend_mix = None
    dx, dx_b = dy, dy_b
    for l in reversed(range(depth)):
        sv = saved[l]
        is_a = l % 2 == 0
        g_in, w_kv, w_o, g_gu, w_dn = sv["w"]
        gm_row, gf_row, gmem_row, gmq_row, gmk_row = sv["rows"]
        mix = sv["mix"]
        dw_dn = _matmul(sv["act"], dx_b, ta=True, out_dtype=bf16)
        dact = _matmul(dx_b, w_dn, tb=True, out_dtype=bf16)
        dgu = _swiglu_bwd(sv["gu"], dact)
        dw_gu = _mm_cols_wgrad(sv["h2"], dgu, g_gu.shape[2])
        dh2 = _mm_cols_dgrad(dgu, g_gu)
        dx1, dx1_b, dgf = _rms_bwd(sv["x1"], gf_row, dh2, dx)
        small["g_ffn"][l] = dgf.reshape(d)
        if pend_mix is not None:
            scatter_finish(*pend_mix, dx1, f"mix{l + 1}")
        pend_ffn, dx1_b = scatter_start([dw_gu, dw_dn.reshape(N_DEV, -1, d)], dx1_b, dx1, f"ffn{l}")
        dw_o = _matmul(sv["cat"], dx1_b, ta=True, out_dtype=bf16)
        dcat = _matmul(dx1_b, w_o, tb=True, out_dtype=bf16)
        if is_a:
            dz, dws, dbt, dgv = _gmlp_bwd(sv["z"], mix["g_v"], mix["w_s"], mix["b_t"], dcat)
            small["w_spatial"][l // 2], small["b_spatial"][l // 2], small["g_v_a"][l // 2] = dws, dbt.T, dgv.reshape(t)
        else:
            dq, dk, dv = _flash_bwd(mix["q"], mix["k"], mix["v"], sv["cat"], dcat, mix["lse"])
            dz, dgq, dgk = _attn_prep_bwd(sv["z"], mix["g_q"], mix["g_k"], tables, dq, dk, dv, t)
            small["g_q_b"][l // 2], small["g_k_b"][l // 2] = dgq.reshape(HEAD), dgk.reshape(HEAD)
        dz, dkn, dvm, dgmq = _mem_bwd(sv["z"], sv["qblk"], sv["kv"], gmq_row, gmk_row, dcat, dz)
        dkv, dgmk = _mem_kv_bwd(sv["kv"], gmk_row, dkn, dvm)
        dw_kv = _matmul(sv["hm"], dkv, ta=True, out_dtype=bf16)
        dhm = _matmul(dkv, w_kv, tb=True)
        small["g_mem"][l] = _rms_bwd(mem0, gmem_row, dhm, None).reshape(d)
        small["g_mq"][l] = dgmq.reshape(HEAD)
        small["g_mk"][l] = dgmk.reshape(HEAD)
        dw_in = _mm_cols_wgrad(sv["h"], dz, g_in.shape[2])
        dh = _mm_cols_dgrad(dz, g_in)
        dx, dx_b, dgm = _rms_bwd(sv["x"], gm_row, dh, dx1)
        small["g_mix"][l] = dgm.reshape(d)

        scatter_finish(pend_ffn, big_names[n_mix:], l, dx, f"ffn{l}")
        handle, dx_b = scatter_start([dw_in] + [dw.reshape(N_DEV, -1, dw.shape[1]) for dw in (dw_kv, dw_o)], dx_b, dx, f"mix{l}")
        pend_mix = (handle, big_names[:n_mix], l)
    scatter_finish(*pend_mix, dx, "mix0")

    small_names = tuple(small)
    small_grads = [jnp.stack(small[n]) for n in small_names]
    small_shapes = [g.shape for g in small_grads]
    (all_parts,) = _all_gather([_to_flat(small_grads)])
    souts = _adamw(all_parts, *[_to_flat([given[p + n] for n in small_names])[None] for p in ("", "m_", "v_")], 0, None)
    small_out = dict(zip(small_names, zip(*[_from_flat(flat, small_shapes) for flat in souts])))

    weights = ("g_mix", "g_ffn", "w_in_a", "g_v_a", "w_spatial", "b_spatial", "w_in_b", "g_q_b", "g_k_b", "g_mem", "w_mem_kv",
               "g_mq", "g_mk", "w_out", "w_gate_up", "w_down")
    results = {n: (small_out[n] if n in small_out else big_out[n]) for n in weights}
    grad_x = dx.reshape(1, s, d)
    return (loss, grad_x, *[results[n][kind] for kind in range(4) for n in weights])
```

```python
import functools

import jax
import jax.numpy as jnp
from jax import lax
from jax.experimental import pallas as pl
from jax.experimental.pallas import tpu as pltpu

f32 = jnp.float32
bf16 = jnp.bfloat16

HEAD = 128
CHUNK = 128
GRID_W = 64
MEM_HEADS = 4
KV_HEADS = 4
MEM_WIDTH = MEM_HEADS * HEAD
KV_WIDTH = KV_HEADS * HEAD
ROPE_THETA = 10000.0
ROPE_PAIRS = HEAD // 4
EPS = 1e-6
SCALE = HEAD ** -0.5
LOG2E = 1.4426950408889634
N_DEV = 8
LANES = 1024
VMEM_LIMIT = 56 * 1024 * 1024

ADAM_LR, ADAM_B1, ADAM_B2, ADAM_EPS, ADAM_WD, ADAM_STEP = 0.001, 0.9, 0.999, 1e-08, 0.01, 10

MESH = pl.DeviceIdType.MESH
_pallas_call = pl.pallas_call


def _pick(dim, cands):
    for c in cands:
        if dim % c == 0:
            return c
    return dim


def _cp(sem):
    return pltpu.CompilerParams(dimension_semantics=sem, vmem_limit_bytes=VMEM_LIMIT)


def _sds(shape, dtype):
    return jax.ShapeDtypeStruct(shape, dtype)


def _dot(a, b, ca, cb):
    return lax.dot_general(a, b, (((ca,), (cb,)), ((), ())), preferred_element_type=f32)


def _gelu(z):
    return 0.5 * z * (1.0 + lax.erf(z * 0.7071067811865476))


def _gelu_grad(z):
    return 0.5 * (1.0 + lax.erf(z * 0.7071067811865476)) + z * jnp.exp(-0.5 * z * z) * 0.3989422804014327


def _rot(x, sin_a, sin_b):
    return pltpu.roll(x, 96, 1) * sin_a + pltpu.roll(x, 32, 1) * sin_b


def _matmul(a, b, *, ta=False, tb=False, out_dtype=f32, res=None, tm=None, tn=None, tk=None):
    assert a.dtype == bf16 and b.dtype == bf16
    kdim, m = a.shape if ta else a.shape[::-1]
    n, k2 = b.shape if tb else b.shape[::-1]
    assert kdim == k2, (a.shape, b.shape, ta, tb)
    tm = tm or _pick(m, (1024, 1408, 512, 256, 128))
    tn = tn or _pick(n, (1024, 1408, 512, 256, 128))
    if tk is None:
        tk = kdim if kdim <= 2048 else _pick(kdim, (1408, 1024, 512, 256, 128))
    nk = kdim // tk
    ca, cb = (0 if ta else 1), (1 if tb else 0)
    has_res = res is not None

    def body(*refs):
        a_ref, b_ref = refs[0], refs[1]
        r_ref = refs[2] if has_res else None
        o_ref = refs[3] if has_res else refs[2]
        prod = _dot(a_ref[...], b_ref[...], ca, cb)
        if nk == 1:
            if has_res:
                prod = prod + r_ref[...]
            o_ref[...] = prod.astype(o_ref.dtype)
        else:
            acc = refs[-1]
            k = pl.program_id(2)

            @pl.when(k == 0)
            def _():
                acc[...] = prod

            @pl.when(k > 0)
            def _():
                acc[...] += prod

            @pl.when(k == nk - 1)
            def _():
                out = acc[...]
                if has_res:
                    out = out + r_ref[...]
                o_ref[...] = out.astype(o_ref.dtype)

    a_spec = pl.BlockSpec((tk, tm), lambda i, j, k: (k, i)) if ta else pl.BlockSpec((tm, tk), lambda i, j, k: (i, k))
    b_spec = pl.BlockSpec((tn, tk), lambda i, j, k: (j, k)) if tb else pl.BlockSpec((tk, tn), lambda i, j, k: (k, j))
    o_spec = pl.BlockSpec((tm, tn), lambda i, j, k: (i, j))
    in_specs = [a_spec, b_spec] + ([o_spec] if has_res else [])
    args = (a, b) + ((res,) if has_res else ())
    mode = ("t" if ta else "n") + ("t" if tb else "n")
    return _pallas_call(
        body, out_shape=_sds((m, n), out_dtype), grid=(m // tm, n // tn, nk),
        in_specs=in_specs, out_specs=o_spec,
        scratch_shapes=([pltpu.VMEM((tm, tn), f32)] if nk > 1 else []),
        compiler_params=_cp(("parallel", "parallel", "arbitrary")),
        name=f"mm_{mode}_{m}x{kdim}x{n}{'_res' if has_res else ''}_{jnp.dtype(out_dtype).name}",
    )(*args)


def _shards_per_step(n, pair_bytes=0):
    p = 2 if (n % 128 != 0 or 0 < 2 * pair_bytes <= VMEM_LIMIT // 2) else 1
    assert (p * n) % 128 == 0 and N_DEV % p == 0
    return p


def _lane_pieces(v, p, n):
    return [v] if p == 1 else [v[:, q * n:(q + 1) * n] for q in range(p)]


def _mm_cols_fwd(a, g, out_dtype):
    m, kdim = a.shape
    nd, k2, n = g.shape
    assert kdim == k2 and a.dtype == bf16 and g.dtype == bf16
    p = _shards_per_step(n)
    tm = _pick(m, (1024, 512, 256, 128))

    def body(a_ref, g_ref, o_ref):
        av = a_ref[...]
        parts = [_dot(av, g_ref[q], 1, 0) for q in range(p)]
        out = parts[0] if p == 1 else jnp.concatenate(parts, axis=1)
        o_ref[...] = out.astype(o_ref.dtype)

    return _pallas_call(
        body, out_shape=_sds((m, nd * n), out_dtype), grid=(m // tm, nd // p),
        in_specs=[pl.BlockSpec((tm, kdim), lambda i, j: (i, 0)), pl.BlockSpec((p, kdim, n), lambda i, j: (j, 0, 0))],
        out_specs=pl.BlockSpec((tm, p * n), lambda i, j: (i, j)),
        compiler_params=_cp(("parallel", "arbitrary")), name=f"mm_cols_fwd_{m}x{kdim}x{nd * n}_{jnp.dtype(out_dtype).name}",
    )(a, g)


def _mm_cols_dgrad(dz, g):
    m, nn = dz.shape
    nd, kdim, n = g.shape
    assert nn == nd * n and dz.dtype == bf16 and g.dtype == bf16
    p = _shards_per_step(n, pair_bytes=2 * kdim * n * 2)
    nj = nd // p
    tm = _pick(m, (512, 256, 128))

    def body(dz_ref, g_ref, o_ref, acc):
        j = pl.program_id(1)
        tot = None
        for q, piece in enumerate(_lane_pieces(dz_ref[...], p, n)):
            dd = _dot(piece, g_ref[q], 1, 1)
            tot = dd if tot is None else tot + dd

        @pl.when(j == 0)
        def _():
            acc[...] = tot

        @pl.when(j > 0)
        def _():
            acc[...] += tot

        @pl.when(j == nj - 1)
        def _():
            o_ref[...] = acc[...]

    return _pallas_call(
        body, out_shape=_sds((m, kdim), f32), grid=(m // tm, nj),
        in_specs=[pl.BlockSpec((tm, p * n), lambda i, j: (i, j)), pl.BlockSpec((p, kdim, n), lambda i, j: (j, 0, 0))],
        out_specs=pl.BlockSpec((tm, kdim), lambda i, j: (i, 0)),
        scratch_shapes=[pltpu.VMEM((tm, kdim), f32)],
        compiler_params=_cp(("parallel", "arbitrary")), name=f"mm_cols_dgrad_{m}x{nn}x{kdim}",
    )(dz, g)


def _mm_cols_wgrad(a, dz, n):
    s, kdim = a.shape
    nd = dz.shape[1] // n
    assert a.dtype == bf16 and dz.dtype == bf16
    p = _shards_per_step(n)
    tkw = _pick(kdim, (1024, 512, 256, 128))
    ts = _pick(s, (2048, 1024, 512, 256, 128))
    ns = s // ts

    def body(a_ref, dz_ref, o_ref, acc):
        si = pl.program_id(2)
        av = a_ref[...]
        prods = [_dot(av, piece, 0, 0) for piece in _lane_pieces(dz_ref[...], p, n)]

        @pl.when(si == 0)
        def _():
            for q in range(p):
                acc[q] = prods[q]

        @pl.when(si > 0)
        def _():
            for q in range(p):
                acc[q] += prods[q]

        @pl.when(si == ns - 1)
        def _():
            o_ref[...] = acc[...].astype(bf16)

    return _pallas_call(
        body, out_shape=_sds((nd, kdim, n), bf16), grid=(kdim // tkw, nd // p, ns),
        in_specs=[pl.BlockSpec((ts, tkw), lambda i, j, k: (k, i)), pl.BlockSpec((ts, p * n), lambda i, j, k: (k, j))],
        out_specs=pl.BlockSpec((p, tkw, n), lambda i, j, k: (j, i, 0)),
        scratch_shapes=[pltpu.VMEM((p, tkw, n), f32)],
        compiler_params=_cp(("parallel", "parallel", "arbitrary")), name=f"mm_cols_wgrad_{kdim}x{s}x{nd * n}",
    )(a, dz)


def _rms_fwd(x, g_row):
    s, d = x.shape
    tr = _pick(s, (512, 256, 128))

    def body(x_ref, g_ref, o_ref):
        xv = x_ref[...]
        r = lax.rsqrt(jnp.mean(xv * xv, axis=-1, keepdims=True) + EPS)
        o_ref[...] = (xv * r * g_ref[...]).astype(bf16)

    return _pallas_call(
        body, out_shape=_sds((s, d), bf16), grid=(s // tr,),
        in_specs=[pl.BlockSpec((tr, d), lambda i: (i, 0)), pl.BlockSpec((1, d), lambda i: (0, 0))],
        out_specs=pl.BlockSpec((tr, d), lambda i: (i, 0)),
        compiler_params=_cp(("parallel",)), name=f"rms_fwd_{s}x{d}",
    )(x, g_row)


def _rms_bwd(x, g_row, dh, dres):
    s, d = x.shape
    tr = _pick(s, (512, 256, 128))
    with_dx = dres is not None

    def body(*refs):
        if with_dx:
            x_ref, g_ref, dh_ref, dres_ref, dx_ref, dxb_ref, dg_ref = refs
        else:
            x_ref, g_ref, dh_ref, dg_ref = refs

        @pl.when(pl.program_id(0) == 0)
        def _():
            dg_ref[...] = jnp.zeros_like(dg_ref)

        xv = x_ref[...]
        r = lax.rsqrt(jnp.mean(xv * xv, axis=-1, keepdims=True) + EPS)
        xh = xv * r
        dy = dh_ref[...].astype(f32)
        dg_ref[...] += jnp.sum(dy * xh, axis=0, keepdims=True)
        if with_dx:
            gy = dy * g_ref[...]
            dx = dres_ref[...] + r * (gy - xh * jnp.mean(gy * xh, axis=-1, keepdims=True))
            dx_ref[...] = dx
            dxb_ref[...] = dx.astype(bf16)

    row = pl.BlockSpec((tr, d), lambda i: (i, 0))
    vec = pl.BlockSpec((1, d), lambda i: (0, 0))
    if with_dx:
        return _pallas_call(
            body, out_shape=(_sds((s, d), f32), _sds((s, d), bf16), _sds((1, d), f32)), grid=(s // tr,),
            in_specs=[row, vec, row, row], out_specs=(row, row, vec),
            compiler_params=_cp(("arbitrary",)), name=f"rms_bwd_{s}x{d}",
        )(x, g_row, dh, dres)
    return _pallas_call(
        body, out_shape=_sds((1, d), f32), grid=(s // tr,),
        in_specs=[row, vec, row], out_specs=vec,
        compiler_params=_cp(("arbitrary",)), name=f"rms_bwd_gain_{s}x{d}",
    )(x, g_row, dh)


def _gmlp_rows(s):
    return CHUNK * (2 if (s // CHUNK) % 2 == 0 else 1)


def _gmlp_fwd(z, g_v, w_s, b_t):
    s = z.shape[0]
    t = g_v.shape[1]
    ng = t // HEAD
    rb = _gmlp_rows(s)

    def body(z_ref, gv_ref, ws_ref, bt_ref, o_ref):
        for ci in range(rb // CHUNK):
            lo = ci * CHUNK
            a = _gelu(z_ref[lo:lo + CHUNK, :])
            u, vv = a[:, :t], a[:, t:]
            r = lax.rsqrt(jnp.mean(vv * vv, axis=-1, keepdims=True) + EPS)
            vn = (vv * r * gv_ref[...]).astype(bf16)
            for g in range(ng):
                cs = slice(g * HEAD, (g + 1) * HEAD)
                sg = _dot(ws_ref[g].astype(bf16), vn[:, cs], 1, 0) + bt_ref[:, g:g + 1]
                o_ref[lo:lo + CHUNK, cs] = (u[:, cs] * sg).astype(bf16)

    return _pallas_call(
        body, out_shape=_sds((s, t + MEM_WIDTH), bf16), grid=(s // rb,),
        in_specs=[pl.BlockSpec((rb, 2 * t), lambda i: (i, 0)), pl.BlockSpec((1, t), lambda i: (0, 0)),
                  pl.BlockSpec((ng, CHUNK, CHUNK), lambda i: (0, 0, 0)), pl.BlockSpec((CHUNK, ng), lambda i: (0, 0))],
        out_specs=pl.BlockSpec((rb, t), lambda i: (i, 0)),
        compiler_params=_cp(("parallel",)), name=f"gmlp_fwd_{s}",
    )(z, g_v, w_s, b_t)


def _gmlp_bwd(z, g_v, w_s, b_t, dtok):
    s = z.shape[0]
    t = g_v.shape[1]
    ng = t // HEAD
    rb = _gmlp_rows(s)
    nsteps = s // rb

    def body(z_ref, gv_ref, ws_ref, bt_ref, dt_ref, dz_ref, dws_ref, dbt_ref, dgv_ref, ds_acc):
        step = pl.program_id(0)

        @pl.when(step == 0)
        def _():
            dws_ref[...] = jnp.zeros_like(dws_ref)
            dgv_ref[...] = jnp.zeros_like(dgv_ref)
            ds_acc[...] = jnp.zeros_like(ds_acc)

        for ci in range(rb // CHUNK):
            lo = ci * CHUNK
            zz = z_ref[lo:lo + CHUNK, :]
            a = _gelu(zz)
            u, vv = a[:, :t], a[:, t:]
            r = lax.rsqrt(jnp.mean(vv * vv, axis=-1, keepdims=True) + EPS)
            vh = vv * r
            vn = (vh * gv_ref[...]).astype(bf16)
            dtok = dt_ref[lo:lo + CHUNK, :].astype(f32)
            ds = dtok * u
            ds_acc[...] += ds
            dsb = ds.astype(bf16)
            du_parts, dvn_parts = [], []
            for g in range(ng):
                cs = slice(g * HEAD, (g + 1) * HEAD)
                wg = ws_ref[g].astype(bf16)
                sg = _dot(wg, vn[:, cs], 1, 0) + bt_ref[:, g:g + 1]
                du_parts.append(dtok[:, cs] * sg)
                dws_ref[g] += _dot(dsb[:, cs], vn[:, cs], 1, 1)
                dvn_parts.append(_dot(wg, dsb[:, cs], 0, 0))
            dvn = jnp.concatenate(dvn_parts, axis=1)
            dgv_ref[...] += jnp.sum(dvn * vh, axis=0, keepdims=True)
            gy = dvn * gv_ref[...]
            dvv = r * (gy - vh * jnp.mean(gy * vh, axis=-1, keepdims=True))
            da = jnp.concatenate(du_parts + [dvv], axis=1)
            dz_ref[lo:lo + CHUNK, :] = (da * _gelu_grad(zz)).astype(bf16)

        @pl.when(step == nsteps - 1)
        def _():
            for g in range(ng):
                dbt_ref[:, g:g + 1] = jnp.sum(ds_acc[:, g * HEAD:(g + 1) * HEAD], axis=1, keepdims=True)

    return _pallas_call(
        body,
        out_shape=(_sds((s, z.shape[1]), bf16), _sds((ng, CHUNK, CHUNK), f32), _sds((CHUNK, ng), f32), _sds((1, t), f32)),
        grid=(nsteps,),
        in_specs=[pl.BlockSpec((rb, 2 * t), lambda i: (i, 0)), pl.BlockSpec((1, t), lambda i: (0, 0)),
                  pl.BlockSpec((ng, CHUNK, CHUNK), lambda i: (0, 0, 0)), pl.BlockSpec((CHUNK, ng), lambda i: (0, 0)),
                  pl.BlockSpec((rb, t), lambda i: (i, 0))],
        out_specs=(pl.BlockSpec((rb, 2 * t), lambda i: (i, 0)), pl.BlockSpec((ng, CHUNK, CHUNK), lambda i: (0, 0, 0)),
                   pl.BlockSpec((CHUNK, ng), lambda i: (0, 0)), pl.BlockSpec((1, t), lambda i: (0, 0))),
        scratch_shapes=[pltpu.VMEM((CHUNK, t), f32)],
        compiler_params=_cp(("arbitrary",)), name=f"gmlp_bwd_{s}",
    )(z, g_v, w_s, b_t, dtok)


def _rope_tables(s):
    n_rows = s // GRID_W
    rows = jnp.broadcast_to(jnp.arange(n_rows)[:, None], (n_rows, GRID_W)).reshape(s)
    cols = jnp.broadcast_to(jnp.arange(GRID_W)[None, :], (n_rows, GRID_W)).reshape(s)
    freqs = ROPE_THETA ** (-jnp.arange(ROPE_PAIRS, dtype=f32) / ROPE_PAIRS)
    ang_r = rows.astype(f32)[:, None] * freqs
    ang_c = cols.astype(f32)[:, None] * freqs
    ang = jnp.concatenate([ang_r, ang_r, ang_c, ang_c], axis=-1)
    cos, sin = jnp.cos(ang), jnp.sin(ang)
    first = (jnp.arange(HEAD) % (2 * ROPE_PAIRS)) < ROPE_PAIRS
    return cos, jnp.where(first, -sin, 0.0), jnp.where(first, 0.0, sin)


def _attn_prep_fwd(z, g_q, g_k, tables, t):
    s = z.shape[0]
    tr = _pick(s, (256, 128))
    nq = t // HEAD
    width = t + 2 * KV_WIDTH

    def body(z_ref, gq_ref, gk_ref, cos_ref, sa_ref, sb_ref, q_ref, k_ref, v_ref):
        cos, sa, sb = cos_ref[...], sa_ref[...], sb_ref[...]
        for h in range(nq + KV_HEADS):
            cs = slice(h * HEAD, (h + 1) * HEAD)
            xv = z_ref[:, cs]
            r = lax.rsqrt(jnp.mean(xv * xv, axis=-1, keepdims=True) + EPS)
            xn = xv * r * (gq_ref[...] if h < nq else gk_ref[...])
            y = xn * cos + _rot(xn, sa, sb)
            if h < nq:
                q_ref[:, cs] = (y * (SCALE * LOG2E)).astype(bf16)
            else:
                k_ref[:, (h - nq) * HEAD:(h - nq + 1) * HEAD] = y.astype(bf16)
        v_ref[...] = z_ref[:, t + KV_WIDTH:width].astype(bf16)

    row = lambda w: pl.BlockSpec((tr, w), lambda i: (i, 0))
    vec = pl.BlockSpec((1, HEAD), lambda i: (0, 0))
    return _pallas_call(
        body, out_shape=(_sds((s, t), bf16), _sds((s, KV_WIDTH), bf16), _sds((s, KV_WIDTH), bf16)), grid=(s // tr,),
        in_specs=[row(width), vec, vec, row(HEAD), row(HEAD), row(HEAD)],
        out_specs=(row(t), row(KV_WIDTH), row(KV_WIDTH)),
        compiler_params=_cp(("parallel",)), name=f"attn_prep_fwd_{s}",
    )(z, g_q, g_k, *tables)


def _attn_prep_bwd(z, g_q, g_k, tables, dq, dk, dv, t):
    s = z.shape[0]
    tr = _pick(s, (256, 128))
    nq = t // HEAD
    width = t + 2 * KV_WIDTH

    def body(z_ref, gq_ref, gk_ref, cos_ref, sa_ref, sb_ref, dq_ref, dk_ref, dv_ref, dz_ref, dgq_ref, dgk_ref):
        @pl.when(pl.program_id(0) == 0)
        def _():
            dgq_ref[...] = jnp.zeros_like(dgq_ref)
            dgk_ref[...] = jnp.zeros_like(dgk_ref)

        cos, sa, sb = cos_ref[...], sa_ref[...], sb_ref[...]
        for h in range(nq + KV_HEADS):
            cs = slice(h * HEAD, (h + 1) * HEAD)
            xv = z_ref[:, cs]
            r = lax.rsqrt(jnp.mean(xv * xv, axis=-1, keepdims=True) + EPS)
            xh = xv * r
            if h < nq:
                dy, g_ref, dg_ref = dq_ref[:, cs], gq_ref, dgq_ref
            else:
                dy, g_ref, dg_ref = dk_ref[:, (h - nq) * HEAD:(h - nq + 1) * HEAD], gk_ref, dgk_ref
            dy = dy.astype(f32)
            dxn = dy * cos - _rot(dy, sa, sb)
            dg_ref[...] += jnp.sum(dxn * xh, axis=0, keepdims=True)
            gy = dxn * g_ref[...]
            dz_ref[:, cs] = (r * (gy - xh * jnp.mean(gy * xh, axis=-1, keepdims=True))).astype(bf16)
        dz_ref[:, t + KV_WIDTH:width] = dv_ref[...].astype(bf16)

    row = lambda w: pl.BlockSpec((tr, w), lambda i: (i, 0))
    vec = pl.BlockSpec((1, HEAD), lambda i: (0, 0))
    return _pallas_call(
        body, out_shape=(_sds((s, z.shape[1]), bf16), _sds((1, HEAD), f32), _sds((1, HEAD), f32)), grid=(s // tr,),
        in_specs=[row(width), vec, vec, row(HEAD), row(HEAD), row(HEAD), row(t), row(KV_WIDTH), row(KV_WIDTH)],
        out_specs=(row(width), vec, vec),
        compiler_params=_cp(("arbitrary",)), name=f"attn_prep_bwd_{s}",
    )(z, g_q, g_k, *tables, dq, dk, dv)


def _flash_tiles(s):
    return _pick(s, (512, 256, 128)), _pick(s, (512, 256, 128))


def _stack_heads(ref, grp):
    return jnp.concatenate([ref[:, g * HEAD:(g + 1) * HEAD] for g in range(grp)], axis=0)


def _flash_fwd(q, k, v):
    s, t = q.shape
    grp = t // KV_WIDTH
    tq, tk = _flash_tiles(s)
    nkv = s // tk
    rows = grp * tq

    def body(q_ref, k_ref, v_ref, o_ref, lse_ref, m_sc, acc_sc):
        ki = pl.program_id(2)

        @pl.when(ki == 0)
        def _():
            m_sc[...] = jnp.full(m_sc.shape, -jnp.inf, f32)
            acc_sc[...] = jnp.zeros_like(acc_sc)

        qs = _stack_heads(q_ref, grp)
        sc = _dot(qs, k_ref[...], 1, 1)
        m_prev = m_sc[...]
        m_new = jnp.maximum(m_prev, jnp.max(sc, axis=-1, keepdims=True))
        alpha = jnp.exp2(m_prev - m_new)
        p = jnp.exp2((sc - m_new).astype(bf16))
        v1 = jnp.concatenate([v_ref[...], jnp.ones((tk, HEAD), bf16)], axis=1)
        acc_sc[...] = alpha * acc_sc[...] + _dot(p, v1, 1, 0)
        m_sc[...] = m_new

        @pl.when(ki == nkv - 1)
        def _():
            acc = acc_sc[...]
            l = acc[:, HEAD:HEAD + 1]
            o = acc[:, :HEAD] / l
            for g in range(grp):
                o_ref[:, g * HEAD:(g + 1) * HEAD] = o[g * tq:(g + 1) * tq].astype(bf16)
            lse_ref[0] = jnp.broadcast_to(m_sc[...] + jnp.log(l) * LOG2E, (rows, HEAD))

    return _pallas_call(
        body, out_shape=(_sds((s, t + MEM_WIDTH), bf16), _sds((KV_HEADS, grp * s, HEAD), f32)), grid=(KV_HEADS, s // tq, nkv),
        in_specs=[pl.BlockSpec((tq, grp * HEAD), lambda h, i, j: (i, h)), pl.BlockSpec((tk, HEAD), lambda h, i, j: (j, h)),
                  pl.BlockSpec((tk, HEAD), lambda h, i, j: (j, h))],
        out_specs=(pl.BlockSpec((tq, grp * HEAD), lambda h, i, j: (i, h)), pl.BlockSpec((1, rows, HEAD), lambda h, i, j: (h, i, 0))),
        scratch_shapes=[pltpu.VMEM((rows, 1), f32), pltpu.VMEM((rows, 2 * HEAD), f32)],
        compiler_params=_cp(("parallel", "parallel", "arbitrary")), name=f"flash_fwd_{s}",
    )(q, k, v)


def _flash_delta(o, do, t):
    s = o.shape[0]
    grp = t // KV_WIDTH
    tq, _ = _flash_tiles(s)
    rows = grp * tq

    def body(o_ref, do_ref, d_ref):
        for g in range(grp):
            cs = slice(g * HEAD, (g + 1) * HEAD)
            dd = jnp.sum(o_ref[:, cs].astype(f32) * do_ref[:, cs].astype(f32), axis=-1, keepdims=True)
            d_ref[0, g * tq:(g + 1) * tq, :] = jnp.broadcast_to(dd, (tq, HEAD))

    qb = pl.BlockSpec((tq, grp * HEAD), lambda h, i: (i, h))
    return _pallas_call(
        body, out_shape=_sds((KV_HEADS, grp * s, HEAD), f32), grid=(KV_HEADS, s // tq),
        in_specs=[qb, qb], out_specs=pl.BlockSpec((1, rows, HEAD), lambda h, i: (h, i, 0)),
        compiler_params=_cp(("parallel", "parallel")), name=f"flash_delta_{s}",
    )(o, do)


def _flash_probs(q_ref, k_ref, v_ref, do_ref, lse_ref, delta_ref, grp):
    qs = _stack_heads(q_ref, grp)
    dos = _stack_heads(do_ref, grp)
    sc = _dot(qs, k_ref[...], 1, 1)
    p = jnp.exp2((sc - lse_ref[0, :, 0:1]).astype(bf16))
    dp = _dot(dos, v_ref[...], 1, 1)
    ds = p * (dp - delta_ref[0, :, 0:1]).astype(bf16)
    return qs, dos, p, ds


def _flash_bwd(q, k, v, o, do, lse):
    s, t = q.shape
    grp = t // KV_WIDTH
    tq, tk = _flash_tiles(s)
    nq, nkv = s // tq, s // tk
    rows = grp * tq
    delta = _flash_delta(o, do, t)

    def body(q_ref, k_ref, v_ref, do_ref, lse_ref, delta_ref, dq_ref, dk_ref, dv_ref, dq_acc, dk_acc, dv_acc):
        kj, qi = pl.program_id(1), pl.program_id(2)

        @pl.when(qi == 0)
        def _():
            dk_acc[...] = jnp.zeros_like(dk_acc)
            dv_acc[...] = jnp.zeros_like(dv_acc)

        qs, dos, p, ds = _flash_probs(q_ref, k_ref, v_ref, do_ref, lse_ref, delta_ref, grp)
        dv_acc[...] += _dot(p, dos, 0, 0)
        dk_acc[...] += _dot(ds, qs, 0, 0)
        mine = pl.ds(pl.multiple_of(qi * rows, rows), rows)
        term = _dot(ds, k_ref[...], 1, 0)

        @pl.when(kj == 0)
        def _():
            dq_acc[mine, :] = term

        @pl.when(kj > 0)
        def _():
            dq_acc[mine, :] += term

        @pl.when(kj == nkv - 1)
        def _():
            total = dq_acc[mine, :]
            for g in range(grp):
                dq_ref[:, g * HEAD:(g + 1) * HEAD] = total[g * tq:(g + 1) * tq] * SCALE

        @pl.when(qi == nq - 1)
        def _():
            dk_ref[...] = dk_acc[...] * (1.0 / LOG2E)
            dv_ref[...] = dv_acc[...]

    qb = pl.BlockSpec((tq, grp * HEAD), lambda h, j, i: (i, h))
    kb = pl.BlockSpec((tk, HEAD), lambda h, j, i: (j, h))
    lb = pl.BlockSpec((1, rows, HEAD), lambda h, j, i: (h, i, 0))
    dqb = pl.BlockSpec((tq, grp * HEAD), lambda h, j, i: (jnp.where(j == nkv - 1, i, 0), h))
    return _pallas_call(
        body, out_shape=(_sds((s, t), f32), _sds((s, KV_WIDTH), f32), _sds((s, KV_WIDTH), f32)), grid=(KV_HEADS, nkv, nq),
        in_specs=[qb, kb, kb, qb, lb, lb], out_specs=(dqb, kb, kb),
        scratch_shapes=[pltpu.VMEM((nq * rows, HEAD), f32), pltpu.VMEM((tk, HEAD), f32), pltpu.VMEM((tk, HEAD), f32)],
        compiler_params=_cp(("parallel", "arbitrary", "arbitrary")), name=f"flash_bwd_{s}",
    )(q, k, v, do, lse, delta)


def _mem_heads(z_ref, kv_ref, gq_ref, gk_ref, h):
    cs = slice(h * HEAD, (h + 1) * HEAD)
    xv = z_ref[:, cs]
    r = lax.rsqrt(jnp.mean(xv * xv, axis=-1, keepdims=True) + EPS)
    xh = xv * r
    kx = kv_ref[:, cs]
    rk = lax.rsqrt(jnp.mean(kx * kx, axis=-1, keepdims=True) + EPS)
    kn = (kx * rk * gk_ref[...]).astype(bf16)
    vv = kv_ref[:, MEM_WIDTH + h * HEAD:MEM_WIDTH + (h + 1) * HEAD].astype(bf16)
    qn = (xh * gq_ref[...]).astype(bf16)
    sc = _dot(qn, kn, 1, 1) * SCALE
    e = jnp.exp(sc - jnp.max(sc, axis=-1, keepdims=True))
    p = e / jnp.sum(e, axis=-1, keepdims=True)
    return cs, r, xh, qn, kn, vv, p


def _mem_fwd(z, qblk, kv, g_mq, g_mk, cat):
    s = z.shape[0]
    nm = kv.shape[0]
    tr = _pick(s, (512, 256, 128))
    oblk = cat.shape[1] // MEM_WIDTH - 1

    def body(z_ref, kv_ref, gq_ref, gk_ref, cat_ref, o_ref):
        for h in range(MEM_HEADS):
            cs, _, _, _, _, vv, p = _mem_heads(z_ref, kv_ref, gq_ref, gk_ref, h)
            o_ref[:, cs] = _dot(p.astype(bf16), vv, 1, 0).astype(bf16)

    vec = pl.BlockSpec((1, HEAD), lambda i: (0, 0))
    return _pallas_call(
        body, out_shape=_sds(cat.shape, bf16), grid=(s // tr,),
        in_specs=[pl.BlockSpec((tr, MEM_WIDTH), lambda i: (i, qblk)), pl.BlockSpec((nm, 2 * MEM_WIDTH), lambda i: (0, 0)), vec, vec,
                  pl.BlockSpec(memory_space=pl.ANY)],
        out_specs=pl.BlockSpec((tr, MEM_WIDTH), lambda i: (i, oblk)),
        input_output_aliases={4: 0},
        compiler_params=_cp(("parallel",)), name=f"mem_fwd_{s}_{qblk}",
    )(z, kv, g_mq, g_mk, cat)


def _mem_bwd(z, qblk, kv, g_mq, g_mk, dcat, dz):
    s = z.shape[0]
    nm = kv.shape[0]
    tr = _pick(s, (512, 256, 128))
    dblk = dcat.shape[1] // MEM_WIDTH - 1

    def body(z_ref, kv_ref, gq_ref, gk_ref, dm_ref, dzin_ref, dz_ref, dkn_ref, dv_ref, dgq_ref):
        @pl.when(pl.program_id(0) == 0)
        def _():
            dkn_ref[...] = jnp.zeros_like(dkn_ref)
            dv_ref[...] = jnp.zeros_like(dv_ref)
            dgq_ref[...] = jnp.zeros_like(dgq_ref)

        for h in range(MEM_HEADS):
            cs, r, xh, qn, kn, vv, p = _mem_heads(z_ref, kv_ref, gq_ref, gk_ref, h)
            dm = dm_ref[:, cs]
            dv_ref[:, cs] += _dot(p.astype(bf16), dm, 0, 0)
            dp = _dot(dm, vv, 1, 1)
            ds = (p * (dp - jnp.sum(dp * p, axis=-1, keepdims=True)) * SCALE).astype(bf16)
            dqn = _dot(ds, kn, 1, 0)
            dkn_ref[:, cs] += _dot(ds, qn, 0, 0)
            dgq_ref[...] += jnp.sum(dqn * xh, axis=0, keepdims=True)
            gy = dqn * gq_ref[...]
            dz_ref[:, cs] = (r * (gy - xh * jnp.mean(gy * xh, axis=-1, keepdims=True))).astype(bf16)

    vec = pl.BlockSpec((1, HEAD), lambda i: (0, 0))
    acc = pl.BlockSpec((nm, MEM_WIDTH), lambda i: (0, 0))
    return _pallas_call(
        body, out_shape=(_sds(dz.shape, bf16), _sds((nm, MEM_WIDTH), f32), _sds((nm, MEM_WIDTH), f32), _sds((1, HEAD), f32)),
        grid=(s // tr,),
        in_specs=[pl.BlockSpec((tr, MEM_WIDTH), lambda i: (i, qblk)), pl.BlockSpec((nm, 2 * MEM_WIDTH), lambda i: (0, 0)), vec, vec,
                  pl.BlockSpec((tr, MEM_WIDTH), lambda i: (i, dblk)), pl.BlockSpec(memory_space=pl.ANY)],
        out_specs=(pl.BlockSpec((tr, MEM_WIDTH), lambda i: (i, qblk)), acc, acc, vec),
        input_output_aliases={5: 0},
        compiler_params=_cp(("arbitrary",)), name=f"mem_bwd_{s}_{qblk}",
    )(z, kv, g_mq, g_mk, dcat, dz)


def _mem_kv_bwd(kv, g_mk, dkn, dv):
    nm = kv.shape[0]

    def body(kv_ref, gk_ref, dkn_ref, dv_ref, dkv_ref, dgk_ref):
        dgk = jnp.zeros((1, HEAD), f32)
        for h in range(MEM_HEADS):
            cs = slice(h * HEAD, (h + 1) * HEAD)
            kx = kv_ref[:, cs]
            rk = lax.rsqrt(jnp.mean(kx * kx, axis=-1, keepdims=True) + EPS)
            kh = kx * rk
            dkn_h = dkn_ref[:, cs]
            dgk = dgk + jnp.sum(dkn_h * kh, axis=0, keepdims=True)
            gy = dkn_h * gk_ref[...]
            dkv_ref[:, cs] = (rk * (gy - kh * jnp.mean(gy * kh, axis=-1, keepdims=True))).astype(bf16)
        dkv_ref[:, MEM_WIDTH:] = dv_ref[...].astype(bf16)
        dgk_ref[...] = dgk

    return _pallas_call(
        body, out_shape=(_sds((nm, 2 * MEM_WIDTH), bf16), _sds((1, HEAD), f32)),
        compiler_params=pltpu.CompilerParams(vmem_limit_bytes=VMEM_LIMIT), name=f"mem_kv_bwd_{nm}",
    )(kv, g_mk, dkn, dv)


def _ffn_tiles(s, ff):
    return _pick(s, (512, 256, 128)), _pick(ff, (1408, 1024, 512, 256, 128))


def _swiglu_fwd(gu):
    s, ff2 = gu.shape
    ff = ff2 // 2
    tr, tf = _ffn_tiles(s, ff)
    nf = ff // tf

    def body(g_ref, u_ref, o_ref):
        g = g_ref[...].astype(f32)
        o_ref[...] = (g * jax.nn.sigmoid(g) * u_ref[...].astype(f32)).astype(bf16)

    return _pallas_call(
        body, out_shape=_sds((s, ff), bf16), grid=(s // tr, nf),
        in_specs=[pl.BlockSpec((tr, tf), lambda i, j: (i, j)), pl.BlockSpec((tr, tf), lambda i, j: (i, j + nf))],
        out_specs=pl.BlockSpec((tr, tf), lambda i, j: (i, j)),
        compiler_params=_cp(("parallel", "parallel")), name=f"swiglu_fwd_{s}x{ff}",
    )(gu, gu)


def _swiglu_bwd(gu, dact):
    s, ff2 = gu.shape
    ff = ff2 // 2
    tr, tf = _ffn_tiles(s, ff)
    nf = ff // tf

    def body(g_ref, u_ref, da_ref, o_ref):
        g = g_ref[...].astype(f32)
        da = da_ref[...].astype(f32)
        sg = jax.nn.sigmoid(g)

        @pl.when(pl.program_id(1) < nf)
        def _():
            o_ref[...] = (da * u_ref[...].astype(f32) * sg * (1.0 + g * (1.0 - sg))).astype(bf16)

        @pl.when(pl.program_id(1) >= nf)
        def _():
            o_ref[...] = (da * g * sg).astype(bf16)

    return _pallas_call(
        body, out_shape=_sds((s, ff2), bf16), grid=(s // tr, 2 * nf),
        in_specs=[pl.BlockSpec((tr, tf), lambda i, j: (i, j % nf)), pl.BlockSpec((tr, tf), lambda i, j: (i, j % nf + nf)),
                  pl.BlockSpec((tr, tf), lambda i, j: (i, j % nf))],
        out_specs=pl.BlockSpec((tr, tf), lambda i, j: (i, j)),
        compiler_params=_cp(("parallel", "parallel")), name=f"swiglu_bwd_{s}x{ff}",
    )(gu, gu, dact)


def _loss_head(y, target):
    s, d = y.shape
    tr = _pick(s, (512, 256, 128))

    def body(y_ref, t_ref, l_ref, dy_ref, dyb_ref):
        @pl.when(pl.program_id(0) == 0)
        def _():
            l_ref[...] = jnp.zeros_like(l_ref)

        err = y_ref[...] - t_ref[...]
        l_ref[...] += 0.5 * jnp.sum(jnp.mean(err * err, axis=-1, keepdims=True), axis=0, keepdims=True)
        dy = err * (1.0 / d)
        dy_ref[...] = dy
        dyb_ref[...] = dy.astype(bf16)

    row = pl.BlockSpec((tr, d), lambda i: (i, 0))
    return _pallas_call(
        body, out_shape=(_sds((1, HEAD), f32), _sds((s, d), f32), _sds((s, d), bf16)), grid=(s // tr,),
        in_specs=[row, row], out_specs=(pl.BlockSpec((1, HEAD), lambda i: (0, 0)), row, row),
        compiler_params=_cp(("arbitrary",)), name=f"loss_{s}x{d}",
    )(y, target)


def _place():
    return lax.axis_index("x"), lax.axis_index("y"), lax.axis_index("c")


def _tag(arrays):
    return "_".join("x".join(str(dd) for dd in a.shape) for a in arrays)


def _all_gather(shards):
    nw = len(shards)
    hbm = pl.BlockSpec(memory_space=pl.ANY)

    def body(*refs):
        x_refs, out_refs = refs[:nw], refs[nw:2 * nw]
        send_sems, recv_sems, local_sems = refs[2 * nw:]
        x, y, c = _place()
        me, sibling = (x, y, c), (x, y, 1 - c)
        chips = [(1 - x, y), (x, 1 - y), (1 - x, 1 - y)]

        def slot(w, place):
            px, py, pc = place
            return out_refs[w].at[4 * px + 2 * py + pc]

        def copy(k, w, block_of, to, from_input=False):
            return pltpu.make_async_remote_copy(
                src_ref=x_refs[w] if from_input else slot(w, block_of), dst_ref=slot(w, block_of),
                send_sem=send_sems.at[k, w], recv_sem=recv_sems.at[k, w], device_id=to, device_id_type=MESH)

        mine = [pltpu.make_async_copy(x_refs[w], slot(w, me), local_sems.at[w]) for w in range(nw)]
        for cp in mine:
            cp.start()
        first = []
        for w in range(nw):
            first.append(copy(0, w, me, sibling, from_input=True))
            first += [copy(1 + j, w, me, (*chip, c), from_input=True) for j, chip in enumerate(chips)]
        for cp in first:
            cp.start()
        passed = []
        for w in range(nw):
            for j, chip in enumerate(chips):
                copy(1 + j, w, (*chip, c), me).wait_recv()
                fwd = copy(4 + j, w, (*chip, c), sibling)
                fwd.start()
                passed.append(fwd)
        for w in range(nw):
            copy(0, w, sibling, me).wait_recv()
            for j, chip in enumerate(chips):
                copy(4 + j, w, (*chip, 1 - c), me).wait_recv()
        for cp in first + passed:
            cp.wait_send()
        for cp in mine:
            cp.wait()

    return _pallas_call(
        body, out_shape=tuple(_sds((N_DEV,) + a.shape, a.dtype) for a in shards), in_specs=[hbm] * nw, out_specs=tuple([hbm] * nw),
        scratch_shapes=[pltpu.SemaphoreType.DMA((7, nw)), pltpu.SemaphoreType.DMA((7, nw)), pltpu.SemaphoreType.DMA((nw,))],
        name=f"all_gather_{_tag(shards)}_{jnp.dtype(shards[0].dtype).name}",
    )(*shards)


def _swap_with_sibling(grads):
    nw = len(grads)
    nchip = N_DEV // 2
    hbm = pl.BlockSpec(memory_space=pl.ANY)

    def body(*refs):
        g_refs, got_refs = refs[:nw], refs[nw:2 * nw]
        send_sems, recv_sems = refs[2 * nw:]
        x, y, c = _place()
        copies = [pltpu.make_async_remote_copy(
            src_ref=g_refs[w].at[2 * k + (1 - c)], dst_ref=got_refs[w].at[k],
            send_sem=send_sems.at[w, k], recv_sem=recv_sems.at[w, k], device_id=(x, y, 1 - c), device_id_type=MESH)
            for w in range(nw) for k in range(nchip)]
        for cp in copies:
            cp.start()
        for cp in copies:
            cp.wait()

    return _pallas_call(
        body, out_shape=tuple(_sds((nchip,) + g.shape[1:], g.dtype) for g in grads), in_specs=[hbm] * nw, out_specs=tuple([hbm] * nw),
        scratch_shapes=[pltpu.SemaphoreType.DMA((nw, nchip)), pltpu.SemaphoreType.DMA((nw, nchip))],
        name=f"swap_sibling_{_tag(grads)}",
    )(*grads)


def _pair_sum(grad, got):
    nd, a, b = grad.shape
    nchip = nd // 2
    ta = _pick(a, (1024, 704, 512, 352, 256, 128, 64, 32, 16))

    def my_chip():
        return 2 * lax.axis_index("x") + lax.axis_index("y")

    def body(a_ref, b_ref, o_ref, land_ref):
        tot = (a_ref[...].astype(f32) + b_ref[...].astype(f32)).astype(o_ref.dtype)
        o_ref[...] = tot

        @pl.when(pl.program_id(1) == my_chip())
        def _():
            land_ref[...] = tot

    return _pallas_call(
        body, out_shape=(_sds(got.shape, grad.dtype), _sds(got.shape, grad.dtype)), grid=(a // ta, nchip),
        in_specs=[pl.BlockSpec((None, ta, b), lambda i, k: (2 * k + lax.axis_index("c"), i, 0)),
                  pl.BlockSpec((None, ta, b), lambda i, k: (k, i, 0))],
        out_specs=(pl.BlockSpec((None, ta, b), lambda i, k: (k, i, 0)),
                   pl.BlockSpec((None, ta, b), lambda i, k: (my_chip(), i, 0))),
        compiler_params=_cp(("parallel", "arbitrary")), name=f"pair_sum_{a}x{b}",
    )(grad, got)


_HBM = pl.BlockSpec(memory_space=pltpu.HBM)
_SEM = pl.BlockSpec(memory_space=pltpu.SEMAPHORE)
_ANY = pl.BlockSpec(memory_space=pl.ANY)
_DATAFLOW = pltpu.SideEffectType.DATAFLOW_SIDE_EFFECTING


def _in_hbm(a):
    return pltpu.with_memory_space_constraint(a, pltpu.HBM)


def _exchange_begin(bufs, nw, route, after, copies_of, n_copies, name):
    nb = len(bufs)

    def body(*refs):
        send_sems, recv_sems = refs[nb + 2], refs[nb + 3]
        for w in range(nw):
            for k, (src, dst, to) in enumerate(copies_of(w, refs[:nb])):
                pltpu.make_async_remote_copy(src_ref=src, dst_ref=dst, send_sem=send_sems.at[k * nw + w],
                                             recv_sem=recv_sems.at[k * nw + w], device_id=to, device_id_type=MESH).start()

    out = _pallas_call(
        body, name=name,
        out_shape=(pltpu.SemaphoreType.DMA((n_copies * nw,)), pltpu.SemaphoreType.DMA((n_copies * nw,)),
                   *[pltpu.HBM(a.shape, a.dtype) for a in bufs], pltpu.HBM(route.shape, route.dtype)),
        in_specs=[_HBM] * (nb + 1) + [_ANY], out_specs=(_SEM, _SEM, *[_HBM] * (nb + 1)),
        input_output_aliases={i: 2 + i for i in range(nb + 1)},
        compiler_params=pltpu.CompilerParams(has_side_effects=_DATAFLOW),
    )(*[_in_hbm(a) for a in bufs], _in_hbm(route), after)
    return (out[0], out[1], out[2:2 + nb], nw), out[2 + nb]


def _exchange_end(handle, after, copies_of, n_copies, name):
    send_sems, recv_sems, thru, nw = handle
    nb = len(thru)

    def body(*refs):
        send_sems, recv_sems = refs[nb], refs[nb + 1]
        for w in range(nw):
            for k, (src, dst, to) in enumerate(copies_of(w, refs[:nb])):
                cp = pltpu.make_async_remote_copy(src_ref=src, dst_ref=dst, send_sem=send_sems.at[k * nw + w],
                                                  recv_sem=recv_sems.at[k * nw + w], device_id=to, device_id_type=MESH)
                cp.wait_send()
                cp.wait_recv()

    out = _pallas_call(
        body, name=name, out_shape=tuple(pltpu.HBM(a.shape, a.dtype) for a in thru),
        in_specs=[_HBM] * nb + [_SEM, _SEM, _ANY], out_specs=tuple([_HBM] * nb),
        input_output_aliases={i: i for i in range(nb)},
        compiler_params=pltpu.CompilerParams(has_side_effects=_DATAFLOW),
    )(*thru, send_sems, recv_sems, after)
    return list(out)


def _my_slot():
    return 4 * lax.axis_index("x") + 2 * lax.axis_index("y") + lax.axis_index("c")


def _shard_into_land(w_all, idx):
    _, a, b = w_all.shape
    ta = next(cc for cc in (1024, 704, 512, 352, 256, 128, 64, 32, 16) if a % cc == 0 and (cc * b * 4 <= 2 ** 21 or cc == 16))

    def body(w_ref, o_ref):
        o_ref[...] = w_ref[...].astype(bf16)

    return _pallas_call(
        body, out_shape=_sds((N_DEV, a, b), bf16), grid=(a // ta,),
        in_specs=[pl.BlockSpec((None, ta, b), lambda i: (idx, i, 0))],
        out_specs=pl.BlockSpec((None, ta, b), lambda i: (_my_slot(), i, 0)),
        compiler_params=_cp(("parallel",)), name=f"shard_into_land_{a}x{b}_{idx}",
    )(w_all)


def _gather_copies(w, land_refs):
    x, y, c = _place()
    blk = land_refs[w].at[4 * x + 2 * y + c]
    return [(blk, blk, to) for to in ((x, y, 1 - c), (1 - x, y, c), (x, 1 - y, c), (1 - x, 1 - y, c))]


def _gather_begin(lands, route, after, tag):
    return _exchange_begin(lands, len(lands), route, after, _gather_copies, 4, f"gather_begin_{tag}")


def _gather_end(handle, after, tag):
    return _exchange_end(handle, after, _gather_copies, 4, f"gather_end_{tag}")


def _gather_pass_on(lands):
    nw = len(lands)

    def body(*refs):
        l_refs = refs[nw:2 * nw]
        send_sems, recv_sems = refs[2 * nw:]
        x, y, c = _place()
        copies = []
        for w in range(nw):
            for j, (px, py) in enumerate([(1 - x, y), (x, 1 - y), (1 - x, 1 - y)]):
                blk = l_refs[w].at[4 * px + 2 * py + c]
                copies.append(pltpu.make_async_remote_copy(
                    src_ref=blk, dst_ref=blk, send_sem=send_sems.at[j, w], recv_sem=recv_sems.at[j, w],
                    device_id=(x, y, 1 - c), device_id_type=MESH))
        for cp in copies:
            cp.start()
        for cp in copies:
            cp.wait_send()
        for w in range(nw):
            for j, (px, py) in enumerate([(1 - x, y), (x, 1 - y), (1 - x, 1 - y)]):
                blk = l_refs[w].at[4 * px + 2 * py + (1 - c)]
                pltpu.make_async_remote_copy(src_ref=blk, dst_ref=blk, send_sem=send_sems.at[j, w], recv_sem=recv_sems.at[j, w],
                                             device_id=(x, y, 1 - c), device_id_type=MESH).wait_recv()

    return _pallas_call(
        body, out_shape=tuple(_sds(a.shape, a.dtype) for a in lands), in_specs=[_ANY] * nw, out_specs=tuple([_ANY] * nw),
        input_output_aliases={w: w for w in range(nw)},
        scratch_shapes=[pltpu.SemaphoreType.DMA((3, nw)), pltpu.SemaphoreType.DMA((3, nw))],
        name=f"gather_pass_on_{_tag(lands)}",
    )(*lands)


def _scatter_copies(w, refs):
    x, y, c = _place()
    nw = len(refs) // 2
    dst = refs[nw + w].at[2 * x + y]
    return [(refs[w].at[2 * px + py], dst, (px, py, c)) for px, py in ((1 - x, y), (x, 1 - y), (1 - x, 1 - y))]


def _scatter_begin(psums, lands, route, after, tag):
    return _exchange_begin(list(psums) + list(lands), len(psums), route, after, _scatter_copies, 3, f"scatter_begin_{tag}")


def _scatter_end(handle, after, tag):
    return _exchange_end(handle, after, _scatter_copies, 3, f"scatter_end_{tag}")


def _adamw(parts, w_all, m_all, v_all, l, carried):
    nparts, a, b = parts.shape
    nl = w_all.shape[0]
    ta = next(cc for cc in (1024, 704, 512, 352, 256, 128, 64, 32, 16, 8) if a % cc == 0 and (cc * b * 4 <= 2 ** 20 or cc == 8))
    c1 = 1.0 / (1.0 - ADAM_B1 ** ADAM_STEP)
    c2 = 1.0 / (1.0 - ADAM_B2 ** ADAM_STEP)

    def body(p_ref, w_ref, m_ref, v_ref, *rest):
        g_out, d_out, m_out, v_out = rest[-4:]
        g = p_ref[0].astype(f32)
        for k in range(1, nparts):
            g = g + p_ref[k].astype(f32)
        m_new = ADAM_B1 * m_ref[...] + (1.0 - ADAM_B1) * g
        v_new = ADAM_B2 * v_ref[...] + (1.0 - ADAM_B2) * (g * g)
        m_hat = m_new * c1
        v_hat = v_new * c2
        g_out[...] = g
        d_out[...] = -ADAM_LR * (m_hat / (jnp.sqrt(v_hat) + ADAM_EPS) + ADAM_WD * w_ref[...])
        m_out[...] = m_new
        v_out[...] = v_new

    one = pl.BlockSpec((None, ta, b), lambda i: (l, i, 0))
    keep = [] if carried is None else [pl.BlockSpec(memory_space=pl.ANY)] * 4
    return _pallas_call(
        body, out_shape=tuple(_sds((nl, a, b), f32) for _ in range(4)), grid=(a // ta,),
        in_specs=[pl.BlockSpec((nparts, ta, b), lambda i: (0, i, 0)), one, one, one] + keep, out_specs=(one, one, one, one),
        input_output_aliases=({} if carried is None else {4 + q: q for q in range(4)}),
        compiler_params=_cp(("parallel",)), name=f"adamw_{nparts}x{nl}x{a}x{b}_{l}{'' if carried is None else '_carried'}",
    )(parts, w_all, m_all, v_all, *(carried or ()))


def _to_flat(arrays):
    flat = jnp.concatenate([a.reshape(-1).astype(f32) for a in arrays])
    rows = -(-flat.shape[0] // (8 * LANES)) * 8
    return jnp.pad(flat, (0, rows * LANES - flat.shape[0])).reshape(rows, LANES)


def _from_flat(flat, shapes):
    flat = flat.reshape(-1)
    out, off = [], 0
    for shp in shapes:
        n = 1
        for dd in shp:
            n *= dd
        out.append(flat[off:off + n].reshape(shp))
        off += n
    return out


def kernel(x, mem, g_mix, g_ffn, w_in_a, g_v_a, w_spatial, b_spatial, w_in_b, g_q_b, g_k_b, g_mem, w_mem_kv, g_mq, g_mk, w_out, w_gate_up, w_down, loss_target, m_g_mix, m_g_ffn, m_w_in_a, m_g_v_a, m_w_spatial, m_b_spatial, m_w_in_b, m_g_q_b, m_g_k_b, m_g_mem, m_w_mem_kv, m_g_mq, m_g_mk, m_w_out, m_w_gate_up, m_w_down, v_g_mix, v_g_ffn, v_w_in_a, v_g_v_a, v_w_spatial, v_b_spatial, v_w_in_b, v_g_q_b, v_g_k_b, v_g_mem, v_w_mem_kv, v_g_mq, v_g_mk, v_w_out, v_w_gate_up, v_w_down):
    given = dict(locals())
    depth = g_mix.shape[0]
    s, d = x.shape[1], x.shape[2]
    nm = mem.shape[1]
    t = d - MEM_WIDTH
    ff = w_down.shape[1] * N_DEV
    x0 = x.reshape(s, d)
    mem0 = mem.reshape(nm, d)
    target = loss_target.reshape(s, d)
    tables = _rope_tables(s)

    big_names = ("w_in", "w_mem_kv", "w_out", "w_gate_up", "w_down")

    def stacked_key(name, l):
        if name == "w_in":
            return ("w_in_a" if l % 2 == 0 else "w_in_b"), l // 2
        return name, l

    n_mix = 3

    def own_shards(l):
        return [_shard_into_land(given[key], idx) for key, idx in (stacked_key(name, l) for name in big_names)]

    def gather_start(l, route, after):
        lands = cast_shards[l]
        h_mix, route = _gather_begin(lands[:n_mix], route, after, f"mix{l}")
        h_ffn, route = _gather_begin(lands[n_mix:], route, after, f"ffn{l}")
        return h_mix, h_ffn, route

    def gather_finish(handle, after, tag):
        return _gather_pass_on(_gather_end(handle, after, tag))

    saved = []
    xc = x0
    cast_shards = {0: own_shards(0)}
    h_mix, h_ffn, _ = gather_start(0, g_mix[0].reshape(1, d), mem0)
    for l in range(1, depth):
        cast_shards[l] = own_shards(l)
    w_mix = gather_finish(h_mix, cast_shards[depth - 1][0], "mix0")
    w_ffn = None
    for l in range(depth):
        is_a = l % 2 == 0
        g_in, g_kv, g_out = w_mix
        w_kv, w_o = (g.reshape(-1, g.shape[2]) for g in (g_kv, g_out))
        qblk = (N_DEV * g_in.shape[2] - MEM_WIDTH) // MEM_WIDTH

        gm_row, gf_row, gmem_row = g_mix[l].reshape(1, d), g_ffn[l].reshape(1, d), g_mem[l].reshape(1, d)
        gmq_row, gmk_row = g_mq[l].reshape(1, HEAD), g_mk[l].reshape(1, HEAD)
        if l + 1 < depth:
            next_mix, next_ffn, gm_row = gather_start(l + 1, gm_row, g_in)
        h = _rms_fwd(xc, gm_row)
        z = _mm_cols_fwd(h, g_in, f32)
        if is_a:
            ia = l // 2
            mix = dict(g_v=g_v_a[ia].reshape(1, t), w_s=w_spatial[ia], b_t=b_spatial[ia].T)
            cat = _gmlp_fwd(z, mix["g_v"], mix["w_s"], mix["b_t"])
        else:
            ib = l // 2
            mix = dict(g_q=g_q_b[ib].reshape(1, HEAD), g_k=g_k_b[ib].reshape(1, HEAD))
            q, k, v = _attn_prep_fwd(z, mix["g_q"], mix["g_k"], tables, t)
            cat, lse = _flash_fwd(q, k, v)
            mix.update(q=q, k=k, v=v, lse=lse)
        hm = _rms_fwd(mem0, gmem_row)
        kv = _matmul(hm, w_kv)
        cat = _mem_fwd(z, qblk, kv, gmq_row, gmk_row, cat)
        x1 = _matmul(cat, w_o, res=xc)
        if l == 0:
            w_ffn = gather_finish(h_ffn, x1, "ffn0")
        g_gu, g_dn = w_ffn
        w_dn = g_dn.reshape(-1, g_dn.shape[2])
        h2 = _rms_fwd(x1, gf_row)
        gu = _mm_cols_fwd(h2, g_gu, bf16)
        act = _swiglu_fwd(gu)
        x2 = _matmul(act, w_dn, res=x1)
        saved.append(dict(x=xc, h=h, z=z, mix=mix, cat=cat, hm=hm, kv=kv, x1=x1, h2=h2, gu=gu, act=act, qblk=qblk,
                          w=(g_in, w_kv, w_o, g_gu, w_dn), rows=(gm_row, gf_row, gmem_row, gmq_row, gmk_row)))
        if l + 1 < depth:
            w_mix = gather_finish(next_mix, x2, f"mix{l + 1}")
            w_ffn = gather_finish(next_ffn, x2, f"ffn{l + 1}")
        xc = x2

    loss_row, dy, dy_b = _loss_head(xc, target)
    loss = lax.psum(loss_row[0, 0], ("x", "y", "c"))

    small = {n: [None] * given[n].shape[0] for n in ("g_mix", "g_ffn", "g_v_a", "w_spatial", "b_spatial", "g_q_b", "g_k_b", "g_mem", "g_mq", "g_mk")}
    big_out = {}

    def scatter_start(grads, route, after, tag):
        got = _swap_with_sibling(grads)
        sums = [_pair_sum(g, r) for g, r in zip(grads, got)]
        return _scatter_begin([p for p, _ in sums], [q for _, q in sums], route, after, tag)

    def scatter_finish(handle, names, l, after, tag):
        arrived = _scatter_end(handle, after, tag)[len(names):]
        for name, parts in zip(names, arrived):
            key, idx = stacked_key(name, l)
            big_out[key] = _adamw(parts, given[key], given["m_" + key], given["v_" + key], idx, big_out.get(key))

    pend_mix = None
    dx, dx_b = dy, dy_b
    for l in reversed(range(depth)):
        sv = saved[l]
        is_a = l % 2 == 0
        g_in, w_kv, w_o, g_gu, w_dn = sv["w"]
        gm_row, gf_row, gmem_row, gmq_row, gmk_row = sv["rows"]
        mix = sv["mix"]
        dw_dn = _matmul(sv["act"], dx_b, ta=True, out_dtype=bf16)
        dact = _matmul(dx_b, w_dn, tb=True, out_dtype=bf16)
        dgu = _swiglu_bwd(sv["gu"], dact)
        dw_gu = _mm_cols_wgrad(sv["h2"], dgu, g_gu.shape[2])
        dh2 = _mm_cols_dgrad(dgu, g_gu)
        dx1, dx1_b, dgf = _rms_bwd(sv["x1"], gf_row, dh2, dx)
        small["g_ffn"][l] = dgf.reshape(d)
        pend_ffn, dx1_b = scatter_start([dw_gu, dw_dn.reshape(N_DEV, -1, d)], dx1_b, dx1, f"ffn{l}")
        if pend_mix is not None:
            scatter_finish(*pend_mix, dx1_b, f"mix{l + 1}")
        dw_o = _matmul(sv["cat"], dx1_b, ta=True, out_dtype=bf16)
        dcat = _matmul(dx1_b, w_o, tb=True, out_dtype=bf16)
        if is_a:
            dz, dws, dbt, dgv = _gmlp_bwd(sv["z"], mix["g_v"], mix["w_s"], mix["b_t"], dcat)
            small["w_spatial"][l // 2], small["b_spatial"][l // 2], small["g_v_a"][l // 2] = dws, dbt.T, dgv.reshape(t)
        else:
            dq, dk, dv = _flash_bwd(mix["q"], mix["k"], mix["v"], sv["cat"], dcat, mix["lse"])
            dz, dgq, dgk = _attn_prep_bwd(sv["z"], mix["g_q"], mix["g_k"], tables, dq, dk, dv, t)
            small["g_q_b"][l // 2], small["g_k_b"][l // 2] = dgq.reshape(HEAD), dgk.reshape(HEAD)
        dz, dkn, dvm, dgmq = _mem_bwd(sv["z"], sv["qblk"], sv["kv"], gmq_row, gmk_row, dcat, dz)
        dkv, dgmk = _mem_kv_bwd(sv["kv"], gmk_row, dkn, dvm)
        dw_kv = _matmul(sv["hm"], dkv, ta=True, out_dtype=bf16)
        dhm = _matmul(dkv, w_kv, tb=True)
        small["g_mem"][l] = _rms_bwd(mem0, gmem_row, dhm, None).reshape(d)
        small["g_mq"][l] = dgmq.reshape(HEAD)
        small["g_mk"][l] = dgmk.reshape(HEAD)
        dw_in = _mm_cols_wgrad(sv["h"], dz, g_in.shape[2])
        dh = _mm_cols_dgrad(dz, g_in)
        dx, dx_b, dgm = _rms_bwd(sv["x"], gm_row, dh, dx1)
        small["g_mix"][l] = dgm.reshape(d)

        handle, dx_b = scatter_start([dw_in] + [dw.reshape(N_DEV, -1, dw.shape[1]) for dw in (dw_kv, dw_o)], dx_b, dx, f"mix{l}")
        pend_mix = (handle, big_names[:n_mix], l)
        scatter_finish(pend_ffn, big_names[n_mix:], l, dx_b, f"ffn{l}")
    scatter_finish(*pend_mix, dx, "mix0")

    small_names = tuple(small)
    small_grads = [jnp.stack(small[n]) for n in small_names]
    small_shapes = [g.shape for g in small_grads]
    (all_parts,) = _all_gather([_to_flat(small_grads)])
    souts = _adamw(all_parts, *[_to_flat([given[p + n] for n in small_names])[None] for p in ("", "m_", "v_")], 0, None)
    small_out = dict(zip(small_names, zip(*[_from_flat(flat, small_shapes) for flat in souts])))

    weights = ("g_mix", "g_ffn", "w_in_a", "g_v_a", "w_spatial", "b_spatial", "w_in_b", "g_q_b", "g_k_b", "g_mem", "w_mem_kv",
               "g_mq", "g_mk", "w_out", "w_gate_up", "w_down")
    results = {n: (small_out[n] if n in small_out else big_out[n]) for n in weights}
    grad_x = dx.reshape(1, s, d)
    return (loss, grad_x, *[results[n][kind] for kind in range(4) for n in weights])
```

```python
import functools

import jax
import jax.numpy as jnp
from jax import lax
from jax.experimental import pallas as pl
from jax.experimental.pallas import tpu as pltpu

f32 = jnp.float32
bf16 = jnp.bfloat16

HEAD = 128
CHUNK = 128
GRID_W = 64
MEM_HEADS = 4
KV_HEADS = 4
MEM_WIDTH = MEM_HEADS * HEAD
KV_WIDTH = KV_HEADS * HEAD
ROPE_THETA = 10000.0
ROPE_PAIRS = HEAD // 4
EPS = 1e-6
SCALE = HEAD ** -0.5
LOG2E = 1.4426950408889634
N_DEV = 8
LANES = 1024
VMEM_LIMIT = 56 * 1024 * 1024

ADAM_LR, ADAM_B1, ADAM_B2, ADAM_EPS, ADAM_WD, ADAM_STEP = 0.001, 0.9, 0.999, 1e-08, 0.01, 10

MESH = pl.DeviceIdType.MESH
_pallas_call = pl.pallas_call


def _pick(dim, cands):
    for c in cands:
        if dim % c == 0:
            return c
    return dim


def _cp(sem):
    return pltpu.CompilerParams(dimension_semantics=sem, vmem_limit_bytes=VMEM_LIMIT)


def _sds(shape, dtype):
    return jax.ShapeDtypeStruct(shape, dtype)


def _dot(a, b, ca, cb):
    return lax.dot_general(a, b, (((ca,), (cb,)), ((), ())), preferred_element_type=f32)


def _gelu(z):
    return 0.5 * z * (1.0 + lax.erf(z * 0.7071067811865476))


def _gelu_grad(z):
    return 0.5 * (1.0 + lax.erf(z * 0.7071067811865476)) + z * jnp.exp(-0.5 * z * z) * 0.3989422804014327


def _rot(x, sin_a, sin_b):
    return pltpu.roll(x, 96, 1) * sin_a + pltpu.roll(x, 32, 1) * sin_b


def _matmul(a, b, *, ta=False, tb=False, out_dtype=f32, res=None, tm=None, tn=None, tk=None):
    assert a.dtype == bf16 and b.dtype == bf16
    kdim, m = a.shape if ta else a.shape[::-1]
    n, k2 = b.shape if tb else b.shape[::-1]
    assert kdim == k2, (a.shape, b.shape, ta, tb)
    tm = tm or _pick(m, (1024, 1408, 512, 256, 128))
    tn = tn or _pick(n, (1024, 1408, 512, 256, 128))
    if tk is None:
        tk = kdim if kdim <= 2048 else _pick(kdim, (2816, 1024, 512, 256, 128))
    nk = kdim // tk
    ca, cb = (0 if ta else 1), (1 if tb else 0)
    has_res = res is not None

    def body(*refs):
        a_ref, b_ref = refs[0], refs[1]
        r_ref = refs[2] if has_res else None
        o_ref = refs[3] if has_res else refs[2]
        prod = _dot(a_ref[...], b_ref[...], ca, cb)
        if nk == 1:
            if has_res:
                prod = prod + r_ref[...]
            o_ref[...] = prod.astype(o_ref.dtype)
        else:
            acc = refs[-1]
            k = pl.program_id(2)

            @pl.when(k == 0)
            def _():
                acc[...] = prod

            @pl.when(k > 0)
            def _():
                acc[...] += prod

            @pl.when(k == nk - 1)
            def _():
                out = acc[...]
                if has_res:
                    out = out + r_ref[...]
                o_ref[...] = out.astype(o_ref.dtype)

    a_spec = pl.BlockSpec((tk, tm), lambda i, j, k: (k, i)) if ta else pl.BlockSpec((tm, tk), lambda i, j, k: (i, k))
    b_spec = pl.BlockSpec((tn, tk), lambda i, j, k: (j, k)) if tb else pl.BlockSpec((tk, tn), lambda i, j, k: (k, j))
    o_spec = pl.BlockSpec((tm, tn), lambda i, j, k: (i, j))
    in_specs = [a_spec, b_spec] + ([o_spec] if has_res else [])
    args = (a, b) + ((res,) if has_res else ())
    mode = ("t" if ta else "n") + ("t" if tb else "n")
    return _pallas_call(
        body, out_shape=_sds((m, n), out_dtype), grid=(m // tm, n // tn, nk),
        in_specs=in_specs, out_specs=o_spec,
        scratch_shapes=([pltpu.VMEM((tm, tn), f32)] if nk > 1 else []),
        compiler_params=_cp(("parallel", "parallel", "arbitrary")),
        name=f"mm_{mode}_{m}x{kdim}x{n}{'_res' if has_res else ''}_{jnp.dtype(out_dtype).name}",
    )(*args)


def _shards_per_step(n, pair_bytes=0):
    p = 2 if (n % 128 != 0 or 0 < 2 * pair_bytes <= VMEM_LIMIT // 2) else 1
    assert (p * n) % 128 == 0 and N_DEV % p == 0
    return p


def _lane_pieces(v, p, n):
    return [v] if p == 1 else [v[:, q * n:(q + 1) * n] for q in range(p)]


def _mm_cols_fwd(a, g, out_dtype):
    m, kdim = a.shape
    nd, k2, n = g.shape
    assert kdim == k2 and a.dtype == bf16 and g.dtype == bf16
    p = _shards_per_step(n)
    tm = _pick(m, (1024, 512, 256, 128))

    def body(a_ref, g_ref, o_ref):
        av = a_ref[...]
        parts = [_dot(av, g_ref[q], 1, 0) for q in range(p)]
        out = parts[0] if p == 1 else jnp.concatenate(parts, axis=1)
        o_ref[...] = out.astype(o_ref.dtype)

    return _pallas_call(
        body, out_shape=_sds((m, nd * n), out_dtype), grid=(m // tm, nd // p),
        in_specs=[pl.BlockSpec((tm, kdim), lambda i, j: (i, 0)), pl.BlockSpec((p, kdim, n), lambda i, j: (j, 0, 0))],
        out_specs=pl.BlockSpec((tm, p * n), lambda i, j: (i, j)),
        compiler_params=_cp(("parallel", "arbitrary")), name=f"mm_cols_fwd_{m}x{kdim}x{nd * n}_{jnp.dtype(out_dtype).name}",
    )(a, g)


def _pick_part(refs, nparts, step, per):
    val = refs[0][...]
    for hh in range(1, nparts):
        val = jnp.where(step >= hh * per, refs[hh][...], val)
    return val


def _part_step(step, hh, per):
    return jnp.clip(step - hh * per, 0, per - 1)


def _mm_cols_dgrad(dz, g):
    nd, kdim, n = g.shape
    p = _shards_per_step(n, pair_bytes=2 * kdim * n * 2)
    nj = nd // p
    parts = dz if isinstance(dz, tuple) else (dz,)
    m = parts[0].shape[0]
    nn = sum(part.shape[1] for part in parts)
    assert nn == nd * n and all(part.dtype == bf16 for part in parts) and g.dtype == bf16
    tm = _pick(m, (512, 256, 128))
    nparts = len(parts)
    per = nj // nparts

    def body(*refs):
        g_ref, o_ref, acc = refs[nparts:]
        j = pl.program_id(1)
        tot = None
        for q, piece in enumerate(_lane_pieces(_pick_part(refs, nparts, j, per), p, n)):
            dd = _dot(piece, g_ref[q], 1, 1)
            tot = dd if tot is None else tot + dd

        @pl.when(j == 0)
        def _():
            acc[...] = tot

        @pl.when(j > 0)
        def _():
            acc[...] += tot

        @pl.when(j == nj - 1)
        def _():
            o_ref[...] = acc[...]

    return _pallas_call(
        body, out_shape=_sds((m, kdim), f32), grid=(m // tm, nj),
        in_specs=[pl.BlockSpec((tm, p * n), (lambda hh: lambda i, j: (i, _part_step(j, hh, per)))(hh)) for hh in range(nparts)]
        + [pl.BlockSpec((p, kdim, n), lambda i, j: (j, 0, 0))],
        out_specs=pl.BlockSpec((tm, kdim), lambda i, j: (i, 0)),
        scratch_shapes=[pltpu.VMEM((tm, kdim), f32)],
        compiler_params=_cp(("parallel", "arbitrary")), name=f"mm_cols_dgrad_{m}x{nn}x{kdim}_{nparts}",
    )(*parts, g)


def _mm_cols_wgrad(a, dz, n):
    s, kdim = a.shape
    parts = dz if isinstance(dz, tuple) else (dz,)
    nparts = len(parts)
    nd = sum(part.shape[1] for part in parts) // n
    assert a.dtype == bf16 and all(part.dtype == bf16 for part in parts)
    p = _shards_per_step(n)
    per = nd // p // nparts
    tkw = _pick(kdim, (1024, 512, 256, 128))
    ts = _pick(s, (2048, 1024, 512, 256, 128))
    ns = s // ts

    def body(a_ref, *refs):
        o_ref, acc = refs[nparts:]
        si = pl.program_id(2)
        av = a_ref[...]
        prods = [_dot(av, piece, 0, 0) for piece in _lane_pieces(_pick_part(refs, nparts, pl.program_id(1), per), p, n)]

        @pl.when(si == 0)
        def _():
            for q in range(p):
                acc[q] = prods[q]

        @pl.when(si > 0)
        def _():
            for q in range(p):
                acc[q] += prods[q]

        @pl.when(si == ns - 1)
        def _():
            o_ref[...] = acc[...].astype(bf16)

    return _pallas_call(
        body, out_shape=_sds((nd, kdim, n), bf16), grid=(kdim // tkw, nd // p, ns),
        in_specs=[pl.BlockSpec((ts, tkw), lambda i, j, k: (k, i))]
        + [pl.BlockSpec((ts, p * n), (lambda hh: lambda i, j, k: (k, _part_step(j, hh, per)))(hh)) for hh in range(nparts)],
        out_specs=pl.BlockSpec((p, tkw, n), lambda i, j, k: (j, i, 0)),
        scratch_shapes=[pltpu.VMEM((p, tkw, n), f32)],
        compiler_params=_cp(("parallel", "parallel", "arbitrary")), name=f"mm_cols_wgrad_{kdim}x{s}x{nd * n}_{nparts}",
    )(a, *parts)


def _rms_fwd(x, g_row):
    s, d = x.shape
    tr = _pick(s, (512, 256, 128))

    def body(x_ref, g_ref, o_ref):
        xv = x_ref[...]
        r = lax.rsqrt(jnp.mean(xv * xv, axis=-1, keepdims=True) + EPS)
        o_ref[...] = (xv * r * g_ref[...]).astype(bf16)

    return _pallas_call(
        body, out_shape=_sds((s, d), bf16), grid=(s // tr,),
        in_specs=[pl.BlockSpec((tr, d), lambda i: (i, 0)), pl.BlockSpec((1, d), lambda i: (0, 0))],
        out_specs=pl.BlockSpec((tr, d), lambda i: (i, 0)),
        compiler_params=_cp(("parallel",)), name=f"rms_fwd_{s}x{d}",
    )(x, g_row)


def _rms_bwd(x, g_row, dh, dres):
    s, d = x.shape
    tr = _pick(s, (512, 256, 128))
    with_dx = dres is not None

    def body(*refs):
        if with_dx:
            x_ref, g_ref, dh_ref, dres_ref, dx_ref, dxb_ref, dg_ref = refs
        else:
            x_ref, g_ref, dh_ref, dg_ref = refs

        @pl.when(pl.program_id(0) == 0)
        def _():
            dg_ref[...] = jnp.zeros_like(dg_ref)

        xv = x_ref[...]
        r = lax.rsqrt(jnp.mean(xv * xv, axis=-1, keepdims=True) + EPS)
        xh = xv * r
        dy = dh_ref[...].astype(f32)
        dg_ref[...] += jnp.sum(dy * xh, axis=0, keepdims=True)
        if with_dx:
            gy = dy * g_ref[...]
            dx = dres_ref[...] + r * (gy - xh * jnp.mean(gy * xh, axis=-1, keepdims=True))
            dx_ref[...] = dx
            dxb_ref[...] = dx.astype(bf16)

    row = pl.BlockSpec((tr, d), lambda i: (i, 0))
    vec = pl.BlockSpec((1, d), lambda i: (0, 0))
    if with_dx:
        return _pallas_call(
            body, out_shape=(_sds((s, d), f32), _sds((s, d), bf16), _sds((1, d), f32)), grid=(s // tr,),
            in_specs=[row, vec, row, row], out_specs=(row, row, vec),
            compiler_params=_cp(("arbitrary",)), name=f"rms_bwd_{s}x{d}",
        )(x, g_row, dh, dres)
    return _pallas_call(
        body, out_shape=_sds((1, d), f32), grid=(s // tr,),
        in_specs=[row, vec, row], out_specs=vec,
        compiler_params=_cp(("arbitrary",)), name=f"rms_bwd_gain_{s}x{d}",
    )(x, g_row, dh)


def _gmlp_rows(s):
    return CHUNK * (2 if (s // CHUNK) % 2 == 0 else 1)


def _gmlp_fwd(z, g_v, w_s, b_t):
    s = z.shape[0]
    t = g_v.shape[1]
    ng = t // HEAD
    rb = _gmlp_rows(s)

    def body(z_ref, gv_ref, ws_ref, bt_ref, o_ref):
        for ci in range(rb // CHUNK):
            lo = ci * CHUNK
            a = _gelu(z_ref[lo:lo + CHUNK, :])
            u, vv = a[:, :t], a[:, t:]
            r = lax.rsqrt(jnp.mean(vv * vv, axis=-1, keepdims=True) + EPS)
            vn = (vv * r * gv_ref[...]).astype(bf16)
            for g in range(ng):
                cs = slice(g * HEAD, (g + 1) * HEAD)
                sg = _dot(ws_ref[g].astype(bf16), vn[:, cs], 1, 0) + bt_ref[:, g:g + 1]
                o_ref[lo:lo + CHUNK, cs] = (u[:, cs] * sg).astype(bf16)

    return _pallas_call(
        body, out_shape=_sds((s, t + MEM_WIDTH), bf16), grid=(s // rb,),
        in_specs=[pl.BlockSpec((rb, 2 * t), lambda i: (i, 0)), pl.BlockSpec((1, t), lambda i: (0, 0)),
                  pl.BlockSpec((ng, CHUNK, CHUNK), lambda i: (0, 0, 0)), pl.BlockSpec((CHUNK, ng), lambda i: (0, 0))],
        out_specs=pl.BlockSpec((rb, t), lambda i: (i, 0)),
        compiler_params=_cp(("parallel",)), name=f"gmlp_fwd_{s}",
    )(z, g_v, w_s, b_t)


def _gmlp_bwd(z, g_v, w_s, b_t, dtok):
    s = z.shape[0]
    t = g_v.shape[1]
    ng = t // HEAD
    rb = _gmlp_rows(s)
    nsteps = s // rb

    def body(z_ref, gv_ref, ws_ref, bt_ref, dt_ref, dz_ref, dws_ref, dbt_ref, dgv_ref, ds_acc):
        step = pl.program_id(0)

        @pl.when(step == 0)
        def _():
            dws_ref[...] = jnp.zeros_like(dws_ref)
            dgv_ref[...] = jnp.zeros_like(dgv_ref)
            ds_acc[...] = jnp.zeros_like(ds_acc)

        for ci in range(rb // CHUNK):
            lo = ci * CHUNK
            zz = z_ref[lo:lo + CHUNK, :]
            a = _gelu(zz)
            u, vv = a[:, :t], a[:, t:]
            r = lax.rsqrt(jnp.mean(vv * vv, axis=-1, keepdims=True) + EPS)
            vh = vv * r
            vn = (vh * gv_ref[...]).astype(bf16)
            dtok = dt_ref[lo:lo + CHUNK, :].astype(f32)
            ds = dtok * u
            ds_acc[...] += ds
            dsb = ds.astype(bf16)
            du_parts, dvn_parts = [], []
            for g in range(ng):
                cs = slice(g * HEAD, (g + 1) * HEAD)
                wg = ws_ref[g].astype(bf16)
                sg = _dot(wg, vn[:, cs], 1, 0) + bt_ref[:, g:g + 1]
                du_parts.append(dtok[:, cs] * sg)
                dws_ref[g] += _dot(dsb[:, cs], vn[:, cs], 1, 1)
                dvn_parts.append(_dot(wg, dsb[:, cs], 0, 0))
            dvn = jnp.concatenate(dvn_parts, axis=1)
            dgv_ref[...] += jnp.sum(dvn * vh, axis=0, keepdims=True)
            gy = dvn * gv_ref[...]
            dvv = r * (gy - vh * jnp.mean(gy * vh, axis=-1, keepdims=True))
            da = jnp.concatenate(du_parts + [dvv], axis=1)
            dz_ref[lo:lo + CHUNK, :] = (da * _gelu_grad(zz)).astype(bf16)

        @pl.when(step == nsteps - 1)
        def _():
            for g in range(ng):
                dbt_ref[:, g:g + 1] = jnp.sum(ds_acc[:, g * HEAD:(g + 1) * HEAD], axis=1, keepdims=True)

    return _pallas_call(
        body,
        out_shape=(_sds((s, z.shape[1]), bf16), _sds((ng, CHUNK, CHUNK), f32), _sds((CHUNK, ng), f32), _sds((1, t), f32)),
        grid=(nsteps,),
        in_specs=[pl.BlockSpec((rb, 2 * t), lambda i: (i, 0)), pl.BlockSpec((1, t), lambda i: (0, 0)),
                  pl.BlockSpec((ng, CHUNK, CHUNK), lambda i: (0, 0, 0)), pl.BlockSpec((CHUNK, ng), lambda i: (0, 0)),
                  pl.BlockSpec((rb, t), lambda i: (i, 0))],
        out_specs=(pl.BlockSpec((rb, 2 * t), lambda i: (i, 0)), pl.BlockSpec((ng, CHUNK, CHUNK), lambda i: (0, 0, 0)),
                   pl.BlockSpec((CHUNK, ng), lambda i: (0, 0)), pl.BlockSpec((1, t), lambda i: (0, 0))),
        scratch_shapes=[pltpu.VMEM((CHUNK, t), f32)],
        compiler_params=_cp(("arbitrary",)), name=f"gmlp_bwd_{s}",
    )(z, g_v, w_s, b_t, dtok)


def _rope_tables(s):
    n_rows = s // GRID_W
    rows = jnp.broadcast_to(jnp.arange(n_rows)[:, None], (n_rows, GRID_W)).reshape(s)
    cols = jnp.broadcast_to(jnp.arange(GRID_W)[None, :], (n_rows, GRID_W)).reshape(s)
    freqs = ROPE_THETA ** (-jnp.arange(ROPE_PAIRS, dtype=f32) / ROPE_PAIRS)
    ang_r = rows.astype(f32)[:, None] * freqs
    ang_c = cols.astype(f32)[:, None] * freqs
    ang = jnp.concatenate([ang_r, ang_r, ang_c, ang_c], axis=-1)
    cos, sin = jnp.cos(ang), jnp.sin(ang)
    first = (jnp.arange(HEAD) % (2 * ROPE_PAIRS)) < ROPE_PAIRS
    return cos, jnp.where(first, -sin, 0.0), jnp.where(first, 0.0, sin)


def _attn_prep_fwd(z, g_q, g_k, tables, t):
    s = z.shape[0]
    tr = _pick(s, (256, 128))
    nq = t // HEAD
    width = t + 2 * KV_WIDTH

    def body(z_ref, gq_ref, gk_ref, cos_ref, sa_ref, sb_ref, q_ref, k_ref, v_ref):
        cos, sa, sb = cos_ref[...], sa_ref[...], sb_ref[...]
        for h in range(nq + KV_HEADS):
            cs = slice(h * HEAD, (h + 1) * HEAD)
            xv = z_ref[:, cs]
            r = lax.rsqrt(jnp.mean(xv * xv, axis=-1, keepdims=True) + EPS)
            xn = xv * r * (gq_ref[...] if h < nq else gk_ref[...])
            y = xn * cos + _rot(xn, sa, sb)
            if h < nq:
                q_ref[:, cs] = (y * (SCALE * LOG2E)).astype(bf16)
            else:
                k_ref[:, (h - nq) * HEAD:(h - nq + 1) * HEAD] = y.astype(bf16)
        v_ref[...] = z_ref[:, t + KV_WIDTH:width].astype(bf16)

    row = lambda w: pl.BlockSpec((tr, w), lambda i: (i, 0))
    vec = pl.BlockSpec((1, HEAD), lambda i: (0, 0))
    return _pallas_call(
        body, out_shape=(_sds((s, t), bf16), _sds((s, KV_WIDTH), bf16), _sds((s, KV_WIDTH), bf16)), grid=(s // tr,),
        in_specs=[row(width), vec, vec, row(HEAD), row(HEAD), row(HEAD)],
        out_specs=(row(t), row(KV_WIDTH), row(KV_WIDTH)),
        compiler_params=_cp(("parallel",)), name=f"attn_prep_fwd_{s}",
    )(z, g_q, g_k, *tables)


def _attn_prep_bwd(z, g_q, g_k, tables, dq, dk, dv, t):
    s = z.shape[0]
    tr = _pick(s, (256, 128))
    nq = t // HEAD
    width = t + 2 * KV_WIDTH

    def body(z_ref, gq_ref, gk_ref, cos_ref, sa_ref, sb_ref, dq_ref, dk_ref, dv_ref, dz_ref, dgq_ref, dgk_ref):
        @pl.when(pl.program_id(0) == 0)
        def _():
            dgq_ref[...] = jnp.zeros_like(dgq_ref)
            dgk_ref[...] = jnp.zeros_like(dgk_ref)

        cos, sa, sb = cos_ref[...], sa_ref[...], sb_ref[...]
        for h in range(nq + KV_HEADS):
            cs = slice(h * HEAD, (h + 1) * HEAD)
            xv = z_ref[:, cs]
            r = lax.rsqrt(jnp.mean(xv * xv, axis=-1, keepdims=True) + EPS)
            xh = xv * r
            if h < nq:
                dy, g_ref, dg_ref = dq_ref[:, cs], gq_ref, dgq_ref
            else:
                dy, g_ref, dg_ref = dk_ref[:, (h - nq) * HEAD:(h - nq + 1) * HEAD], gk_ref, dgk_ref
            dy = dy.astype(f32)
            dxn = dy * cos - _rot(dy, sa, sb)
            dg_ref[...] += jnp.sum(dxn * xh, axis=0, keepdims=True)
            gy = dxn * g_ref[...]
            dz_ref[:, cs] = (r * (gy - xh * jnp.mean(gy * xh, axis=-1, keepdims=True))).astype(bf16)
        dz_ref[:, t + KV_WIDTH:width] = dv_ref[...].astype(bf16)

    row = lambda w: pl.BlockSpec((tr, w), lambda i: (i, 0))
    vec = pl.BlockSpec((1, HEAD), lambda i: (0, 0))
    return _pallas_call(
        body, out_shape=(_sds((s, z.shape[1]), bf16), _sds((1, HEAD), f32), _sds((1, HEAD), f32)), grid=(s // tr,),
        in_specs=[row(width), vec, vec, row(HEAD), row(HEAD), row(HEAD), row(t), row(KV_WIDTH), row(KV_WIDTH)],
        out_specs=(row(width), vec, vec),
        compiler_params=_cp(("arbitrary",)), name=f"attn_prep_bwd_{s}",
    )(z, g_q, g_k, *tables, dq, dk, dv)


def _flash_tiles(s):
    return _pick(s, (512, 256, 128)), _pick(s, (1024, 512, 256, 128))


def _stack_heads(ref, grp):
    return jnp.concatenate([ref[:, g * HEAD:(g + 1) * HEAD] for g in range(grp)], axis=0)


def _flash_fwd(q, k, v):
    s, t = q.shape
    grp = t // KV_WIDTH
    tq, tk = _flash_tiles(s)
    nkv = s // tk
    rows = grp * tq

    def body(q_ref, k_ref, v_ref, o_ref, lse_ref, m_sc, acc_sc):
        ki = pl.program_id(2)

        @pl.when(ki == 0)
        def _():
            m_sc[...] = jnp.full(m_sc.shape, -jnp.inf, f32)
            acc_sc[...] = jnp.zeros_like(acc_sc)

        qs = _stack_heads(q_ref, grp)
        sc = _dot(qs, k_ref[...], 1, 1)
        m_prev = m_sc[...]
        m_new = jnp.maximum(m_prev, jnp.max(sc, axis=-1, keepdims=True))
        alpha = jnp.exp2(m_prev - m_new)
        p = jnp.exp2((sc - m_new).astype(bf16))
        v1 = jnp.concatenate([v_ref[...], jnp.ones((tk, HEAD), bf16)], axis=1)
        acc_sc[...] = alpha * acc_sc[...] + _dot(p, v1, 1, 0)
        m_sc[...] = m_new

        @pl.when(ki == nkv - 1)
        def _():
            acc = acc_sc[...]
            l = acc[:, HEAD:HEAD + 1]
            o = acc[:, :HEAD] / l
            for g in range(grp):
                o_ref[:, g * HEAD:(g + 1) * HEAD] = o[g * tq:(g + 1) * tq].astype(bf16)
            lse_ref[0] = jnp.broadcast_to(m_sc[...] + jnp.log(l) * LOG2E, (rows, HEAD))

    return _pallas_call(
        body, out_shape=(_sds((s, t + MEM_WIDTH), bf16), _sds((KV_HEADS, grp * s, HEAD), f32)), grid=(KV_HEADS, s // tq, nkv),
        in_specs=[pl.BlockSpec((tq, grp * HEAD), lambda h, i, j: (i, h)), pl.BlockSpec((tk, HEAD), lambda h, i, j: (j, h)),
                  pl.BlockSpec((tk, HEAD), lambda h, i, j: (j, h))],
        out_specs=(pl.BlockSpec((tq, grp * HEAD), lambda h, i, j: (i, h)), pl.BlockSpec((1, rows, HEAD), lambda h, i, j: (h, i, 0))),
        scratch_shapes=[pltpu.VMEM((rows, 1), f32), pltpu.VMEM((rows, 2 * HEAD), f32)],
        compiler_params=_cp(("parallel", "parallel", "arbitrary")), name=f"flash_fwd_{s}",
    )(q, k, v)


def _flash_delta(o, do, t):
    s = o.shape[0]
    grp = t // KV_WIDTH
    tq, _ = _flash_tiles(s)
    rows = grp * tq

    def body(o_ref, do_ref, d_ref):
        for g in range(grp):
            cs = slice(g * HEAD, (g + 1) * HEAD)
            dd = jnp.sum(o_ref[:, cs].astype(f32) * do_ref[:, cs].astype(f32), axis=-1, keepdims=True)
            d_ref[0, g * tq:(g + 1) * tq, :] = jnp.broadcast_to(dd, (tq, HEAD))

    qb = pl.BlockSpec((tq, grp * HEAD), lambda h, i: (i, h))
    return _pallas_call(
        body, out_shape=_sds((KV_HEADS, grp * s, HEAD), f32), grid=(KV_HEADS, s // tq),
        in_specs=[qb, qb], out_specs=pl.BlockSpec((1, rows, HEAD), lambda h, i: (h, i, 0)),
        compiler_params=_cp(("parallel", "parallel")), name=f"flash_delta_{s}",
    )(o, do)


def _flash_probs(q_ref, k_ref, v_ref, do_ref, lse_ref, delta_ref, grp):
    qs = _stack_heads(q_ref, grp)
    dos = _stack_heads(do_ref, grp)
    sc = _dot(qs, k_ref[...], 1, 1)
    p = jnp.exp2((sc - lse_ref[0, :, 0:1]).astype(bf16))
    dp = _dot(dos, v_ref[...], 1, 1)
    ds = p * (dp - delta_ref[0, :, 0:1]).astype(bf16)
    return qs, dos, p, ds


def _flash_bwd(q, k, v, o, do, lse):
    s, t = q.shape
    grp = t // KV_WIDTH
    tq, tk = _flash_tiles(s)
    nq, nkv = s // tq, s // tk
    rows = grp * tq
    delta = _flash_delta(o, do, t)

    def body(q_ref, k_ref, v_ref, do_ref, lse_ref, delta_ref, dq_ref, dk_ref, dv_ref, dq_acc, dk_acc, dv_acc):
        kj, qi = pl.program_id(1), pl.program_id(2)

        @pl.when(qi == 0)
        def _():
            dk_acc[...] = jnp.zeros_like(dk_acc)
            dv_acc[...] = jnp.zeros_like(dv_acc)

        qs, dos, p, ds = _flash_probs(q_ref, k_ref, v_ref, do_ref, lse_ref, delta_ref, grp)
        dv_acc[...] += _dot(p, dos, 0, 0)
        dk_acc[...] += _dot(ds, qs, 0, 0)
        mine = pl.ds(pl.multiple_of(qi * rows, rows), rows)
        term = _dot(ds, k_ref[...], 1, 0)

        @pl.when(kj == 0)
        def _():
            dq_acc[mine, :] = term

        @pl.when(kj > 0)
        def _():
            dq_acc[mine, :] += term

        @pl.when(kj == nkv - 1)
        def _():
            total = dq_acc[mine, :]
            for g in range(grp):
                dq_ref[:, g * HEAD:(g + 1) * HEAD] = total[g * tq:(g + 1) * tq] * SCALE

        @pl.when(qi == nq - 1)
        def _():
            dk_ref[...] = dk_acc[...] * (1.0 / LOG2E)
            dv_ref[...] = dv_acc[...]

    qb = pl.BlockSpec((tq, grp * HEAD), lambda h, j, i: (i, h))
    kb = pl.BlockSpec((tk, HEAD), lambda h, j, i: (j, h))
    lb = pl.BlockSpec((1, rows, HEAD), lambda h, j, i: (h, i, 0))
    dqb = pl.BlockSpec((tq, grp * HEAD), lambda h, j, i: (jnp.where(j == nkv - 1, i, 0), h))
    return _pallas_call(
        body, out_shape=(_sds((s, t), f32), _sds((s, KV_WIDTH), f32), _sds((s, KV_WIDTH), f32)), grid=(KV_HEADS, nkv, nq),
        in_specs=[qb, kb, kb, qb, lb, lb], out_specs=(dqb, kb, kb),
        scratch_shapes=[pltpu.VMEM((nq * rows, HEAD), f32), pltpu.VMEM((tk, HEAD), f32), pltpu.VMEM((tk, HEAD), f32)],
        compiler_params=_cp(("parallel", "arbitrary", "arbitrary")), name=f"flash_bwd_{s}",
    )(q, k, v, do, lse, delta)


def _mem_heads(z_ref, kv_ref, gq_ref, gk_ref, h):
    cs = slice(h * HEAD, (h + 1) * HEAD)
    xv = z_ref[:, cs]
    r = lax.rsqrt(jnp.mean(xv * xv, axis=-1, keepdims=True) + EPS)
    xh = xv * r
    kx = kv_ref[:, cs]
    rk = lax.rsqrt(jnp.mean(kx * kx, axis=-1, keepdims=True) + EPS)
    kn = (kx * rk * gk_ref[...]).astype(bf16)
    vv = kv_ref[:, MEM_WIDTH + h * HEAD:MEM_WIDTH + (h + 1) * HEAD].astype(bf16)
    qn = (xh * gq_ref[...]).astype(bf16)
    sc = _dot(qn, kn, 1, 1) * SCALE
    e = jnp.exp(sc - jnp.max(sc, axis=-1, keepdims=True))
    p = e / jnp.sum(e, axis=-1, keepdims=True)
    return cs, r, xh, qn, kn, vv, p


def _mem_fwd(z, qblk, kv, g_mq, g_mk, cat):
    s = z.shape[0]
    nm = kv.shape[0]
    tr = _pick(s, (512, 256, 128))
    oblk = cat.shape[1] // MEM_WIDTH - 1

    def body(z_ref, kv_ref, gq_ref, gk_ref, cat_ref, o_ref):
        for h in range(MEM_HEADS):
            cs, _, _, _, _, vv, p = _mem_heads(z_ref, kv_ref, gq_ref, gk_ref, h)
            o_ref[:, cs] = _dot(p.astype(bf16), vv, 1, 0).astype(bf16)

    vec = pl.BlockSpec((1, HEAD), lambda i: (0, 0))
    return _pallas_call(
        body, out_shape=_sds(cat.shape, bf16), grid=(s // tr,),
        in_specs=[pl.BlockSpec((tr, MEM_WIDTH), lambda i: (i, qblk)), pl.BlockSpec((nm, 2 * MEM_WIDTH), lambda i: (0, 0)), vec, vec,
                  pl.BlockSpec(memory_space=pl.ANY)],
        out_specs=pl.BlockSpec((tr, MEM_WIDTH), lambda i: (i, oblk)),
        input_output_aliases={4: 0},
        compiler_params=_cp(("parallel",)), name=f"mem_fwd_{s}_{qblk}",
    )(z, kv, g_mq, g_mk, cat)


def _mem_bwd(z, qblk, kv, g_mq, g_mk, dcat, dz):
    s = z.shape[0]
    nm = kv.shape[0]
    tr = _pick(s, (512, 256, 128))
    dblk = dcat.shape[1] // MEM_WIDTH - 1

    def body(z_ref, kv_ref, gq_ref, gk_ref, dm_ref, dzin_ref, dz_ref, dkn_ref, dv_ref, dgq_ref):
        @pl.when(pl.program_id(0) == 0)
        def _():
            dkn_ref[...] = jnp.zeros_like(dkn_ref)
            dv_ref[...] = jnp.zeros_like(dv_ref)
            dgq_ref[...] = jnp.zeros_like(dgq_ref)

        for h in range(MEM_HEADS):
            cs, r, xh, qn, kn, vv, p = _mem_heads(z_ref, kv_ref, gq_ref, gk_ref, h)
            dm = dm_ref[:, cs]
            dv_ref[:, cs] += _dot(p.astype(bf16), dm, 0, 0)
            dp = _dot(dm, vv, 1, 1)
            ds = (p * (dp - jnp.sum(dp * p, axis=-1, keepdims=True)) * SCALE).astype(bf16)
            dqn = _dot(ds, kn, 1, 0)
            dkn_ref[:, cs] += _dot(ds, qn, 0, 0)
            dgq_ref[...] += jnp.sum(dqn * xh, axis=0, keepdims=True)
            gy = dqn * gq_ref[...]
            dz_ref[:, cs] = (r * (gy - xh * jnp.mean(gy * xh, axis=-1, keepdims=True))).astype(bf16)

    vec = pl.BlockSpec((1, HEAD), lambda i: (0, 0))
    acc = pl.BlockSpec((nm, MEM_WIDTH), lambda i: (0, 0))
    return _pallas_call(
        body, out_shape=(_sds(dz.shape, bf16), _sds((nm, MEM_WIDTH), f32), _sds((nm, MEM_WIDTH), f32), _sds((1, HEAD), f32)),
        grid=(s // tr,),
        in_specs=[pl.BlockSpec((tr, MEM_WIDTH), lambda i: (i, qblk)), pl.BlockSpec((nm, 2 * MEM_WIDTH), lambda i: (0, 0)), vec, vec,
                  pl.BlockSpec((tr, MEM_WIDTH), lambda i: (i, dblk)), pl.BlockSpec(memory_space=pl.ANY)],
        out_specs=(pl.BlockSpec((tr, MEM_WIDTH), lambda i: (i, qblk)), acc, acc, vec),
        input_output_aliases={5: 0},
        compiler_params=_cp(("arbitrary",)), name=f"mem_bwd_{s}_{qblk}",
    )(z, kv, g_mq, g_mk, dcat, dz)


def _mem_kv_bwd(kv, g_mk, dkn, dv):
    nm = kv.shape[0]

    def body(kv_ref, gk_ref, dkn_ref, dv_ref, dkv_ref, dgk_ref):
        dgk = jnp.zeros((1, HEAD), f32)
        for h in range(MEM_HEADS):
            cs = slice(h * HEAD, (h + 1) * HEAD)
            kx = kv_ref[:, cs]
            rk = lax.rsqrt(jnp.mean(kx * kx, axis=-1, keepdims=True) + EPS)
            kh = kx * rk
            dkn_h = dkn_ref[:, cs]
            dgk = dgk + jnp.sum(dkn_h * kh, axis=0, keepdims=True)
            gy = dkn_h * gk_ref[...]
            dkv_ref[:, cs] = (rk * (gy - kh * jnp.mean(gy * kh, axis=-1, keepdims=True))).astype(bf16)
        dkv_ref[:, MEM_WIDTH:] = dv_ref[...].astype(bf16)
        dgk_ref[...] = dgk

    return _pallas_call(
        body, out_shape=(_sds((nm, 2 * MEM_WIDTH), bf16), _sds((1, HEAD), f32)),
        compiler_params=pltpu.CompilerParams(vmem_limit_bytes=VMEM_LIMIT), name=f"mem_kv_bwd_{nm}",
    )(kv, g_mk, dkn, dv)


def _ffn_up(h2, g):
    m, kdim = h2.shape
    nd, _, n = g.shape
    half = nd // 2
    assert n % 128 == 0 and h2.dtype == bf16 and g.dtype == bf16
    tm = _pick(m, (512, 256, 128))

    def body(a_ref, wg_ref, wu_ref, gate_ref, up_ref, act_ref):
        av = a_ref[...]
        gt = _dot(av, wg_ref[0], 1, 0)
        up = _dot(av, wu_ref[0], 1, 0)
        gate_ref[...] = gt.astype(bf16)
        up_ref[...] = up.astype(bf16)
        act_ref[...] = (gt * jax.nn.sigmoid(gt) * up).astype(bf16)

    out = pl.BlockSpec((tm, n), lambda i, j: (i, j))
    return _pallas_call(
        body, out_shape=tuple(_sds((m, half * n), bf16) for _ in range(3)), grid=(m // tm, half),
        in_specs=[pl.BlockSpec((tm, kdim), lambda i, j: (i, 0)), pl.BlockSpec((1, kdim, n), lambda i, j: (j, 0, 0)),
                  pl.BlockSpec((1, kdim, n), lambda i, j: (j + half, 0, 0))],
        out_specs=(out, out, out),
        compiler_params=_cp(("parallel", "arbitrary")), name=f"ffn_up_{m}x{kdim}x{half * n}",
    )(h2, g, g)


def _ffn_down_bwd(dx, w_dn, gate, up):
    m, d = dx.shape
    ff = w_dn.shape[0]
    tm = _pick(m, (512, 256, 128))
    tf = _pick(ff, (1408, 1024, 512, 256, 128))

    def body(dx_ref, w_ref, g_ref, u_ref, dg_ref, du_ref):
        da = _dot(dx_ref[...], w_ref[...], 1, 1)
        gt = g_ref[...].astype(f32)
        sg = jax.nn.sigmoid(gt)
        dg_ref[...] = (da * u_ref[...].astype(f32) * sg * (1.0 + gt * (1.0 - sg))).astype(bf16)
        du_ref[...] = (da * gt * sg).astype(bf16)

    tile = pl.BlockSpec((tm, tf), lambda i, j: (i, j))
    return _pallas_call(
        body, out_shape=(_sds((m, ff), bf16), _sds((m, ff), bf16)), grid=(m // tm, ff // tf),
        in_specs=[pl.BlockSpec((tm, d), lambda i, j: (i, 0)), pl.BlockSpec((tf, d), lambda i, j: (j, 0)), tile, tile],
        out_specs=(tile, tile),
        compiler_params=_cp(("parallel", "arbitrary")), name=f"ffn_down_bwd_{m}x{d}x{ff}",
    )(dx, w_dn, gate, up)


def _loss_head(y, target):
    s, d = y.shape
    tr = _pick(s, (512, 256, 128))

    def body(y_ref, t_ref, l_ref, dy_ref, dyb_ref):
        @pl.when(pl.program_id(0) == 0)
        def _():
            l_ref[...] = jnp.zeros_like(l_ref)

        err = y_ref[...] - t_ref[...]
        l_ref[...] += 0.5 * jnp.sum(jnp.mean(err * err, axis=-1, keepdims=True), axis=0, keepdims=True)
        dy = err * (1.0 / d)
        dy_ref[...] = dy
        dyb_ref[...] = dy.astype(bf16)

    row = pl.BlockSpec((tr, d), lambda i: (i, 0))
    return _pallas_call(
        body, out_shape=(_sds((1, HEAD), f32), _sds((s, d), f32), _sds((s, d), bf16)), grid=(s // tr,),
        in_specs=[row, row], out_specs=(pl.BlockSpec((1, HEAD), lambda i: (0, 0)), row, row),
        compiler_params=_cp(("arbitrary",)), name=f"loss_{s}x{d}",
    )(y, target)


def _place():
    return lax.axis_index("x"), lax.axis_index("y"), lax.axis_index("c")


def _tag(arrays):
    return "_".join("x".join(str(dd) for dd in a.shape) for a in arrays)


def _all_gather(shards):
    nw = len(shards)
    hbm = pl.BlockSpec(memory_space=pl.ANY)

    def body(*refs):
        x_refs, out_refs = refs[:nw], refs[nw:2 * nw]
        send_sems, recv_sems, local_sems = refs[2 * nw:]
        x, y, c = _place()
        me, sibling = (x, y, c), (x, y, 1 - c)
        chips = [(1 - x, y), (x, 1 - y), (1 - x, 1 - y)]

        def slot(w, place):
            px, py, pc = place
            return out_refs[w].at[4 * px + 2 * py + pc]

        def copy(k, w, block_of, to, from_input=False):
            return pltpu.make_async_remote_copy(
                src_ref=x_refs[w] if from_input else slot(w, block_of), dst_ref=slot(w, block_of),
                send_sem=send_sems.at[k, w], recv_sem=recv_sems.at[k, w], device_id=to, device_id_type=MESH)

        mine = [pltpu.make_async_copy(x_refs[w], slot(w, me), local_sems.at[w]) for w in range(nw)]
        for cp in mine:
            cp.start()
        first = []
        for w in range(nw):
            first.append(copy(0, w, me, sibling, from_input=True))
            first += [copy(1 + j, w, me, (*chip, c), from_input=True) for j, chip in enumerate(chips)]
        for cp in first:
            cp.start()
        passed = []
        for w in range(nw):
            for j, chip in enumerate(chips):
                copy(1 + j, w, (*chip, c), me).wait_recv()
                fwd = copy(4 + j, w, (*chip, c), sibling)
                fwd.start()
                passed.append(fwd)
        for w in range(nw):
            copy(0, w, sibling, me).wait_recv()
            for j, chip in enumerate(chips):
                copy(4 + j, w, (*chip, 1 - c), me).wait_recv()
        for cp in first + passed:
            cp.wait_send()
        for cp in mine:
            cp.wait()

    return _pallas_call(
        body, out_shape=tuple(_sds((N_DEV,) + a.shape, a.dtype) for a in shards), in_specs=[hbm] * nw, out_specs=tuple([hbm] * nw),
        scratch_shapes=[pltpu.SemaphoreType.DMA((7, nw)), pltpu.SemaphoreType.DMA((7, nw)), pltpu.SemaphoreType.DMA((nw,))],
        name=f"all_gather_{_tag(shards)}_{jnp.dtype(shards[0].dtype).name}",
    )(*shards)


def _swap_with_sibling(grads):
    nw = len(grads)
    nchip = N_DEV // 2
    hbm = pl.BlockSpec(memory_space=pl.ANY)

    def body(*refs):
        g_refs, got_refs = refs[:nw], refs[nw:2 * nw]
        send_sems, recv_sems = refs[2 * nw:]
        x, y, c = _place()
        copies = [pltpu.make_async_remote_copy(
            src_ref=g_refs[w].at[2 * k + (1 - c)], dst_ref=got_refs[w].at[k],
            send_sem=send_sems.at[w, k], recv_sem=recv_sems.at[w, k], device_id=(x, y, 1 - c), device_id_type=MESH)
            for w in range(nw) for k in range(nchip)]
        for cp in copies:
            cp.start()
        for cp in copies:
            cp.wait()

    return _pallas_call(
        body, out_shape=tuple(_sds((nchip,) + g.shape[1:], g.dtype) for g in grads), in_specs=[hbm] * nw, out_specs=tuple([hbm] * nw),
        scratch_shapes=[pltpu.SemaphoreType.DMA((nw, nchip)), pltpu.SemaphoreType.DMA((nw, nchip))],
        name=f"swap_sibling_{_tag(grads)}",
    )(*grads)


def _pair_sum(grad, got):
    nd, a, b = grad.shape
    nchip = nd // 2
    ta = _pick(a, (1024, 704, 512, 352, 256, 128, 64, 32, 16))

    def my_chip():
        return 2 * lax.axis_index("x") + lax.axis_index("y")

    def body(a_ref, b_ref, o_ref, land_ref):
        tot = (a_ref[...].astype(f32) + b_ref[...].astype(f32)).astype(o_ref.dtype)
        o_ref[...] = tot

        @pl.when(pl.program_id(1) == my_chip())
        def _():
            land_ref[...] = tot

    return _pallas_call(
        body, out_shape=(_sds(got.shape, grad.dtype), _sds(got.shape, grad.dtype)), grid=(a // ta, nchip),
        in_specs=[pl.BlockSpec((None, ta, b), lambda i, k: (2 * k + lax.axis_index("c"), i, 0)),
                  pl.BlockSpec((None, ta, b), lambda i, k: (k, i, 0))],
        out_specs=(pl.BlockSpec((None, ta, b), lambda i, k: (k, i, 0)),
                   pl.BlockSpec((None, ta, b), lambda i, k: (my_chip(), i, 0))),
        compiler_params=_cp(("parallel", "arbitrary")), name=f"pair_sum_{a}x{b}",
    )(grad, got)


_HBM = pl.BlockSpec(memory_space=pltpu.HBM)
_SEM = pl.BlockSpec(memory_space=pltpu.SEMAPHORE)
_ANY = pl.BlockSpec(memory_space=pl.ANY)
_DATAFLOW = pltpu.SideEffectType.DATAFLOW_SIDE_EFFECTING


def _in_hbm(a):
    return pltpu.with_memory_space_constraint(a, pltpu.HBM)


def _exchange_begin(bufs, nw, route, after, copies_of, n_copies, name):
    nb = len(bufs)

    def body(*refs):
        send_sems, recv_sems = refs[nb + 2], refs[nb + 3]
        for w in range(nw):
            for k, (src, dst, to) in enumerate(copies_of(w, refs[:nb])):
                pltpu.make_async_remote_copy(src_ref=src, dst_ref=dst, send_sem=send_sems.at[k * nw + w],
                                             recv_sem=recv_sems.at[k * nw + w], device_id=to, device_id_type=MESH).start()

    out = _pallas_call(
        body, name=name,
        out_shape=(pltpu.SemaphoreType.DMA((n_copies * nw,)), pltpu.SemaphoreType.DMA((n_copies * nw,)),
                   *[pltpu.HBM(a.shape, a.dtype) for a in bufs], pltpu.HBM(route.shape, route.dtype)),
        in_specs=[_HBM] * (nb + 1) + [_ANY], out_specs=(_SEM, _SEM, *[_HBM] * (nb + 1)),
        input_output_aliases={i: 2 + i for i in range(nb + 1)},
        compiler_params=pltpu.CompilerParams(has_side_effects=_DATAFLOW),
    )(*[_in_hbm(a) for a in bufs], _in_hbm(route), after)
    return (out[0], out[1], out[2:2 + nb], nw), out[2 + nb]


def _exchange_end(handle, after, copies_of, n_copies, name):
    send_sems, recv_sems, thru, nw = handle
    nb = len(thru)

    def body(*refs):
        send_sems, recv_sems = refs[nb], refs[nb + 1]
        for w in range(nw):
            for k, (src, dst, to) in enumerate(copies_of(w, refs[:nb])):
                cp = pltpu.make_async_remote_copy(src_ref=src, dst_ref=dst, send_sem=send_sems.at[k * nw + w],
                                                  recv_sem=recv_sems.at[k * nw + w], device_id=to, device_id_type=MESH)
                cp.wait_send()
                cp.wait_recv()

    out = _pallas_call(
        body, name=name, out_shape=tuple(pltpu.HBM(a.shape, a.dtype) for a in thru),
        in_specs=[_HBM] * nb + [_SEM, _SEM, _ANY], out_specs=tuple([_HBM] * nb),
        input_output_aliases={i: i for i in range(nb)},
        compiler_params=pltpu.CompilerParams(has_side_effects=_DATAFLOW),
    )(*thru, send_sems, recv_sems, after)
    return list(out)


def _my_slot():
    return 4 * lax.axis_index("x") + 2 * lax.axis_index("y") + lax.axis_index("c")


def _shard_into_land(w_all, idx):
    _, a, b = w_all.shape
    ta = next(cc for cc in (1024, 704, 512, 352, 256, 128, 64, 32, 16) if a % cc == 0 and (cc * b * 4 <= 2 ** 21 or cc == 16))

    def body(w_ref, o_ref):
        o_ref[...] = w_ref[...].astype(bf16)

    return _pallas_call(
        body, out_shape=_sds((N_DEV, a, b), bf16), grid=(a // ta,),
        in_specs=[pl.BlockSpec((None, ta, b), lambda i: (idx, i, 0))],
        out_specs=pl.BlockSpec((None, ta, b), lambda i: (_my_slot(), i, 0)),
        compiler_params=_cp(("parallel",)), name=f"shard_into_land_{a}x{b}_{idx}",
    )(w_all)


def _gather_copies(w, land_refs):
    x, y, c = _place()
    blk = land_refs[w].at[4 * x + 2 * y + c]
    return [(blk, blk, to) for to in ((x, y, 1 - c), (1 - x, y, c), (x, 1 - y, c), (1 - x, 1 - y, c))]


def _gather_begin(lands, route, after, tag):
    return _exchange_begin(lands, len(lands), route, after, _gather_copies, 4, f"gather_begin_{tag}")


def _gather_end(handle, after, tag):
    return _exchange_end(handle, after, _gather_copies, 4, f"gather_end_{tag}")


def _gather_pass_on(lands):
    nw = len(lands)

    def body(*refs):
        l_refs = refs[nw:2 * nw]
        send_sems, recv_sems = refs[2 * nw:]
        x, y, c = _place()
        copies = []
        for w in range(nw):
            for j, (px, py) in enumerate([(1 - x, y), (x, 1 - y), (1 - x, 1 - y)]):
                blk = l_refs[w].at[4 * px + 2 * py + c]
                copies.append(pltpu.make_async_remote_copy(
                    src_ref=blk, dst_ref=blk, send_sem=send_sems.at[j, w], recv_sem=recv_sems.at[j, w],
                    device_id=(x, y, 1 - c), device_id_type=MESH))
        for cp in copies:
            cp.start()
        for cp in copies:
            cp.wait_send()
        for w in range(nw):
            for j, (px, py) in enumerate([(1 - x, y), (x, 1 - y), (1 - x, 1 - y)]):
                blk = l_refs[w].at[4 * px + 2 * py + (1 - c)]
                pltpu.make_async_remote_copy(src_ref=blk, dst_ref=blk, send_sem=send_sems.at[j, w], recv_sem=recv_sems.at[j, w],
                                             device_id=(x, y, 1 - c), device_id_type=MESH).wait_recv()

    return _pallas_call(
        body, out_shape=tuple(_sds(a.shape, a.dtype) for a in lands), in_specs=[_ANY] * nw, out_specs=tuple([_ANY] * nw),
        input_output_aliases={w: w for w in range(nw)},
        scratch_shapes=[pltpu.SemaphoreType.DMA((3, nw)), pltpu.SemaphoreType.DMA((3, nw))],
        name=f"gather_pass_on_{_tag(lands)}",
    )(*lands)


def _scatter_copies(w, refs):
    x, y, c = _place()
    nw = len(refs) // 2
    dst = refs[nw + w].at[2 * x + y]
    return [(refs[w].at[2 * px + py], dst, (px, py, c)) for px, py in ((1 - x, y), (x, 1 - y), (1 - x, 1 - y))]


def _scatter_begin(psums, lands, route, after, tag):
    return _exchange_begin(list(psums) + list(lands), len(psums), route, after, _scatter_copies, 3, f"scatter_begin_{tag}")


def _scatter_end(handle, after, tag):
    return _exchange_end(handle, after, _scatter_copies, 3, f"scatter_end_{tag}")


def _adamw(parts, w_all, m_all, v_all, l, carried):
    nparts, a, b = parts.shape
    nl = w_all.shape[0]
    ta = next(cc for cc in (1024, 704, 512, 352, 256, 128, 64, 32, 16, 8) if a % cc == 0 and (cc * b * 4 <= 2 ** 20 or cc == 8))
    c1 = 1.0 / (1.0 - ADAM_B1 ** ADAM_STEP)
    c2 = 1.0 / (1.0 - ADAM_B2 ** ADAM_STEP)

    def body(p_ref, w_ref, m_ref, v_ref, *rest):
        g_out, d_out, m_out, v_out = rest[-4:]
        g = p_ref[0].astype(f32)
        for k in range(1, nparts):
            g = g + p_ref[k].astype(f32)
        m_new = ADAM_B1 * m_ref[...] + (1.0 - ADAM_B1) * g
        v_new = ADAM_B2 * v_ref[...] + (1.0 - ADAM_B2) * (g * g)
        m_hat = m_new * c1
        v_hat = v_new * c2
        g_out[...] = g
        d_out[...] = -ADAM_LR * (m_hat / (jnp.sqrt(v_hat) + ADAM_EPS) + ADAM_WD * w_ref[...])
        m_out[...] = m_new
        v_out[...] = v_new

    one = pl.BlockSpec((None, ta, b), lambda i: (l, i, 0))
    keep = [] if carried is None else [pl.BlockSpec(memory_space=pl.ANY)] * 4
    return _pallas_call(
        body, out_shape=tuple(_sds((nl, a, b), f32) for _ in range(4)), grid=(a // ta,),
        in_specs=[pl.BlockSpec((nparts, ta, b), lambda i: (0, i, 0)), one, one, one] + keep, out_specs=(one, one, one, one),
        input_output_aliases=({} if carried is None else {4 + q: q for q in range(4)}),
        compiler_params=_cp(("parallel",)), name=f"adamw_{nparts}x{nl}x{a}x{b}_{l}{'' if carried is None else '_carried'}",
    )(parts, w_all, m_all, v_all, *(carried or ()))


def _to_flat(arrays):
    flat = jnp.concatenate([a.reshape(-1).astype(f32) for a in arrays])
    rows = -(-flat.shape[0] // (8 * LANES)) * 8
    return jnp.pad(flat, (0, rows * LANES - flat.shape[0])).reshape(rows, LANES)


def _from_flat(flat, shapes):
    flat = flat.reshape(-1)
    out, off = [], 0
    for shp in shapes:
        n = 1
        for dd in shp:
            n *= dd
        out.append(flat[off:off + n].reshape(shp))
        off += n
    return out


def kernel(x, mem, g_mix, g_ffn, w_in_a, g_v_a, w_spatial, b_spatial, w_in_b, g_q_b, g_k_b, g_mem, w_mem_kv, g_mq, g_mk, w_out, w_gate_up, w_down, loss_target, m_g_mix, m_g_ffn, m_w_in_a, m_g_v_a, m_w_spatial, m_b_spatial, m_w_in_b, m_g_q_b, m_g_k_b, m_g_mem, m_w_mem_kv, m_g_mq, m_g_mk, m_w_out, m_w_gate_up, m_w_down, v_g_mix, v_g_ffn, v_w_in_a, v_g_v_a, v_w_spatial, v_b_spatial, v_w_in_b, v_g_q_b, v_g_k_b, v_g_mem, v_w_mem_kv, v_g_mq, v_g_mk, v_w_out, v_w_gate_up, v_w_down):
    given = dict(locals())
    depth = g_mix.shape[0]
    s, d = x.shape[1], x.shape[2]
    nm = mem.shape[1]
    t = d - MEM_WIDTH
    ff = w_down.shape[1] * N_DEV
    x0 = x.reshape(s, d)
    mem0 = mem.reshape(nm, d)
    target = loss_target.reshape(s, d)
    tables = _rope_tables(s)

    big_names = ("w_in", "w_mem_kv", "w_out", "w_gate_up", "w_down")

    def stacked_key(name, l):
        if name == "w_in":
            return ("w_in_a" if l % 2 == 0 else "w_in_b"), l // 2
        return name, l

    n_mix = 3

    def own_shards(l):
        return [_shard_into_land(given[key], idx) for key, idx in (stacked_key(name, l) for name in big_names)]

    def gather_start(l, route, after):
        lands = cast_shards[l]
        h_mix, route = _gather_begin(lands[:n_mix], route, after, f"mix{l}")
        h_ffn, route = _gather_begin(lands[n_mix:], route, after, f"ffn{l}")
        return h_mix, h_ffn, route

    def gather_finish(handle, after, tag):
        return _gather_pass_on(_gather_end(handle, after, tag))

    saved = []
    xc = x0
    cast_shards = {0: own_shards(0)}
    h_mix, h_ffn, _ = gather_start(0, g_mix[0].reshape(1, d), mem0)
    for l in range(1, depth):
        cast_shards[l] = own_shards(l)
    w_mix = gather_finish(h_mix, cast_shards[depth - 1][0], "mix0")
    w_ffn = None
    for l in range(depth):
        is_a = l % 2 == 0
        g_in, g_kv, g_out = w_mix
        w_kv, w_o = (g.reshape(-1, g.shape[2]) for g in (g_kv, g_out))
        qblk = (N_DEV * g_in.shape[2] - MEM_WIDTH) // MEM_WIDTH

        gm_row, gf_row, gmem_row = g_mix[l].reshape(1, d), g_ffn[l].reshape(1, d), g_mem[l].reshape(1, d)
        gmq_row, gmk_row = g_mq[l].reshape(1, HEAD), g_mk[l].reshape(1, HEAD)
        if l + 1 < depth:
            next_mix, next_ffn, gm_row = gather_start(l + 1, gm_row, g_in)
        h = _rms_fwd(xc, gm_row)
        z = _mm_cols_fwd(h, g_in, f32)
        if is_a:
            ia = l // 2
            mix = dict(g_v=g_v_a[ia].reshape(1, t), w_s=w_spatial[ia], b_t=b_spatial[ia].T)
            cat = _gmlp_fwd(z, mix["g_v"], mix["w_s"], mix["b_t"])
        else:
            ib = l // 2
            mix = dict(g_q=g_q_b[ib].reshape(1, HEAD), g_k=g_k_b[ib].reshape(1, HEAD))
            q, k, v = _attn_prep_fwd(z, mix["g_q"], mix["g_k"], tables, t)
            cat, lse = _flash_fwd(q, k, v)
            mix.update(q=q, k=k, v=v, lse=lse)
        hm = _rms_fwd(mem0, gmem_row)
        kv = _matmul(hm, w_kv)
        cat = _mem_fwd(z, qblk, kv, gmq_row, gmk_row, cat)
        x1 = _matmul(cat, w_o, res=xc)
        if l == 0:
            w_ffn = gather_finish(h_ffn, x1, "ffn0")
        g_gu, g_dn = w_ffn
        w_dn = g_dn.reshape(-1, g_dn.shape[2])
        h2 = _rms_fwd(x1, gf_row)
        gate, up, act = _ffn_up(h2, g_gu)
        x2 = _matmul(act, w_dn, res=x1)
        saved.append(dict(x=xc, h=h, z=z, mix=mix, cat=cat, hm=hm, kv=kv, x1=x1, h2=h2, gate=gate, up=up, act=act, qblk=qblk,
                          w=(g_in, w_kv, w_o, g_gu, w_dn), rows=(gm_row, gf_row, gmem_row, gmq_row, gmk_row)))
        if l + 1 < depth:
            w_mix = gather_finish(next_mix, x2, f"mix{l + 1}")
            w_ffn = gather_finish(next_ffn, x2, f"ffn{l + 1}")
        xc = x2

    loss_row, dy, dy_b = _loss_head(xc, target)
    loss = lax.psum(loss_row[0, 0], ("x", "y", "c"))

    small = {n: [None] * given[n].shape[0] for n in ("g_mix", "g_ffn", "g_v_a", "w_spatial", "b_spatial", "g_q_b", "g_k_b", "g_mem", "g_mq", "g_mk")}
    big_out = {}

    def scatter_start(grads, route, after, tag):
        got = _swap_with_sibling(grads)
        sums = [_pair_sum(g, r) for g, r in zip(grads, got)]
        return _scatter_begin([p for p, _ in sums], [q for _, q in sums], route, after, tag)

    def scatter_finish(handle, names, l, after, tag):
        arrived = _scatter_end(handle, after, tag)[len(names):]
        for name, parts in zip(names, arrived):
            key, idx = stacked_key(name, l)
            big_out[key] = _adamw(parts, given[key], given["m_" + key], given["v_" + key], idx, big_out.get(key))

    pend_mix = None
    dx, dx_b = dy, dy_b
    for l in reversed(range(depth)):
        sv = saved[l]
        is_a = l % 2 == 0
        g_in, w_kv, w_o, g_gu, w_dn = sv["w"]
        gm_row, gf_row, gmem_row, gmq_row, gmk_row = sv["rows"]
        mix = sv["mix"]
        dw_dn = _matmul(sv["act"], dx_b, ta=True, out_dtype=bf16)
        dgu = _ffn_down_bwd(dx_b, w_dn, sv["gate"], sv["up"])
        dw_gu = _mm_cols_wgrad(sv["h2"], dgu, g_gu.shape[2])
        dh2 = _mm_cols_dgrad(dgu, g_gu)
        dx1, dx1_b, dgf = _rms_bwd(sv["x1"], gf_row, dh2, dx)
        small["g_ffn"][l] = dgf.reshape(d)
        pend_ffn, dx1_b = scatter_start([dw_gu, dw_dn.reshape(N_DEV, -1, d)], dx1_b, dx1, f"ffn{l}")
        if pend_mix is not None:
            scatter_finish(*pend_mix, dx1_b, f"mix{l + 1}")
        dw_o = _matmul(sv["cat"], dx1_b, ta=True, out_dtype=bf16)
        dcat = _matmul(dx1_b, w_o, tb=True, out_dtype=bf16)
        if is_a:
            dz, dws, dbt, dgv = _gmlp_bwd(sv["z"], mix["g_v"], mix["w_s"], mix["b_t"], dcat)
            small["w_spatial"][l // 2], small["b_spatial"][l // 2], small["g_v_a"][l // 2] = dws, dbt.T, dgv.reshape(t)
        else:
            dq, dk, dv = _flash_bwd(mix["q"], mix["k"], mix["v"], sv["cat"], dcat, mix["lse"])
            dz, dgq, dgk = _attn_prep_bwd(sv["z"], mix["g_q"], mix["g_k"], tables, dq, dk, dv, t)
            small["g_q_b"][l // 2], small["g_k_b"][l // 2] = dgq.reshape(HEAD), dgk.reshape(HEAD)
        dz, dkn, dvm, dgmq = _mem_bwd(sv["z"], sv["qblk"], sv["kv"], gmq_row, gmk_row, dcat, dz)
        dkv, dgmk = _mem_kv_bwd(sv["kv"], gmk_row, dkn, dvm)
        dw_kv = _matmul(sv["hm"], dkv, ta=True, out_dtype=bf16)
        dhm = _matmul(dkv, w_kv, tb=True)
        small["g_mem"][l] = _rms_bwd(mem0, gmem_row, dhm, None).reshape(d)
        small["g_mq"][l] = dgmq.reshape(HEAD)
        small["g_mk"][l] = dgmk.reshape(HEAD)
        dw_in = _mm_cols_wgrad(sv["h"], dz, g_in.shape[2])
        dh = _mm_cols_dgrad(dz, g_in)
        dx, dx_b, dgm = _rms_bwd(sv["x"], gm_row, dh, dx1)
        small["g_mix"][l] = dgm.reshape(d)

        handle, dx_b = scatter_start([dw_in] + [dw.reshape(N_DEV, -1, dw.shape[1]) for dw in (dw_kv, dw_o)], dx_b, dx, f"mix{l}")
        pend_mix = (handle, big_names[:n_mix], l)
        scatter_finish(pend_ffn, big_names[n_mix:], l, dx_b, f"ffn{l}")
    scatter_finish(*pend_mix, dx, "mix0")

    small_names = tuple(small)
    small_grads = [jnp.stack(small[n]) for n in small_names]
    small_shapes = [g.shape for g in small_grads]
    (all_parts,) = _all_gather([_to_flat(small_grads)])
    souts = _adamw(all_parts, *[_to_flat([given[p + n] for n in small_names])[None] for p in ("", "m_", "v_")], 0, None)
    small_out = dict(zip(small_names, zip(*[_from_flat(flat, small_shapes) for flat in souts])))

    weights = ("g_mix", "g_ffn", "w_in_a", "g_v_a", "w_spatial", "b_spatial", "w_in_b", "g_q_b", "g_k_b", "g_mem", "w_mem_kv",
               "g_mq", "g_mk", "w_out", "w_gate_up", "w_down")
    results = {n: (small_out[n] if n in small_out else big_out[n]) for n in weights}
    grad_x = dx.reshape(1, s, d)
    return (loss, grad_x, *[results[n][kind] for kind in range(4) for n in weights])
```

```python
import functools

import jax
import jax.numpy as jnp
from jax import lax
from jax.experimental import pallas as pl
from jax.experimental.pallas import tpu as pltpu

f32 = jnp.float32
bf16 = jnp.bfloat16

HEAD = 128
CHUNK = 128
GRID_W = 64
MEM_HEADS = 4
KV_HEADS = 4
MEM_WIDTH = MEM_HEADS * HEAD
KV_WIDTH = KV_HEADS * HEAD
ROPE_THETA = 10000.0
ROPE_PAIRS = HEAD // 4
EPS = 1e-6
SCALE = HEAD ** -0.5
LOG2E = 1.4426950408889634
N_DEV = 8
LANES = 1024
VMEM_LIMIT = 56 * 1024 * 1024

ADAM_LR, ADAM_B1, ADAM_B2, ADAM_EPS, ADAM_WD, ADAM_STEP = 0.001, 0.9, 0.999, 1e-08, 0.01, 10

MESH = pl.DeviceIdType.MESH
_pallas_call = pl.pallas_call


def _pick(dim, cands):
    for c in cands:
        if dim % c == 0:
            return c
    return dim


def _cp(sem):
    return pltpu.CompilerParams(dimension_semantics=sem, vmem_limit_bytes=VMEM_LIMIT)


def _sds(shape, dtype):
    return jax.ShapeDtypeStruct(shape, dtype)


def _dot(a, b, ca, cb):
    return lax.dot_general(a, b, (((ca,), (cb,)), ((), ())), preferred_element_type=f32)


def _gelu(z):
    return 0.5 * z * (1.0 + lax.erf(z * 0.7071067811865476))


def _gelu_grad(z):
    return 0.5 * (1.0 + lax.erf(z * 0.7071067811865476)) + z * jnp.exp(-0.5 * z * z) * 0.3989422804014327


def _rot(x, sin_a, sin_b):
    return pltpu.roll(x, 96, 1) * sin_a + pltpu.roll(x, 32, 1) * sin_b


def _matmul(a, b, *, ta=False, tb=False, out_dtype=f32, res=None, tm=None, tn=None, tk=None):
    assert a.dtype == bf16 and b.dtype == bf16
    kdim, m = a.shape if ta else a.shape[::-1]
    n, k2 = b.shape if tb else b.shape[::-1]
    assert kdim == k2, (a.shape, b.shape, ta, tb)
    tm = tm or _pick(m, (1024, 1408, 512, 256, 128))
    tn = tn or _pick(n, (1024, 1408, 512, 256, 128))
    if tk is None:
        tk = kdim if kdim <= 2048 else _pick(kdim, (2816, 1024, 512, 256, 128))
    nk = kdim // tk
    ca, cb = (0 if ta else 1), (1 if tb else 0)
    has_res = res is not None

    def body(*refs):
        a_ref, b_ref = refs[0], refs[1]
        r_ref = refs[2] if has_res else None
        o_ref = refs[3] if has_res else refs[2]
        prod = _dot(a_ref[...], b_ref[...], ca, cb)
        if nk == 1:
            if has_res:
                prod = prod + r_ref[...]
            o_ref[...] = prod.astype(o_ref.dtype)
        else:
            acc = refs[-1]
            k = pl.program_id(2)

            @pl.when(k == 0)
            def _():
                acc[...] = prod

            @pl.when(k > 0)
            def _():
                acc[...] += prod

            @pl.when(k == nk - 1)
            def _():
                out = acc[...]
                if has_res:
                    out = out + r_ref[...]
                o_ref[...] = out.astype(o_ref.dtype)

    a_spec = pl.BlockSpec((tk, tm), lambda i, j, k: (k, i)) if ta else pl.BlockSpec((tm, tk), lambda i, j, k: (i, k))
    b_spec = pl.BlockSpec((tn, tk), lambda i, j, k: (j, k)) if tb else pl.BlockSpec((tk, tn), lambda i, j, k: (k, j))
    o_spec = pl.BlockSpec((tm, tn), lambda i, j, k: (i, j))
    in_specs = [a_spec, b_spec] + ([o_spec] if has_res else [])
    args = (a, b) + ((res,) if has_res else ())
    mode = ("t" if ta else "n") + ("t" if tb else "n")
    return _pallas_call(
        body, out_shape=_sds((m, n), out_dtype), grid=(m // tm, n // tn, nk),
        in_specs=in_specs, out_specs=o_spec,
        scratch_shapes=([pltpu.VMEM((tm, tn), f32)] if nk > 1 else []),
        compiler_params=_cp(("parallel", "parallel", "arbitrary")),
        name=f"mm_{mode}_{m}x{kdim}x{n}{'_res' if has_res else ''}_{jnp.dtype(out_dtype).name}",
    )(*args)


def _shards_per_step(n, pair_bytes=0):
    p = 2 if (n % 128 != 0 or 0 < 2 * pair_bytes <= VMEM_LIMIT // 2) else 1
    assert (p * n) % 128 == 0 and N_DEV % p == 0
    return p


def _lane_pieces(v, p, n):
    return [v] if p == 1 else [v[:, q * n:(q + 1) * n] for q in range(p)]


def _mm_cols_fwd(a, g, out_dtype):
    m, kdim = a.shape
    nd, k2, n = g.shape
    assert kdim == k2 and a.dtype == bf16 and g.dtype == bf16
    p = _shards_per_step(n)
    tm = _pick(m, (1024, 512, 256, 128))

    def body(a_ref, g_ref, o_ref):
        av = a_ref[...]
        parts = [_dot(av, g_ref[q], 1, 0) for q in range(p)]
        out = parts[0] if p == 1 else jnp.concatenate(parts, axis=1)
        o_ref[...] = out.astype(o_ref.dtype)

    return _pallas_call(
        body, out_shape=_sds((m, nd * n), out_dtype), grid=(m // tm, nd // p),
        in_specs=[pl.BlockSpec((tm, kdim), lambda i, j: (i, 0)), pl.BlockSpec((p, kdim, n), lambda i, j: (j, 0, 0))],
        out_specs=pl.BlockSpec((tm, p * n), lambda i, j: (i, j)),
        compiler_params=_cp(("parallel", "arbitrary")), name=f"mm_cols_fwd_{m}x{kdim}x{nd * n}_{jnp.dtype(out_dtype).name}",
    )(a, g)


def _pick_part(refs, nparts, step, per):
    val = refs[0][...]
    for hh in range(1, nparts):
        val = jnp.where(step >= hh * per, refs[hh][...], val)
    return val


def _part_step(step, hh, per):
    return jnp.clip(step - hh * per, 0, per - 1)


def _mm_cols_dgrad(dz, g):
    nd, kdim, n = g.shape
    p = _shards_per_step(n, pair_bytes=2 * kdim * n * 2)
    nj = nd // p
    parts = dz if isinstance(dz, tuple) else (dz,)
    m = parts[0].shape[0]
    nn = sum(part.shape[1] for part in parts)
    assert nn == nd * n and all(part.dtype == bf16 for part in parts) and g.dtype == bf16
    tm = _pick(m, (512, 256, 128))
    nparts = len(parts)
    per = nj // nparts

    def body(*refs):
        g_ref, o_ref, acc = refs[nparts:]
        j = pl.program_id(1)
        tot = None
        for q, piece in enumerate(_lane_pieces(_pick_part(refs, nparts, j, per), p, n)):
            dd = _dot(piece, g_ref[q], 1, 1)
            tot = dd if tot is None else tot + dd

        @pl.when(j == 0)
        def _():
            acc[...] = tot

        @pl.when(j > 0)
        def _():
            acc[...] += tot

        @pl.when(j == nj - 1)
        def _():
            o_ref[...] = acc[...]

    return _pallas_call(
        body, out_shape=_sds((m, kdim), f32), grid=(m // tm, nj),
        in_specs=[pl.BlockSpec((tm, p * n), (lambda hh: lambda i, j: (i, _part_step(j, hh, per)))(hh)) for hh in range(nparts)]
        + [pl.BlockSpec((p, kdim, n), lambda i, j: (j, 0, 0))],
        out_specs=pl.BlockSpec((tm, kdim), lambda i, j: (i, 0)),
        scratch_shapes=[pltpu.VMEM((tm, kdim), f32)],
        compiler_params=_cp(("parallel", "arbitrary")), name=f"mm_cols_dgrad_{m}x{nn}x{kdim}_{nparts}",
    )(*parts, g)


def _mm_cols_wgrad(a, dz, n):
    s, kdim = a.shape
    parts = dz if isinstance(dz, tuple) else (dz,)
    nparts = len(parts)
    nd = sum(part.shape[1] for part in parts) // n
    assert a.dtype == bf16 and all(part.dtype == bf16 for part in parts)
    p = _shards_per_step(n)
    per = nd // p // nparts
    tkw = _pick(kdim, (1024, 512, 256, 128))
    ts = _pick(s, (2048, 1024, 512, 256, 128))
    ns = s // ts

    def body(a_ref, *refs):
        o_ref, acc = refs[nparts:]
        si = pl.program_id(2)
        av = a_ref[...]
        prods = [_dot(av, piece, 0, 0) for piece in _lane_pieces(_pick_part(refs, nparts, pl.program_id(1), per), p, n)]

        @pl.when(si == 0)
        def _():
            for q in range(p):
                acc[q] = prods[q]

        @pl.when(si > 0)
        def _():
            for q in range(p):
                acc[q] += prods[q]

        @pl.when(si == ns - 1)
        def _():
            o_ref[...] = acc[...].astype(bf16)

    return _pallas_call(
        body, out_shape=_sds((nd, kdim, n), bf16), grid=(kdim // tkw, nd // p, ns),
        in_specs=[pl.BlockSpec((ts, tkw), lambda i, j, k: (k, i))]
        + [pl.BlockSpec((ts, p * n), (lambda hh: lambda i, j, k: (k, _part_step(j, hh, per)))(hh)) for hh in range(nparts)],
        out_specs=pl.BlockSpec((p, tkw, n), lambda i, j, k: (j, i, 0)),
        scratch_shapes=[pltpu.VMEM((p, tkw, n), f32)],
        compiler_params=_cp(("parallel", "parallel", "arbitrary")), name=f"mm_cols_wgrad_{kdim}x{s}x{nd * n}_{nparts}",
    )(a, *parts)


def _rms_fwd(x, g_row):
    s, d = x.shape
    tr = _pick(s, (512, 256, 128))

    def body(x_ref, g_ref, o_ref):
        xv = x_ref[...]
        r = lax.rsqrt(jnp.mean(xv * xv, axis=-1, keepdims=True) + EPS)
        o_ref[...] = (xv * r * g_ref[...]).astype(bf16)

    return _pallas_call(
        body, out_shape=_sds((s, d), bf16), grid=(s // tr,),
        in_specs=[pl.BlockSpec((tr, d), lambda i: (i, 0)), pl.BlockSpec((1, d), lambda i: (0, 0))],
        out_specs=pl.BlockSpec((tr, d), lambda i: (i, 0)),
        compiler_params=_cp(("parallel",)), name=f"rms_fwd_{s}x{d}",
    )(x, g_row)


def _rms_bwd(x, g_row, dh, dres):
    s, d = x.shape
    tr = _pick(s, (512, 256, 128))
    with_dx = dres is not None

    def body(*refs):
        if with_dx:
            x_ref, g_ref, dh_ref, dres_ref, dx_ref, dxb_ref, dg_ref = refs
        else:
            x_ref, g_ref, dh_ref, dg_ref = refs

        @pl.when(pl.program_id(0) == 0)
        def _():
            dg_ref[...] = jnp.zeros_like(dg_ref)

        xv = x_ref[...]
        r = lax.rsqrt(jnp.mean(xv * xv, axis=-1, keepdims=True) + EPS)
        xh = xv * r
        dy = dh_ref[...].astype(f32)
        dg_ref[...] += jnp.sum(dy * xh, axis=0, keepdims=True)
        if with_dx:
            gy = dy * g_ref[...]
            dx = dres_ref[...] + r * (gy - xh * jnp.mean(gy * xh, axis=-1, keepdims=True))
            dx_ref[...] = dx
            dxb_ref[...] = dx.astype(bf16)

    row = pl.BlockSpec((tr, d), lambda i: (i, 0))
    vec = pl.BlockSpec((1, d), lambda i: (0, 0))
    if with_dx:
        return _pallas_call(
            body, out_shape=(_sds((s, d), f32), _sds((s, d), bf16), _sds((1, d), f32)), grid=(s // tr,),
            in_specs=[row, vec, row, row], out_specs=(row, row, vec),
            compiler_params=_cp(("arbitrary",)), name=f"rms_bwd_{s}x{d}",
        )(x, g_row, dh, dres)
    return _pallas_call(
        body, out_shape=_sds((1, d), f32), grid=(s // tr,),
        in_specs=[row, vec, row], out_specs=vec,
        compiler_params=_cp(("arbitrary",)), name=f"rms_bwd_gain_{s}x{d}",
    )(x, g_row, dh)


def _gmlp_rows(s):
    return CHUNK * (2 if (s // CHUNK) % 2 == 0 else 1)


def _gmlp_fwd(z, g_v, w_s, b_t):
    s = z.shape[0]
    t = g_v.shape[1]
    ng = t // HEAD
    rb = _gmlp_rows(s)

    def body(z_ref, gv_ref, ws_ref, bt_ref, o_ref):
        for ci in range(rb // CHUNK):
            lo = ci * CHUNK
            a = _gelu(z_ref[lo:lo + CHUNK, :])
            u, vv = a[:, :t], a[:, t:]
            r = lax.rsqrt(jnp.mean(vv * vv, axis=-1, keepdims=True) + EPS)
            vn = (vv * r * gv_ref[...]).astype(bf16)
            for g in range(ng):
                cs = slice(g * HEAD, (g + 1) * HEAD)
                sg = _dot(ws_ref[g].astype(bf16), vn[:, cs], 1, 0) + bt_ref[:, g:g + 1]
                o_ref[lo:lo + CHUNK, cs] = (u[:, cs] * sg).astype(bf16)

    return _pallas_call(
        body, out_shape=_sds((s, t + MEM_WIDTH), bf16), grid=(s // rb,),
        in_specs=[pl.BlockSpec((rb, 2 * t), lambda i: (i, 0)), pl.BlockSpec((1, t), lambda i: (0, 0)),
                  pl.BlockSpec((ng, CHUNK, CHUNK), lambda i: (0, 0, 0)), pl.BlockSpec((CHUNK, ng), lambda i: (0, 0))],
        out_specs=pl.BlockSpec((rb, t), lambda i: (i, 0)),
        compiler_params=_cp(("parallel",)), name=f"gmlp_fwd_{s}",
    )(z, g_v, w_s, b_t)


def _gmlp_bwd(z, g_v, w_s, b_t, dtok):
    s = z.shape[0]
    t = g_v.shape[1]
    ng = t // HEAD
    rb = _gmlp_rows(s)
    nsteps = s // rb

    def body(z_ref, gv_ref, ws_ref, bt_ref, dt_ref, dz_ref, dws_ref, dbt_ref, dgv_ref, ds_acc):
        step = pl.program_id(0)

        @pl.when(step == 0)
        def _():
            dws_ref[...] = jnp.zeros_like(dws_ref)
            dgv_ref[...] = jnp.zeros_like(dgv_ref)
            ds_acc[...] = jnp.zeros_like(ds_acc)

        for ci in range(rb // CHUNK):
            lo = ci * CHUNK
            zz = z_ref[lo:lo + CHUNK, :]
            a = _gelu(zz)
            u, vv = a[:, :t], a[:, t:]
            r = lax.rsqrt(jnp.mean(vv * vv, axis=-1, keepdims=True) + EPS)
            vh = vv * r
            vn = (vh * gv_ref[...]).astype(bf16)
            dtok = dt_ref[lo:lo + CHUNK, :].astype(f32)
            ds = dtok * u
            ds_acc[...] += ds
            dsb = ds.astype(bf16)
            du_parts, dvn_parts = [], []
            for g in range(ng):
                cs = slice(g * HEAD, (g + 1) * HEAD)
                wg = ws_ref[g].astype(bf16)
                sg = _dot(wg, vn[:, cs], 1, 0) + bt_ref[:, g:g + 1]
                du_parts.append(dtok[:, cs] * sg)
                dws_ref[g] += _dot(dsb[:, cs], vn[:, cs], 1, 1)
                dvn_parts.append(_dot(wg, dsb[:, cs], 0, 0))
            dvn = jnp.concatenate(dvn_parts, axis=1)
            dgv_ref[...] += jnp.sum(dvn * vh, axis=0, keepdims=True)
            gy = dvn * gv_ref[...]
            dvv = r * (gy - vh * jnp.mean(gy * vh, axis=-1, keepdims=True))
            da = jnp.concatenate(du_parts + [dvv], axis=1)
            dz_ref[lo:lo + CHUNK, :] = (da * _gelu_grad(zz)).astype(bf16)

        @pl.when(step == nsteps - 1)
        def _():
            for g in range(ng):
                dbt_ref[:, g:g + 1] = jnp.sum(ds_acc[:, g * HEAD:(g + 1) * HEAD], axis=1, keepdims=True)

    return _pallas_call(
        body,
        out_shape=(_sds((s, z.shape[1]), bf16), _sds((ng, CHUNK, CHUNK), f32), _sds((CHUNK, ng), f32), _sds((1, t), f32)),
        grid=(nsteps,),
        in_specs=[pl.BlockSpec((rb, 2 * t), lambda i: (i, 0)), pl.BlockSpec((1, t), lambda i: (0, 0)),
                  pl.BlockSpec((ng, CHUNK, CHUNK), lambda i: (0, 0, 0)), pl.BlockSpec((CHUNK, ng), lambda i: (0, 0)),
                  pl.BlockSpec((rb, t), lambda i: (i, 0))],
        out_specs=(pl.BlockSpec((rb, 2 * t), lambda i: (i, 0)), pl.BlockSpec((ng, CHUNK, CHUNK), lambda i: (0, 0, 0)),
                   pl.BlockSpec((CHUNK, ng), lambda i: (0, 0)), pl.BlockSpec((1, t), lambda i: (0, 0))),
        scratch_shapes=[pltpu.VMEM((CHUNK, t), f32)],
        compiler_params=_cp(("arbitrary",)), name=f"gmlp_bwd_{s}",
    )(z, g_v, w_s, b_t, dtok)


def _rope_tables(s):
    n_rows = s // GRID_W
    rows = jnp.broadcast_to(jnp.arange(n_rows)[:, None], (n_rows, GRID_W)).reshape(s)
    cols = jnp.broadcast_to(jnp.arange(GRID_W)[None, :], (n_rows, GRID_W)).reshape(s)
    freqs = ROPE_THETA ** (-jnp.arange(ROPE_PAIRS, dtype=f32) / ROPE_PAIRS)
    ang_r = rows.astype(f32)[:, None] * freqs
    ang_c = cols.astype(f32)[:, None] * freqs
    ang = jnp.concatenate([ang_r, ang_r, ang_c, ang_c], axis=-1)
    cos, sin = jnp.cos(ang), jnp.sin(ang)
    first = (jnp.arange(HEAD) % (2 * ROPE_PAIRS)) < ROPE_PAIRS
    return cos, jnp.where(first, -sin, 0.0), jnp.where(first, 0.0, sin)


def _attn_prep_fwd(z, g_q, g_k, tables, t):
    s = z.shape[0]
    tr = _pick(s, (256, 128))
    nq = t // HEAD
    width = t + 2 * KV_WIDTH

    def body(z_ref, gq_ref, gk_ref, cos_ref, sa_ref, sb_ref, q_ref, k_ref, v_ref):
        cos, sa, sb = cos_ref[...], sa_ref[...], sb_ref[...]
        for h in range(nq + KV_HEADS):
            cs = slice(h * HEAD, (h + 1) * HEAD)
            xv = z_ref[:, cs]
            r = lax.rsqrt(jnp.mean(xv * xv, axis=-1, keepdims=True) + EPS)
            xn = xv * r * (gq_ref[...] if h < nq else gk_ref[...])
            y = xn * cos + _rot(xn, sa, sb)
            if h < nq:
                q_ref[:, cs] = (y * (SCALE * LOG2E)).astype(bf16)
            else:
                k_ref[:, (h - nq) * HEAD:(h - nq + 1) * HEAD] = y.astype(bf16)
        v_ref[...] = z_ref[:, t + KV_WIDTH:width].astype(bf16)

    row = lambda w: pl.BlockSpec((tr, w), lambda i: (i, 0))
    vec = pl.BlockSpec((1, HEAD), lambda i: (0, 0))
    return _pallas_call(
        body, out_shape=(_sds((s, t), bf16), _sds((s, KV_WIDTH), bf16), _sds((s, KV_WIDTH), bf16)), grid=(s // tr,),
        in_specs=[row(width), vec, vec, row(HEAD), row(HEAD), row(HEAD)],
        out_specs=(row(t), row(KV_WIDTH), row(KV_WIDTH)),
        compiler_params=_cp(("parallel",)), name=f"attn_prep_fwd_{s}",
    )(z, g_q, g_k, *tables)


def _attn_prep_bwd(z, g_q, g_k, tables, dq, dk, dv, t):
    s = z.shape[0]
    tr = _pick(s, (256, 128))
    nq = t // HEAD
    width = t + 2 * KV_WIDTH

    def body(z_ref, gq_ref, gk_ref, cos_ref, sa_ref, sb_ref, dq_ref, dk_ref, dv_ref, dz_ref, dgq_ref, dgk_ref):
        @pl.when(pl.program_id(0) == 0)
        def _():
            dgq_ref[...] = jnp.zeros_like(dgq_ref)
            dgk_ref[...] = jnp.zeros_like(dgk_ref)

        cos, sa, sb = cos_ref[...], sa_ref[...], sb_ref[...]
        for h in range(nq + KV_HEADS):
            cs = slice(h * HEAD, (h + 1) * HEAD)
            xv = z_ref[:, cs]
            r = lax.rsqrt(jnp.mean(xv * xv, axis=-1, keepdims=True) + EPS)
            xh = xv * r
            if h < nq:
                dy, g_ref, dg_ref = dq_ref[:, cs], gq_ref, dgq_ref
            else:
                dy, g_ref, dg_ref = dk_ref[:, (h - nq) * HEAD:(h - nq + 1) * HEAD], gk_ref, dgk_ref
            dy = dy.astype(f32)
            dxn = dy * cos - _rot(dy, sa, sb)
            dg_ref[...] += jnp.sum(dxn * xh, axis=0, keepdims=True)
            gy = dxn * g_ref[...]
            dz_ref[:, cs] = (r * (gy - xh * jnp.mean(gy * xh, axis=-1, keepdims=True))).astype(bf16)
        dz_ref[:, t + KV_WIDTH:width] = dv_ref[...].astype(bf16)

    row = lambda w: pl.BlockSpec((tr, w), lambda i: (i, 0))
    vec = pl.BlockSpec((1, HEAD), lambda i: (0, 0))
    return _pallas_call(
        body, out_shape=(_sds((s, z.shape[1]), bf16), _sds((1, HEAD), f32), _sds((1, HEAD), f32)), grid=(s // tr,),
        in_specs=[row(width), vec, vec, row(HEAD), row(HEAD), row(HEAD), row(t), row(KV_WIDTH), row(KV_WIDTH)],
        out_specs=(row(width), vec, vec),
        compiler_params=_cp(("arbitrary",)), name=f"attn_prep_bwd_{s}",
    )(z, g_q, g_k, *tables, dq, dk, dv)


def _flash_tiles(s):
    return _pick(s, (512, 256, 128)), _pick(s, (1024, 512, 256, 128))


def _flash_fwd(q, k, v):
    s, t = q.shape
    grp = t // KV_WIDTH
    tq, tk = _flash_tiles(s)
    nkv = s // tk
    rows = grp * tq

    def body(q_ref, k_ref, v_ref, o_ref, lse_ref, m_sc, acc_sc):
        ki = pl.program_id(2)

        @pl.when(ki == 0)
        def _():
            m_sc[...] = jnp.full(m_sc.shape, -jnp.inf, f32)
            acc_sc[...] = jnp.zeros_like(acc_sc)

        kk = k_ref[...]
        v1 = jnp.concatenate([v_ref[...], jnp.ones((tk, HEAD), bf16)], axis=1)
        sc = _dot(q_ref[:, 0:HEAD], kk, 1, 1)
        for g in range(grp):
            sc_next = _dot(q_ref[:, (g + 1) * HEAD:(g + 2) * HEAD], kk, 1, 1) if g + 1 < grp else None
            mine = slice(g * tq, (g + 1) * tq)
            m_prev = m_sc[mine, :]
            m_new = jnp.maximum(m_prev, jnp.max(sc, axis=-1, keepdims=True))
            alpha = jnp.exp2(m_prev - m_new)
            p = jnp.exp2((sc - m_new).astype(bf16))
            acc_sc[mine, :] = alpha * acc_sc[mine, :] + _dot(p, v1, 1, 0)
            m_sc[mine, :] = m_new
            sc = sc_next

        @pl.when(ki == nkv - 1)
        def _():
            acc = acc_sc[...]
            l = acc[:, HEAD:HEAD + 1]
            o = acc[:, :HEAD] / l
            for g in range(grp):
                o_ref[:, g * HEAD:(g + 1) * HEAD] = o[g * tq:(g + 1) * tq].astype(bf16)
            lse_ref[0] = jnp.broadcast_to(m_sc[...] + jnp.log(l) * LOG2E, (rows, HEAD))

    return _pallas_call(
        body, out_shape=(_sds((s, t + MEM_WIDTH), bf16), _sds((KV_HEADS, grp * s, HEAD), f32)), grid=(KV_HEADS, s // tq, nkv),
        in_specs=[pl.BlockSpec((tq, grp * HEAD), lambda h, i, j: (i, h)), pl.BlockSpec((tk, HEAD), lambda h, i, j: (j, h)),
                  pl.BlockSpec((tk, HEAD), lambda h, i, j: (j, h))],
        out_specs=(pl.BlockSpec((tq, grp * HEAD), lambda h, i, j: (i, h)), pl.BlockSpec((1, rows, HEAD), lambda h, i, j: (h, i, 0))),
        scratch_shapes=[pltpu.VMEM((rows, 1), f32), pltpu.VMEM((rows, 2 * HEAD), f32)],
        compiler_params=_cp(("parallel", "parallel", "arbitrary")), name=f"flash_fwd_{s}",
    )(q, k, v)


def _flash_delta(o, do, t):
    s = o.shape[0]
    grp = t // KV_WIDTH
    tq, _ = _flash_tiles(s)
    rows = grp * tq

    def body(o_ref, do_ref, d_ref):
        for g in range(grp):
            cs = slice(g * HEAD, (g + 1) * HEAD)
            dd = jnp.sum(o_ref[:, cs].astype(f32) * do_ref[:, cs].astype(f32), axis=-1, keepdims=True)
            d_ref[0, g * tq:(g + 1) * tq, :] = jnp.broadcast_to(dd, (tq, HEAD))

    qb = pl.BlockSpec((tq, grp * HEAD), lambda h, i: (i, h))
    return _pallas_call(
        body, out_shape=_sds((KV_HEADS, grp * s, HEAD), f32), grid=(KV_HEADS, s // tq),
        in_specs=[qb, qb], out_specs=pl.BlockSpec((1, rows, HEAD), lambda h, i: (h, i, 0)),
        compiler_params=_cp(("parallel", "parallel")), name=f"flash_delta_{s}",
    )(o, do)


def _flash_bwd(q, k, v, o, do, lse):
    s, t = q.shape
    grp = t // KV_WIDTH
    tq, tk = _flash_tiles(s)
    nq, nkv = s // tq, s // tk
    rows = grp * tq
    delta = _flash_delta(o, do, t)

    def body(q_ref, k_ref, v_ref, do_ref, lse_ref, delta_ref, dq_ref, dk_ref, dv_ref, dq_acc, dk_acc, dv_acc):
        kj, qi = pl.program_id(1), pl.program_id(2)

        @pl.when(qi == 0)
        def _():
            dk_acc[...] = jnp.zeros_like(dk_acc)
            dv_acc[...] = jnp.zeros_like(dv_acc)

        kk, vv = k_ref[...], v_ref[...]

        def products(g):
            qg, dog = q_ref[:, g * HEAD:(g + 1) * HEAD], do_ref[:, g * HEAD:(g + 1) * HEAD]
            return qg, dog, _dot(qg, kk, 1, 1), _dot(dog, vv, 1, 1)

        ahead = products(0)
        dv_sum = dk_sum = None
        terms = []
        for g in range(grp):
            qg, dog, sc, dp = ahead
            if g + 1 < grp:
                ahead = products(g + 1)
            head_rows = slice(g * tq, (g + 1) * tq)
            p = jnp.exp2((sc - lse_ref[0, head_rows, 0:1]).astype(bf16))
            ds = p * (dp - delta_ref[0, head_rows, 0:1]).astype(bf16)
            dv_g, dk_g = _dot(p, dog, 0, 0), _dot(ds, qg, 0, 0)
            dv_sum = dv_g if dv_sum is None else dv_sum + dv_g
            dk_sum = dk_g if dk_sum is None else dk_sum + dk_g
            terms.append(_dot(ds, kk, 1, 0))
        dv_acc[...] += dv_sum
        dk_acc[...] += dk_sum
        mine = pl.ds(pl.multiple_of(qi * rows, rows), rows)
        term = jnp.concatenate(terms, axis=0)

        @pl.when(kj == 0)
        def _():
            dq_acc[mine, :] = term

        @pl.when(kj > 0)
        def _():
            dq_acc[mine, :] += term

        @pl.when(kj == nkv - 1)
        def _():
            total = dq_acc[mine, :]
            for g in range(grp):
                dq_ref[:, g * HEAD:(g + 1) * HEAD] = total[g * tq:(g + 1) * tq] * SCALE

        @pl.when(qi == nq - 1)
        def _():
            dk_ref[...] = dk_acc[...] * (1.0 / LOG2E)
            dv_ref[...] = dv_acc[...]

    qb = pl.BlockSpec((tq, grp * HEAD), lambda h, j, i: (i, h))
    kb = pl.BlockSpec((tk, HEAD), lambda h, j, i: (j, h))
    lb = pl.BlockSpec((1, rows, HEAD), lambda h, j, i: (h, i, 0))
    dqb = pl.BlockSpec((tq, grp * HEAD), lambda h, j, i: (jnp.where(j == nkv - 1, i, 0), h))
    return _pallas_call(
        body, out_shape=(_sds((s, t), f32), _sds((s, KV_WIDTH), f32), _sds((s, KV_WIDTH), f32)), grid=(KV_HEADS, nkv, nq),
        in_specs=[qb, kb, kb, qb, lb, lb], out_specs=(dqb, kb, kb),
        scratch_shapes=[pltpu.VMEM((nq * rows, HEAD), f32), pltpu.VMEM((tk, HEAD), f32), pltpu.VMEM((tk, HEAD), f32)],
        compiler_params=_cp(("parallel", "arbitrary", "arbitrary")), name=f"flash_bwd_{s}",
    )(q, k, v, do, lse, delta)


def _mem_heads(z_ref, kv_ref, gq_ref, gk_ref, h):
    cs = slice(h * HEAD, (h + 1) * HEAD)
    xv = z_ref[:, cs]
    r = lax.rsqrt(jnp.mean(xv * xv, axis=-1, keepdims=True) + EPS)
    xh = xv * r
    kx = kv_ref[:, cs]
    rk = lax.rsqrt(jnp.mean(kx * kx, axis=-1, keepdims=True) + EPS)
    kn = (kx * rk * gk_ref[...]).astype(bf16)
    vv = kv_ref[:, MEM_WIDTH + h * HEAD:MEM_WIDTH + (h + 1) * HEAD].astype(bf16)
    qn = (xh * gq_ref[...]).astype(bf16)
    sc = _dot(qn, kn, 1, 1) * SCALE
    e = jnp.exp(sc - jnp.max(sc, axis=-1, keepdims=True))
    p = e / jnp.sum(e, axis=-1, keepdims=True)
    return cs, r, xh, qn, kn, vv, p


def _mem_fwd(z, qblk, kv, g_mq, g_mk, cat):
    s = z.shape[0]
    nm = kv.shape[0]
    tr = _pick(s, (512, 256, 128))
    oblk = cat.shape[1] // MEM_WIDTH - 1

    def body(z_ref, kv_ref, gq_ref, gk_ref, cat_ref, o_ref):
        for h in range(MEM_HEADS):
            cs, _, _, _, _, vv, p = _mem_heads(z_ref, kv_ref, gq_ref, gk_ref, h)
            o_ref[:, cs] = _dot(p.astype(bf16), vv, 1, 0).astype(bf16)

    vec = pl.BlockSpec((1, HEAD), lambda i: (0, 0))
    return _pallas_call(
        body, out_shape=_sds(cat.shape, bf16), grid=(s // tr,),
        in_specs=[pl.BlockSpec((tr, MEM_WIDTH), lambda i: (i, qblk)), pl.BlockSpec((nm, 2 * MEM_WIDTH), lambda i: (0, 0)), vec, vec,
                  pl.BlockSpec(memory_space=pl.ANY)],
        out_specs=pl.BlockSpec((tr, MEM_WIDTH), lambda i: (i, oblk)),
        input_output_aliases={4: 0},
        compiler_params=_cp(("parallel",)), name=f"mem_fwd_{s}_{qblk}",
    )(z, kv, g_mq, g_mk, cat)


def _mem_bwd(z, qblk, kv, g_mq, g_mk, dcat, dz):
    s = z.shape[0]
    nm = kv.shape[0]
    tr = _pick(s, (512, 256, 128))
    dblk = dcat.shape[1] // MEM_WIDTH - 1

    def body(z_ref, kv_ref, gq_ref, gk_ref, dm_ref, dzin_ref, dz_ref, dkn_ref, dv_ref, dgq_ref):
        @pl.when(pl.program_id(0) == 0)
        def _():
            dkn_ref[...] = jnp.zeros_like(dkn_ref)
            dv_ref[...] = jnp.zeros_like(dv_ref)
            dgq_ref[...] = jnp.zeros_like(dgq_ref)

        for h in range(MEM_HEADS):
            cs, r, xh, qn, kn, vv, p = _mem_heads(z_ref, kv_ref, gq_ref, gk_ref, h)
            dm = dm_ref[:, cs]
            dv_ref[:, cs] += _dot(p.astype(bf16), dm, 0, 0)
            dp = _dot(dm, vv, 1, 1)
            ds = (p * (dp - jnp.sum(dp * p, axis=-1, keepdims=True)) * SCALE).astype(bf16)
            dqn = _dot(ds, kn, 1, 0)
            dkn_ref[:, cs] += _dot(ds, qn, 0, 0)
            dgq_ref[...] += jnp.sum(dqn * xh, axis=0, keepdims=True)
            gy = dqn * gq_ref[...]
            dz_ref[:, cs] = (r * (gy - xh * jnp.mean(gy * xh, axis=-1, keepdims=True))).astype(bf16)

    vec = pl.BlockSpec((1, HEAD), lambda i: (0, 0))
    acc = pl.BlockSpec((nm, MEM_WIDTH), lambda i: (0, 0))
    return _pallas_call(
        body, out_shape=(_sds(dz.shape, bf16), _sds((nm, MEM_WIDTH), f32), _sds((nm, MEM_WIDTH), f32), _sds((1, HEAD), f32)),
        grid=(s // tr,),
        in_specs=[pl.BlockSpec((tr, MEM_WIDTH), lambda i: (i, qblk)), pl.BlockSpec((nm, 2 * MEM_WIDTH), lambda i: (0, 0)), vec, vec,
                  pl.BlockSpec((tr, MEM_WIDTH), lambda i: (i, dblk)), pl.BlockSpec(memory_space=pl.ANY)],
        out_specs=(pl.BlockSpec((tr, MEM_WIDTH), lambda i: (i, qblk)), acc, acc, vec),
        input_output_aliases={5: 0},
        compiler_params=_cp(("arbitrary",)), name=f"mem_bwd_{s}_{qblk}",
    )(z, kv, g_mq, g_mk, dcat, dz)


def _mem_kv_bwd(kv, g_mk, dkn, dv):
    nm = kv.shape[0]

    def body(kv_ref, gk_ref, dkn_ref, dv_ref, dkv_ref, dgk_ref):
        dgk = jnp.zeros((1, HEAD), f32)
        for h in range(MEM_HEADS):
            cs = slice(h * HEAD, (h + 1) * HEAD)
            kx = kv_ref[:, cs]
            rk = lax.rsqrt(jnp.mean(kx * kx, axis=-1, keepdims=True) + EPS)
            kh = kx * rk
            dkn_h = dkn_ref[:, cs]
            dgk = dgk + jnp.sum(dkn_h * kh, axis=0, keepdims=True)
            gy = dkn_h * gk_ref[...]
            dkv_ref[:, cs] = (rk * (gy - kh * jnp.mean(gy * kh, axis=-1, keepdims=True))).astype(bf16)
        dkv_ref[:, MEM_WIDTH:] = dv_ref[...].astype(bf16)
        dgk_ref[...] = dgk

    return _pallas_call(
        body, out_shape=(_sds((nm, 2 * MEM_WIDTH), bf16), _sds((1, HEAD), f32)),
        compiler_params=pltpu.CompilerParams(vmem_limit_bytes=VMEM_LIMIT), name=f"mem_kv_bwd_{nm}",
    )(kv, g_mk, dkn, dv)


def _ffn_up(h2, g):
    m, kdim = h2.shape
    nd, _, n = g.shape
    half = nd // 2
    assert n % 128 == 0 and h2.dtype == bf16 and g.dtype == bf16
    tm = _pick(m, (512, 256, 128))

    def body(a_ref, wg_ref, wu_ref, gate_ref, up_ref, act_ref):
        av = a_ref[...]
        gt = _dot(av, wg_ref[0], 1, 0)
        up = _dot(av, wu_ref[0], 1, 0)
        gate_ref[...] = gt.astype(bf16)
        up_ref[...] = up.astype(bf16)
        act_ref[...] = (gt * jax.nn.sigmoid(gt) * up).astype(bf16)

    out = pl.BlockSpec((tm, n), lambda i, j: (i, j))
    return _pallas_call(
        body, out_shape=tuple(_sds((m, half * n), bf16) for _ in range(3)), grid=(m // tm, half),
        in_specs=[pl.BlockSpec((tm, kdim), lambda i, j: (i, 0)), pl.BlockSpec((1, kdim, n), lambda i, j: (j, 0, 0)),
                  pl.BlockSpec((1, kdim, n), lambda i, j: (j + half, 0, 0))],
        out_specs=(out, out, out),
        compiler_params=_cp(("parallel", "arbitrary")), name=f"ffn_up_{m}x{kdim}x{half * n}",
    )(h2, g, g)


def _ffn_down_bwd(dx, w_dn, gate, up):
    m, d = dx.shape
    ff = w_dn.shape[0]
    tm = _pick(m, (512, 256, 128))
    tf = _pick(ff, (1408, 1024, 512, 256, 128))

    nsub = 2 if tm % 32 == 0 else 1
    rs = tm // nsub

    def body(dx_ref, w_ref, g_ref, u_ref, dg_ref, du_ref):
        wv = w_ref[...]
        da = _dot(dx_ref[0:rs, :], wv, 1, 1)
        for r in range(nsub):
            da_next = _dot(dx_ref[(r + 1) * rs:(r + 2) * rs, :], wv, 1, 1) if r + 1 < nsub else None
            mine = slice(r * rs, (r + 1) * rs)
            gt = g_ref[mine, :].astype(f32)
            sg = jax.nn.sigmoid(gt)
            dg_ref[mine, :] = (da * u_ref[mine, :].astype(f32) * sg * (1.0 + gt * (1.0 - sg))).astype(bf16)
            du_ref[mine, :] = (da * gt * sg).astype(bf16)
            da = da_next

    tile = pl.BlockSpec((tm, tf), lambda i, j: (i, j))
    return _pallas_call(
        body, out_shape=(_sds((m, ff), bf16), _sds((m, ff), bf16)), grid=(m // tm, ff // tf),
        in_specs=[pl.BlockSpec((tm, d), lambda i, j: (i, 0)), pl.BlockSpec((tf, d), lambda i, j: (j, 0)), tile, tile],
        out_specs=(tile, tile),
        compiler_params=_cp(("parallel", "arbitrary")), name=f"ffn_down_bwd_{m}x{d}x{ff}",
    )(dx, w_dn, gate, up)


def _loss_head(y, target):
    s, d = y.shape
    tr = _pick(s, (512, 256, 128))

    def body(y_ref, t_ref, l_ref, dy_ref, dyb_ref):
        @pl.when(pl.program_id(0) == 0)
        def _():
            l_ref[...] = jnp.zeros_like(l_ref)

        err = y_ref[...] - t_ref[...]
        l_ref[...] += 0.5 * jnp.sum(jnp.mean(err * err, axis=-1, keepdims=True), axis=0, keepdims=True)
        dy = err * (1.0 / d)
        dy_ref[...] = dy
        dyb_ref[...] = dy.astype(bf16)

    row = pl.BlockSpec((tr, d), lambda i: (i, 0))
    return _pallas_call(
        body, out_shape=(_sds((1, HEAD), f32), _sds((s, d), f32), _sds((s, d), bf16)), grid=(s // tr,),
        in_specs=[row, row], out_specs=(pl.BlockSpec((1, HEAD), lambda i: (0, 0)), row, row),
        compiler_params=_cp(("arbitrary",)), name=f"loss_{s}x{d}",
    )(y, target)


def _place():
    return lax.axis_index("x"), lax.axis_index("y"), lax.axis_index("c")


def _tag(arrays):
    return "_".join("x".join(str(dd) for dd in a.shape) for a in arrays)


def _all_gather(shards):
    nw = len(shards)
    hbm = pl.BlockSpec(memory_space=pl.ANY)

    def body(*refs):
        x_refs, out_refs = refs[:nw], refs[nw:2 * nw]
        send_sems, recv_sems, local_sems = refs[2 * nw:]
        x, y, c = _place()
        me, sibling = (x, y, c), (x, y, 1 - c)
        chips = [(1 - x, y), (x, 1 - y), (1 - x, 1 - y)]

        def slot(w, place):
            px, py, pc = place
            return out_refs[w].at[4 * px + 2 * py + pc]

        def copy(k, w, block_of, to, from_input=False):
            return pltpu.make_async_remote_copy(
                src_ref=x_refs[w] if from_input else slot(w, block_of), dst_ref=slot(w, block_of),
                send_sem=send_sems.at[k, w], recv_sem=recv_sems.at[k, w], device_id=to, device_id_type=MESH)

        mine = [pltpu.make_async_copy(x_refs[w], slot(w, me), local_sems.at[w]) for w in range(nw)]
        for cp in mine:
            cp.start()
        first = []
        for w in range(nw):
            first.append(copy(0, w, me, sibling, from_input=True))
            first += [copy(1 + j, w, me, (*chip, c), from_input=True) for j, chip in enumerate(chips)]
        for cp in first:
            cp.start()
        passed = []
        for w in range(nw):
            for j, chip in enumerate(chips):
                copy(1 + j, w, (*chip, c), me).wait_recv()
                fwd = copy(4 + j, w, (*chip, c), sibling)
                fwd.start()
                passed.append(fwd)
        for w in range(nw):
            copy(0, w, sibling, me).wait_recv()
            for j, chip in enumerate(chips):
                copy(4 + j, w, (*chip, 1 - c), me).wait_recv()
        for cp in first + passed:
            cp.wait_send()
        for cp in mine:
            cp.wait()

    return _pallas_call(
        body, out_shape=tuple(_sds((N_DEV,) + a.shape, a.dtype) for a in shards), in_specs=[hbm] * nw, out_specs=tuple([hbm] * nw),
        scratch_shapes=[pltpu.SemaphoreType.DMA((7, nw)), pltpu.SemaphoreType.DMA((7, nw)), pltpu.SemaphoreType.DMA((nw,))],
        name=f"all_gather_{_tag(shards)}_{jnp.dtype(shards[0].dtype).name}",
    )(*shards)


def _swap_with_sibling(grads):
    nw = len(grads)
    nchip = N_DEV // 2
    hbm = pl.BlockSpec(memory_space=pl.ANY)

    def body(*refs):
        g_refs, got_refs = refs[:nw], refs[nw:2 * nw]
        send_sems, recv_sems = refs[2 * nw:]
        x, y, c = _place()
        copies = [pltpu.make_async_remote_copy(
            src_ref=g_refs[w].at[2 * k + (1 - c)], dst_ref=got_refs[w].at[k],
            send_sem=send_sems.at[w, k], recv_sem=recv_sems.at[w, k], device_id=(x, y, 1 - c), device_id_type=MESH)
            for w in range(nw) for k in range(nchip)]
        for cp in copies:
            cp.start()
        for cp in copies:
            cp.wait()

    return _pallas_call(
        body, out_shape=tuple(_sds((nchip,) + g.shape[1:], g.dtype) for g in grads), in_specs=[hbm] * nw, out_specs=tuple([hbm] * nw),
        scratch_shapes=[pltpu.SemaphoreType.DMA((nw, nchip)), pltpu.SemaphoreType.DMA((nw, nchip))],
        name=f"swap_sibling_{_tag(grads)}",
    )(*grads)


def _pair_sum(grad, got):
    nd, a, b = grad.shape
    nchip = nd // 2
    ta = _pick(a, (1024, 704, 512, 352, 256, 128, 64, 32, 16))

    def my_chip():
        return 2 * lax.axis_index("x") + lax.axis_index("y")

    def body(a_ref, b_ref, o_ref, land_ref):
        tot = (a_ref[...].astype(f32) + b_ref[...].astype(f32)).astype(o_ref.dtype)
        o_ref[...] = tot

        @pl.when(pl.program_id(1) == my_chip())
        def _():
            land_ref[...] = tot

    return _pallas_call(
        body, out_shape=(_sds(got.shape, grad.dtype), _sds(got.shape, grad.dtype)), grid=(a // ta, nchip),
        in_specs=[pl.BlockSpec((None, ta, b), lambda i, k: (2 * k + lax.axis_index("c"), i, 0)),
                  pl.BlockSpec((None, ta, b), lambda i, k: (k, i, 0))],
        out_specs=(pl.BlockSpec((None, ta, b), lambda i, k: (k, i, 0)),
                   pl.BlockSpec((None, ta, b), lambda i, k: (my_chip(), i, 0))),
        compiler_params=_cp(("parallel", "arbitrary")), name=f"pair_sum_{a}x{b}",
    )(grad, got)


_HBM = pl.BlockSpec(memory_space=pltpu.HBM)
_SEM = pl.BlockSpec(memory_space=pltpu.SEMAPHORE)
_ANY = pl.BlockSpec(memory_space=pl.ANY)
_DATAFLOW = pltpu.SideEffectType.DATAFLOW_SIDE_EFFECTING


def _in_hbm(a):
    return pltpu.with_memory_space_constraint(a, pltpu.HBM)


def _exchange_begin(bufs, nw, route, after, copies_of, n_copies, name):
    nb = len(bufs)

    def body(*refs):
        send_sems, recv_sems = refs[nb + 2], refs[nb + 3]
        for w in range(nw):
            for k, (src, dst, to) in enumerate(copies_of(w, refs[:nb])):
                pltpu.make_async_remote_copy(src_ref=src, dst_ref=dst, send_sem=send_sems.at[k * nw + w],
                                             recv_sem=recv_sems.at[k * nw + w], device_id=to, device_id_type=MESH).start()

    out = _pallas_call(
        body, name=name,
        out_shape=(pltpu.SemaphoreType.DMA((n_copies * nw,)), pltpu.SemaphoreType.DMA((n_copies * nw,)),
                   *[pltpu.HBM(a.shape, a.dtype) for a in bufs], pltpu.HBM(route.shape, route.dtype)),
        in_specs=[_HBM] * (nb + 1) + [_ANY], out_specs=(_SEM, _SEM, *[_HBM] * (nb + 1)),
        input_output_aliases={i: 2 + i for i in range(nb + 1)},
        compiler_params=pltpu.CompilerParams(has_side_effects=_DATAFLOW),
    )(*[_in_hbm(a) for a in bufs], _in_hbm(route), after)
    return (out[0], out[1], out[2:2 + nb], nw), out[2 + nb]


def _exchange_end(handle, after, copies_of, n_copies, name):
    send_sems, recv_sems, thru, nw = handle
    nb = len(thru)

    def body(*refs):
        send_sems, recv_sems = refs[nb], refs[nb + 1]
        for w in range(nw):
            for k, (src, dst, to) in enumerate(copies_of(w, refs[:nb])):
                cp = pltpu.make_async_remote_copy(src_ref=src, dst_ref=dst, send_sem=send_sems.at[k * nw + w],
                                                  recv_sem=recv_sems.at[k * nw + w], device_id=to, device_id_type=MESH)
                cp.wait_send()
                cp.wait_recv()

    out = _pallas_call(
        body, name=name, out_shape=tuple(pltpu.HBM(a.shape, a.dtype) for a in thru),
        in_specs=[_HBM] * nb + [_SEM, _SEM, _ANY], out_specs=tuple([_HBM] * nb),
        input_output_aliases={i: i for i in range(nb)},
        compiler_params=pltpu.CompilerParams(has_side_effects=_DATAFLOW),
    )(*thru, send_sems, recv_sems, after)
    return list(out)


def _my_slot():
    return 4 * lax.axis_index("x") + 2 * lax.axis_index("y") + lax.axis_index("c")


def _shard_into_land(w_all, idx):
    _, a, b = w_all.shape
    ta = next(cc for cc in (1024, 704, 512, 352, 256, 128, 64, 32, 16) if a % cc == 0 and (cc * b * 4 <= 2 ** 21 or cc == 16))

    def body(w_ref, o_ref):
        o_ref[...] = w_ref[...].astype(bf16)

    return _pallas_call(
        body, out_shape=_sds((N_DEV, a, b), bf16), grid=(a // ta,),
        in_specs=[pl.BlockSpec((None, ta, b), lambda i: (idx, i, 0))],
        out_specs=pl.BlockSpec((None, ta, b), lambda i: (_my_slot(), i, 0)),
        compiler_params=_cp(("parallel",)), name=f"shard_into_land_{a}x{b}_{idx}",
    )(w_all)


def _gather_copies(w, land_refs):
    x, y, c = _place()
    blk = land_refs[w].at[4 * x + 2 * y + c]
    return [(blk, blk, to) for to in ((x, y, 1 - c), (1 - x, y, c), (x, 1 - y, c), (1 - x, 1 - y, c))]


def _gather_begin(lands, route, after, tag):
    return _exchange_begin(lands, len(lands), route, after, _gather_copies, 4, f"gather_begin_{tag}")


def _gather_end(handle, after, tag):
    return _exchange_end(handle, after, _gather_copies, 4, f"gather_end_{tag}")


def _gather_pass_on(lands):
    nw = len(lands)

    def body(*refs):
        l_refs = refs[nw:2 * nw]
        send_sems, recv_sems = refs[2 * nw:]
        x, y, c = _place()
        copies = []
        for w in range(nw):
            for j, (px, py) in enumerate([(1 - x, y), (x, 1 - y), (1 - x, 1 - y)]):
                blk = l_refs[w].at[4 * px + 2 * py + c]
                copies.append(pltpu.make_async_remote_copy(
                    src_ref=blk, dst_ref=blk, send_sem=send_sems.at[j, w], recv_sem=recv_sems.at[j, w],
                    device_id=(x, y, 1 - c), device_id_type=MESH))
        for cp in copies:
            cp.start()
        for cp in copies:
            cp.wait_send()
        for w in range(nw):
            for j, (px, py) in enumerate([(1 - x, y), (x, 1 - y), (1 - x, 1 - y)]):
                blk = l_refs[w].at[4 * px + 2 * py + (1 - c)]
                pltpu.make_async_remote_copy(src_ref=blk, dst_ref=blk, send_sem=send_sems.at[j, w], recv_sem=recv_sems.at[j, w],
                                             device_id=(x, y, 1 - c), device_id_type=MESH).wait_recv()

    return _pallas_call(
        body, out_shape=tuple(_sds(a.shape, a.dtype) for a in lands), in_specs=[_ANY] * nw, out_specs=tuple([_ANY] * nw),
        input_output_aliases={w: w for w in range(nw)},
        scratch_shapes=[pltpu.SemaphoreType.DMA((3, nw)), pltpu.SemaphoreType.DMA((3, nw))],
        name=f"gather_pass_on_{_tag(lands)}",
    )(*lands)


def _scatter_copies(w, refs):
    x, y, c = _place()
    nw = len(refs) // 2
    dst = refs[nw + w].at[2 * x + y]
    return [(refs[w].at[2 * px + py], dst, (px, py, c)) for px, py in ((1 - x, y), (x, 1 - y), (1 - x, 1 - y))]


def _scatter_begin(psums, lands, route, after, tag):
    return _exchange_begin(list(psums) + list(lands), len(psums), route, after, _scatter_copies, 3, f"scatter_begin_{tag}")


def _scatter_end(handle, after, tag):
    return _exchange_end(handle, after, _scatter_copies, 3, f"scatter_end_{tag}")


def _adamw(parts, w_all, m_all, v_all, l, carried):
    nparts, a, b = parts.shape
    nl = w_all.shape[0]
    ta = next(cc for cc in (1024, 704, 512, 352, 256, 128, 64, 32, 16, 8) if a % cc == 0 and (cc * b * 4 <= 2 ** 20 or cc == 8))
    c1 = 1.0 / (1.0 - ADAM_B1 ** ADAM_STEP)
    c2 = 1.0 / (1.0 - ADAM_B2 ** ADAM_STEP)

    def body(p_ref, w_ref, m_ref, v_ref, *rest):
        g_out, d_out, m_out, v_out = rest[-4:]
        g = p_ref[0].astype(f32)
        for k in range(1, nparts):
            g = g + p_ref[k].astype(f32)
        m_new = ADAM_B1 * m_ref[...] + (1.0 - ADAM_B1) * g
        v_new = ADAM_B2 * v_ref[...] + (1.0 - ADAM_B2) * (g * g)
        m_hat = m_new * c1
        v_hat = v_new * c2
        g_out[...] = g
        d_out[...] = -ADAM_LR * (m_hat / (jnp.sqrt(v_hat) + ADAM_EPS) + ADAM_WD * w_ref[...])
        m_out[...] = m_new
        v_out[...] = v_new

    one = pl.BlockSpec((None, ta, b), lambda i: (l, i, 0))
    keep = [] if carried is None else [pl.BlockSpec(memory_space=pl.ANY)] * 4
    return _pallas_call(
        body, out_shape=tuple(_sds((nl, a, b), f32) for _ in range(4)), grid=(a // ta,),
        in_specs=[pl.BlockSpec((nparts, ta, b), lambda i: (0, i, 0)), one, one, one] + keep, out_specs=(one, one, one, one),
        input_output_aliases=({} if carried is None else {4 + q: q for q in range(4)}),
        compiler_params=_cp(("parallel",)), name=f"adamw_{nparts}x{nl}x{a}x{b}_{l}{'' if carried is None else '_carried'}",
    )(parts, w_all, m_all, v_all, *(carried or ()))


def _to_flat(arrays):
    flat = jnp.concatenate([a.reshape(-1).astype(f32) for a in arrays])
    rows = -(-flat.shape[0] // (8 * LANES)) * 8
    return jnp.pad(flat, (0, rows * LANES - flat.shape[0])).reshape(rows, LANES)


def _from_flat(flat, shapes):
    flat = flat.reshape(-1)
    out, off = [], 0
    for shp in shapes:
        n = 1
        for dd in shp:
            n *= dd
        out.append(flat[off:off + n].reshape(shp))
        off += n
    return out


def kernel(x, mem, g_mix, g_ffn, w_in_a, g_v_a, w_spatial, b_spatial, w_in_b, g_q_b, g_k_b, g_mem, w_mem_kv, g_mq, g_mk, w_out, w_gate_up, w_down, loss_target, m_g_mix, m_g_ffn, m_w_in_a, m_g_v_a, m_w_spatial, m_b_spatial, m_w_in_b, m_g_q_b, m_g_k_b, m_g_mem, m_w_mem_kv, m_g_mq, m_g_mk, m_w_out, m_w_gate_up, m_w_down, v_g_mix, v_g_ffn, v_w_in_a, v_g_v_a, v_w_spatial, v_b_spatial, v_w_in_b, v_g_q_b, v_g_k_b, v_g_mem, v_w_mem_kv, v_g_mq, v_g_mk, v_w_out, v_w_gate_up, v_w_down):
    given = dict(locals())
    depth = g_mix.shape[0]
    s, d = x.shape[1], x.shape[2]
    nm = mem.shape[1]
    t = d - MEM_WIDTH
    ff = w_down.shape[1] * N_DEV
    x0 = x.reshape(s, d)
    mem0 = mem.reshape(nm, d)
    target = loss_target.reshape(s, d)
    tables = _rope_tables(s)

    big_names = ("w_in", "w_mem_kv", "w_out", "w_gate_up", "w_down")

    def stacked_key(name, l):
        if name == "w_in":
            return ("w_in_a" if l % 2 == 0 else "w_in_b"), l // 2
        return name, l

    n_mix = 3

    def own_shards(l):
        return [_shard_into_land(given[key], idx) for key, idx in (stacked_key(name, l) for name in big_names)]

    def gather_start(l, route, after):
        lands = cast_shards[l]
        h_mix, route = _gather_begin(lands[:n_mix], route, after, f"mix{l}")
        h_ffn, route = _gather_begin(lands[n_mix:], route, after, f"ffn{l}")
        return h_mix, h_ffn, route

    def gather_finish(handle, after, tag):
        return list(_gather_pass_on(_gather_end(handle, after, tag)))

    saved = []
    xc = x0
    cast_shards = {0: own_shards(0)}
    h_in, route0 = _gather_begin(cast_shards[0][:1], g_mix[0].reshape(1, d), mem0, "in0")
    h_rest, route0 = _gather_begin(cast_shards[0][1:n_mix], route0, mem0, "rest0")
    h_ffn, _ = _gather_begin(cast_shards[0][n_mix:], route0, mem0, "ffn0")
    for l in range(1, depth):
        cast_shards[l] = own_shards(l)
    w_mix = gather_finish(h_in, cast_shards[depth - 1][0], "in0")
    w_ffn = None
    for l in range(depth):
        is_a = l % 2 == 0
        g_in = w_mix[0]
        qblk = (N_DEV * g_in.shape[2] - MEM_WIDTH) // MEM_WIDTH

        gm_row, gf_row, gmem_row = g_mix[l].reshape(1, d), g_ffn[l].reshape(1, d), g_mem[l].reshape(1, d)
        gmq_row, gmk_row = g_mq[l].reshape(1, HEAD), g_mk[l].reshape(1, HEAD)
        if l + 1 < depth:
            next_mix, next_ffn, gm_row = gather_start(l + 1, gm_row, g_in)
        h = _rms_fwd(xc, gm_row)
        z = _mm_cols_fwd(h, g_in, f32)
        if l == 0:
            w_mix = w_mix + gather_finish(h_rest, z, "rest0")
        g_kv, g_out = w_mix[1:]
        w_kv, w_o = (g.reshape(-1, g.shape[2]) for g in (g_kv, g_out))
        if is_a:
            ia = l // 2
            mix = dict(g_v=g_v_a[ia].reshape(1, t), w_s=w_spatial[ia], b_t=b_spatial[ia].T)
            cat = _gmlp_fwd(z, mix["g_v"], mix["w_s"], mix["b_t"])
        else:
            ib = l // 2
            mix = dict(g_q=g_q_b[ib].reshape(1, HEAD), g_k=g_k_b[ib].reshape(1, HEAD))
            q, k, v = _attn_prep_fwd(z, mix["g_q"], mix["g_k"], tables, t)
            cat, lse = _flash_fwd(q, k, v)
            mix.update(q=q, k=k, v=v, lse=lse)
        hm = _rms_fwd(mem0, gmem_row)
        kv = _matmul(hm, w_kv)
        cat = _mem_fwd(z, qblk, kv, gmq_row, gmk_row, cat)
        x1 = _matmul(cat, w_o, res=xc)
        if l == 0:
            w_ffn = gather_finish(h_ffn, x1, "ffn0")
        g_gu, g_dn = w_ffn
        w_dn = g_dn.reshape(-1, g_dn.shape[2])
        h2 = _rms_fwd(x1, gf_row)
        gate, up, act = _ffn_up(h2, g_gu)
        x2 = _matmul(act, w_dn, res=x1)
        saved.append(dict(x=xc, h=h, z=z, mix=mix, cat=cat, hm=hm, kv=kv, x1=x1, h2=h2, gate=gate, up=up, act=act, qblk=qblk,
                          w=(g_in, w_kv, w_o, g_gu, w_dn), rows=(gm_row, gf_row, gmem_row, gmq_row, gmk_row)))
        if l + 1 < depth:
            w_mix = gather_finish(next_mix, x2, f"mix{l + 1}")
            w_ffn = gather_finish(next_ffn, x2, f"ffn{l + 1}")
        xc = x2

    loss_row, dy, dy_b = _loss_head(xc, target)
    loss = lax.psum(loss_row[0, 0], ("x", "y", "c"))

    small = {n: [None] * given[n].shape[0] for n in ("g_mix", "g_ffn", "g_v_a", "w_spatial", "b_spatial", "g_q_b", "g_k_b", "g_mem", "g_mq", "g_mk")}
    big_out = {}

    def scatter_start(grads, route, after, tag):
        got = _swap_with_sibling(grads)
        sums = [_pair_sum(g, r) for g, r in zip(grads, got)]
        return _scatter_begin([p for p, _ in sums], [q for _, q in sums], route, after, tag)

    def scatter_finish(handle, names, l, after, tag):
        arrived = _scatter_end(handle, after, tag)[len(names):]
        for name, parts in zip(names, arrived):
            key, idx = stacked_key(name, l)
            big_out[key] = _adamw(parts, given[key], given["m_" + key], given["v_" + key], idx, big_out.get(key))

    pend_mix = None
    dx, dx_b = dy, dy_b
    for l in reversed(range(depth)):
        sv = saved[l]
        is_a = l % 2 == 0
        g_in, w_kv, w_o, g_gu, w_dn = sv["w"]
        gm_row, gf_row, gmem_row, gmq_row, gmk_row = sv["rows"]
        mix = sv["mix"]
        dw_dn = _matmul(sv["act"], dx_b, ta=True, out_dtype=bf16)
        dgu = _ffn_down_bwd(dx_b, w_dn, sv["gate"], sv["up"])
        dw_gu = _mm_cols_wgrad(sv["h2"], dgu, g_gu.shape[2])
        dh2 = _mm_cols_dgrad(dgu, g_gu)
        dx1, dx1_b, dgf = _rms_bwd(sv["x1"], gf_row, dh2, dx)
        small["g_ffn"][l] = dgf.reshape(d)
        pend_ffn, dx1_b = scatter_start([dw_gu, dw_dn.reshape(N_DEV, -1, d)], dx1_b, dx1, f"ffn{l}")
        if pend_mix is not None:
            scatter_finish(*pend_mix, dx1_b, f"mix{l + 1}")
        dw_o = _matmul(sv["cat"], dx1_b, ta=True, out_dtype=bf16)
        dcat = _matmul(dx1_b, w_o, tb=True, out_dtype=bf16)
        if is_a:
            dz, dws, dbt, dgv = _gmlp_bwd(sv["z"], mix["g_v"], mix["w_s"], mix["b_t"], dcat)
            small["w_spatial"][l // 2], small["b_spatial"][l // 2], small["g_v_a"][l // 2] = dws, dbt.T, dgv.reshape(t)
        else:
            dq, dk, dv = _flash_bwd(mix["q"], mix["k"], mix["v"], sv["cat"], dcat, mix["lse"])
            dz, dgq, dgk = _attn_prep_bwd(sv["z"], mix["g_q"], mix["g_k"], tables, dq, dk, dv, t)
            small["g_q_b"][l // 2], small["g_k_b"][l // 2] = dgq.reshape(HEAD), dgk.reshape(HEAD)
        dz, dkn, dvm, dgmq = _mem_bwd(sv["z"], sv["qblk"], sv["kv"], gmq_row, gmk_row, dcat, dz)
        dkv, dgmk = _mem_kv_bwd(sv["kv"], gmk_row, dkn, dvm)
        dw_kv = _matmul(sv["hm"], dkv, ta=True, out_dtype=bf16)
        dhm = _matmul(dkv, w_kv, tb=True)
        small["g_mem"][l] = _rms_bwd(mem0, gmem_row, dhm, None).reshape(d)
        small["g_mq"][l] = dgmq.reshape(HEAD)
        small["g_mk"][l] = dgmk.reshape(HEAD)
        dw_in = _mm_cols_wgrad(sv["h"], dz, g_in.shape[2])
        dh = _mm_cols_dgrad(dz, g_in)
        dx, dx_b, dgm = _rms_bwd(sv["x"], gm_row, dh, dx1)
        small["g_mix"][l] = dgm.reshape(d)

        handle, dx_b = scatter_start([dw_in] + [dw.reshape(N_DEV, -1, dw.shape[1]) for dw in (dw_kv, dw_o)], dx_b, dx, f"mix{l}")
        pend_mix = (handle, big_names[:n_mix], l)
        scatter_finish(pend_ffn, big_names[n_mix:], l, dx_b, f"ffn{l}")
    small_names = tuple(small)
    small_grads = [jnp.stack(small[n]) for n in small_names]
    small_shapes = [g.shape for g in small_grads]
    (all_parts,) = _all_gather([_to_flat(small_grads)])
    souts = _adamw(all_parts, *[_to_flat([given[p + n] for n in small_names])[None] for p in ("", "m_", "v_")], 0, None)
    small_out = dict(zip(small_names, zip(*[_from_flat(flat, small_shapes) for flat in souts])))
    scatter_finish(*pend_mix, souts[0], "mix0")

    weights = ("g_mix", "g_ffn", "w_in_a", "g_v_a", "w_spatial", "b_spatial", "w_in_b", "g_q_b", "g_k_b", "g_mem", "w_mem_kv",
               "g_mq", "g_mk", "w_out", "w_gate_up", "w_down")
    results = {n: (small_out[n] if n in small_out else big_out[n]) for n in weights}
    grad_x = dx.reshape(1, s, d)
    return (loss, grad_x, *[results[n][kind] for kind in range(4) for n in weights])
```

```python
import functools

import jax
import jax.numpy as jnp
from jax import lax
from jax.experimental import pallas as pl
from jax.experimental.pallas import tpu as pltpu

f32 = jnp.float32
bf16 = jnp.bfloat16

HEAD = 128
CHUNK = 128
GRID_W = 64
MEM_HEADS = 4
KV_HEADS = 4
MEM_WIDTH = MEM_HEADS * HEAD
KV_WIDTH = KV_HEADS * HEAD
ROPE_THETA = 10000.0
ROPE_PAIRS = HEAD // 4
EPS = 1e-6
SCALE = HEAD ** -0.5
LOG2E = 1.4426950408889634
N_DEV = 8
LANES = 1024
VMEM_LIMIT = 56 * 1024 * 1024

ADAM_LR, ADAM_B1, ADAM_B2, ADAM_EPS, ADAM_WD, ADAM_STEP = 0.001, 0.9, 0.999, 1e-08, 0.01, 10

MESH = pl.DeviceIdType.MESH
_pallas_call = pl.pallas_call


def _pick(dim, cands):
    for c in cands:
        if dim % c == 0:
            return c
    return dim


def _cp(sem):
    return pltpu.CompilerParams(dimension_semantics=sem, vmem_limit_bytes=VMEM_LIMIT)


def _sds(shape, dtype):
    return jax.ShapeDtypeStruct(shape, dtype)


def _dot(a, b, ca, cb):
    return lax.dot_general(a, b, (((ca,), (cb,)), ((), ())), preferred_element_type=f32)


def _gelu(z):
    return 0.5 * z * (1.0 + lax.erf(z * 0.7071067811865476))


def _gelu_grad(z):
    return 0.5 * (1.0 + lax.erf(z * 0.7071067811865476)) + z * jnp.exp(-0.5 * z * z) * 0.3989422804014327


def _rot(x, sin_a, sin_b):
    return pltpu.roll(x, 96, 1) * sin_a + pltpu.roll(x, 32, 1) * sin_b


def _matmul(a, b, *, ta=False, tb=False, out_dtype=f32, res=None, tm=None, tn=None, tk=None):
    assert a.dtype == bf16 and b.dtype == bf16
    kdim, m = a.shape if ta else a.shape[::-1]
    n, k2 = b.shape if tb else b.shape[::-1]
    assert kdim == k2, (a.shape, b.shape, ta, tb)
    tm = tm or _pick(m, (1024, 1408, 512, 256, 128))
    tn = tn or _pick(n, (1024, 1408, 512, 256, 128))
    if tk is None:
        tk = kdim if kdim <= 2048 else _pick(kdim, (2816, 1024, 512, 256, 128))
    nk = kdim // tk
    ca, cb = (0 if ta else 1), (1 if tb else 0)
    has_res = res is not None

    def body(*refs):
        a_ref, b_ref = refs[0], refs[1]
        r_ref = refs[2] if has_res else None
        o_ref = refs[3] if has_res else refs[2]
        prod = _dot(a_ref[...], b_ref[...], ca, cb)
        if nk == 1:
            if has_res:
                prod = prod + r_ref[...]
            o_ref[...] = prod.astype(o_ref.dtype)
        else:
            acc = refs[-1]
            k = pl.program_id(2)

            @pl.when(k == 0)
            def _():
                acc[...] = prod

            @pl.when(k > 0)
            def _():
                acc[...] += prod

            @pl.when(k == nk - 1)
            def _():
                out = acc[...]
                if has_res:
                    out = out + r_ref[...]
                o_ref[...] = out.astype(o_ref.dtype)

    a_spec = pl.BlockSpec((tk, tm), lambda i, j, k: (k, i)) if ta else pl.BlockSpec((tm, tk), lambda i, j, k: (i, k))
    b_spec = pl.BlockSpec((tn, tk), lambda i, j, k: (j, k)) if tb else pl.BlockSpec((tk, tn), lambda i, j, k: (k, j))
    o_spec = pl.BlockSpec((tm, tn), lambda i, j, k: (i, j))
    in_specs = [a_spec, b_spec] + ([o_spec] if has_res else [])
    args = (a, b) + ((res,) if has_res else ())
    mode = ("t" if ta else "n") + ("t" if tb else "n")
    return _pallas_call(
        body, out_shape=_sds((m, n), out_dtype), grid=(m // tm, n // tn, nk),
        in_specs=in_specs, out_specs=o_spec,
        scratch_shapes=([pltpu.VMEM((tm, tn), f32)] if nk > 1 else []),
        compiler_params=_cp(("parallel", "parallel", "arbitrary")),
        name=f"mm_{mode}_{m}x{kdim}x{n}{'_res' if has_res else ''}_{jnp.dtype(out_dtype).name}",
    )(*args)


def _shards_per_step(n, pair_bytes=0):
    p = 2 if (n % 128 != 0 or 0 < 2 * pair_bytes <= VMEM_LIMIT // 2) else 1
    assert (p * n) % 128 == 0 and N_DEV % p == 0
    return p


def _lane_pieces(v, p, n):
    return [v] if p == 1 else [v[:, q * n:(q + 1) * n] for q in range(p)]


def _mm_cols_fwd(a, g, out_dtype):
    m, kdim = a.shape
    nd, k2, n = g.shape
    assert kdim == k2 and a.dtype == bf16 and g.dtype == bf16
    p = _shards_per_step(n)
    tm = _pick(m, (1024, 512, 256, 128))

    def body(a_ref, g_ref, o_ref):
        av = a_ref[...]
        parts = [_dot(av, g_ref[q], 1, 0) for q in range(p)]
        out = parts[0] if p == 1 else jnp.concatenate(parts, axis=1)
        o_ref[...] = out.astype(o_ref.dtype)

    return _pallas_call(
        body, out_shape=_sds((m, nd * n), out_dtype), grid=(m // tm, nd // p),
        in_specs=[pl.BlockSpec((tm, kdim), lambda i, j: (i, 0)), pl.BlockSpec((p, kdim, n), lambda i, j: (j, 0, 0))],
        out_specs=pl.BlockSpec((tm, p * n), lambda i, j: (i, j)),
        compiler_params=_cp(("parallel", "arbitrary")), name=f"mm_cols_fwd_{m}x{kdim}x{nd * n}_{jnp.dtype(out_dtype).name}",
    )(a, g)


def _pick_part(refs, nparts, step, per):
    val = refs[0][...]
    for hh in range(1, nparts):
        val = jnp.where(step >= hh * per, refs[hh][...], val)
    return val


def _part_step(step, hh, per):
    return jnp.clip(step - hh * per, 0, per - 1)


def _mm_cols_dgrad(dz, g):
    nd, kdim, n = g.shape
    p = _shards_per_step(n, pair_bytes=2 * kdim * n * 2)
    nj = nd // p
    parts = dz if isinstance(dz, tuple) else (dz,)
    m = parts[0].shape[0]
    nn = sum(part.shape[1] for part in parts)
    assert nn == nd * n and all(part.dtype == bf16 for part in parts) and g.dtype == bf16
    tm = _pick(m, (512, 256, 128))
    nparts = len(parts)
    per = nj // nparts

    def body(*refs):
        g_ref, o_ref, acc = refs[nparts:]
        j = pl.program_id(1)
        tot = None
        for q, piece in enumerate(_lane_pieces(_pick_part(refs, nparts, j, per), p, n)):
            dd = _dot(piece, g_ref[q], 1, 1)
            tot = dd if tot is None else tot + dd

        @pl.when(j == 0)
        def _():
            acc[...] = tot

        @pl.when(j > 0)
        def _():
            acc[...] += tot

        @pl.when(j == nj - 1)
        def _():
            o_ref[...] = acc[...]

    return _pallas_call(
        body, out_shape=_sds((m, kdim), f32), grid=(m // tm, nj),
        in_specs=[pl.BlockSpec((tm, p * n), (lambda hh: lambda i, j: (i, _part_step(j, hh, per)))(hh)) for hh in range(nparts)]
        + [pl.BlockSpec((p, kdim, n), lambda i, j: (j, 0, 0))],
        out_specs=pl.BlockSpec((tm, kdim), lambda i, j: (i, 0)),
        scratch_shapes=[pltpu.VMEM((tm, kdim), f32)],
        compiler_params=_cp(("parallel", "arbitrary")), name=f"mm_cols_dgrad_{m}x{nn}x{kdim}_{nparts}",
    )(*parts, g)


def _mm_cols_wgrad(a, dz, n):
    s, kdim = a.shape
    parts = dz if isinstance(dz, tuple) else (dz,)
    nparts = len(parts)
    nd = sum(part.shape[1] for part in parts) // n
    assert a.dtype == bf16 and all(part.dtype == bf16 for part in parts)
    p = _shards_per_step(n)
    per = nd // p // nparts
    tkw = _pick(kdim, (1024, 512, 256, 128))
    ts = _pick(s, (2048, 1024, 512, 256, 128))
    ns = s // ts

    def body(a_ref, *refs):
        o_ref, acc = refs[nparts:]
        si = pl.program_id(2)
        av = a_ref[...]
        prods = [_dot(av, piece, 0, 0) for piece in _lane_pieces(_pick_part(refs, nparts, pl.program_id(1), per), p, n)]

        @pl.when(si == 0)
        def _():
            for q in range(p):
                acc[q] = prods[q]

        @pl.when(si > 0)
        def _():
            for q in range(p):
                acc[q] += prods[q]

        @pl.when(si == ns - 1)
        def _():
            o_ref[...] = acc[...].astype(bf16)

    return _pallas_call(
        body, out_shape=_sds((nd, kdim, n), bf16), grid=(kdim // tkw, nd // p, ns),
        in_specs=[pl.BlockSpec((ts, tkw), lambda i, j, k: (k, i))]
        + [pl.BlockSpec((ts, p * n), (lambda hh: lambda i, j, k: (k, _part_step(j, hh, per)))(hh)) for hh in range(nparts)],
        out_specs=pl.BlockSpec((p, tkw, n), lambda i, j, k: (j, i, 0)),
        scratch_shapes=[pltpu.VMEM((p, tkw, n), f32)],
        compiler_params=_cp(("parallel", "parallel", "arbitrary")), name=f"mm_cols_wgrad_{kdim}x{s}x{nd * n}_{nparts}",
    )(a, *parts)


def _rms_fwd(x, g_row):
    s, d = x.shape
    tr = _pick(s, (512, 256, 128))

    def body(x_ref, g_ref, o_ref):
        xv = x_ref[...]
        r = lax.rsqrt(jnp.mean(xv * xv, axis=-1, keepdims=True) + EPS)
        o_ref[...] = (xv * r * g_ref[...]).astype(bf16)

    return _pallas_call(
        body, out_shape=_sds((s, d), bf16), grid=(s // tr,),
        in_specs=[pl.BlockSpec((tr, d), lambda i: (i, 0)), pl.BlockSpec((1, d), lambda i: (0, 0))],
        out_specs=pl.BlockSpec((tr, d), lambda i: (i, 0)),
        compiler_params=_cp(("parallel",)), name=f"rms_fwd_{s}x{d}",
    )(x, g_row)


def _rms_bwd(x, g_row, dh, dres):
    s, d = x.shape
    tr = _pick(s, (512, 256, 128))
    with_dx = dres is not None

    def body(*refs):
        if with_dx:
            x_ref, g_ref, dh_ref, dres_ref, dx_ref, dxb_ref, dg_ref = refs
        else:
            x_ref, g_ref, dh_ref, dg_ref = refs

        @pl.when(pl.program_id(0) == 0)
        def _():
            dg_ref[...] = jnp.zeros_like(dg_ref)

        xv = x_ref[...]
        r = lax.rsqrt(jnp.mean(xv * xv, axis=-1, keepdims=True) + EPS)
        xh = xv * r
        dy = dh_ref[...].astype(f32)
        dg_ref[...] += jnp.sum(dy * xh, axis=0, keepdims=True)
        if with_dx:
            gy = dy * g_ref[...]
            dx = dres_ref[...] + r * (gy - xh * jnp.mean(gy * xh, axis=-1, keepdims=True))
            dx_ref[...] = dx
            dxb_ref[...] = dx.astype(bf16)

    row = pl.BlockSpec((tr, d), lambda i: (i, 0))
    vec = pl.BlockSpec((1, d), lambda i: (0, 0))
    if with_dx:
        return _pallas_call(
            body, out_shape=(_sds((s, d), f32), _sds((s, d), bf16), _sds((1, d), f32)), grid=(s // tr,),
            in_specs=[row, vec, row, row], out_specs=(row, row, vec),
            compiler_params=_cp(("arbitrary",)), name=f"rms_bwd_{s}x{d}",
        )(x, g_row, dh, dres)
    return _pallas_call(
        body, out_shape=_sds((1, d), f32), grid=(s // tr,),
        in_specs=[row, vec, row], out_specs=vec,
        compiler_params=_cp(("arbitrary",)), name=f"rms_bwd_gain_{s}x{d}",
    )(x, g_row, dh)


def _gmlp_rows(s):
    return CHUNK * (2 if (s // CHUNK) % 2 == 0 else 1)


def _gmlp_fwd(z, g_v, w_s, b_t):
    s = z.shape[0]
    t = g_v.shape[1]
    ng = t // HEAD
    rb = _gmlp_rows(s)

    def body(z_ref, gv_ref, ws_ref, bt_ref, o_ref):
        for ci in range(rb // CHUNK):
            lo = ci * CHUNK
            a = _gelu(z_ref[lo:lo + CHUNK, :])
            u, vv = a[:, :t], a[:, t:]
            r = lax.rsqrt(jnp.mean(vv * vv, axis=-1, keepdims=True) + EPS)
            vn = (vv * r * gv_ref[...]).astype(bf16)
            for g in range(ng):
                cs = slice(g * HEAD, (g + 1) * HEAD)
                sg = _dot(ws_ref[g].astype(bf16), vn[:, cs], 1, 0) + bt_ref[:, g:g + 1]
                o_ref[lo:lo + CHUNK, cs] = (u[:, cs] * sg).astype(bf16)

    return _pallas_call(
        body, out_shape=_sds((s, t + MEM_WIDTH), bf16), grid=(s // rb,),
        in_specs=[pl.BlockSpec((rb, 2 * t), lambda i: (i, 0)), pl.BlockSpec((1, t), lambda i: (0, 0)),
                  pl.BlockSpec((ng, CHUNK, CHUNK), lambda i: (0, 0, 0)), pl.BlockSpec((CHUNK, ng), lambda i: (0, 0))],
        out_specs=pl.BlockSpec((rb, t), lambda i: (i, 0)),
        compiler_params=_cp(("parallel",)), name=f"gmlp_fwd_{s}",
    )(z, g_v, w_s, b_t)


def _gmlp_bwd(z, g_v, w_s, b_t, dtok):
    s = z.shape[0]
    t = g_v.shape[1]
    ng = t // HEAD
    rb = _gmlp_rows(s)
    nsteps = s // rb

    def body(z_ref, gv_ref, ws_ref, bt_ref, dt_ref, dz_ref, dws_ref, dbt_ref, dgv_ref, ds_acc):
        step = pl.program_id(0)

        @pl.when(step == 0)
        def _():
            dws_ref[...] = jnp.zeros_like(dws_ref)
            dgv_ref[...] = jnp.zeros_like(dgv_ref)
            ds_acc[...] = jnp.zeros_like(ds_acc)

        for ci in range(rb // CHUNK):
            lo = ci * CHUNK
            zz = z_ref[lo:lo + CHUNK, :]
            a = _gelu(zz)
            u, vv = a[:, :t], a[:, t:]
            r = lax.rsqrt(jnp.mean(vv * vv, axis=-1, keepdims=True) + EPS)
            vh = vv * r
            vn = (vh * gv_ref[...]).astype(bf16)
            dtok = dt_ref[lo:lo + CHUNK, :].astype(f32)
            ds = dtok * u
            ds_acc[...] += ds
            dsb = ds.astype(bf16)
            du_parts, dvn_parts = [], []
            for g in range(ng):
                cs = slice(g * HEAD, (g + 1) * HEAD)
                wg = ws_ref[g].astype(bf16)
                sg = _dot(wg, vn[:, cs], 1, 0) + bt_ref[:, g:g + 1]
                du_parts.append(dtok[:, cs] * sg)
                dws_ref[g] += _dot(dsb[:, cs], vn[:, cs], 1, 1)
                dvn_parts.append(_dot(wg, dsb[:, cs], 0, 0))
            dvn = jnp.concatenate(dvn_parts, axis=1)
            dgv_ref[...] += jnp.sum(dvn * vh, axis=0, keepdims=True)
            gy = dvn * gv_ref[...]
            dvv = r * (gy - vh * jnp.mean(gy * vh, axis=-1, keepdims=True))
            da = jnp.concatenate(du_parts + [dvv], axis=1)
            dz_ref[lo:lo + CHUNK, :] = (da * _gelu_grad(zz)).astype(bf16)

        @pl.when(step == nsteps - 1)
        def _():
            for g in range(ng):
                dbt_ref[:, g:g + 1] = jnp.sum(ds_acc[:, g * HEAD:(g + 1) * HEAD], axis=1, keepdims=True)

    return _pallas_call(
        body,
        out_shape=(_sds((s, z.shape[1]), bf16), _sds((ng, CHUNK, CHUNK), f32), _sds((CHUNK, ng), f32), _sds((1, t), f32)),
        grid=(nsteps,),
        in_specs=[pl.BlockSpec((rb, 2 * t), lambda i: (i, 0)), pl.BlockSpec((1, t), lambda i: (0, 0)),
                  pl.BlockSpec((ng, CHUNK, CHUNK), lambda i: (0, 0, 0)), pl.BlockSpec((CHUNK, ng), lambda i: (0, 0)),
                  pl.BlockSpec((rb, t), lambda i: (i, 0))],
        out_specs=(pl.BlockSpec((rb, 2 * t), lambda i: (i, 0)), pl.BlockSpec((ng, CHUNK, CHUNK), lambda i: (0, 0, 0)),
                   pl.BlockSpec((CHUNK, ng), lambda i: (0, 0)), pl.BlockSpec((1, t), lambda i: (0, 0))),
        scratch_shapes=[pltpu.VMEM((CHUNK, t), f32)],
        compiler_params=_cp(("arbitrary",)), name=f"gmlp_bwd_{s}",
    )(z, g_v, w_s, b_t, dtok)


def _rope_tables(s):
    n_rows = s // GRID_W
    rows = jnp.broadcast_to(jnp.arange(n_rows)[:, None], (n_rows, GRID_W)).reshape(s)
    cols = jnp.broadcast_to(jnp.arange(GRID_W)[None, :], (n_rows, GRID_W)).reshape(s)
    freqs = ROPE_THETA ** (-jnp.arange(ROPE_PAIRS, dtype=f32) / ROPE_PAIRS)
    ang_r = rows.astype(f32)[:, None] * freqs
    ang_c = cols.astype(f32)[:, None] * freqs
    ang = jnp.concatenate([ang_r, ang_r, ang_c, ang_c], axis=-1)
    cos, sin = jnp.cos(ang), jnp.sin(ang)
    first = (jnp.arange(HEAD) % (2 * ROPE_PAIRS)) < ROPE_PAIRS
    return cos, jnp.where(first, -sin, 0.0), jnp.where(first, 0.0, sin)


def _attn_prep_fwd(z, g_q, g_k, tables, t):
    s = z.shape[0]
    tr = _pick(s, (256, 128))
    nq = t // HEAD
    width = t + 2 * KV_WIDTH

    def body(z_ref, gq_ref, gk_ref, cos_ref, sa_ref, sb_ref, q_ref, k_ref, v_ref):
        cos, sa, sb = cos_ref[...], sa_ref[...], sb_ref[...]
        for h in range(nq + KV_HEADS):
            cs = slice(h * HEAD, (h + 1) * HEAD)
            xv = z_ref[:, cs]
            r = lax.rsqrt(jnp.mean(xv * xv, axis=-1, keepdims=True) + EPS)
            xn = xv * r * (gq_ref[...] if h < nq else gk_ref[...])
            y = xn * cos + _rot(xn, sa, sb)
            if h < nq:
                q_ref[:, cs] = (y * (SCALE * LOG2E)).astype(bf16)
            else:
                k_ref[:, (h - nq) * HEAD:(h - nq + 1) * HEAD] = y.astype(bf16)
        v_ref[...] = z_ref[:, t + KV_WIDTH:width].astype(bf16)

    row = lambda w: pl.BlockSpec((tr, w), lambda i: (i, 0))
    vec = pl.BlockSpec((1, HEAD), lambda i: (0, 0))
    return _pallas_call(
        body, out_shape=(_sds((s, t), bf16), _sds((s, KV_WIDTH), bf16), _sds((s, KV_WIDTH), bf16)), grid=(s // tr,),
        in_specs=[row(width), vec, vec, row(HEAD), row(HEAD), row(HEAD)],
        out_specs=(row(t), row(KV_WIDTH), row(KV_WIDTH)),
        compiler_params=_cp(("parallel",)), name=f"attn_prep_fwd_{s}",
    )(z, g_q, g_k, *tables)


def _attn_prep_bwd(z, g_q, g_k, tables, dq, dk, dv, t):
    s = z.shape[0]
    tr = _pick(s, (256, 128))
    nq = t // HEAD
    width = t + 2 * KV_WIDTH

    def body(z_ref, gq_ref, gk_ref, cos_ref, sa_ref, sb_ref, dq_ref, dk_ref, dv_ref, dz_ref, dgq_ref, dgk_ref):
        @pl.when(pl.program_id(0) == 0)
        def _():
            dgq_ref[...] = jnp.zeros_like(dgq_ref)
            dgk_ref[...] = jnp.zeros_like(dgk_ref)

        cos, sa, sb = cos_ref[...], sa_ref[...], sb_ref[...]
        for h in range(nq + KV_HEADS):
            cs = slice(h * HEAD, (h + 1) * HEAD)
            xv = z_ref[:, cs]
            r = lax.rsqrt(jnp.mean(xv * xv, axis=-1, keepdims=True) + EPS)
            xh = xv * r
            if h < nq:
                dy, g_ref, dg_ref = dq_ref[:, cs], gq_ref, dgq_ref
            else:
                dy, g_ref, dg_ref = dk_ref[:, (h - nq) * HEAD:(h - nq + 1) * HEAD], gk_ref, dgk_ref
            dy = dy.astype(f32)
            dxn = dy * cos - _rot(dy, sa, sb)
            dg_ref[...] += jnp.sum(dxn * xh, axis=0, keepdims=True)
            gy = dxn * g_ref[...]
            dz_ref[:, cs] = (r * (gy - xh * jnp.mean(gy * xh, axis=-1, keepdims=True))).astype(bf16)
        dz_ref[:, t + KV_WIDTH:width] = dv_ref[...].astype(bf16)

    row = lambda w: pl.BlockSpec((tr, w), lambda i: (i, 0))
    vec = pl.BlockSpec((1, HEAD), lambda i: (0, 0))
    return _pallas_call(
        body, out_shape=(_sds((s, z.shape[1]), bf16), _sds((1, HEAD), f32), _sds((1, HEAD), f32)), grid=(s // tr,),
        in_specs=[row(width), vec, vec, row(HEAD), row(HEAD), row(HEAD), row(t), row(KV_WIDTH), row(KV_WIDTH)],
        out_specs=(row(width), vec, vec),
        compiler_params=_cp(("arbitrary",)), name=f"attn_prep_bwd_{s}",
    )(z, g_q, g_k, *tables, dq, dk, dv)


def _flash_tiles(s):
    return _pick(s, (512, 256, 128)), _pick(s, (1024, 512, 256, 128))


def _flash_fwd(q, k, v):
    s, t = q.shape
    grp = t // KV_WIDTH
    tq, tk = _flash_tiles(s)
    nkv = s // tk
    rows = grp * tq

    def body(q_ref, k_ref, v_ref, o_ref, lse_ref, m_sc, acc_sc):
        ki = pl.program_id(2)

        @pl.when(ki == 0)
        def _():
            m_sc[...] = jnp.full(m_sc.shape, -jnp.inf, f32)
            acc_sc[...] = jnp.zeros_like(acc_sc)

        kk = k_ref[...]
        v1 = jnp.concatenate([v_ref[...], jnp.ones((tk, HEAD), bf16)], axis=1)
        sc = _dot(q_ref[:, 0:HEAD], kk, 1, 1)
        for g in range(grp):
            sc_next = _dot(q_ref[:, (g + 1) * HEAD:(g + 2) * HEAD], kk, 1, 1) if g + 1 < grp else None
            mine = slice(g * tq, (g + 1) * tq)
            m_prev = m_sc[mine, :]
            m_new = jnp.maximum(m_prev, jnp.max(sc, axis=-1, keepdims=True))
            alpha = jnp.exp2(m_prev - m_new)
            p = jnp.exp2((sc - m_new).astype(bf16))
            acc_sc[mine, :] = alpha * acc_sc[mine, :] + _dot(p, v1, 1, 0)
            m_sc[mine, :] = m_new
            sc = sc_next

        @pl.when(ki == nkv - 1)
        def _():
            acc = acc_sc[...]
            l = acc[:, HEAD:HEAD + 1]
            o = acc[:, :HEAD] / l
            for g in range(grp):
                o_ref[:, g * HEAD:(g + 1) * HEAD] = o[g * tq:(g + 1) * tq].astype(bf16)
            lse_ref[0] = jnp.broadcast_to(m_sc[...] + jnp.log(l) * LOG2E, (rows, HEAD))

    return _pallas_call(
        body, out_shape=(_sds((s, t + MEM_WIDTH), bf16), _sds((KV_HEADS, grp * s, HEAD), f32)), grid=(KV_HEADS, s // tq, nkv),
        in_specs=[pl.BlockSpec((tq, grp * HEAD), lambda h, i, j: (i, h)), pl.BlockSpec((tk, HEAD), lambda h, i, j: (j, h)),
                  pl.BlockSpec((tk, HEAD), lambda h, i, j: (j, h))],
        out_specs=(pl.BlockSpec((tq, grp * HEAD), lambda h, i, j: (i, h)), pl.BlockSpec((1, rows, HEAD), lambda h, i, j: (h, i, 0))),
        scratch_shapes=[pltpu.VMEM((rows, 1), f32), pltpu.VMEM((rows, 2 * HEAD), f32)],
        compiler_params=_cp(("parallel", "parallel", "arbitrary")), name=f"flash_fwd_{s}",
    )(q, k, v)


def _flash_delta(o, do, t):
    s = o.shape[0]
    grp = t // KV_WIDTH
    tq, _ = _flash_tiles(s)
    rows = grp * tq

    def body(o_ref, do_ref, d_ref):
        for g in range(grp):
            cs = slice(g * HEAD, (g + 1) * HEAD)
            dd = jnp.sum(o_ref[:, cs].astype(f32) * do_ref[:, cs].astype(f32), axis=-1, keepdims=True)
            d_ref[0, g * tq:(g + 1) * tq, :] = jnp.broadcast_to(dd, (tq, HEAD))

    qb = pl.BlockSpec((tq, grp * HEAD), lambda h, i: (i, h))
    return _pallas_call(
        body, out_shape=_sds((KV_HEADS, grp * s, HEAD), f32), grid=(KV_HEADS, s // tq),
        in_specs=[qb, qb], out_specs=pl.BlockSpec((1, rows, HEAD), lambda h, i: (h, i, 0)),
        compiler_params=_cp(("parallel", "parallel")), name=f"flash_delta_{s}",
    )(o, do)


def _flash_bwd(q, k, v, o, do, lse):
    s, t = q.shape
    grp = t // KV_WIDTH
    tq, tk = _flash_tiles(s)
    nq, nkv = s // tq, s // tk
    rows = grp * tq
    delta = _flash_delta(o, do, t)

    def body(q_ref, k_ref, v_ref, do_ref, lse_ref, delta_ref, dq_ref, dk_ref, dv_ref, dq_acc, dk_acc, dv_acc):
        kj, qi = pl.program_id(1), pl.program_id(2)

        @pl.when(qi == 0)
        def _():
            dk_acc[...] = jnp.zeros_like(dk_acc)
            dv_acc[...] = jnp.zeros_like(dv_acc)

        kk, vv = k_ref[...], v_ref[...]

        def products(g):
            qg, dog = q_ref[:, g * HEAD:(g + 1) * HEAD], do_ref[:, g * HEAD:(g + 1) * HEAD]
            return qg, dog, _dot(qg, kk, 1, 1), _dot(dog, vv, 1, 1)

        ahead = products(0)
        dv_sum = dk_sum = None
        terms = []
        for g in range(grp):
            qg, dog, sc, dp = ahead
            if g + 1 < grp:
                ahead = products(g + 1)
            head_rows = slice(g * tq, (g + 1) * tq)
            p = jnp.exp2((sc - lse_ref[0, head_rows, 0:1]).astype(bf16))
            ds = p * (dp - delta_ref[0, head_rows, 0:1]).astype(bf16)
            dv_g, dk_g = _dot(p, dog, 0, 0), _dot(ds, qg, 0, 0)
            dv_sum = dv_g if dv_sum is None else dv_sum + dv_g
            dk_sum = dk_g if dk_sum is None else dk_sum + dk_g
            terms.append(_dot(ds, kk, 1, 0))
        dv_acc[...] += dv_sum
        dk_acc[...] += dk_sum
        mine = pl.ds(pl.multiple_of(qi * rows, rows), rows)
        term = jnp.concatenate(terms, axis=0)

        @pl.when(kj == 0)
        def _():
            dq_acc[mine, :] = term

        @pl.when(kj > 0)
        def _():
            dq_acc[mine, :] += term

        @pl.when(kj == nkv - 1)
        def _():
            total = dq_acc[mine, :]
            for g in range(grp):
                dq_ref[:, g * HEAD:(g + 1) * HEAD] = total[g * tq:(g + 1) * tq] * SCALE

        @pl.when(qi == nq - 1)
        def _():
            dk_ref[...] = dk_acc[...] * (1.0 / LOG2E)
            dv_ref[...] = dv_acc[...]

    qb = pl.BlockSpec((tq, grp * HEAD), lambda h, j, i: (i, h))
    kb = pl.BlockSpec((tk, HEAD), lambda h, j, i: (j, h))
    lb = pl.BlockSpec((1, rows, HEAD), lambda h, j, i: (h, i, 0))
    dqb = pl.BlockSpec((tq, grp * HEAD), lambda h, j, i: (jnp.where(j == nkv - 1, i, 0), h))
    return _pallas_call(
        body, out_shape=(_sds((s, t), f32), _sds((s, KV_WIDTH), f32), _sds((s, KV_WIDTH), f32)), grid=(KV_HEADS, nkv, nq),
        in_specs=[qb, kb, kb, qb, lb, lb], out_specs=(dqb, kb, kb),
        scratch_shapes=[pltpu.VMEM((nq * rows, HEAD), f32), pltpu.VMEM((tk, HEAD), f32), pltpu.VMEM((tk, HEAD), f32)],
        compiler_params=_cp(("parallel", "arbitrary", "arbitrary")), name=f"flash_bwd_{s}",
    )(q, k, v, do, lse, delta)


def _mem_heads(z_ref, kv_ref, gq_ref, gk_ref, h):
    cs = slice(h * HEAD, (h + 1) * HEAD)
    xv = z_ref[:, cs]
    r = lax.rsqrt(jnp.mean(xv * xv, axis=-1, keepdims=True) + EPS)
    xh = xv * r
    kx = kv_ref[:, cs]
    rk = lax.rsqrt(jnp.mean(kx * kx, axis=-1, keepdims=True) + EPS)
    kn = (kx * rk * gk_ref[...]).astype(bf16)
    vv = kv_ref[:, MEM_WIDTH + h * HEAD:MEM_WIDTH + (h + 1) * HEAD].astype(bf16)
    qn = (xh * gq_ref[...]).astype(bf16)
    sc = _dot(qn, kn, 1, 1) * SCALE
    e = jnp.exp(sc - jnp.max(sc, axis=-1, keepdims=True))
    p = e / jnp.sum(e, axis=-1, keepdims=True)
    return cs, r, xh, qn, kn, vv, p


def _mem_fwd(z, qblk, kv, g_mq, g_mk, cat):
    s = z.shape[0]
    nm = kv.shape[0]
    tr = _pick(s, (512, 256, 128))
    oblk = cat.shape[1] // MEM_WIDTH - 1

    def body(z_ref, kv_ref, gq_ref, gk_ref, cat_ref, o_ref):
        for h in range(MEM_HEADS):
            cs, _, _, _, _, vv, p = _mem_heads(z_ref, kv_ref, gq_ref, gk_ref, h)
            o_ref[:, cs] = _dot(p.astype(bf16), vv, 1, 0).astype(bf16)

    vec = pl.BlockSpec((1, HEAD), lambda i: (0, 0))
    return _pallas_call(
        body, out_shape=_sds(cat.shape, bf16), grid=(s // tr,),
        in_specs=[pl.BlockSpec((tr, MEM_WIDTH), lambda i: (i, qblk)), pl.BlockSpec((nm, 2 * MEM_WIDTH), lambda i: (0, 0)), vec, vec,
                  pl.BlockSpec(memory_space=pl.ANY)],
        out_specs=pl.BlockSpec((tr, MEM_WIDTH), lambda i: (i, oblk)),
        input_output_aliases={4: 0},
        compiler_params=_cp(("parallel",)), name=f"mem_fwd_{s}_{qblk}",
    )(z, kv, g_mq, g_mk, cat)


def _mem_bwd(z, qblk, kv, g_mq, g_mk, dcat, dz):
    s = z.shape[0]
    nm = kv.shape[0]
    tr = _pick(s, (512, 256, 128))
    dblk = dcat.shape[1] // MEM_WIDTH - 1

    def body(z_ref, kv_ref, gq_ref, gk_ref, dm_ref, dzin_ref, dz_ref, dkn_ref, dv_ref, dgq_ref):
        @pl.when(pl.program_id(0) == 0)
        def _():
            dkn_ref[...] = jnp.zeros_like(dkn_ref)
            dv_ref[...] = jnp.zeros_like(dv_ref)
            dgq_ref[...] = jnp.zeros_like(dgq_ref)

        for h in range(MEM_HEADS):
            cs, r, xh, qn, kn, vv, p = _mem_heads(z_ref, kv_ref, gq_ref, gk_ref, h)
            dm = dm_ref[:, cs]
            dv_ref[:, cs] += _dot(p.astype(bf16), dm, 0, 0)
            dp = _dot(dm, vv, 1, 1)
            ds = (p * (dp - jnp.sum(dp * p, axis=-1, keepdims=True)) * SCALE).astype(bf16)
            dqn = _dot(ds, kn, 1, 0)
            dkn_ref[:, cs] += _dot(ds, qn, 0, 0)
            dgq_ref[...] += jnp.sum(dqn * xh, axis=0, keepdims=True)
            gy = dqn * gq_ref[...]
            dz_ref[:, cs] = (r * (gy - xh * jnp.mean(gy * xh, axis=-1, keepdims=True))).astype(bf16)

    vec = pl.BlockSpec((1, HEAD), lambda i: (0, 0))
    acc = pl.BlockSpec((nm, MEM_WIDTH), lambda i: (0, 0))
    return _pallas_call(
        body, out_shape=(_sds(dz.shape, bf16), _sds((nm, MEM_WIDTH), f32), _sds((nm, MEM_WIDTH), f32), _sds((1, HEAD), f32)),
        grid=(s // tr,),
        in_specs=[pl.BlockSpec((tr, MEM_WIDTH), lambda i: (i, qblk)), pl.BlockSpec((nm, 2 * MEM_WIDTH), lambda i: (0, 0)), vec, vec,
                  pl.BlockSpec((tr, MEM_WIDTH), lambda i: (i, dblk)), pl.BlockSpec(memory_space=pl.ANY)],
        out_specs=(pl.BlockSpec((tr, MEM_WIDTH), lambda i: (i, qblk)), acc, acc, vec),
        input_output_aliases={5: 0},
        compiler_params=_cp(("arbitrary",)), name=f"mem_bwd_{s}_{qblk}",
    )(z, kv, g_mq, g_mk, dcat, dz)


def _mem_kv_bwd(kv, g_mk, dkn, dv):
    nm = kv.shape[0]

    def body(kv_ref, gk_ref, dkn_ref, dv_ref, dkv_ref, dgk_ref):
        dgk = jnp.zeros((1, HEAD), f32)
        for h in range(MEM_HEADS):
            cs = slice(h * HEAD, (h + 1) * HEAD)
            kx = kv_ref[:, cs]
            rk = lax.rsqrt(jnp.mean(kx * kx, axis=-1, keepdims=True) + EPS)
            kh = kx * rk
            dkn_h = dkn_ref[:, cs]
            dgk = dgk + jnp.sum(dkn_h * kh, axis=0, keepdims=True)
            gy = dkn_h * gk_ref[...]
            dkv_ref[:, cs] = (rk * (gy - kh * jnp.mean(gy * kh, axis=-1, keepdims=True))).astype(bf16)
        dkv_ref[:, MEM_WIDTH:] = dv_ref[...].astype(bf16)
        dgk_ref[...] = dgk

    return _pallas_call(
        body, out_shape=(_sds((nm, 2 * MEM_WIDTH), bf16), _sds((1, HEAD), f32)),
        compiler_params=pltpu.CompilerParams(vmem_limit_bytes=VMEM_LIMIT), name=f"mem_kv_bwd_{nm}",
    )(kv, g_mk, dkn, dv)


def _ffn_up(h2, g):
    m, kdim = h2.shape
    nd, _, n = g.shape
    half = nd // 2
    assert n % 128 == 0 and h2.dtype == bf16 and g.dtype == bf16
    tm = _pick(m, (512, 256, 128))

    def body(a_ref, wg_ref, wu_ref, gate_ref, up_ref, act_ref):
        av = a_ref[...]
        gt = _dot(av, wg_ref[0], 1, 0)
        up = _dot(av, wu_ref[0], 1, 0)
        gate_ref[...] = gt.astype(bf16)
        up_ref[...] = up.astype(bf16)
        act_ref[...] = (gt * jax.nn.sigmoid(gt) * up).astype(bf16)

    out = pl.BlockSpec((tm, n), lambda i, j: (i, j))
    return _pallas_call(
        body, out_shape=tuple(_sds((m, half * n), bf16) for _ in range(3)), grid=(m // tm, half),
        in_specs=[pl.BlockSpec((tm, kdim), lambda i, j: (i, 0)), pl.BlockSpec((1, kdim, n), lambda i, j: (j, 0, 0)),
                  pl.BlockSpec((1, kdim, n), lambda i, j: (j + half, 0, 0))],
        out_specs=(out, out, out),
        compiler_params=_cp(("parallel", "arbitrary")), name=f"ffn_up_{m}x{kdim}x{half * n}",
    )(h2, g, g)


def _ffn_down_bwd(dx, w_dn, gate, up):
    m, d = dx.shape
    ff = w_dn.shape[0]
    tm = _pick(m, (512, 256, 128))
    tf = _pick(ff, (1408, 1024, 512, 256, 128))

    nsub = 2 if tm % 32 == 0 else 1
    rs = tm // nsub

    def body(dx_ref, w_ref, g_ref, u_ref, dg_ref, du_ref):
        wv = w_ref[...]
        da = _dot(dx_ref[0:rs, :], wv, 1, 1)
        for r in range(nsub):
            da_next = _dot(dx_ref[(r + 1) * rs:(r + 2) * rs, :], wv, 1, 1) if r + 1 < nsub else None
            mine = slice(r * rs, (r + 1) * rs)
            gt = g_ref[mine, :].astype(f32)
            sg = jax.nn.sigmoid(gt)
            dg_ref[mine, :] = (da * u_ref[mine, :].astype(f32) * sg * (1.0 + gt * (1.0 - sg))).astype(bf16)
            du_ref[mine, :] = (da * gt * sg).astype(bf16)
            da = da_next

    tile = pl.BlockSpec((tm, tf), lambda i, j: (i, j))
    return _pallas_call(
        body, out_shape=(_sds((m, ff), bf16), _sds((m, ff), bf16)), grid=(m // tm, ff // tf),
        in_specs=[pl.BlockSpec((tm, d), lambda i, j: (i, 0)), pl.BlockSpec((tf, d), lambda i, j: (j, 0)), tile, tile],
        out_specs=(tile, tile),
        compiler_params=_cp(("parallel", "arbitrary")), name=f"ffn_down_bwd_{m}x{d}x{ff}",
    )(dx, w_dn, gate, up)


def _loss_head(y, target):
    s, d = y.shape
    tr = _pick(s, (512, 256, 128))

    def body(y_ref, t_ref, l_ref, dy_ref, dyb_ref):
        @pl.when(pl.program_id(0) == 0)
        def _():
            l_ref[...] = jnp.zeros_like(l_ref)

        err = y_ref[...] - t_ref[...]
        l_ref[...] += 0.5 * jnp.sum(jnp.mean(err * err, axis=-1, keepdims=True), axis=0, keepdims=True)
        dy = err * (1.0 / d)
        dy_ref[...] = dy
        dyb_ref[...] = dy.astype(bf16)

    row = pl.BlockSpec((tr, d), lambda i: (i, 0))
    return _pallas_call(
        body, out_shape=(_sds((1, HEAD), f32), _sds((s, d), f32), _sds((s, d), bf16)), grid=(s // tr,),
        in_specs=[row, row], out_specs=(pl.BlockSpec((1, HEAD), lambda i: (0, 0)), row, row),
        compiler_params=_cp(("arbitrary",)), name=f"loss_{s}x{d}",
    )(y, target)


def _place():
    return lax.axis_index("x"), lax.axis_index("y"), lax.axis_index("c")


def _tag(arrays):
    return "_".join("x".join(str(dd) for dd in a.shape) for a in arrays)


def _all_gather(shards):
    nw = len(shards)
    hbm = pl.BlockSpec(memory_space=pl.ANY)

    def body(*refs):
        x_refs, out_refs = refs[:nw], refs[nw:2 * nw]
        send_sems, recv_sems, local_sems = refs[2 * nw:]
        x, y, c = _place()
        me, sibling = (x, y, c), (x, y, 1 - c)
        chips = [(1 - x, y), (x, 1 - y), (1 - x, 1 - y)]

        def slot(w, place):
            px, py, pc = place
            return out_refs[w].at[4 * px + 2 * py + pc]

        def copy(k, w, block_of, to, from_input=False):
            return pltpu.make_async_remote_copy(
                src_ref=x_refs[w] if from_input else slot(w, block_of), dst_ref=slot(w, block_of),
                send_sem=send_sems.at[k, w], recv_sem=recv_sems.at[k, w], device_id=to, device_id_type=MESH)

        mine = [pltpu.make_async_copy(x_refs[w], slot(w, me), local_sems.at[w]) for w in range(nw)]
        for cp in mine:
            cp.start()
        first = []
        for w in range(nw):
            first.append(copy(0, w, me, sibling, from_input=True))
            first += [copy(1 + j, w, me, (*chip, c), from_input=True) for j, chip in enumerate(chips)]
        for cp in first:
            cp.start()
        passed = []
        for w in range(nw):
            for j, chip in enumerate(chips):
                copy(1 + j, w, (*chip, c), me).wait_recv()
                fwd = copy(4 + j, w, (*chip, c), sibling)
                fwd.start()
                passed.append(fwd)
        for w in range(nw):
            copy(0, w, sibling, me).wait_recv()
            for j, chip in enumerate(chips):
                copy(4 + j, w, (*chip, 1 - c), me).wait_recv()
        for cp in first + passed:
            cp.wait_send()
        for cp in mine:
            cp.wait()

    return _pallas_call(
        body, out_shape=tuple(_sds((N_DEV,) + a.shape, a.dtype) for a in shards), in_specs=[hbm] * nw, out_specs=tuple([hbm] * nw),
        scratch_shapes=[pltpu.SemaphoreType.DMA((7, nw)), pltpu.SemaphoreType.DMA((7, nw)), pltpu.SemaphoreType.DMA((nw,))],
        name=f"all_gather_{_tag(shards)}_{jnp.dtype(shards[0].dtype).name}",
    )(*shards)


def _pair_sum(grad, got):
    nd, a, b = grad.shape
    nchip = nd // 2
    ta = _pick(a, (1024, 704, 512, 352, 256, 128, 64, 32, 16))

    def my_chip():
        return 2 * lax.axis_index("x") + lax.axis_index("y")

    def body(a_ref, b_ref, o_ref, land_ref):
        tot = (a_ref[...].astype(f32) + b_ref[...].astype(f32)).astype(o_ref.dtype)
        o_ref[...] = tot

        @pl.when(pl.program_id(1) == my_chip())
        def _():
            land_ref[...] = tot

    return _pallas_call(
        body, out_shape=(_sds(got.shape, grad.dtype), _sds(got.shape, grad.dtype)), grid=(a // ta, nchip),
        in_specs=[pl.BlockSpec((None, ta, b), lambda i, k: (2 * k + lax.axis_index("c"), i, 0)),
                  pl.BlockSpec((None, ta, b), lambda i, k: (k, i, 0))],
        out_specs=(pl.BlockSpec((None, ta, b), lambda i, k: (k, i, 0)),
                   pl.BlockSpec((None, ta, b), lambda i, k: (my_chip(), i, 0))),
        compiler_params=_cp(("parallel", "arbitrary")), name=f"pair_sum_{a}x{b}",
    )(grad, got)


_HBM = pl.BlockSpec(memory_space=pltpu.HBM)
_SEM = pl.BlockSpec(memory_space=pltpu.SEMAPHORE)
_ANY = pl.BlockSpec(memory_space=pl.ANY)
_DATAFLOW = pltpu.SideEffectType.DATAFLOW_SIDE_EFFECTING


def _in_hbm(a):
    return pltpu.with_memory_space_constraint(a, pltpu.HBM)


def _exchange_begin(bufs, nw, route, after, copies_of, n_copies, name):
    nb = len(bufs)

    def body(*refs):
        send_sems, recv_sems = refs[nb + 2], refs[nb + 3]
        for w in range(nw):
            for k, (src, dst, to) in enumerate(copies_of(w, refs[:nb])):
                pltpu.make_async_remote_copy(src_ref=src, dst_ref=dst, send_sem=send_sems.at[k * nw + w],
                                             recv_sem=recv_sems.at[k * nw + w], device_id=to, device_id_type=MESH).start()

    out = _pallas_call(
        body, name=name,
        out_shape=(pltpu.SemaphoreType.DMA((n_copies * nw,)), pltpu.SemaphoreType.DMA((n_copies * nw,)),
                   *[pltpu.HBM(a.shape, a.dtype) for a in bufs], pltpu.HBM(route.shape, route.dtype)),
        in_specs=[_HBM] * (nb + 1) + [_ANY], out_specs=(_SEM, _SEM, *[_HBM] * (nb + 1)),
        input_output_aliases={i: 2 + i for i in range(nb + 1)},
        compiler_params=pltpu.CompilerParams(has_side_effects=_DATAFLOW),
    )(*[_in_hbm(a) for a in bufs], _in_hbm(route), after)
    return (out[0], out[1], out[2:2 + nb], nw), out[2 + nb]


def _exchange_end(handle, after, copies_of, n_copies, name):
    send_sems, recv_sems, thru, nw = handle
    nb = len(thru)

    def body(*refs):
        send_sems, recv_sems = refs[nb], refs[nb + 1]
        for w in range(nw):
            for k, (src, dst, to) in enumerate(copies_of(w, refs[:nb])):
                cp = pltpu.make_async_remote_copy(src_ref=src, dst_ref=dst, send_sem=send_sems.at[k * nw + w],
                                                  recv_sem=recv_sems.at[k * nw + w], device_id=to, device_id_type=MESH)
                cp.wait_send()
                cp.wait_recv()

    out = _pallas_call(
        body, name=name, out_shape=tuple(pltpu.HBM(a.shape, a.dtype) for a in thru),
        in_specs=[_HBM] * nb + [_SEM, _SEM, _ANY], out_specs=tuple([_HBM] * nb),
        input_output_aliases={i: i for i in range(nb)},
        compiler_params=pltpu.CompilerParams(has_side_effects=_DATAFLOW),
    )(*thru, send_sems, recv_sems, after)
    return list(out)


def _my_slot():
    return 4 * lax.axis_index("x") + 2 * lax.axis_index("y") + lax.axis_index("c")


def _shard_into_land(w_all, idx):
    _, a, b = w_all.shape
    ta = next(cc for cc in (1024, 704, 512, 352, 256, 128, 64, 32, 16) if a % cc == 0 and (cc * b * 4 <= 2 ** 21 or cc == 16))

    def body(w_ref, o_ref):
        o_ref[...] = w_ref[...].astype(bf16)

    return _pallas_call(
        body, out_shape=_sds((N_DEV, a, b), bf16), grid=(a // ta,),
        in_specs=[pl.BlockSpec((None, ta, b), lambda i: (idx, i, 0))],
        out_specs=pl.BlockSpec((None, ta, b), lambda i: (_my_slot(), i, 0)),
        compiler_params=_cp(("parallel",)), name=f"shard_into_land_{a}x{b}_{idx}",
    )(w_all)


def _gather_copies(w, land_refs):
    x, y, c = _place()
    blk = land_refs[w].at[4 * x + 2 * y + c]
    return [(blk, blk, to) for to in ((x, y, 1 - c), (1 - x, y, c), (x, 1 - y, c), (1 - x, 1 - y, c))]


def _gather_begin(lands, route, after, tag):
    return _exchange_begin(lands, len(lands), route, after, _gather_copies, 4, f"gather_begin_{tag}")


def _gather_end(handle, after, tag):
    return _exchange_end(handle, after, _gather_copies, 4, f"gather_end_{tag}")


def _gather_pass_on(lands):
    nw = len(lands)

    def body(*refs):
        l_refs = refs[nw:2 * nw]
        send_sems, recv_sems = refs[2 * nw:]
        x, y, c = _place()
        copies = []
        for w in range(nw):
            for j, (px, py) in enumerate([(1 - x, y), (x, 1 - y), (1 - x, 1 - y)]):
                blk = l_refs[w].at[4 * px + 2 * py + c]
                copies.append(pltpu.make_async_remote_copy(
                    src_ref=blk, dst_ref=blk, send_sem=send_sems.at[j, w], recv_sem=recv_sems.at[j, w],
                    device_id=(x, y, 1 - c), device_id_type=MESH))
        for cp in copies:
            cp.start()
        for cp in copies:
            cp.wait_send()
        for w in range(nw):
            for j, (px, py) in enumerate([(1 - x, y), (x, 1 - y), (1 - x, 1 - y)]):
                blk = l_refs[w].at[4 * px + 2 * py + (1 - c)]
                pltpu.make_async_remote_copy(src_ref=blk, dst_ref=blk, send_sem=send_sems.at[j, w], recv_sem=recv_sems.at[j, w],
                                             device_id=(x, y, 1 - c), device_id_type=MESH).wait_recv()

    return _pallas_call(
        body, out_shape=tuple(_sds(a.shape, a.dtype) for a in lands), in_specs=[_ANY] * nw, out_specs=tuple([_ANY] * nw),
        input_output_aliases={w: w for w in range(nw)},
        scratch_shapes=[pltpu.SemaphoreType.DMA((3, nw)), pltpu.SemaphoreType.DMA((3, nw))],
        name=f"gather_pass_on_{_tag(lands)}",
    )(*lands)


def _swap_copies(w, refs):
    x, y, c = _place()
    nw = len(refs) // 2
    return [(refs[w].at[2 * k + (1 - c)], refs[nw + w].at[k], (x, y, 1 - c)) for k in range(N_DEV // 2)]


def _swap_begin(grads, route, after, tag):
    lands = [lax.empty((N_DEV // 2,) + g.shape[1:], g.dtype) for g in grads]
    return _exchange_begin(list(grads) + lands, len(grads), route, after, _swap_copies, N_DEV // 2, f"swap_begin_{tag}")


def _swap_end(handle, after, tag):
    out = _exchange_end(handle, after, _swap_copies, N_DEV // 2, f"swap_end_{tag}")
    return out[:len(out) // 2], out[len(out) // 2:]


def _scatter_copies(w, refs):
    x, y, c = _place()
    nw = len(refs) // 2
    dst = refs[nw + w].at[2 * x + y]
    return [(refs[w].at[2 * px + py], dst, (px, py, c)) for px, py in ((1 - x, y), (x, 1 - y), (1 - x, 1 - y))]


def _scatter_begin(psums, lands, route, after, tag):
    return _exchange_begin(list(psums) + list(lands), len(psums), route, after, _scatter_copies, 3, f"scatter_begin_{tag}")


def _scatter_end(handle, after, tag):
    return _exchange_end(handle, after, _scatter_copies, 3, f"scatter_end_{tag}")


def _adamw(parts, w_all, m_all, v_all, l, carried):
    nparts, a, b = parts.shape
    nl = w_all.shape[0]
    ta = next(cc for cc in (1024, 704, 512, 352, 256, 128, 64, 32, 16, 8) if a % cc == 0 and (cc * b * 4 <= 2 ** 20 or cc == 8))
    c1 = 1.0 / (1.0 - ADAM_B1 ** ADAM_STEP)
    c2 = 1.0 / (1.0 - ADAM_B2 ** ADAM_STEP)

    def body(p_ref, w_ref, m_ref, v_ref, *rest):
        g_out, d_out, m_out, v_out = rest[-4:]
        g = p_ref[0].astype(f32)
        for k in range(1, nparts):
            g = g + p_ref[k].astype(f32)
        m_new = ADAM_B1 * m_ref[...] + (1.0 - ADAM_B1) * g
        v_new = ADAM_B2 * v_ref[...] + (1.0 - ADAM_B2) * (g * g)
        m_hat = m_new * c1
        v_hat = v_new * c2
        g_out[...] = g
        d_out[...] = -ADAM_LR * (m_hat / (jnp.sqrt(v_hat) + ADAM_EPS) + ADAM_WD * w_ref[...])
        m_out[...] = m_new
        v_out[...] = v_new

    one = pl.BlockSpec((None, ta, b), lambda i: (l, i, 0))
    keep = [] if carried is None else [pl.BlockSpec(memory_space=pl.ANY)] * 4
    return _pallas_call(
        body, out_shape=tuple(_sds((nl, a, b), f32) for _ in range(4)), grid=(a // ta,),
        in_specs=[pl.BlockSpec((nparts, ta, b), lambda i: (0, i, 0)), one, one, one] + keep, out_specs=(one, one, one, one),
        input_output_aliases=({} if carried is None else {4 + q: q for q in range(4)}),
        compiler_params=_cp(("parallel",)), name=f"adamw_{nparts}x{nl}x{a}x{b}_{l}{'' if carried is None else '_carried'}",
    )(parts, w_all, m_all, v_all, *(carried or ()))


def _to_flat(arrays):
    flat = jnp.concatenate([a.reshape(-1).astype(f32) for a in arrays])
    rows = -(-flat.shape[0] // (8 * LANES)) * 8
    return jnp.pad(flat, (0, rows * LANES - flat.shape[0])).reshape(rows, LANES)


def _from_flat(flat, shapes):
    flat = flat.reshape(-1)
    out, off = [], 0
    for shp in shapes:
        n = 1
        for dd in shp:
            n *= dd
        out.append(flat[off:off + n].reshape(shp))
        off += n
    return out


def kernel(x, mem, g_mix, g_ffn, w_in_a, g_v_a, w_spatial, b_spatial, w_in_b, g_q_b, g_k_b, g_mem, w_mem_kv, g_mq, g_mk, w_out, w_gate_up, w_down, loss_target, m_g_mix, m_g_ffn, m_w_in_a, m_g_v_a, m_w_spatial, m_b_spatial, m_w_in_b, m_g_q_b, m_g_k_b, m_g_mem, m_w_mem_kv, m_g_mq, m_g_mk, m_w_out, m_w_gate_up, m_w_down, v_g_mix, v_g_ffn, v_w_in_a, v_g_v_a, v_w_spatial, v_b_spatial, v_w_in_b, v_g_q_b, v_g_k_b, v_g_mem, v_w_mem_kv, v_g_mq, v_g_mk, v_w_out, v_w_gate_up, v_w_down):
    given = dict(locals())
    depth = g_mix.shape[0]
    s, d = x.shape[1], x.shape[2]
    nm = mem.shape[1]
    t = d - MEM_WIDTH
    ff = w_down.shape[1] * N_DEV
    x0 = x.reshape(s, d)
    mem0 = mem.reshape(nm, d)
    target = loss_target.reshape(s, d)
    tables = _rope_tables(s)

    big_names = ("w_in", "w_mem_kv", "w_out", "w_gate_up", "w_down")

    def stacked_key(name, l):
        if name == "w_in":
            return ("w_in_a" if l % 2 == 0 else "w_in_b"), l // 2
        return name, l

    n_mix = 3

    def own_shards(l):
        return [_shard_into_land(given[key], idx) for key, idx in (stacked_key(name, l) for name in big_names)]

    def gather_start(l, route, after):
        lands = cast_shards[l]
        h_mix, route = _gather_begin(lands[:n_mix], route, after, f"mix{l}")
        h_ffn, route = _gather_begin(lands[n_mix:], route, after, f"ffn{l}")
        return h_mix, h_ffn, route

    def gather_finish(handle, after, tag):
        return list(_gather_pass_on(_gather_end(handle, after, tag)))

    saved = []
    xc = x0
    cast_shards = {0: own_shards(0)}
    h_in, route0 = _gather_begin(cast_shards[0][:1], g_mix[0].reshape(1, d), mem0, "in0")
    h_rest, route0 = _gather_begin(cast_shards[0][1:n_mix], route0, mem0, "rest0")
    h_gu, route0 = _gather_begin(cast_shards[0][n_mix:n_mix + 1], route0, mem0, "gu0")
    h_dn, _ = _gather_begin(cast_shards[0][n_mix + 1:], route0, mem0, "dn0")
    for l in range(1, depth):
        cast_shards[l] = own_shards(l)
    w_mix = gather_finish(h_in, cast_shards[depth - 1][0], "in0")
    w_ffn = None
    for l in range(depth):
        is_a = l % 2 == 0
        g_in = w_mix[0]
        qblk = (N_DEV * g_in.shape[2] - MEM_WIDTH) // MEM_WIDTH

        gm_row, gf_row, gmem_row = g_mix[l].reshape(1, d), g_ffn[l].reshape(1, d), g_mem[l].reshape(1, d)
        gmq_row, gmk_row = g_mq[l].reshape(1, HEAD), g_mk[l].reshape(1, HEAD)
        if l + 1 < depth:
            next_mix, next_ffn, gm_row = gather_start(l + 1, gm_row, g_in)
        h = _rms_fwd(xc, gm_row)
        z = _mm_cols_fwd(h, g_in, f32)
        if l == 0:
            w_mix = w_mix + gather_finish(h_rest, z, "rest0")
        g_kv, g_out = w_mix[1:]
        w_kv, w_o = (g.reshape(-1, g.shape[2]) for g in (g_kv, g_out))
        if is_a:
            ia = l // 2
            mix = dict(g_v=g_v_a[ia].reshape(1, t), w_s=w_spatial[ia], b_t=b_spatial[ia].T)
            cat = _gmlp_fwd(z, mix["g_v"], mix["w_s"], mix["b_t"])
        else:
            ib = l // 2
            mix = dict(g_q=g_q_b[ib].reshape(1, HEAD), g_k=g_k_b[ib].reshape(1, HEAD))
            q, k, v = _attn_prep_fwd(z, mix["g_q"], mix["g_k"], tables, t)
            cat, lse = _flash_fwd(q, k, v)
            mix.update(q=q, k=k, v=v, lse=lse)
        hm = _rms_fwd(mem0, gmem_row)
        kv = _matmul(hm, w_kv)
        cat = _mem_fwd(z, qblk, kv, gmq_row, gmk_row, cat)
        x1 = _matmul(cat, w_o, res=xc)
        h2 = _rms_fwd(x1, gf_row)
        if l == 0:
            w_ffn = gather_finish(h_gu, h2, "gu0")
        g_gu = w_ffn[0]
        gate, up, act = _ffn_up(h2, g_gu)
        if l == 0:
            w_ffn = w_ffn + gather_finish(h_dn, act, "dn0")
        g_dn = w_ffn[1]
        w_dn = g_dn.reshape(-1, g_dn.shape[2])
        x2 = _matmul(act, w_dn, res=x1)
        saved.append(dict(x=xc, h=h, z=z, mix=mix, cat=cat, hm=hm, kv=kv, x1=x1, h2=h2, gate=gate, up=up, act=act, qblk=qblk,
                          w=(g_in, w_kv, w_o, g_gu, w_dn), rows=(gm_row, gf_row, gmem_row, gmq_row, gmk_row)))
        if l + 1 < depth:
            w_mix = gather_finish(next_mix, x2, f"mix{l + 1}")
            w_ffn = gather_finish(next_ffn, x2, f"ffn{l + 1}")
        xc = x2

    loss_row, dy, dy_b = _loss_head(xc, target)
    loss = lax.psum(loss_row[0, 0], ("x", "y", "c"))

    small = {n: [None] * given[n].shape[0] for n in ("g_mix", "g_ffn", "g_v_a", "w_spatial", "b_spatial", "g_q_b", "g_k_b", "g_mem", "g_mq", "g_mk")}
    big_out = {}

    def scatter_start(swap, route, after, tag):
        grads, got = _swap_end(swap, after, tag)
        sums = [_pair_sum(g, r) for g, r in zip(grads, got)]
        return _scatter_begin([p for p, _ in sums], [q for _, q in sums], route, after, tag)

    def scatter_finish(handle, names, l, after, tag):
        arrived = _scatter_end(handle, after, tag)[len(names):]
        for name, parts in zip(names, arrived):
            key, idx = stacked_key(name, l)
            big_out[key] = _adamw(parts, given[key], given["m_" + key], given["v_" + key], idx, big_out.get(key))

    pend_mix = None
    dx, dx_b = dy, dy_b
    for l in reversed(range(depth)):
        sv = saved[l]
        is_a = l % 2 == 0
        g_in, w_kv, w_o, g_gu, w_dn = sv["w"]
        gm_row, gf_row, gmem_row, gmq_row, gmk_row = sv["rows"]
        mix = sv["mix"]
        dw_dn = _matmul(sv["act"], dx_b, ta=True, out_dtype=bf16)
        dgu = _ffn_down_bwd(dx_b, w_dn, sv["gate"], sv["up"])
        dw_gu = _mm_cols_wgrad(sv["h2"], dgu, g_gu.shape[2])
        swap, dgate = _swap_begin([dw_gu, dw_dn.reshape(N_DEV, -1, d)], dgu[0], dx_b, f"ffn{l}")
        dh2 = _mm_cols_dgrad((dgate, dgu[1]), g_gu)
        dx1, dx1_b, dgf = _rms_bwd(sv["x1"], gf_row, dh2, dx)
        small["g_ffn"][l] = dgf.reshape(d)
        pend_ffn, dx1_b = scatter_start(swap, dx1_b, dx1, f"ffn{l}")
        if pend_mix is not None:
            scatter_finish(*pend_mix, dx1_b, f"mix{l + 1}")
        dw_o = _matmul(sv["cat"], dx1_b, ta=True, out_dtype=bf16)
        dcat = _matmul(dx1_b, w_o, tb=True, out_dtype=bf16)
        if is_a:
            dz, dws, dbt, dgv = _gmlp_bwd(sv["z"], mix["g_v"], mix["w_s"], mix["b_t"], dcat)
            small["w_spatial"][l // 2], small["b_spatial"][l // 2], small["g_v_a"][l // 2] = dws, dbt.T, dgv.reshape(t)
        else:
            dq, dk, dv = _flash_bwd(mix["q"], mix["k"], mix["v"], sv["cat"], dcat, mix["lse"])
            dz, dgq, dgk = _attn_prep_bwd(sv["z"], mix["g_q"], mix["g_k"], tables, dq, dk, dv, t)
            small["g_q_b"][l // 2], small["g_k_b"][l // 2] = dgq.reshape(HEAD), dgk.reshape(HEAD)
        dz, dkn, dvm, dgmq = _mem_bwd(sv["z"], sv["qblk"], sv["kv"], gmq_row, gmk_row, dcat, dz)
        dkv, dgmk = _mem_kv_bwd(sv["kv"], gmk_row, dkn, dvm)
        dw_kv = _matmul(sv["hm"], dkv, ta=True, out_dtype=bf16)
        dhm = _matmul(dkv, w_kv, tb=True)
        small["g_mem"][l] = _rms_bwd(mem0, gmem_row, dhm, None).reshape(d)
        small["g_mq"][l] = dgmq.reshape(HEAD)
        small["g_mk"][l] = dgmk.reshape(HEAD)
        dw_in = _mm_cols_wgrad(sv["h"], dz, g_in.shape[2])
        swap, dz = _swap_begin([dw_in] + [dw.reshape(N_DEV, -1, dw.shape[1]) for dw in (dw_kv, dw_o)], dz, dx1, f"mix{l}")
        dh = _mm_cols_dgrad(dz, g_in)
        dx, dx_b, dgm = _rms_bwd(sv["x"], gm_row, dh, dx1)
        small["g_mix"][l] = dgm.reshape(d)

        handle, dx_b = scatter_start(swap, dx_b, dx, f"mix{l}")
        pend_mix = (handle, big_names[:n_mix], l)
        scatter_finish(pend_ffn, big_names[n_mix:], l, dx_b, f"ffn{l}")
    small_names = tuple(small)
    small_grads = [jnp.stack(small[n]) for n in small_names]
    small_shapes = [g.shape for g in small_grads]
    (all_parts,) = _all_gather([_to_flat(small_grads)])
    souts = _adamw(all_parts, *[_to_flat([given[p + n] for n in small_names])[None] for p in ("", "m_", "v_")], 0, None)
    small_out = dict(zip(small_names, zip(*[_from_flat(flat, small_shapes) for flat in souts])))
    scatter_finish(*pend_mix, souts[0], "mix0")

    weights = ("g_mix", "g_ffn", "w_in_a", "g_v_a", "w_spatial", "b_spatial", "w_in_b", "g_q_b", "g_k_b", "g_mem", "w_mem_kv",
               "g_mq", "g_mk", "w_out", "w_gate_up", "w_down")
    results = {n: (small_out[n] if n in small_out else big_out[n]) for n in weights}
    grad_x = dx.reshape(1, s, d)
    return (loss, grad_x, *[results[n][kind] for kind in range(4) for n in weights])
```

```python
import functools

import jax
import jax.numpy as jnp
from jax import lax
from jax.experimental import pallas as pl
from jax.experimental.pallas import tpu as pltpu

f32 = jnp.float32
bf16 = jnp.bfloat16

HEAD = 128
CHUNK = 128
GRID_W = 64
MEM_HEADS = 4
KV_HEADS = 4
MEM_WIDTH = MEM_HEADS * HEAD
KV_WIDTH = KV_HEADS * HEAD
ROPE_THETA = 10000.0
ROPE_PAIRS = HEAD // 4
EPS = 1e-6
SCALE = HEAD ** -0.5
LOG2E = 1.4426950408889634
N_DEV = 8
LANES = 1024
VMEM_LIMIT = 56 * 1024 * 1024

ADAM_LR, ADAM_B1, ADAM_B2, ADAM_EPS, ADAM_WD, ADAM_STEP = 0.001, 0.9, 0.999, 1e-08, 0.01, 10

MESH = pl.DeviceIdType.MESH
_pallas_call = pl.pallas_call


def _pick(dim, cands):
    for c in cands:
        if dim % c == 0:
            return c
    return dim


def _cp(sem):
    return pltpu.CompilerParams(dimension_semantics=sem, vmem_limit_bytes=VMEM_LIMIT)


def _sds(shape, dtype):
    return jax.ShapeDtypeStruct(shape, dtype)


def _dot(a, b, ca, cb):
    return lax.dot_general(a, b, (((ca,), (cb,)), ((), ())), preferred_element_type=f32)


def _gelu(z):
    return 0.5 * z * (1.0 + lax.erf(z * 0.7071067811865476))


def _gelu_grad(z):
    return 0.5 * (1.0 + lax.erf(z * 0.7071067811865476)) + z * jnp.exp(-0.5 * z * z) * 0.3989422804014327


def _rot(x, sin_a, sin_b):
    return pltpu.roll(x, 96, 1) * sin_a + pltpu.roll(x, 32, 1) * sin_b


def _matmul(a, b, *, ta=False, tb=False, out_dtype=f32, res=None, tm=None, tn=None, tk=None):
    assert a.dtype == bf16 and b.dtype == bf16
    kdim, m = a.shape if ta else a.shape[::-1]
    n, k2 = b.shape if tb else b.shape[::-1]
    assert kdim == k2, (a.shape, b.shape, ta, tb)
    tm = tm or _pick(m, (1024, 1408, 512, 256, 128))
    tn = tn or _pick(n, (1024, 1408, 512, 256, 128))
    if tk is None:
        tk = kdim if kdim <= 2048 else _pick(kdim, (2816, 1024, 512, 256, 128))
    nk = kdim // tk
    ca, cb = (0 if ta else 1), (1 if tb else 0)
    has_res = res is not None

    def body(*refs):
        a_ref, b_ref = refs[0], refs[1]
        r_ref = refs[2] if has_res else None
        o_ref = refs[3] if has_res else refs[2]
        prod = _dot(a_ref[...], b_ref[...], ca, cb)
        if nk == 1:
            if has_res:
                prod = prod + r_ref[...]
            o_ref[...] = prod.astype(o_ref.dtype)
        else:
            acc = refs[-1]
            k = pl.program_id(2)

            @pl.when(k == 0)
            def _():
                acc[...] = prod

            @pl.when(k > 0)
            def _():
                acc[...] += prod

            @pl.when(k == nk - 1)
            def _():
                out = acc[...]
                if has_res:
                    out = out + r_ref[...]
                o_ref[...] = out.astype(o_ref.dtype)

    a_spec = pl.BlockSpec((tk, tm), lambda i, j, k: (k, i)) if ta else pl.BlockSpec((tm, tk), lambda i, j, k: (i, k))
    b_spec = pl.BlockSpec((tn, tk), lambda i, j, k: (j, k)) if tb else pl.BlockSpec((tk, tn), lambda i, j, k: (k, j))
    o_spec = pl.BlockSpec((tm, tn), lambda i, j, k: (i, j))
    in_specs = [a_spec, b_spec] + ([o_spec] if has_res else [])
    args = (a, b) + ((res,) if has_res else ())
    mode = ("t" if ta else "n") + ("t" if tb else "n")
    return _pallas_call(
        body, out_shape=_sds((m, n), out_dtype), grid=(m // tm, n // tn, nk),
        in_specs=in_specs, out_specs=o_spec,
        scratch_shapes=([pltpu.VMEM((tm, tn), f32)] if nk > 1 else []),
        compiler_params=_cp(("parallel", "parallel", "arbitrary")),
        name=f"mm_{mode}_{m}x{kdim}x{n}{'_res' if has_res else ''}_{jnp.dtype(out_dtype).name}",
    )(*args)


def _shards_per_step(n, pair_bytes=0):
    p = 2 if (n % 128 != 0 or 0 < 2 * pair_bytes <= VMEM_LIMIT // 2) else 1
    assert (p * n) % 128 == 0 and N_DEV % p == 0
    return p


def _lane_pieces(v, p, n):
    return [v] if p == 1 else [v[:, q * n:(q + 1) * n] for q in range(p)]


def _mm_cols_fwd(a, g, out_dtype):
    m, kdim = a.shape
    nd, k2, n = g.shape
    assert kdim == k2 and a.dtype == bf16 and g.dtype == bf16
    p = _shards_per_step(n)
    tm = _pick(m, (1024, 512, 256, 128))

    def body(a_ref, g_ref, o_ref):
        av = a_ref[...]
        parts = [_dot(av, g_ref[q], 1, 0) for q in range(p)]
        out = parts[0] if p == 1 else jnp.concatenate(parts, axis=1)
        o_ref[...] = out.astype(o_ref.dtype)

    return _pallas_call(
        body, out_shape=_sds((m, nd * n), out_dtype), grid=(m // tm, nd // p),
        in_specs=[pl.BlockSpec((tm, kdim), lambda i, j: (i, 0)), pl.BlockSpec((p, kdim, n), lambda i, j: (j, 0, 0))],
        out_specs=pl.BlockSpec((tm, p * n), lambda i, j: (i, j)),
        compiler_params=_cp(("parallel", "arbitrary")), name=f"mm_cols_fwd_{m}x{kdim}x{nd * n}_{jnp.dtype(out_dtype).name}",
    )(a, g)


def _pick_part(refs, nparts, step, per):
    val = refs[0][...]
    for hh in range(1, nparts):
        val = jnp.where(step >= hh * per, refs[hh][...], val)
    return val


def _part_step(step, hh, per):
    return jnp.clip(step - hh * per, 0, per - 1)


def _mm_cols_dgrad(dz, g):
    nd, kdim, n = g.shape
    p = _shards_per_step(n, pair_bytes=2 * kdim * n * 2)
    nj = nd // p
    parts = dz if isinstance(dz, tuple) else (dz,)
    m = parts[0].shape[0]
    nn = sum(part.shape[1] for part in parts)
    assert nn == nd * n and all(part.dtype == bf16 for part in parts) and g.dtype == bf16
    tm = _pick(m, (512, 256, 128))
    nparts = len(parts)
    per = nj // nparts

    def body(*refs):
        g_ref, o_ref, acc = refs[nparts:]
        j = pl.program_id(1)
        tot = None
        for q, piece in enumerate(_lane_pieces(_pick_part(refs, nparts, j, per), p, n)):
            dd = _dot(piece, g_ref[q], 1, 1)
            tot = dd if tot is None else tot + dd

        @pl.when(j == 0)
        def _():
            acc[...] = tot

        @pl.when(j > 0)
        def _():
            acc[...] += tot

        @pl.when(j == nj - 1)
        def _():
            o_ref[...] = acc[...]

    return _pallas_call(
        body, out_shape=_sds((m, kdim), f32), grid=(m // tm, nj),
        in_specs=[pl.BlockSpec((tm, p * n), (lambda hh: lambda i, j: (i, _part_step(j, hh, per)))(hh)) for hh in range(nparts)]
        + [pl.BlockSpec((p, kdim, n), lambda i, j: (j, 0, 0))],
        out_specs=pl.BlockSpec((tm, kdim), lambda i, j: (i, 0)),
        scratch_shapes=[pltpu.VMEM((tm, kdim), f32)],
        compiler_params=_cp(("parallel", "arbitrary")), name=f"mm_cols_dgrad_{m}x{nn}x{kdim}_{nparts}",
    )(*parts, g)


def _mm_cols_wgrad(a, dz, n):
    s, kdim = a.shape
    parts = dz if isinstance(dz, tuple) else (dz,)
    nparts = len(parts)
    nd = sum(part.shape[1] for part in parts) // n
    assert a.dtype == bf16 and all(part.dtype == bf16 for part in parts)
    p = _shards_per_step(n)
    per = nd // p // nparts
    tkw = _pick(kdim, (1024, 512, 256, 128))
    ts = _pick(s, (2048, 1024, 512, 256, 128))
    ns = s // ts

    def body(a_ref, *refs):
        o_ref, acc = refs[nparts:]
        si = pl.program_id(2)
        av = a_ref[...]
        prods = [_dot(av, piece, 0, 0) for piece in _lane_pieces(_pick_part(refs, nparts, pl.program_id(1), per), p, n)]

        @pl.when(si == 0)
        def _():
            for q in range(p):
                acc[q] = prods[q]

        @pl.when(si > 0)
        def _():
            for q in range(p):
                acc[q] += prods[q]

        @pl.when(si == ns - 1)
        def _():
            o_ref[...] = acc[...].astype(bf16)

    return _pallas_call(
        body, out_shape=_sds((nd, kdim, n), bf16), grid=(kdim // tkw, nd // p, ns),
        in_specs=[pl.BlockSpec((ts, tkw), lambda i, j, k: (k, i))]
        + [pl.BlockSpec((ts, p * n), (lambda hh: lambda i, j, k: (k, _part_step(j, hh, per)))(hh)) for hh in range(nparts)],
        out_specs=pl.BlockSpec((p, tkw, n), lambda i, j, k: (j, i, 0)),
        scratch_shapes=[pltpu.VMEM((p, tkw, n), f32)],
        compiler_params=_cp(("parallel", "parallel", "arbitrary")), name=f"mm_cols_wgrad_{kdim}x{s}x{nd * n}_{nparts}",
    )(a, *parts)


def _rms_fwd(x, g_row):
    s, d = x.shape
    tr = _pick(s, (512, 256, 128))

    def body(x_ref, g_ref, o_ref):
        xv = x_ref[...]
        r = lax.rsqrt(jnp.mean(xv * xv, axis=-1, keepdims=True) + EPS)
        o_ref[...] = (xv * r * g_ref[...]).astype(bf16)

    return _pallas_call(
        body, out_shape=_sds((s, d), bf16), grid=(s // tr,),
        in_specs=[pl.BlockSpec((tr, d), lambda i: (i, 0)), pl.BlockSpec((1, d), lambda i: (0, 0))],
        out_specs=pl.BlockSpec((tr, d), lambda i: (i, 0)),
        compiler_params=_cp(("parallel",)), name=f"rms_fwd_{s}x{d}",
    )(x, g_row)


def _rms_bwd(x, g_row, dh, dres):
    s, d = x.shape
    tr = _pick(s, (512, 256, 128))
    with_dx = dres is not None

    def body(*refs):
        if with_dx:
            x_ref, g_ref, dh_ref, dres_ref, dx_ref, dxb_ref, dg_ref = refs
        else:
            x_ref, g_ref, dh_ref, dg_ref = refs

        @pl.when(pl.program_id(0) == 0)
        def _():
            dg_ref[...] = jnp.zeros_like(dg_ref)

        xv = x_ref[...]
        r = lax.rsqrt(jnp.mean(xv * xv, axis=-1, keepdims=True) + EPS)
        xh = xv * r
        dy = dh_ref[...].astype(f32)
        dg_ref[...] += jnp.sum(dy * xh, axis=0, keepdims=True)
        if with_dx:
            gy = dy * g_ref[...]
            dx = dres_ref[...] + r * (gy - xh * jnp.mean(gy * xh, axis=-1, keepdims=True))
            dx_ref[...] = dx
            dxb_ref[...] = dx.astype(bf16)

    row = pl.BlockSpec((tr, d), lambda i: (i, 0))
    vec = pl.BlockSpec((1, d), lambda i: (0, 0))
    if with_dx:
        return _pallas_call(
            body, out_shape=(_sds((s, d), f32), _sds((s, d), bf16), _sds((1, d), f32)), grid=(s // tr,),
            in_specs=[row, vec, row, row], out_specs=(row, row, vec),
            compiler_params=_cp(("arbitrary",)), name=f"rms_bwd_{s}x{d}",
        )(x, g_row, dh, dres)
    return _pallas_call(
        body, out_shape=_sds((1, d), f32), grid=(s // tr,),
        in_specs=[row, vec, row], out_specs=vec,
        compiler_params=_cp(("arbitrary",)), name=f"rms_bwd_gain_{s}x{d}",
    )(x, g_row, dh)


def _gmlp_rows(s):
    return CHUNK * (2 if (s // CHUNK) % 2 == 0 else 1)


def _gmlp_fwd(z, g_v, w_s, b_t):
    s = z.shape[0]
    t = g_v.shape[1]
    ng = t // HEAD
    rb = _gmlp_rows(s)

    def body(z_ref, gv_ref, ws_ref, bt_ref, o_ref):
        for ci in range(rb // CHUNK):
            lo = ci * CHUNK
            a = _gelu(z_ref[lo:lo + CHUNK, :])
            u, vv = a[:, :t], a[:, t:]
            r = lax.rsqrt(jnp.mean(vv * vv, axis=-1, keepdims=True) + EPS)
            vn = (vv * r * gv_ref[...]).astype(bf16)
            for g in range(ng):
                cs = slice(g * HEAD, (g + 1) * HEAD)
                sg = _dot(ws_ref[g].astype(bf16), vn[:, cs], 1, 0) + bt_ref[:, g:g + 1]
                o_ref[lo:lo + CHUNK, cs] = (u[:, cs] * sg).astype(bf16)

    return _pallas_call(
        body, out_shape=_sds((s, t + MEM_WIDTH), bf16), grid=(s // rb,),
        in_specs=[pl.BlockSpec((rb, 2 * t), lambda i: (i, 0)), pl.BlockSpec((1, t), lambda i: (0, 0)),
                  pl.BlockSpec((ng, CHUNK, CHUNK), lambda i: (0, 0, 0)), pl.BlockSpec((CHUNK, ng), lambda i: (0, 0))],
        out_specs=pl.BlockSpec((rb, t), lambda i: (i, 0)),
        compiler_params=_cp(("parallel",)), name=f"gmlp_fwd_{s}",
    )(z, g_v, w_s, b_t)


def _gmlp_bwd(z, g_v, w_s, b_t, dtok):
    s = z.shape[0]
    t = g_v.shape[1]
    ng = t // HEAD
    rb = _gmlp_rows(s)
    nsteps = s // rb

    def body(z_ref, gv_ref, ws_ref, bt_ref, dt_ref, dz_ref, dws_ref, dbt_ref, dgv_ref, ds_acc):
        step = pl.program_id(0)

        @pl.when(step == 0)
        def _():
            dws_ref[...] = jnp.zeros_like(dws_ref)
            dgv_ref[...] = jnp.zeros_like(dgv_ref)
            ds_acc[...] = jnp.zeros_like(ds_acc)

        for ci in range(rb // CHUNK):
            lo = ci * CHUNK
            zz = z_ref[lo:lo + CHUNK, :]
            a = _gelu(zz)
            u, vv = a[:, :t], a[:, t:]
            r = lax.rsqrt(jnp.mean(vv * vv, axis=-1, keepdims=True) + EPS)
            vh = vv * r
            vn = (vh * gv_ref[...]).astype(bf16)
            dtok = dt_ref[lo:lo + CHUNK, :].astype(f32)
            ds = dtok * u
            ds_acc[...] += ds
            dsb = ds.astype(bf16)
            du_parts, dvn_parts = [], []
            for g in range(ng):
                cs = slice(g * HEAD, (g + 1) * HEAD)
                wg = ws_ref[g].astype(bf16)
                sg = _dot(wg, vn[:, cs], 1, 0) + bt_ref[:, g:g + 1]
                du_parts.append(dtok[:, cs] * sg)
                dws_ref[g] += _dot(dsb[:, cs], vn[:, cs], 1, 1)
                dvn_parts.append(_dot(wg, dsb[:, cs], 0, 0))
            dvn = jnp.concatenate(dvn_parts, axis=1)
            dgv_ref[...] += jnp.sum(dvn * vh, axis=0, keepdims=True)
            gy = dvn * gv_ref[...]
            dvv = r * (gy - vh * jnp.mean(gy * vh, axis=-1, keepdims=True))
            da = jnp.concatenate(du_parts + [dvv], axis=1)
            dz_ref[lo:lo + CHUNK, :] = (da * _gelu_grad(zz)).astype(bf16)

        @pl.when(step == nsteps - 1)
        def _():
            for g in range(ng):
                dbt_ref[:, g:g + 1] = jnp.sum(ds_acc[:, g * HEAD:(g + 1) * HEAD], axis=1, keepdims=True)

    return _pallas_call(
        body,
        out_shape=(_sds((s, z.shape[1]), bf16), _sds((ng, CHUNK, CHUNK), f32), _sds((CHUNK, ng), f32), _sds((1, t), f32)),
        grid=(nsteps,),
        in_specs=[pl.BlockSpec((rb, 2 * t), lambda i: (i, 0)), pl.BlockSpec((1, t), lambda i: (0, 0)),
                  pl.BlockSpec((ng, CHUNK, CHUNK), lambda i: (0, 0, 0)), pl.BlockSpec((CHUNK, ng), lambda i: (0, 0)),
                  pl.BlockSpec((rb, t), lambda i: (i, 0))],
        out_specs=(pl.BlockSpec((rb, 2 * t), lambda i: (i, 0)), pl.BlockSpec((ng, CHUNK, CHUNK), lambda i: (0, 0, 0)),
                   pl.BlockSpec((CHUNK, ng), lambda i: (0, 0)), pl.BlockSpec((1, t), lambda i: (0, 0))),
        scratch_shapes=[pltpu.VMEM((CHUNK, t), f32)],
        compiler_params=_cp(("arbitrary",)), name=f"gmlp_bwd_{s}",
    )(z, g_v, w_s, b_t, dtok)


def _rope_tables(s):
    n_rows = s // GRID_W
    rows = jnp.broadcast_to(jnp.arange(n_rows)[:, None], (n_rows, GRID_W)).reshape(s)
    cols = jnp.broadcast_to(jnp.arange(GRID_W)[None, :], (n_rows, GRID_W)).reshape(s)
    freqs = ROPE_THETA ** (-jnp.arange(ROPE_PAIRS, dtype=f32) / ROPE_PAIRS)
    ang_r = rows.astype(f32)[:, None] * freqs
    ang_c = cols.astype(f32)[:, None] * freqs
    ang = jnp.concatenate([ang_r, ang_r, ang_c, ang_c], axis=-1)
    cos, sin = jnp.cos(ang), jnp.sin(ang)
    first = (jnp.arange(HEAD) % (2 * ROPE_PAIRS)) < ROPE_PAIRS
    return cos, jnp.where(first, -sin, 0.0), jnp.where(first, 0.0, sin)


def _attn_prep_fwd(z, g_q, g_k, tables, t):
    s = z.shape[0]
    tr = _pick(s, (256, 128))
    nq = t // HEAD
    width = t + 2 * KV_WIDTH

    def body(z_ref, gq_ref, gk_ref, cos_ref, sa_ref, sb_ref, q_ref, k_ref, v_ref):
        cos, sa, sb = cos_ref[...], sa_ref[...], sb_ref[...]
        for h in range(nq + KV_HEADS):
            cs = slice(h * HEAD, (h + 1) * HEAD)
            xv = z_ref[:, cs]
            r = lax.rsqrt(jnp.mean(xv * xv, axis=-1, keepdims=True) + EPS)
            xn = xv * r * (gq_ref[...] if h < nq else gk_ref[...])
            y = xn * cos + _rot(xn, sa, sb)
            if h < nq:
                q_ref[:, cs] = (y * (SCALE * LOG2E)).astype(bf16)
            else:
                k_ref[:, (h - nq) * HEAD:(h - nq + 1) * HEAD] = y.astype(bf16)
        v_ref[...] = z_ref[:, t + KV_WIDTH:width].astype(bf16)

    row = lambda w: pl.BlockSpec((tr, w), lambda i: (i, 0))
    vec = pl.BlockSpec((1, HEAD), lambda i: (0, 0))
    return _pallas_call(
        body, out_shape=(_sds((s, t), bf16), _sds((s, KV_WIDTH), bf16), _sds((s, KV_WIDTH), bf16)), grid=(s // tr,),
        in_specs=[row(width), vec, vec, row(HEAD), row(HEAD), row(HEAD)],
        out_specs=(row(t), row(KV_WIDTH), row(KV_WIDTH)),
        compiler_params=_cp(("parallel",)), name=f"attn_prep_fwd_{s}",
    )(z, g_q, g_k, *tables)


def _attn_prep_bwd(z, g_q, g_k, tables, dq, dk, dv, t):
    s = z.shape[0]
    tr = _pick(s, (256, 128))
    nq = t // HEAD
    width = t + 2 * KV_WIDTH

    def body(z_ref, gq_ref, gk_ref, cos_ref, sa_ref, sb_ref, dq_ref, dk_ref, dv_ref, dz_ref, dgq_ref, dgk_ref):
        @pl.when(pl.program_id(0) == 0)
        def _():
            dgq_ref[...] = jnp.zeros_like(dgq_ref)
            dgk_ref[...] = jnp.zeros_like(dgk_ref)

        cos, sa, sb = cos_ref[...], sa_ref[...], sb_ref[...]
        for h in range(nq + KV_HEADS):
            cs = slice(h * HEAD, (h + 1) * HEAD)
            xv = z_ref[:, cs]
            r = lax.rsqrt(jnp.mean(xv * xv, axis=-1, keepdims=True) + EPS)
            xh = xv * r
            if h < nq:
                dy, g_ref, dg_ref = dq_ref[:, cs], gq_ref, dgq_ref
            else:
                dy, g_ref, dg_ref = dk_ref[:, (h - nq) * HEAD:(h - nq + 1) * HEAD], gk_ref, dgk_ref
            dy = dy.astype(f32)
            dxn = dy * cos - _rot(dy, sa, sb)
            dg_ref[...] += jnp.sum(dxn * xh, axis=0, keepdims=True)
            gy = dxn * g_ref[...]
            dz_ref[:, cs] = (r * (gy - xh * jnp.mean(gy * xh, axis=-1, keepdims=True))).astype(bf16)
        dz_ref[:, t + KV_WIDTH:width] = dv_ref[...].astype(bf16)

    row = lambda w: pl.BlockSpec((tr, w), lambda i: (i, 0))
    vec = pl.BlockSpec((1, HEAD), lambda i: (0, 0))
    return _pallas_call(
        body, out_shape=(_sds((s, z.shape[1]), bf16), _sds((1, HEAD), f32), _sds((1, HEAD), f32)), grid=(s // tr,),
        in_specs=[row(width), vec, vec, row(HEAD), row(HEAD), row(HEAD), row(t), row(KV_WIDTH), row(KV_WIDTH)],
        out_specs=(row(width), vec, vec),
        compiler_params=_cp(("arbitrary",)), name=f"attn_prep_bwd_{s}",
    )(z, g_q, g_k, *tables, dq, dk, dv)


def _flash_tiles(s):
    return _pick(s, (512, 256, 128)), _pick(s, (1024, 512, 256, 128))


def _flash_fwd(q, k, v):
    s, t = q.shape
    grp = t // KV_WIDTH
    tq, tk = _flash_tiles(s)
    nkv = s // tk
    rows = grp * tq

    def body(q_ref, k_ref, v_ref, o_ref, lse_ref, m_sc, acc_sc):
        ki = pl.program_id(2)

        @pl.when(ki == 0)
        def _():
            m_sc[...] = jnp.full(m_sc.shape, -jnp.inf, f32)
            acc_sc[...] = jnp.zeros_like(acc_sc)

        kk = k_ref[...]
        v1 = jnp.concatenate([v_ref[...], jnp.ones((tk, HEAD), bf16)], axis=1)
        sc = _dot(q_ref[:, 0:HEAD], kk, 1, 1)
        for g in range(grp):
            sc_next = _dot(q_ref[:, (g + 1) * HEAD:(g + 2) * HEAD], kk, 1, 1) if g + 1 < grp else None
            mine = slice(g * tq, (g + 1) * tq)
            m_prev = m_sc[mine, :]
            m_new = jnp.maximum(m_prev, jnp.max(sc, axis=-1, keepdims=True))
            alpha = jnp.exp2(m_prev - m_new)
            p = jnp.exp2((sc - m_new).astype(bf16))
            acc_sc[mine, :] = alpha * acc_sc[mine, :] + _dot(p, v1, 1, 0)
            m_sc[mine, :] = m_new
            sc = sc_next

        @pl.when(ki == nkv - 1)
        def _():
            acc = acc_sc[...]
            l = acc[:, HEAD:HEAD + 1]
            o = acc[:, :HEAD] / l
            for g in range(grp):
                o_ref[:, g * HEAD:(g + 1) * HEAD] = o[g * tq:(g + 1) * tq].astype(bf16)
            lse_ref[0] = jnp.broadcast_to(m_sc[...] + jnp.log(l) * LOG2E, (rows, HEAD))

    return _pallas_call(
        body, out_shape=(_sds((s, t + MEM_WIDTH), bf16), _sds((KV_HEADS, grp * s, HEAD), f32)), grid=(KV_HEADS, s // tq, nkv),
        in_specs=[pl.BlockSpec((tq, grp * HEAD), lambda h, i, j: (i, h)), pl.BlockSpec((tk, HEAD), lambda h, i, j: (j, h)),
                  pl.BlockSpec((tk, HEAD), lambda h, i, j: (j, h))],
        out_specs=(pl.BlockSpec((tq, grp * HEAD), lambda h, i, j: (i, h)), pl.BlockSpec((1, rows, HEAD), lambda h, i, j: (h, i, 0))),
        scratch_shapes=[pltpu.VMEM((rows, 1), f32), pltpu.VMEM((rows, 2 * HEAD), f32)],
        compiler_params=_cp(("parallel", "parallel", "arbitrary")), name=f"flash_fwd_{s}",
    )(q, k, v)


def _flash_delta(o, do, t):
    s = o.shape[0]
    grp = t // KV_WIDTH
    tq, _ = _flash_tiles(s)
    rows = grp * tq

    def body(o_ref, do_ref, d_ref):
        for g in range(grp):
            cs = slice(g * HEAD, (g + 1) * HEAD)
            dd = jnp.sum(o_ref[:, cs].astype(f32) * do_ref[:, cs].astype(f32), axis=-1, keepdims=True)
            d_ref[0, g * tq:(g + 1) * tq, :] = jnp.broadcast_to(dd, (tq, HEAD))

    qb = pl.BlockSpec((tq, grp * HEAD), lambda h, i: (i, h))
    return _pallas_call(
        body, out_shape=_sds((KV_HEADS, grp * s, HEAD), f32), grid=(KV_HEADS, s // tq),
        in_specs=[qb, qb], out_specs=pl.BlockSpec((1, rows, HEAD), lambda h, i: (h, i, 0)),
        compiler_params=_cp(("parallel", "parallel")), name=f"flash_delta_{s}",
    )(o, do)


def _flash_bwd(q, k, v, o, do, lse):
    s, t = q.shape
    grp = t // KV_WIDTH
    tq, tk = _flash_tiles(s)
    nq, nkv = s // tq, s // tk
    rows = grp * tq
    delta = _flash_delta(o, do, t)

    def body(q_ref, k_ref, v_ref, do_ref, lse_ref, delta_ref, dq_ref, dk_ref, dv_ref, dq_acc, dk_acc, dv_acc):
        kj, qi = pl.program_id(1), pl.program_id(2)

        @pl.when(qi == 0)
        def _():
            dk_acc[...] = jnp.zeros_like(dk_acc)
            dv_acc[...] = jnp.zeros_like(dv_acc)

        kk, vv = k_ref[...], v_ref[...]

        def products(g):
            qg, dog = q_ref[:, g * HEAD:(g + 1) * HEAD], do_ref[:, g * HEAD:(g + 1) * HEAD]
            return qg, dog, _dot(qg, kk, 1, 1), _dot(dog, vv, 1, 1)

        ahead = products(0)
        dv_sum = dk_sum = None
        terms = []
        for g in range(grp):
            qg, dog, sc, dp = ahead
            if g + 1 < grp:
                ahead = products(g + 1)
            head_rows = slice(g * tq, (g + 1) * tq)
            p = jnp.exp2((sc - lse_ref[0, head_rows, 0:1]).astype(bf16))
            ds = p * (dp - delta_ref[0, head_rows, 0:1]).astype(bf16)
            dv_g, dk_g = _dot(p, dog, 0, 0), _dot(ds, qg, 0, 0)
            dv_sum = dv_g if dv_sum is None else dv_sum + dv_g
            dk_sum = dk_g if dk_sum is None else dk_sum + dk_g
            terms.append(_dot(ds, kk, 1, 0))
        dv_acc[...] += dv_sum
        dk_acc[...] += dk_sum
        mine = pl.ds(pl.multiple_of(qi * rows, rows), rows)
        term = jnp.concatenate(terms, axis=0)

        @pl.when(kj == 0)
        def _():
            dq_acc[mine, :] = term

        @pl.when(kj > 0)
        def _():
            dq_acc[mine, :] += term

        @pl.when(kj == nkv - 1)
        def _():
            total = dq_acc[mine, :]
            for g in range(grp):
                dq_ref[:, g * HEAD:(g + 1) * HEAD] = total[g * tq:(g + 1) * tq] * SCALE

        @pl.when(qi == nq - 1)
        def _():
            dk_ref[...] = dk_acc[...] * (1.0 / LOG2E)
            dv_ref[...] = dv_acc[...]

    qb = pl.BlockSpec((tq, grp * HEAD), lambda h, j, i: (i, h))
    kb = pl.BlockSpec((tk, HEAD), lambda h, j, i: (j, h))
    lb = pl.BlockSpec((1, rows, HEAD), lambda h, j, i: (h, i, 0))
    dqb = pl.BlockSpec((tq, grp * HEAD), lambda h, j, i: (jnp.where(j == nkv - 1, i, 0), h))
    return _pallas_call(
        body, out_shape=(_sds((s, t), f32), _sds((s, KV_WIDTH), f32), _sds((s, KV_WIDTH), f32)), grid=(KV_HEADS, nkv, nq),
        in_specs=[qb, kb, kb, qb, lb, lb], out_specs=(dqb, kb, kb),
        scratch_shapes=[pltpu.VMEM((nq * rows, HEAD), f32), pltpu.VMEM((tk, HEAD), f32), pltpu.VMEM((tk, HEAD), f32)],
        compiler_params=_cp(("parallel", "arbitrary", "arbitrary")), name=f"flash_bwd_{s}",
    )(q, k, v, do, lse, delta)


def _mem_heads(z_ref, kv_ref, gq_ref, gk_ref, h):
    cs = slice(h * HEAD, (h + 1) * HEAD)
    xv = z_ref[:, cs]
    r = lax.rsqrt(jnp.mean(xv * xv, axis=-1, keepdims=True) + EPS)
    xh = xv * r
    kx = kv_ref[:, cs]
    rk = lax.rsqrt(jnp.mean(kx * kx, axis=-1, keepdims=True) + EPS)
    kn = (kx * rk * gk_ref[...]).astype(bf16)
    vv = kv_ref[:, MEM_WIDTH + h * HEAD:MEM_WIDTH + (h + 1) * HEAD].astype(bf16)
    qn = (xh * gq_ref[...]).astype(bf16)
    sc = _dot(qn, kn, 1, 1) * SCALE
    e = jnp.exp(sc - jnp.max(sc, axis=-1, keepdims=True))
    p = e / jnp.sum(e, axis=-1, keepdims=True)
    return cs, r, xh, qn, kn, vv, p


def _mem_fwd(z, qblk, kv, g_mq, g_mk, cat):
    s = z.shape[0]
    nm = kv.shape[0]
    tr = _pick(s, (512, 256, 128))
    oblk = cat.shape[1] // MEM_WIDTH - 1

    def body(z_ref, kv_ref, gq_ref, gk_ref, cat_ref, o_ref):
        for h in range(MEM_HEADS):
            cs, _, _, _, _, vv, p = _mem_heads(z_ref, kv_ref, gq_ref, gk_ref, h)
            o_ref[:, cs] = _dot(p.astype(bf16), vv, 1, 0).astype(bf16)

    vec = pl.BlockSpec((1, HEAD), lambda i: (0, 0))
    return _pallas_call(
        body, out_shape=_sds(cat.shape, bf16), grid=(s // tr,),
        in_specs=[pl.BlockSpec((tr, MEM_WIDTH), lambda i: (i, qblk)), pl.BlockSpec((nm, 2 * MEM_WIDTH), lambda i: (0, 0)), vec, vec,
                  pl.BlockSpec(memory_space=pl.ANY)],
        out_specs=pl.BlockSpec((tr, MEM_WIDTH), lambda i: (i, oblk)),
        input_output_aliases={4: 0},
        compiler_params=_cp(("parallel",)), name=f"mem_fwd_{s}_{qblk}",
    )(z, kv, g_mq, g_mk, cat)


def _mem_bwd(z, qblk, kv, g_mq, g_mk, dcat, dz):
    s = z.shape[0]
    nm = kv.shape[0]
    tr = _pick(s, (512, 256, 128))
    dblk = dcat.shape[1] // MEM_WIDTH - 1

    def body(z_ref, kv_ref, gq_ref, gk_ref, dm_ref, dzin_ref, dz_ref, dkn_ref, dv_ref, dgq_ref):
        @pl.when(pl.program_id(0) == 0)
        def _():
            dkn_ref[...] = jnp.zeros_like(dkn_ref)
            dv_ref[...] = jnp.zeros_like(dv_ref)
            dgq_ref[...] = jnp.zeros_like(dgq_ref)

        for h in range(MEM_HEADS):
            cs, r, xh, qn, kn, vv, p = _mem_heads(z_ref, kv_ref, gq_ref, gk_ref, h)
            dm = dm_ref[:, cs]
            dv_ref[:, cs] += _dot(p.astype(bf16), dm, 0, 0)
            dp = _dot(dm, vv, 1, 1)
            ds = (p * (dp - jnp.sum(dp * p, axis=-1, keepdims=True)) * SCALE).astype(bf16)
            dqn = _dot(ds, kn, 1, 0)
            dkn_ref[:, cs] += _dot(ds, qn, 0, 0)
            dgq_ref[...] += jnp.sum(dqn * xh, axis=0, keepdims=True)
            gy = dqn * gq_ref[...]
            dz_ref[:, cs] = (r * (gy - xh * jnp.mean(gy * xh, axis=-1, keepdims=True))).astype(bf16)

    vec = pl.BlockSpec((1, HEAD), lambda i: (0, 0))
    acc = pl.BlockSpec((nm, MEM_WIDTH), lambda i: (0, 0))
    return _pallas_call(
        body, out_shape=(_sds(dz.shape, bf16), _sds((nm, MEM_WIDTH), f32), _sds((nm, MEM_WIDTH), f32), _sds((1, HEAD), f32)),
        grid=(s // tr,),
        in_specs=[pl.BlockSpec((tr, MEM_WIDTH), lambda i: (i, qblk)), pl.BlockSpec((nm, 2 * MEM_WIDTH), lambda i: (0, 0)), vec, vec,
                  pl.BlockSpec((tr, MEM_WIDTH), lambda i: (i, dblk)), pl.BlockSpec(memory_space=pl.ANY)],
        out_specs=(pl.BlockSpec((tr, MEM_WIDTH), lambda i: (i, qblk)), acc, acc, vec),
        input_output_aliases={5: 0},
        compiler_params=_cp(("arbitrary",)), name=f"mem_bwd_{s}_{qblk}",
    )(z, kv, g_mq, g_mk, dcat, dz)


def _mem_kv_bwd(kv, g_mk, dkn, dv):
    nm = kv.shape[0]

    def body(kv_ref, gk_ref, dkn_ref, dv_ref, dkv_ref, dgk_ref):
        dgk = jnp.zeros((1, HEAD), f32)
        for h in range(MEM_HEADS):
            cs = slice(h * HEAD, (h + 1) * HEAD)
            kx = kv_ref[:, cs]
            rk = lax.rsqrt(jnp.mean(kx * kx, axis=-1, keepdims=True) + EPS)
            kh = kx * rk
            dkn_h = dkn_ref[:, cs]
            dgk = dgk + jnp.sum(dkn_h * kh, axis=0, keepdims=True)
            gy = dkn_h * gk_ref[...]
            dkv_ref[:, cs] = (rk * (gy - kh * jnp.mean(gy * kh, axis=-1, keepdims=True))).astype(bf16)
        dkv_ref[:, MEM_WIDTH:] = dv_ref[...].astype(bf16)
        dgk_ref[...] = dgk

    return _pallas_call(
        body, out_shape=(_sds((nm, 2 * MEM_WIDTH), bf16), _sds((1, HEAD), f32)),
        compiler_params=pltpu.CompilerParams(vmem_limit_bytes=VMEM_LIMIT), name=f"mem_kv_bwd_{nm}",
    )(kv, g_mk, dkn, dv)


def _ffn_up(h2, g):
    m, kdim = h2.shape
    nd, _, n = g.shape
    half = nd // 2
    assert n % 128 == 0 and h2.dtype == bf16 and g.dtype == bf16
    tm = _pick(m, (512, 256, 128))

    def body(a_ref, wg_ref, wu_ref, gate_ref, up_ref, act_ref):
        av = a_ref[...]
        gt = _dot(av, wg_ref[0], 1, 0)
        up = _dot(av, wu_ref[0], 1, 0)
        gate_ref[...] = gt.astype(bf16)
        up_ref[...] = up.astype(bf16)
        act_ref[...] = (gt * jax.nn.sigmoid(gt) * up).astype(bf16)

    out = pl.BlockSpec((tm, n), lambda i, j: (i, j))
    return _pallas_call(
        body, out_shape=tuple(_sds((m, half * n), bf16) for _ in range(3)), grid=(m // tm, half),
        in_specs=[pl.BlockSpec((tm, kdim), lambda i, j: (i, 0)), pl.BlockSpec((1, kdim, n), lambda i, j: (j, 0, 0)),
                  pl.BlockSpec((1, kdim, n), lambda i, j: (j + half, 0, 0))],
        out_specs=(out, out, out),
        compiler_params=_cp(("parallel", "arbitrary")), name=f"ffn_up_{m}x{kdim}x{half * n}",
    )(h2, g, g)


def _ffn_down_bwd(dx, w_dn, gate, up):
    m, d = dx.shape
    ff = w_dn.shape[0]
    tm = _pick(m, (512, 256, 128))
    tf = _pick(ff, (1408, 1024, 512, 256, 128))

    nsub = 2 if tm % 32 == 0 else 1
    rs = tm // nsub

    def body(dx_ref, w_ref, g_ref, u_ref, dg_ref, du_ref):
        wv = w_ref[...]
        da = _dot(dx_ref[0:rs, :], wv, 1, 1)
        for r in range(nsub):
            da_next = _dot(dx_ref[(r + 1) * rs:(r + 2) * rs, :], wv, 1, 1) if r + 1 < nsub else None
            mine = slice(r * rs, (r + 1) * rs)
            gt = g_ref[mine, :].astype(f32)
            sg = jax.nn.sigmoid(gt)
            dg_ref[mine, :] = (da * u_ref[mine, :].astype(f32) * sg * (1.0 + gt * (1.0 - sg))).astype(bf16)
            du_ref[mine, :] = (da * gt * sg).astype(bf16)
            da = da_next

    tile = pl.BlockSpec((tm, tf), lambda i, j: (i, j))
    return _pallas_call(
        body, out_shape=(_sds((m, ff), bf16), _sds((m, ff), bf16)), grid=(m // tm, ff // tf),
        in_specs=[pl.BlockSpec((tm, d), lambda i, j: (i, 0)), pl.BlockSpec((tf, d), lambda i, j: (j, 0)), tile, tile],
        out_specs=(tile, tile),
        compiler_params=_cp(("parallel", "arbitrary")), name=f"ffn_down_bwd_{m}x{d}x{ff}",
    )(dx, w_dn, gate, up)


def _loss_head(y, target):
    s, d = y.shape
    tr = _pick(s, (512, 256, 128))

    def body(y_ref, t_ref, l_ref, dy_ref, dyb_ref):
        @pl.when(pl.program_id(0) == 0)
        def _():
            l_ref[...] = jnp.zeros_like(l_ref)

        err = y_ref[...] - t_ref[...]
        l_ref[...] += 0.5 * jnp.sum(jnp.mean(err * err, axis=-1, keepdims=True), axis=0, keepdims=True)
        dy = err * (1.0 / d)
        dy_ref[...] = dy
        dyb_ref[...] = dy.astype(bf16)

    row = pl.BlockSpec((tr, d), lambda i: (i, 0))
    return _pallas_call(
        body, out_shape=(_sds((1, HEAD), f32), _sds((s, d), f32), _sds((s, d), bf16)), grid=(s // tr,),
        in_specs=[row, row], out_specs=(pl.BlockSpec((1, HEAD), lambda i: (0, 0)), row, row),
        compiler_params=_cp(("arbitrary",)), name=f"loss_{s}x{d}",
    )(y, target)


def _place():
    return lax.axis_index("x"), lax.axis_index("y"), lax.axis_index("c")


def _tag(arrays):
    return "_".join("x".join(str(dd) for dd in a.shape) for a in arrays)


def _all_gather(shards):
    nw = len(shards)
    hbm = pl.BlockSpec(memory_space=pl.ANY)

    def body(*refs):
        x_refs, out_refs = refs[:nw], refs[nw:2 * nw]
        send_sems, recv_sems, local_sems = refs[2 * nw:]
        x, y, c = _place()
        me, sibling = (x, y, c), (x, y, 1 - c)
        chips = [(1 - x, y), (x, 1 - y), (1 - x, 1 - y)]

        def slot(w, place):
            px, py, pc = place
            return out_refs[w].at[4 * px + 2 * py + pc]

        def copy(k, w, block_of, to, from_input=False):
            return pltpu.make_async_remote_copy(
                src_ref=x_refs[w] if from_input else slot(w, block_of), dst_ref=slot(w, block_of),
                send_sem=send_sems.at[k, w], recv_sem=recv_sems.at[k, w], device_id=to, device_id_type=MESH)

        mine = [pltpu.make_async_copy(x_refs[w], slot(w, me), local_sems.at[w]) for w in range(nw)]
        for cp in mine:
            cp.start()
        first = []
        for w in range(nw):
            first.append(copy(0, w, me, sibling, from_input=True))
            first += [copy(1 + j, w, me, (*chip, c), from_input=True) for j, chip in enumerate(chips)]
        for cp in first:
            cp.start()
        passed = []
        for w in range(nw):
            for j, chip in enumerate(chips):
                copy(1 + j, w, (*chip, c), me).wait_recv()
                fwd = copy(4 + j, w, (*chip, c), sibling)
                fwd.start()
                passed.append(fwd)
        for w in range(nw):
            copy(0, w, sibling, me).wait_recv()
            for j, chip in enumerate(chips):
                copy(4 + j, w, (*chip, 1 - c), me).wait_recv()
        for cp in first + passed:
            cp.wait_send()
        for cp in mine:
            cp.wait()

    return _pallas_call(
        body, out_shape=tuple(_sds((N_DEV,) + a.shape, a.dtype) for a in shards), in_specs=[hbm] * nw, out_specs=tuple([hbm] * nw),
        scratch_shapes=[pltpu.SemaphoreType.DMA((7, nw)), pltpu.SemaphoreType.DMA((7, nw)), pltpu.SemaphoreType.DMA((nw,))],
        name=f"all_gather_{_tag(shards)}_{jnp.dtype(shards[0].dtype).name}",
    )(*shards)


def _pair_sum(grad, got):
    nd, a, b = grad.shape
    nchip = nd // 2
    ta = _pick(a, (1024, 704, 512, 352, 256, 128, 64, 32, 16))

    def my_chip():
        return 2 * lax.axis_index("x") + lax.axis_index("y")

    def body(a_ref, b_ref, o_ref, land_ref):
        tot = (a_ref[...].astype(f32) + b_ref[...].astype(f32)).astype(o_ref.dtype)
        o_ref[...] = tot

        @pl.when(pl.program_id(1) == my_chip())
        def _():
            land_ref[...] = tot

    return _pallas_call(
        body, out_shape=(_sds(got.shape, grad.dtype), _sds(got.shape, grad.dtype)), grid=(a // ta, nchip),
        in_specs=[pl.BlockSpec((None, ta, b), lambda i, k: (2 * k + lax.axis_index("c"), i, 0)),
                  pl.BlockSpec((None, ta, b), lambda i, k: (k, i, 0))],
        out_specs=(pl.BlockSpec((None, ta, b), lambda i, k: (k, i, 0)),
                   pl.BlockSpec((None, ta, b), lambda i, k: (my_chip(), i, 0))),
        compiler_params=_cp(("parallel", "arbitrary")), name=f"pair_sum_{a}x{b}",
    )(grad, got)


_HBM = pl.BlockSpec(memory_space=pltpu.HBM)
_SEM = pl.BlockSpec(memory_space=pltpu.SEMAPHORE)
_ANY = pl.BlockSpec(memory_space=pl.ANY)
_DATAFLOW = pltpu.SideEffectType.DATAFLOW_SIDE_EFFECTING


def _in_hbm(a):
    return pltpu.with_memory_space_constraint(a, pltpu.HBM)


def _exchange_begin(bufs, nw, route, after, copies_of, n_copies, name):
    nb = len(bufs)

    def body(*refs):
        send_sems, recv_sems = refs[nb + 2], refs[nb + 3]
        for w in range(nw):
            for k, (src, dst, to) in enumerate(copies_of(w, refs[:nb])):
                pltpu.make_async_remote_copy(src_ref=src, dst_ref=dst, send_sem=send_sems.at[k * nw + w],
                                             recv_sem=recv_sems.at[k * nw + w], device_id=to, device_id_type=MESH).start()

    out = _pallas_call(
        body, name=name,
        out_shape=(pltpu.SemaphoreType.DMA((n_copies * nw,)), pltpu.SemaphoreType.DMA((n_copies * nw,)),
                   *[pltpu.HBM(a.shape, a.dtype) for a in bufs], pltpu.HBM(route.shape, route.dtype)),
        in_specs=[_HBM] * (nb + 1) + [_ANY], out_specs=(_SEM, _SEM, *[_HBM] * (nb + 1)),
        input_output_aliases={i: 2 + i for i in range(nb + 1)},
        compiler_params=pltpu.CompilerParams(has_side_effects=_DATAFLOW),
    )(*[_in_hbm(a) for a in bufs], _in_hbm(route), after)
    return (out[0], out[1], out[2:2 + nb], nw), out[2 + nb]


def _exchange_end(handle, after, copies_of, n_copies, name):
    send_sems, recv_sems, thru, nw = handle
    nb = len(thru)

    def body(*refs):
        send_sems, recv_sems = refs[nb], refs[nb + 1]
        for w in range(nw):
            for k, (src, dst, to) in enumerate(copies_of(w, refs[:nb])):
                cp = pltpu.make_async_remote_copy(src_ref=src, dst_ref=dst, send_sem=send_sems.at[k * nw + w],
                                                  recv_sem=recv_sems.at[k * nw + w], device_id=to, device_id_type=MESH)
                cp.wait_send()
                cp.wait_recv()

    out = _pallas_call(
        body, name=name, out_shape=tuple(pltpu.HBM(a.shape, a.dtype) for a in thru),
        in_specs=[_HBM] * nb + [_SEM, _SEM, _ANY], out_specs=tuple([_HBM] * nb),
        input_output_aliases={i: i for i in range(nb)},
        compiler_params=pltpu.CompilerParams(has_side_effects=_DATAFLOW),
    )(*thru, send_sems, recv_sems, after)
    return list(out)


def _my_slot():
    return 4 * lax.axis_index("x") + 2 * lax.axis_index("y") + lax.axis_index("c")


def _shard_into_land(w_all, idx):
    _, a, b = w_all.shape
    ta = next(cc for cc in (1024, 704, 512, 352, 256, 128, 64, 32, 16) if a % cc == 0 and (cc * b * 4 <= 2 ** 21 or cc == 16))

    def body(w_ref, o_ref):
        o_ref[...] = w_ref[...].astype(bf16)

    return _pallas_call(
        body, out_shape=_sds((N_DEV, a, b), bf16), grid=(a // ta,),
        in_specs=[pl.BlockSpec((None, ta, b), lambda i: (idx, i, 0))],
        out_specs=pl.BlockSpec((None, ta, b), lambda i: (_my_slot(), i, 0)),
        compiler_params=_cp(("parallel",)), name=f"shard_into_land_{a}x{b}_{idx}",
    )(w_all)


def _gather_copies(w, land_refs):
    x, y, c = _place()
    blk = land_refs[w].at[4 * x + 2 * y + c]
    return [(blk, blk, to) for to in ((x, y, 1 - c), (1 - x, y, c), (x, 1 - y, c), (1 - x, 1 - y, c))]


def _gather_begin(lands, route, after, tag):
    return _exchange_begin(lands, len(lands), route, after, _gather_copies, 4, f"gather_begin_{tag}")


def _gather_end(handle, after, tag):
    return _exchange_end(handle, after, _gather_copies, 4, f"gather_end_{tag}")


def _gather_pass_on(lands):
    nw = len(lands)

    def body(*refs):
        l_refs = refs[nw:2 * nw]
        send_sems, recv_sems = refs[2 * nw:]
        x, y, c = _place()
        copies = []
        for w in range(nw):
            for j, (px, py) in enumerate([(1 - x, y), (x, 1 - y), (1 - x, 1 - y)]):
                blk = l_refs[w].at[4 * px + 2 * py + c]
                copies.append(pltpu.make_async_remote_copy(
                    src_ref=blk, dst_ref=blk, send_sem=send_sems.at[j, w], recv_sem=recv_sems.at[j, w],
                    device_id=(x, y, 1 - c), device_id_type=MESH))
        for cp in copies:
            cp.start()
        for cp in copies:
            cp.wait_send()
        for w in range(nw):
            for j, (px, py) in enumerate([(1 - x, y), (x, 1 - y), (1 - x, 1 - y)]):
                blk = l_refs[w].at[4 * px + 2 * py + (1 - c)]
                pltpu.make_async_remote_copy(src_ref=blk, dst_ref=blk, send_sem=send_sems.at[j, w], recv_sem=recv_sems.at[j, w],
                                             device_id=(x, y, 1 - c), device_id_type=MESH).wait_recv()

    return _pallas_call(
        body, out_shape=tuple(_sds(a.shape, a.dtype) for a in lands), in_specs=[_ANY] * nw, out_specs=tuple([_ANY] * nw),
        input_output_aliases={w: w for w in range(nw)},
        scratch_shapes=[pltpu.SemaphoreType.DMA((3, nw)), pltpu.SemaphoreType.DMA((3, nw))],
        name=f"gather_pass_on_{_tag(lands)}",
    )(*lands)


def _swap_copies(w, refs):
    x, y, c = _place()
    nw = len(refs) // 2
    return [(refs[w].at[2 * k + (1 - c)], refs[nw + w].at[k], (x, y, 1 - c)) for k in range(N_DEV // 2)]


def _swap_begin(grads, route, after, tag):
    lands = [lax.empty((N_DEV // 2,) + g.shape[1:], g.dtype) for g in grads]
    return _exchange_begin(list(grads) + lands, len(grads), route, after, _swap_copies, N_DEV // 2, f"swap_begin_{tag}")


def _swap_end(handle, after, tag):
    out = _exchange_end(handle, after, _swap_copies, N_DEV // 2, f"swap_end_{tag}")
    return out[:len(out) // 2], out[len(out) // 2:]


def _scatter_copies(w, refs):
    x, y, c = _place()
    nw = len(refs) // 2
    dst = refs[nw + w].at[2 * x + y]
    return [(refs[w].at[2 * px + py], dst, (px, py, c)) for px, py in ((1 - x, y), (x, 1 - y), (1 - x, 1 - y))]


def _scatter_begin(psums, lands, route, after, tag):
    return _exchange_begin(list(psums) + list(lands), len(psums), route, after, _scatter_copies, 3, f"scatter_begin_{tag}")


def _scatter_end(handle, after, tag):
    return _exchange_end(handle, after, _scatter_copies, 3, f"scatter_end_{tag}")


def _adamw(parts, w_all, m_all, v_all, l, carried):
    nparts, a, b = parts.shape
    nl = w_all.shape[0]
    ta = next(cc for cc in (1024, 704, 512, 352, 256, 128, 64, 32, 16, 8) if a % cc == 0 and (cc * b * 4 <= 2 ** 20 or cc == 8))
    c1 = 1.0 / (1.0 - ADAM_B1 ** ADAM_STEP)
    c2 = 1.0 / (1.0 - ADAM_B2 ** ADAM_STEP)

    def body(p_ref, w_ref, m_ref, v_ref, *rest):
        g_out, d_out, m_out, v_out = rest[-4:]
        g = p_ref[0].astype(f32)
        for k in range(1, nparts):
            g = g + p_ref[k].astype(f32)
        m_new = ADAM_B1 * m_ref[...] + (1.0 - ADAM_B1) * g
        v_new = ADAM_B2 * v_ref[...] + (1.0 - ADAM_B2) * (g * g)
        m_hat = m_new * c1
        v_hat = v_new * c2
        g_out[...] = g
        d_out[...] = -ADAM_LR * (m_hat / (jnp.sqrt(v_hat) + ADAM_EPS) + ADAM_WD * w_ref[...])
        m_out[...] = m_new
        v_out[...] = v_new

    one = pl.BlockSpec((None, ta, b), lambda i: (l, i, 0))
    keep = [] if carried is None else [pl.BlockSpec(memory_space=pl.ANY)] * 4
    return _pallas_call(
        body, out_shape=tuple(_sds((nl, a, b), f32) for _ in range(4)), grid=(a // ta,),
        in_specs=[pl.BlockSpec((nparts, ta, b), lambda i: (0, i, 0)), one, one, one] + keep, out_specs=(one, one, one, one),
        input_output_aliases=({} if carried is None else {4 + q: q for q in range(4)}),
        compiler_params=_cp(("parallel",)), name=f"adamw_{nparts}x{nl}x{a}x{b}_{l}{'' if carried is None else '_carried'}",
    )(parts, w_all, m_all, v_all, *(carried or ()))


def _to_flat(arrays):
    flat = jnp.concatenate([a.reshape(-1).astype(f32) for a in arrays])
    rows = -(-flat.shape[0] // (8 * LANES)) * 8
    return jnp.pad(flat, (0, rows * LANES - flat.shape[0])).reshape(rows, LANES)


def _from_flat(flat, shapes):
    flat = flat.reshape(-1)
    out, off = [], 0
    for shp in shapes:
        n = 1
        for dd in shp:
            n *= dd
        out.append(flat[off:off + n].reshape(shp))
        off += n
    return out


def kernel(x, mem, g_mix, g_ffn, w_in_a, g_v_a, w_spatial, b_spatial, w_in_b, g_q_b, g_k_b, g_mem, w_mem_kv, g_mq, g_mk, w_out, w_gate_up, w_down, loss_target, m_g_mix, m_g_ffn, m_w_in_a, m_g_v_a, m_w_spatial, m_b_spatial, m_w_in_b, m_g_q_b, m_g_k_b, m_g_mem, m_w_mem_kv, m_g_mq, m_g_mk, m_w_out, m_w_gate_up, m_w_down, v_g_mix, v_g_ffn, v_w_in_a, v_g_v_a, v_w_spatial, v_b_spatial, v_w_in_b, v_g_q_b, v_g_k_b, v_g_mem, v_w_mem_kv, v_g_mq, v_g_mk, v_w_out, v_w_gate_up, v_w_down):
    given = dict(locals())
    depth = g_mix.shape[0]
    s, d = x.shape[1], x.shape[2]
    nm = mem.shape[1]
    t = d - MEM_WIDTH
    ff = w_down.shape[1] * N_DEV
    x0 = x.reshape(s, d)
    mem0 = mem.reshape(nm, d)
    target = loss_target.reshape(s, d)
    tables = _rope_tables(s)

    big_names = ("w_in", "w_mem_kv", "w_out", "w_gate_up", "w_down")

    def stacked_key(name, l):
        if name == "w_in":
            return ("w_in_a" if l % 2 == 0 else "w_in_b"), l // 2
        return name, l

    n_mix = 3

    def own_shards(l):
        return [_shard_into_land(given[key], idx) for key, idx in (stacked_key(name, l) for name in big_names)]

    def gather_start(l, route, after):
        lands = cast_shards[l]
        h_mix, route = _gather_begin(lands[:n_mix], route, after, f"mix{l}")
        h_ffn, route = _gather_begin(lands[n_mix:], route, after, f"ffn{l}")
        return h_mix, h_ffn, route

    def gather_finish(handle, after, tag):
        return list(_gather_pass_on(_gather_end(handle, after, tag)))

    saved = []
    xc = x0
    cast_shards = {0: own_shards(0)}
    h_in, route0 = _gather_begin(cast_shards[0][:1], g_mix[0].reshape(1, d), mem0, "in0")
    h_rest, route0 = _gather_begin(cast_shards[0][1:n_mix], route0, mem0, "rest0")
    h_ffn, _ = _gather_begin(cast_shards[0][n_mix:], route0, mem0, "ffn0")
    for l in range(1, depth):
        cast_shards[l] = own_shards(l)
    w_mix = gather_finish(h_in, cast_shards[depth - 1][0], "in0")
    w_ffn = None
    for l in range(depth):
        is_a = l % 2 == 0
        g_in = w_mix[0]
        qblk = (N_DEV * g_in.shape[2] - MEM_WIDTH) // MEM_WIDTH

        gm_row, gf_row, gmem_row = g_mix[l].reshape(1, d), g_ffn[l].reshape(1, d), g_mem[l].reshape(1, d)
        gmq_row, gmk_row = g_mq[l].reshape(1, HEAD), g_mk[l].reshape(1, HEAD)
        if l + 1 < depth:
            next_mix, next_ffn, gm_row = gather_start(l + 1, gm_row, g_in)
        h = _rms_fwd(xc, gm_row)
        z = _mm_cols_fwd(h, g_in, f32)
        if l == 0:
            w_mix = w_mix + gather_finish(h_rest, z, "rest0")
        g_kv, g_out = w_mix[1:]
        w_kv, w_o = (g.reshape(-1, g.shape[2]) for g in (g_kv, g_out))
        if is_a:
            ia = l // 2
            mix = dict(g_v=g_v_a[ia].reshape(1, t), w_s=w_spatial[ia], b_t=b_spatial[ia].T)
            cat = _gmlp_fwd(z, mix["g_v"], mix["w_s"], mix["b_t"])
        else:
            ib = l // 2
            mix = dict(g_q=g_q_b[ib].reshape(1, HEAD), g_k=g_k_b[ib].reshape(1, HEAD))
            q, k, v = _attn_prep_fwd(z, mix["g_q"], mix["g_k"], tables, t)
            cat, lse = _flash_fwd(q, k, v)
            mix.update(q=q, k=k, v=v, lse=lse)
        hm = _rms_fwd(mem0, gmem_row)
        kv = _matmul(hm, w_kv)
        cat = _mem_fwd(z, qblk, kv, gmq_row, gmk_row, cat)
        x1 = _matmul(cat, w_o, res=xc)
        h2 = _rms_fwd(x1, gf_row)
        if l == 0:
            w_ffn = gather_finish(h_ffn, h2, "ffn0")
        g_gu, g_dn = w_ffn
        w_dn = g_dn.reshape(-1, g_dn.shape[2])
        gate, up, act = _ffn_up(h2, g_gu)
        x2 = _matmul(act, w_dn, res=x1)
        saved.append(dict(x=xc, h=h, z=z, mix=mix, cat=cat, hm=hm, kv=kv, x1=x1, h2=h2, gate=gate, up=up, act=act, qblk=qblk,
                          w=(g_in, w_kv, w_o, g_gu, w_dn), rows=(gm_row, gf_row, gmem_row, gmq_row, gmk_row)))
        if l + 1 < depth:
            w_mix = gather_finish(next_mix, x2, f"mix{l + 1}")
            w_ffn = gather_finish(next_ffn, x2, f"ffn{l + 1}")
        xc = x2

    loss_row, dy, dy_b = _loss_head(xc, target)
    loss = lax.psum(loss_row[0, 0], ("x", "y", "c"))

    small = {n: [None] * given[n].shape[0] for n in ("g_mix", "g_ffn", "g_v_a", "w_spatial", "b_spatial", "g_q_b", "g_k_b", "g_mem", "g_mq", "g_mk")}
    big_out = {}

    def scatter_start(swap, route, after, tag):
        grads, got = _swap_end(swap, after, tag)
        sums = [_pair_sum(g, r) for g, r in zip(grads, got)]
        return _scatter_begin([p for p, _ in sums], [q for _, q in sums], route, after, tag)

    def scatter_finish(handle, names, l, after, tag):
        arrived = _scatter_end(handle, after, tag)[len(names):]
        for name, parts in zip(names, arrived):
            key, idx = stacked_key(name, l)
            big_out[key] = _adamw(parts, given[key], given["m_" + key], given["v_" + key], idx, big_out.get(key))

    pend_mix = None
    dx, dx_b = dy, dy_b
    for l in reversed(range(depth)):
        sv = saved[l]
        is_a = l % 2 == 0
        g_in, w_kv, w_o, g_gu, w_dn = sv["w"]
        gm_row, gf_row, gmem_row, gmq_row, gmk_row = sv["rows"]
        mix = sv["mix"]
        dw_dn = _matmul(sv["act"], dx_b, ta=True, out_dtype=bf16)
        dgu = _ffn_down_bwd(dx_b, w_dn, sv["gate"], sv["up"])
        dw_gu = _mm_cols_wgrad(sv["h2"], dgu, g_gu.shape[2])
        swap, dgate = _swap_begin([dw_gu, dw_dn.reshape(N_DEV, -1, d)], dgu[0], dx_b, f"ffn{l}")
        dh2 = _mm_cols_dgrad((dgate, dgu[1]), g_gu)
        dx1, dx1_b, dgf = _rms_bwd(sv["x1"], gf_row, dh2, dx)
        small["g_ffn"][l] = dgf.reshape(d)
        pend_ffn, dx1_b = scatter_start(swap, dx1_b, dx1, f"ffn{l}")
        if pend_mix is not None:
            scatter_finish(*pend_mix, dx1_b, f"mix{l + 1}")
        dw_o = _matmul(sv["cat"], dx1_b, ta=True, out_dtype=bf16)
        dcat = _matmul(dx1_b, w_o, tb=True, out_dtype=bf16)
        if is_a:
            dz, dws, dbt, dgv = _gmlp_bwd(sv["z"], mix["g_v"], mix["w_s"], mix["b_t"], dcat)
            small["w_spatial"][l // 2], small["b_spatial"][l // 2], small["g_v_a"][l // 2] = dws, dbt.T, dgv.reshape(t)
        else:
            dq, dk, dv = _flash_bwd(mix["q"], mix["k"], mix["v"], sv["cat"], dcat, mix["lse"])
            dz, dgq, dgk = _attn_prep_bwd(sv["z"], mix["g_q"], mix["g_k"], tables, dq, dk, dv, t)
            small["g_q_b"][l // 2], small["g_k_b"][l // 2] = dgq.reshape(HEAD), dgk.reshape(HEAD)
        dz, dkn, dvm, dgmq = _mem_bwd(sv["z"], sv["qblk"], sv["kv"], gmq_row, gmk_row, dcat, dz)
        dkv, dgmk = _mem_kv_bwd(sv["kv"], gmk_row, dkn, dvm)
        dw_kv = _matmul(sv["hm"], dkv, ta=True, out_dtype=bf16)
        dhm = _matmul(dkv, w_kv, tb=True)
        small["g_mem"][l] = _rms_bwd(mem0, gmem_row, dhm, None).reshape(d)
        small["g_mq"][l] = dgmq.reshape(HEAD)
        small["g_mk"][l] = dgmk.reshape(HEAD)
        dw_in = _mm_cols_wgrad(sv["h"], dz, g_in.shape[2])
        swap, dz = _swap_begin([dw_in] + [dw.reshape(N_DEV, -1, dw.shape[1]) for dw in (dw_kv, dw_o)], dz, dx1, f"mix{l}")
        dh = _mm_cols_dgrad(dz, g_in)
        dx, dx_b, dgm = _rms_bwd(sv["x"], gm_row, dh, dx1)
        small["g_mix"][l] = dgm.reshape(d)

        handle, dx_b = scatter_start(swap, dx_b, dx, f"mix{l}")
        pend_mix = (handle, big_names[:n_mix], l)
        scatter_finish(pend_ffn, big_names[n_mix:], l, dx_b, f"ffn{l}")
    small_names = tuple(small)
    small_grads = [jnp.stack(small[n]) for n in small_names]
    small_shapes = [g.shape for g in small_grads]
    (all_parts,) = _all_gather([_to_flat(small_grads)])
    souts = _adamw(all_parts, *[_to_flat([given[p + n] for n in small_names])[None] for p in ("", "m_", "v_")], 0, None)
    small_out = dict(zip(small_names, zip(*[_from_flat(flat, small_shapes) for flat in souts])))
    scatter_finish(*pend_mix, souts[0], "mix0")

    weights = ("g_mix", "g_ffn", "w_in_a", "g_v_a", "w_spatial", "b_spatial", "w_in_b", "g_q_b", "g_k_b", "g_mem", "w_mem_kv",
               "g_mq", "g_mk", "w_out", "w_gate_up", "w_down")
    results = {n: (small_out[n] if n in small_out else big_out[n]) for n in weights}
    grad_x = dx.reshape(1, s, d)
    return (loss, grad_x, *[results[n][kind] for kind in range(4) for n in weights])
```

```python
import functools

import jax
import jax.numpy as jnp
from jax import lax
from jax.experimental import pallas as pl
from jax.experimental.pallas import tpu as pltpu

f32 = jnp.float32
bf16 = jnp.bfloat16

HEAD = 128
CHUNK = 128
GRID_W = 64
MEM_HEADS = 4
KV_HEADS = 4
MEM_WIDTH = MEM_HEADS * HEAD
KV_WIDTH = KV_HEADS * HEAD
ROPE_THETA = 10000.0
ROPE_PAIRS = HEAD // 4
EPS = 1e-6
SCALE = HEAD ** -0.5
LOG2E = 1.4426950408889634
N_DEV = 8
LANES = 1024
VMEM_LIMIT = 56 * 1024 * 1024

ADAM_LR, ADAM_B1, ADAM_B2, ADAM_EPS, ADAM_WD, ADAM_STEP = 0.001, 0.9, 0.999, 1e-08, 0.01, 10

MESH = pl.DeviceIdType.MESH
_pallas_call = pl.pallas_call


def _pick(dim, cands):
    for c in cands:
        if dim % c == 0:
            return c
    return dim


def _cp(sem):
    return pltpu.CompilerParams(dimension_semantics=sem, vmem_limit_bytes=VMEM_LIMIT)


def _sds(shape, dtype):
    return jax.ShapeDtypeStruct(shape, dtype)


def _dot(a, b, ca, cb):
    return lax.dot_general(a, b, (((ca,), (cb,)), ((), ())), preferred_element_type=f32)


def _gelu(z):
    return 0.5 * z * (1.0 + lax.erf(z * 0.7071067811865476))


def _gelu_grad(z):
    return 0.5 * (1.0 + lax.erf(z * 0.7071067811865476)) + z * jnp.exp(-0.5 * z * z) * 0.3989422804014327


def _rot(x, sin_a, sin_b):
    return pltpu.roll(x, 96, 1) * sin_a + pltpu.roll(x, 32, 1) * sin_b


def _matmul(a, b, *, ta=False, tb=False, out_dtype=f32, res=None, tm=None, tn=None, tk=None):
    assert a.dtype == bf16 and b.dtype == bf16
    kdim, m = a.shape if ta else a.shape[::-1]
    n, k2 = b.shape if tb else b.shape[::-1]
    assert kdim == k2, (a.shape, b.shape, ta, tb)
    tm = tm or _pick(m, (1024, 1408, 512, 256, 128))
    tn = tn or _pick(n, (1024, 1408, 512, 256, 128))
    if tk is None:
        tk = kdim if kdim <= 2048 else _pick(kdim, (2816, 1024, 512, 256, 128))
    nk = kdim // tk
    ca, cb = (0 if ta else 1), (1 if tb else 0)
    has_res = res is not None

    def body(*refs):
        a_ref, b_ref = refs[0], refs[1]
        r_ref = refs[2] if has_res else None
        o_ref = refs[3] if has_res else refs[2]
        prod = _dot(a_ref[...], b_ref[...], ca, cb)
        if nk == 1:
            if has_res:
                prod = prod + r_ref[...]
            o_ref[...] = prod.astype(o_ref.dtype)
        else:
            acc = refs[-1]
            k = pl.program_id(2)

            @pl.when(k == 0)
            def _():
                acc[...] = prod

            @pl.when(k > 0)
            def _():
                acc[...] += prod

            @pl.when(k == nk - 1)
            def _():
                out = acc[...]
                if has_res:
                    out = out + r_ref[...]
                o_ref[...] = out.astype(o_ref.dtype)

    a_spec = pl.BlockSpec((tk, tm), lambda i, j, k: (k, i)) if ta else pl.BlockSpec((tm, tk), lambda i, j, k: (i, k))
    b_spec = pl.BlockSpec((tn, tk), lambda i, j, k: (j, k)) if tb else pl.BlockSpec((tk, tn), lambda i, j, k: (k, j))
    o_spec = pl.BlockSpec((tm, tn), lambda i, j, k: (i, j))
    in_specs = [a_spec, b_spec] + ([o_spec] if has_res else [])
    args = (a, b) + ((res,) if has_res else ())
    mode = ("t" if ta else "n") + ("t" if tb else "n")
    return _pallas_call(
        body, out_shape=_sds((m, n), out_dtype), grid=(m // tm, n // tn, nk),
        in_specs=in_specs, out_specs=o_spec,
        scratch_shapes=([pltpu.VMEM((tm, tn), f32)] if nk > 1 else []),
        compiler_params=_cp(("parallel", "parallel", "arbitrary")),
        name=f"mm_{mode}_{m}x{kdim}x{n}{'_res' if has_res else ''}_{jnp.dtype(out_dtype).name}",
    )(*args)


def _shards_per_step(n, pair_bytes=0):
    p = 2 if (n % 128 != 0 or 0 < 2 * pair_bytes <= VMEM_LIMIT // 2) else 1
    assert (p * n) % 128 == 0 and N_DEV % p == 0
    return p


def _lane_pieces(v, p, n):
    return [v] if p == 1 else [v[:, q * n:(q + 1) * n] for q in range(p)]


def _mm_cols_fwd(a, g, out_dtype):
    m, kdim = a.shape
    nd, k2, n = g.shape
    assert kdim == k2 and a.dtype == bf16 and g.dtype == bf16
    p = _shards_per_step(n)
    tm = _pick(m, (1024, 512, 256, 128))

    def body(a_ref, g_ref, o_ref):
        av = a_ref[...]
        parts = [_dot(av, g_ref[q], 1, 0) for q in range(p)]
        out = parts[0] if p == 1 else jnp.concatenate(parts, axis=1)
        o_ref[...] = out.astype(o_ref.dtype)

    return _pallas_call(
        body, out_shape=_sds((m, nd * n), out_dtype), grid=(m // tm, nd // p),
        in_specs=[pl.BlockSpec((tm, kdim), lambda i, j: (i, 0)), pl.BlockSpec((p, kdim, n), lambda i, j: (j, 0, 0))],
        out_specs=pl.BlockSpec((tm, p * n), lambda i, j: (i, j)),
        compiler_params=_cp(("parallel", "arbitrary")), name=f"mm_cols_fwd_{m}x{kdim}x{nd * n}_{jnp.dtype(out_dtype).name}",
    )(a, g)


def _pick_part(refs, nparts, step, per):
    val = refs[0][...]
    for hh in range(1, nparts):
        val = jnp.where(step >= hh * per, refs[hh][...], val)
    return val


def _part_step(step, hh, per):
    return jnp.clip(step - hh * per, 0, per - 1)


def _mm_cols_dgrad(dz, g):
    nd, kdim, n = g.shape
    p = _shards_per_step(n, pair_bytes=2 * kdim * n * 2)
    nj = nd // p
    parts = dz if isinstance(dz, tuple) else (dz,)
    m = parts[0].shape[0]
    nn = sum(part.shape[1] for part in parts)
    assert nn == nd * n and all(part.dtype == bf16 for part in parts) and g.dtype == bf16
    tm = _pick(m, (512, 256, 128))
    nparts = len(parts)
    per = nj // nparts

    def body(*refs):
        g_ref, o_ref, acc = refs[nparts:]
        j = pl.program_id(1)
        tot = None
        for q, piece in enumerate(_lane_pieces(_pick_part(refs, nparts, j, per), p, n)):
            dd = _dot(piece, g_ref[q], 1, 1)
            tot = dd if tot is None else tot + dd

        @pl.when(j == 0)
        def _():
            acc[...] = tot

        @pl.when(j > 0)
        def _():
            acc[...] += tot

        @pl.when(j == nj - 1)
        def _():
            o_ref[...] = acc[...]

    return _pallas_call(
        body, out_shape=_sds((m, kdim), f32), grid=(m // tm, nj),
        in_specs=[pl.BlockSpec((tm, p * n), (lambda hh: lambda i, j: (i, _part_step(j, hh, per)))(hh)) for hh in range(nparts)]
        + [pl.BlockSpec((p, kdim, n), lambda i, j: (j, 0, 0))],
        out_specs=pl.BlockSpec((tm, kdim), lambda i, j: (i, 0)),
        scratch_shapes=[pltpu.VMEM((tm, kdim), f32)],
        compiler_params=_cp(("parallel", "arbitrary")), name=f"mm_cols_dgrad_{m}x{nn}x{kdim}_{nparts}",
    )(*parts, g)


def _mm_cols_wgrad(a, dz, n):
    s, kdim = a.shape
    parts = dz if isinstance(dz, tuple) else (dz,)
    nparts = len(parts)
    nd = sum(part.shape[1] for part in parts) // n
    assert a.dtype == bf16 and all(part.dtype == bf16 for part in parts)
    p = _shards_per_step(n)
    per = nd // p // nparts
    tkw = _pick(kdim, (1024, 512, 256, 128))
    ts = _pick(s, (2048, 1024, 512, 256, 128))
    ns = s // ts

    def body(a_ref, *refs):
        o_ref, acc = refs[nparts:]
        si = pl.program_id(2)
        av = a_ref[...]
        prods = [_dot(av, piece, 0, 0) for piece in _lane_pieces(_pick_part(refs, nparts, pl.program_id(1), per), p, n)]

        @pl.when(si == 0)
        def _():
            for q in range(p):
                acc[q] = prods[q]

        @pl.when(si > 0)
        def _():
            for q in range(p):
                acc[q] += prods[q]

        @pl.when(si == ns - 1)
        def _():
            o_ref[...] = acc[...].astype(bf16)

    return _pallas_call(
        body, out_shape=_sds((nd, kdim, n), bf16), grid=(kdim // tkw, nd // p, ns),
        in_specs=[pl.BlockSpec((ts, tkw), lambda i, j, k: (k, i))]
        + [pl.BlockSpec((ts, p * n), (lambda hh: lambda i, j, k: (k, _part_step(j, hh, per)))(hh)) for hh in range(nparts)],
        out_specs=pl.BlockSpec((p, tkw, n), lambda i, j, k: (j, i, 0)),
        scratch_shapes=[pltpu.VMEM((p, tkw, n), f32)],
        compiler_params=_cp(("parallel", "parallel", "arbitrary")), name=f"mm_cols_wgrad_{kdim}x{s}x{nd * n}_{nparts}",
    )(a, *parts)


def _rms_fwd(x, g_row):
    s, d = x.shape
    tr = _pick(s, (512, 256, 128))

    def body(x_ref, g_ref, o_ref):
        xv = x_ref[...]
        r = lax.rsqrt(jnp.mean(xv * xv, axis=-1, keepdims=True) + EPS)
        o_ref[...] = (xv * r * g_ref[...]).astype(bf16)

    return _pallas_call(
        body, out_shape=_sds((s, d), bf16), grid=(s // tr,),
        in_specs=[pl.BlockSpec((tr, d), lambda i: (i, 0)), pl.BlockSpec((1, d), lambda i: (0, 0))],
        out_specs=pl.BlockSpec((tr, d), lambda i: (i, 0)),
        compiler_params=_cp(("parallel",)), name=f"rms_fwd_{s}x{d}",
    )(x, g_row)


def _rms_bwd(x, g_row, dh, dres):
    s, d = x.shape
    tr = _pick(s, (512, 256, 128))
    with_dx = dres is not None

    def body(*refs):
        if with_dx:
            x_ref, g_ref, dh_ref, dres_ref, dx_ref, dxb_ref, dg_ref = refs
        else:
            x_ref, g_ref, dh_ref, dg_ref = refs

        @pl.when(pl.program_id(0) == 0)
        def _():
            dg_ref[...] = jnp.zeros_like(dg_ref)

        xv = x_ref[...]
        r = lax.rsqrt(jnp.mean(xv * xv, axis=-1, keepdims=True) + EPS)
        xh = xv * r
        dy = dh_ref[...].astype(f32)
        dg_ref[...] += jnp.sum(dy * xh, axis=0, keepdims=True)
        if with_dx:
            gy = dy * g_ref[...]
            dx = dres_ref[...] + r * (gy - xh * jnp.mean(gy * xh, axis=-1, keepdims=True))
            dx_ref[...] = dx
            dxb_ref[...] = dx.astype(bf16)

    row = pl.BlockSpec((tr, d), lambda i: (i, 0))
    vec = pl.BlockSpec((1, d), lambda i: (0, 0))
    if with_dx:
        return _pallas_call(
            body, out_shape=(_sds((s, d), f32), _sds((s, d), bf16), _sds((1, d), f32)), grid=(s // tr,),
            in_specs=[row, vec, row, row], out_specs=(row, row, vec),
            compiler_params=_cp(("arbitrary",)), name=f"rms_bwd_{s}x{d}",
        )(x, g_row, dh, dres)
    return _pallas_call(
        body, out_shape=_sds((1, d), f32), grid=(s // tr,),
        in_specs=[row, vec, row], out_specs=vec,
        compiler_params=_cp(("arbitrary",)), name=f"rms_bwd_gain_{s}x{d}",
    )(x, g_row, dh)


def _gmlp_rows(s):
    return CHUNK * (2 if (s // CHUNK) % 2 == 0 else 1)


def _gmlp_fwd(z, g_v, w_s, b_t):
    s = z.shape[0]
    t = g_v.shape[1]
    ng = t // HEAD
    rb = _gmlp_rows(s)

    def body(z_ref, gv_ref, ws_ref, bt_ref, o_ref):
        for ci in range(rb // CHUNK):
            lo = ci * CHUNK
            a = _gelu(z_ref[lo:lo + CHUNK, :])
            u, vv = a[:, :t], a[:, t:]
            r = lax.rsqrt(jnp.mean(vv * vv, axis=-1, keepdims=True) + EPS)
            vn = (vv * r * gv_ref[...]).astype(bf16)
            for g in range(ng):
                cs = slice(g * HEAD, (g + 1) * HEAD)
                sg = _dot(ws_ref[g].astype(bf16), vn[:, cs], 1, 0) + bt_ref[:, g:g + 1]
                o_ref[lo:lo + CHUNK, cs] = (u[:, cs] * sg).astype(bf16)

    return _pallas_call(
        body, out_shape=_sds((s, t + MEM_WIDTH), bf16), grid=(s // rb,),
        in_specs=[pl.BlockSpec((rb, 2 * t), lambda i: (i, 0)), pl.BlockSpec((1, t), lambda i: (0, 0)),
                  pl.BlockSpec((ng, CHUNK, CHUNK), lambda i: (0, 0, 0)), pl.BlockSpec((CHUNK, ng), lambda i: (0, 0))],
        out_specs=pl.BlockSpec((rb, t), lambda i: (i, 0)),
        compiler_params=_cp(("parallel",)), name=f"gmlp_fwd_{s}",
    )(z, g_v, w_s, b_t)


def _gmlp_bwd(z, g_v, w_s, b_t, dtok):
    s = z.shape[0]
    t = g_v.shape[1]
    ng = t // HEAD
    rb = _gmlp_rows(s)
    nsteps = s // rb

    def body(z_ref, gv_ref, ws_ref, bt_ref, dt_ref, dz_ref, dws_ref, dbt_ref, dgv_ref, ds_acc):
        step = pl.program_id(0)

        @pl.when(step == 0)
        def _():
            dws_ref[...] = jnp.zeros_like(dws_ref)
            dgv_ref[...] = jnp.zeros_like(dgv_ref)
            ds_acc[...] = jnp.zeros_like(ds_acc)

        for ci in range(rb // CHUNK):
            lo = ci * CHUNK
            zz = z_ref[lo:lo + CHUNK, :]
            a = _gelu(zz)
            u, vv = a[:, :t], a[:, t:]
            r = lax.rsqrt(jnp.mean(vv * vv, axis=-1, keepdims=True) + EPS)
            vh = vv * r
            vn = (vh * gv_ref[...]).astype(bf16)
            dtok = dt_ref[lo:lo + CHUNK, :].astype(f32)
            ds = dtok * u
            ds_acc[...] += ds
            dsb = ds.astype(bf16)
            du_parts, dvn_parts = [], []
            for g in range(ng):
                cs = slice(g * HEAD, (g + 1) * HEAD)
                wg = ws_ref[g].astype(bf16)
                sg = _dot(wg, vn[:, cs], 1, 0) + bt_ref[:, g:g + 1]
                du_parts.append(dtok[:, cs] * sg)
                dws_ref[g] += _dot(dsb[:, cs], vn[:, cs], 1, 1)
                dvn_parts.append(_dot(wg, dsb[:, cs], 0, 0))
            dvn = jnp.concatenate(dvn_parts, axis=1)
            dgv_ref[...] += jnp.sum(dvn * vh, axis=0, keepdims=True)
            gy = dvn * gv_ref[...]
            dvv = r * (gy - vh * jnp.mean(gy * vh, axis=-1, keepdims=True))
            da = jnp.concatenate(du_parts + [dvv], axis=1)
            dz_ref[lo:lo + CHUNK, :] = (da * _gelu_grad(zz)).astype(bf16)

        @pl.when(step == nsteps - 1)
        def _():
            for g in range(ng):
                dbt_ref[:, g:g + 1] = jnp.sum(ds_acc[:, g * HEAD:(g + 1) * HEAD], axis=1, keepdims=True)

    return _pallas_call(
        body,
        out_shape=(_sds((s, z.shape[1]), bf16), _sds((ng, CHUNK, CHUNK), f32), _sds((CHUNK, ng), f32), _sds((1, t), f32)),
        grid=(nsteps,),
        in_specs=[pl.BlockSpec((rb, 2 * t), lambda i: (i, 0)), pl.BlockSpec((1, t), lambda i: (0, 0)),
                  pl.BlockSpec((ng, CHUNK, CHUNK), lambda i: (0, 0, 0)), pl.BlockSpec((CHUNK, ng), lambda i: (0, 0)),
                  pl.BlockSpec((rb, t), lambda i: (i, 0))],
        out_specs=(pl.BlockSpec((rb, 2 * t), lambda i: (i, 0)), pl.BlockSpec((ng, CHUNK, CHUNK), lambda i: (0, 0, 0)),
                   pl.BlockSpec((CHUNK, ng), lambda i: (0, 0)), pl.BlockSpec((1, t), lambda i: (0, 0))),
        scratch_shapes=[pltpu.VMEM((CHUNK, t), f32)],
        compiler_params=_cp(("arbitrary",)), name=f"gmlp_bwd_{s}",
    )(z, g_v, w_s, b_t, dtok)


def _rope_tables(s):
    n_rows = s // GRID_W
    rows = jnp.broadcast_to(jnp.arange(n_rows)[:, None], (n_rows, GRID_W)).reshape(s)
    cols = jnp.broadcast_to(jnp.arange(GRID_W)[None, :], (n_rows, GRID_W)).reshape(s)
    freqs = ROPE_THETA ** (-jnp.arange(ROPE_PAIRS, dtype=f32) / ROPE_PAIRS)
    ang_r = rows.astype(f32)[:, None] * freqs
    ang_c = cols.astype(f32)[:, None] * freqs
    ang = jnp.concatenate([ang_r, ang_r, ang_c, ang_c], axis=-1)
    cos, sin = jnp.cos(ang), jnp.sin(ang)
    first = (jnp.arange(HEAD) % (2 * ROPE_PAIRS)) < ROPE_PAIRS
    return cos, jnp.where(first, -sin, 0.0), jnp.where(first, 0.0, sin)


def _attn_prep_fwd(z, g_q, g_k, tables, t):
    s = z.shape[0]
    tr = _pick(s, (256, 128))
    nq = t // HEAD
    width = t + 2 * KV_WIDTH

    def body(z_ref, gq_ref, gk_ref, cos_ref, sa_ref, sb_ref, q_ref, k_ref, v_ref):
        cos, sa, sb = cos_ref[...], sa_ref[...], sb_ref[...]
        for h in range(nq + KV_HEADS):
            cs = slice(h * HEAD, (h + 1) * HEAD)
            xv = z_ref[:, cs]
            r = lax.rsqrt(jnp.mean(xv * xv, axis=-1, keepdims=True) + EPS)
            xn = xv * r * (gq_ref[...] if h < nq else gk_ref[...])
            y = xn * cos + _rot(xn, sa, sb)
            if h < nq:
                q_ref[:, cs] = (y * (SCALE * LOG2E)).astype(bf16)
            else:
                k_ref[:, (h - nq) * HEAD:(h - nq + 1) * HEAD] = y.astype(bf16)
        v_ref[...] = z_ref[:, t + KV_WIDTH:width].astype(bf16)

    row = lambda w: pl.BlockSpec((tr, w), lambda i: (i, 0))
    vec = pl.BlockSpec((1, HEAD), lambda i: (0, 0))
    return _pallas_call(
        body, out_shape=(_sds((s, t), bf16), _sds((s, KV_WIDTH), bf16), _sds((s, KV_WIDTH), bf16)), grid=(s // tr,),
        in_specs=[row(width), vec, vec, row(HEAD), row(HEAD), row(HEAD)],
        out_specs=(row(t), row(KV_WIDTH), row(KV_WIDTH)),
        compiler_params=_cp(("parallel",)), name=f"attn_prep_fwd_{s}",
    )(z, g_q, g_k, *tables)


def _attn_prep_bwd(z, g_q, g_k, tables, dq, dk, dv, t):
    s = z.shape[0]
    tr = _pick(s, (256, 128))
    nq = t // HEAD
    width = t + 2 * KV_WIDTH

    def body(z_ref, gq_ref, gk_ref, cos_ref, sa_ref, sb_ref, dq_ref, dk_ref, dv_ref, dz_ref, dgq_ref, dgk_ref):
        @pl.when(pl.program_id(0) == 0)
        def _():
            dgq_ref[...] = jnp.zeros_like(dgq_ref)
            dgk_ref[...] = jnp.zeros_like(dgk_ref)

        cos, sa, sb = cos_ref[...], sa_ref[...], sb_ref[...]
        for h in range(nq + KV_HEADS):
            cs = slice(h * HEAD, (h + 1) * HEAD)
            xv = z_ref[:, cs]
            r = lax.rsqrt(jnp.mean(xv * xv, axis=-1, keepdims=True) + EPS)
            xh = xv * r
            if h < nq:
                dy, g_ref, dg_ref = dq_ref[:, cs], gq_ref, dgq_ref
            else:
                dy, g_ref, dg_ref = dk_ref[:, (h - nq) * HEAD:(h - nq + 1) * HEAD], gk_ref, dgk_ref
            dy = dy.astype(f32)
            dxn = dy * cos - _rot(dy, sa, sb)
            dg_ref[...] += jnp.sum(dxn * xh, axis=0, keepdims=True)
            gy = dxn * g_ref[...]
            dz_ref[:, cs] = (r * (gy - xh * jnp.mean(gy * xh, axis=-1, keepdims=True))).astype(bf16)
        dz_ref[:, t + KV_WIDTH:width] = dv_ref[...].astype(bf16)

    row = lambda w: pl.BlockSpec((tr, w), lambda i: (i, 0))
    vec = pl.BlockSpec((1, HEAD), lambda i: (0, 0))
    return _pallas_call(
        body, out_shape=(_sds((s, z.shape[1]), bf16), _sds((1, HEAD), f32), _sds((1, HEAD), f32)), grid=(s // tr,),
        in_specs=[row(width), vec, vec, row(HEAD), row(HEAD), row(HEAD), row(t), row(KV_WIDTH), row(KV_WIDTH)],
        out_specs=(row(width), vec, vec),
        compiler_params=_cp(("arbitrary",)), name=f"attn_prep_bwd_{s}",
    )(z, g_q, g_k, *tables, dq, dk, dv)


def _flash_tiles(s):
    return _pick(s, (512, 256, 128)), _pick(s, (1024, 512, 256, 128))


def _flash_fwd(q, k, v):
    s, t = q.shape
    grp = t // KV_WIDTH
    tq, tk = _flash_tiles(s)
    nkv = s // tk
    rows = grp * tq

    def body(q_ref, k_ref, v_ref, o_ref, lse_ref, m_sc, acc_sc):
        ki = pl.program_id(2)

        @pl.when(ki == 0)
        def _():
            m_sc[...] = jnp.full(m_sc.shape, -jnp.inf, f32)
            acc_sc[...] = jnp.zeros_like(acc_sc)

        kk = k_ref[...]
        v1 = jnp.concatenate([v_ref[...], jnp.ones((tk, HEAD), bf16)], axis=1)
        sc = _dot(q_ref[:, 0:HEAD], kk, 1, 1)
        for g in range(grp):
            sc_next = _dot(q_ref[:, (g + 1) * HEAD:(g + 2) * HEAD], kk, 1, 1) if g + 1 < grp else None
            mine = slice(g * tq, (g + 1) * tq)
            m_prev = m_sc[mine, :]
            m_new = jnp.maximum(m_prev, jnp.max(sc, axis=-1, keepdims=True))
            alpha = jnp.exp2(m_prev - m_new)
            p = jnp.exp2((sc - m_new).astype(bf16))
            acc_sc[mine, :] = alpha * acc_sc[mine, :] + _dot(p, v1, 1, 0)
            m_sc[mine, :] = m_new
            sc = sc_next

        @pl.when(ki == nkv - 1)
        def _():
            acc = acc_sc[...]
            l = acc[:, HEAD:HEAD + 1]
            o = acc[:, :HEAD] / l
            for g in range(grp):
                o_ref[:, g * HEAD:(g + 1) * HEAD] = o[g * tq:(g + 1) * tq].astype(bf16)
            lse_ref[0] = jnp.broadcast_to(m_sc[...] + jnp.log(l) * LOG2E, (rows, HEAD))

    return _pallas_call(
        body, out_shape=(_sds((s, t + MEM_WIDTH), bf16), _sds((KV_HEADS, grp * s, HEAD), f32)), grid=(KV_HEADS, s // tq, nkv),
        in_specs=[pl.BlockSpec((tq, grp * HEAD), lambda h, i, j: (i, h)), pl.BlockSpec((tk, HEAD), lambda h, i, j: (j, h)),
                  pl.BlockSpec((tk, HEAD), lambda h, i, j: (j, h))],
        out_specs=(pl.BlockSpec((tq, grp * HEAD), lambda h, i, j: (i, h)), pl.BlockSpec((1, rows, HEAD), lambda h, i, j: (h, i, 0))),
        scratch_shapes=[pltpu.VMEM((rows, 1), f32), pltpu.VMEM((rows, 2 * HEAD), f32)],
        compiler_params=_cp(("parallel", "parallel", "arbitrary")), name=f"flash_fwd_{s}",
    )(q, k, v)


def _flash_delta(o, do, t):
    s = o.shape[0]
    grp = t // KV_WIDTH
    tq, _ = _flash_tiles(s)
    rows = grp * tq

    def body(o_ref, do_ref, d_ref):
        for g in range(grp):
            cs = slice(g * HEAD, (g + 1) * HEAD)
            dd = jnp.sum(o_ref[:, cs].astype(f32) * do_ref[:, cs].astype(f32), axis=-1, keepdims=True)
            d_ref[0, g * tq:(g + 1) * tq, :] = jnp.broadcast_to(dd, (tq, HEAD))

    qb = pl.BlockSpec((tq, grp * HEAD), lambda h, i: (i, h))
    return _pallas_call(
        body, out_shape=_sds((KV_HEADS, grp * s, HEAD), f32), grid=(KV_HEADS, s // tq),
        in_specs=[qb, qb], out_specs=pl.BlockSpec((1, rows, HEAD), lambda h, i: (h, i, 0)),
        compiler_params=_cp(("parallel", "parallel")), name=f"flash_delta_{s}",
    )(o, do)


def _flash_bwd(q, k, v, o, do, lse):
    s, t = q.shape
    grp = t // KV_WIDTH
    tq, tk = _flash_tiles(s)
    nq, nkv = s // tq, s // tk
    rows = grp * tq
    delta = _flash_delta(o, do, t)

    def body(q_ref, k_ref, v_ref, do_ref, lse_ref, delta_ref, dq_ref, dk_ref, dv_ref, dq_acc, dk_acc, dv_acc):
        kj, qi = pl.program_id(1), pl.program_id(2)

        @pl.when(qi == 0)
        def _():
            dk_acc[...] = jnp.zeros_like(dk_acc)
            dv_acc[...] = jnp.zeros_like(dv_acc)

        kk, vv = k_ref[...], v_ref[...]

        def products(g):
            qg, dog = q_ref[:, g * HEAD:(g + 1) * HEAD], do_ref[:, g * HEAD:(g + 1) * HEAD]
            return qg, dog, _dot(qg, kk, 1, 1), _dot(dog, vv, 1, 1)

        ahead = products(0)
        dv_sum = dk_sum = None
        terms = []
        for g in range(grp):
            qg, dog, sc, dp = ahead
            if g + 1 < grp:
                ahead = products(g + 1)
            head_rows = slice(g * tq, (g + 1) * tq)
            p = jnp.exp2((sc - lse_ref[0, head_rows, 0:1]).astype(bf16))
            ds = p * (dp - delta_ref[0, head_rows, 0:1]).astype(bf16)
            dv_g, dk_g = _dot(p, dog, 0, 0), _dot(ds, qg, 0, 0)
            dv_sum = dv_g if dv_sum is None else dv_sum + dv_g
            dk_sum = dk_g if dk_sum is None else dk_sum + dk_g
            terms.append(_dot(ds, kk, 1, 0))
        dv_acc[...] += dv_sum
        dk_acc[...] += dk_sum
        mine = pl.ds(pl.multiple_of(qi * rows, rows), rows)
        term = jnp.concatenate(terms, axis=0)

        @pl.when(kj == 0)
        def _():
            dq_acc[mine, :] = term

        @pl.when(kj > 0)
        def _():
            dq_acc[mine, :] += term

        @pl.when(kj == nkv - 1)
        def _():
            total = dq_acc[mine, :]
            for g in range(grp):
                dq_ref[:, g * HEAD:(g + 1) * HEAD] = total[g * tq:(g + 1) * tq] * SCALE

        @pl.when(qi == nq - 1)
        def _():
            dk_ref[...] = dk_acc[...] * (1.0 / LOG2E)
            dv_ref[...] = dv_acc[...]

    qb = pl.BlockSpec((tq, grp * HEAD), lambda h, j, i: (i, h))
    kb = pl.BlockSpec((tk, HEAD), lambda h, j, i: (j, h))
    lb = pl.BlockSpec((1, rows, HEAD), lambda h, j, i: (h, i, 0))
    dqb = pl.BlockSpec((tq, grp * HEAD), lambda h, j, i: (jnp.where(j == nkv - 1, i, 0), h))
    return _pallas_call(
        body, out_shape=(_sds((s, t), f32), _sds((s, KV_WIDTH), f32), _sds((s, KV_WIDTH), f32)), grid=(KV_HEADS, nkv, nq),
        in_specs=[qb, kb, kb, qb, lb, lb], out_specs=(dqb, kb, kb),
        scratch_shapes=[pltpu.VMEM((nq * rows, HEAD), f32), pltpu.VMEM((tk, HEAD), f32), pltpu.VMEM((tk, HEAD), f32)],
        compiler_params=_cp(("parallel", "arbitrary", "arbitrary")), name=f"flash_bwd_{s}",
    )(q, k, v, do, lse, delta)


def _mem_heads(z_ref, kv_ref, gq_ref, gk_ref, h):
    cs = slice(h * HEAD, (h + 1) * HEAD)
    xv = z_ref[:, cs]
    r = lax.rsqrt(jnp.mean(xv * xv, axis=-1, keepdims=True) + EPS)
    xh = xv * r
    kx = kv_ref[:, cs]
    rk = lax.rsqrt(jnp.mean(kx * kx, axis=-1, keepdims=True) + EPS)
    kn = (kx * rk * gk_ref[...]).astype(bf16)
    vv = kv_ref[:, MEM_WIDTH + h * HEAD:MEM_WIDTH + (h + 1) * HEAD].astype(bf16)
    qn = (xh * gq_ref[...]).astype(bf16)
    sc = _dot(qn, kn, 1, 1) * SCALE
    e = jnp.exp(sc - jnp.max(sc, axis=-1, keepdims=True))
    p = e / jnp.sum(e, axis=-1, keepdims=True)
    return cs, r, xh, qn, kn, vv, p


def _mem_fwd(z, qblk, kv, g_mq, g_mk, cat):
    s = z.shape[0]
    nm = kv.shape[0]
    tr = _pick(s, (512, 256, 128))
    oblk = cat.shape[1] // MEM_WIDTH - 1

    def body(z_ref, kv_ref, gq_ref, gk_ref, cat_ref, o_ref):
        for h in range(MEM_HEADS):
            cs, _, _, _, _, vv, p = _mem_heads(z_ref, kv_ref, gq_ref, gk_ref, h)
            o_ref[:, cs] = _dot(p.astype(bf16), vv, 1, 0).astype(bf16)

    vec = pl.BlockSpec((1, HEAD), lambda i: (0, 0))
    return _pallas_call(
        body, out_shape=_sds(cat.shape, bf16), grid=(s // tr,),
        in_specs=[pl.BlockSpec((tr, MEM_WIDTH), lambda i: (i, qblk)), pl.BlockSpec((nm, 2 * MEM_WIDTH), lambda i: (0, 0)), vec, vec,
                  pl.BlockSpec(memory_space=pl.ANY)],
        out_specs=pl.BlockSpec((tr, MEM_WIDTH), lambda i: (i, oblk)),
        input_output_aliases={4: 0},
        compiler_params=_cp(("parallel",)), name=f"mem_fwd_{s}_{qblk}",
    )(z, kv, g_mq, g_mk, cat)


def _mem_bwd(z, qblk, kv, g_mq, g_mk, dcat, dz):
    s = z.shape[0]
    nm = kv.shape[0]
    tr = _pick(s, (512, 256, 128))
    dblk = dcat.shape[1] // MEM_WIDTH - 1

    def body(z_ref, kv_ref, gq_ref, gk_ref, dm_ref, dzin_ref, dz_ref, dkn_ref, dv_ref, dgq_ref):
        @pl.when(pl.program_id(0) == 0)
        def _():
            dkn_ref[...] = jnp.zeros_like(dkn_ref)
            dv_ref[...] = jnp.zeros_like(dv_ref)
            dgq_ref[...] = jnp.zeros_like(dgq_ref)

        for h in range(MEM_HEADS):
            cs, r, xh, qn, kn, vv, p = _mem_heads(z_ref, kv_ref, gq_ref, gk_ref, h)
            dm = dm_ref[:, cs]
            dv_ref[:, cs] += _dot(p.astype(bf16), dm, 0, 0)
            dp = _dot(dm, vv, 1, 1)
            ds = (p * (dp - jnp.sum(dp * p, axis=-1, keepdims=True)) * SCALE).astype(bf16)
            dqn = _dot(ds, kn, 1, 0)
            dkn_ref[:, cs] += _dot(ds, qn, 0, 0)
            dgq_ref[...] += jnp.sum(dqn * xh, axis=0, keepdims=True)
            gy = dqn * gq_ref[...]
            dz_ref[:, cs] = (r * (gy - xh * jnp.mean(gy * xh, axis=-1, keepdims=True))).astype(bf16)

    vec = pl.BlockSpec((1, HEAD), lambda i: (0, 0))
    acc = pl.BlockSpec((nm, MEM_WIDTH), lambda i: (0, 0))
    return _pallas_call(
        body, out_shape=(_sds(dz.shape, bf16), _sds((nm, MEM_WIDTH), f32), _sds((nm, MEM_WIDTH), f32), _sds((1, HEAD), f32)),
        grid=(s // tr,),
        in_specs=[pl.BlockSpec((tr, MEM_WIDTH), lambda i: (i, qblk)), pl.BlockSpec((nm, 2 * MEM_WIDTH), lambda i: (0, 0)), vec, vec,
                  pl.BlockSpec((tr, MEM_WIDTH), lambda i: (i, dblk)), pl.BlockSpec(memory_space=pl.ANY)],
        out_specs=(pl.BlockSpec((tr, MEM_WIDTH), lambda i: (i, qblk)), acc, acc, vec),
        input_output_aliases={5: 0},
        compiler_params=_cp(("arbitrary",)), name=f"mem_bwd_{s}_{qblk}",
    )(z, kv, g_mq, g_mk, dcat, dz)


def _mem_kv_bwd(kv, g_mk, dkn, dv):
    nm = kv.shape[0]

    def body(kv_ref, gk_ref, dkn_ref, dv_ref, dkv_ref, dgk_ref):
        dgk = jnp.zeros((1, HEAD), f32)
        for h in range(MEM_HEADS):
            cs = slice(h * HEAD, (h + 1) * HEAD)
            kx = kv_ref[:, cs]
            rk = lax.rsqrt(jnp.mean(kx * kx, axis=-1, keepdims=True) + EPS)
            kh = kx * rk
            dkn_h = dkn_ref[:, cs]
            dgk = dgk + jnp.sum(dkn_h * kh, axis=0, keepdims=True)
            gy = dkn_h * gk_ref[...]
            dkv_ref[:, cs] = (rk * (gy - kh * jnp.mean(gy * kh, axis=-1, keepdims=True))).astype(bf16)
        dkv_ref[:, MEM_WIDTH:] = dv_ref[...].astype(bf16)
        dgk_ref[...] = dgk

    return _pallas_call(
        body, out_shape=(_sds((nm, 2 * MEM_WIDTH), bf16), _sds((1, HEAD), f32)),
        compiler_params=pltpu.CompilerParams(vmem_limit_bytes=VMEM_LIMIT), name=f"mem_kv_bwd_{nm}",
    )(kv, g_mk, dkn, dv)


def _ffn_up(h2, g):
    m, kdim = h2.shape
    nd, _, n = g.shape
    half = nd // 2
    assert n % 128 == 0 and h2.dtype == bf16 and g.dtype == bf16
    tm = _pick(m, (512, 256, 128))

    def body(a_ref, wg_ref, wu_ref, gate_ref, up_ref, act_ref):
        av = a_ref[...]
        gt = _dot(av, wg_ref[0], 1, 0)
        up = _dot(av, wu_ref[0], 1, 0)
        gate_ref[...] = gt.astype(bf16)
        up_ref[...] = up.astype(bf16)
        act_ref[...] = (gt * jax.nn.sigmoid(gt) * up).astype(bf16)

    out = pl.BlockSpec((tm, n), lambda i, j: (i, j))
    return _pallas_call(
        body, out_shape=tuple(_sds((m, half * n), bf16) for _ in range(3)), grid=(m // tm, half),
        in_specs=[pl.BlockSpec((tm, kdim), lambda i, j: (i, 0)), pl.BlockSpec((1, kdim, n), lambda i, j: (j, 0, 0)),
                  pl.BlockSpec((1, kdim, n), lambda i, j: (j + half, 0, 0))],
        out_specs=(out, out, out),
        compiler_params=_cp(("parallel", "arbitrary")), name=f"ffn_up_{m}x{kdim}x{half * n}",
    )(h2, g, g)


def _ffn_down_bwd(dx, w_dn, gate, up):
    m, d = dx.shape
    ff = w_dn.shape[0]
    tm = _pick(m, (512, 256, 128))
    tf = _pick(ff, (1408, 1024, 512, 256, 128))

    nsub = 2 if tm % 32 == 0 else 1
    rs = tm // nsub

    def body(dx_ref, w_ref, g_ref, u_ref, dg_ref, du_ref):
        wv = w_ref[...]
        da = _dot(dx_ref[0:rs, :], wv, 1, 1)
        for r in range(nsub):
            da_next = _dot(dx_ref[(r + 1) * rs:(r + 2) * rs, :], wv, 1, 1) if r + 1 < nsub else None
            mine = slice(r * rs, (r + 1) * rs)
            gt = g_ref[mine, :].astype(f32)
            sg = jax.nn.sigmoid(gt)
            dg_ref[mine, :] = (da * u_ref[mine, :].astype(f32) * sg * (1.0 + gt * (1.0 - sg))).astype(bf16)
            du_ref[mine, :] = (da * gt * sg).astype(bf16)
            da = da_next

    tile = pl.BlockSpec((tm, tf), lambda i, j: (i, j))
    return _pallas_call(
        body, out_shape=(_sds((m, ff), bf16), _sds((m, ff), bf16)), grid=(m // tm, ff // tf),
        in_specs=[pl.BlockSpec((tm, d), lambda i, j: (i, 0)), pl.BlockSpec((tf, d), lambda i, j: (j, 0)), tile, tile],
        out_specs=(tile, tile),
        compiler_params=_cp(("parallel", "arbitrary")), name=f"ffn_down_bwd_{m}x{d}x{ff}",
    )(dx, w_dn, gate, up)


def _loss_head(y, target):
    s, d = y.shape
    tr = _pick(s, (512, 256, 128))

    def body(y_ref, t_ref, l_ref, dy_ref, dyb_ref):
        @pl.when(pl.program_id(0) == 0)
        def _():
            l_ref[...] = jnp.zeros_like(l_ref)

        err = y_ref[...] - t_ref[...]
        l_ref[...] += 0.5 * jnp.sum(jnp.mean(err * err, axis=-1, keepdims=True), axis=0, keepdims=True)
        dy = err * (1.0 / d)
        dy_ref[...] = dy
        dyb_ref[...] = dy.astype(bf16)

    row = pl.BlockSpec((tr, d), lambda i: (i, 0))
    return _pallas_call(
        body, out_shape=(_sds((1, HEAD), f32), _sds((s, d), f32), _sds((s, d), bf16)), grid=(s // tr,),
        in_specs=[row, row], out_specs=(pl.BlockSpec((1, HEAD), lambda i: (0, 0)), row, row),
        compiler_params=_cp(("arbitrary",)), name=f"loss_{s}x{d}",
    )(y, target)


def _place():
    return lax.axis_index("x"), lax.axis_index("y"), lax.axis_index("c")


def _tag(arrays):
    return "_".join("x".join(str(dd) for dd in a.shape) for a in arrays)


def _all_gather(shards):
    nw = len(shards)
    hbm = pl.BlockSpec(memory_space=pl.ANY)

    def body(*refs):
        x_refs, out_refs = refs[:nw], refs[nw:2 * nw]
        send_sems, recv_sems, local_sems = refs[2 * nw:]
        x, y, c = _place()
        me, sibling = (x, y, c), (x, y, 1 - c)
        chips = [(1 - x, y), (x, 1 - y), (1 - x, 1 - y)]

        def slot(w, place):
            px, py, pc = place
            return out_refs[w].at[4 * px + 2 * py + pc]

        def copy(k, w, block_of, to, from_input=False):
            return pltpu.make_async_remote_copy(
                src_ref=x_refs[w] if from_input else slot(w, block_of), dst_ref=slot(w, block_of),
                send_sem=send_sems.at[k, w], recv_sem=recv_sems.at[k, w], device_id=to, device_id_type=MESH)

        mine = [pltpu.make_async_copy(x_refs[w], slot(w, me), local_sems.at[w]) for w in range(nw)]
        for cp in mine:
            cp.start()
        first = []
        for w in range(nw):
            first.append(copy(0, w, me, sibling, from_input=True))
            first += [copy(1 + j, w, me, (*chip, c), from_input=True) for j, chip in enumerate(chips)]
        for cp in first:
            cp.start()
        passed = []
        for w in range(nw):
            for j, chip in enumerate(chips):
                copy(1 + j, w, (*chip, c), me).wait_recv()
                fwd = copy(4 + j, w, (*chip, c), sibling)
                fwd.start()
                passed.append(fwd)
        for w in range(nw):
            copy(0, w, sibling, me).wait_recv()
            for j, chip in enumerate(chips):
                copy(4 + j, w, (*chip, 1 - c), me).wait_recv()
        for cp in first + passed:
            cp.wait_send()
        for cp in mine:
            cp.wait()

    return _pallas_call(
        body, out_shape=tuple(_sds((N_DEV,) + a.shape, a.dtype) for a in shards), in_specs=[hbm] * nw, out_specs=tuple([hbm] * nw),
        scratch_shapes=[pltpu.SemaphoreType.DMA((7, nw)), pltpu.SemaphoreType.DMA((7, nw)), pltpu.SemaphoreType.DMA((nw,))],
        name=f"all_gather_{_tag(shards)}_{jnp.dtype(shards[0].dtype).name}",
    )(*shards)


def _pair_sum(grad, got):
    nd, a, b = grad.shape
    nchip = nd // 2
    ta = _pick(a, (1024, 704, 512, 352, 256, 128, 64, 32, 16))

    def my_chip():
        return 2 * lax.axis_index("x") + lax.axis_index("y")

    def body(a_ref, b_ref, o_ref, land_ref):
        tot = (a_ref[...].astype(f32) + b_ref[...].astype(f32)).astype(o_ref.dtype)
        o_ref[...] = tot

        @pl.when(pl.program_id(1) == my_chip())
        def _():
            land_ref[...] = tot

    return _pallas_call(
        body, out_shape=(_sds(got.shape, grad.dtype), _sds(got.shape, grad.dtype)), grid=(a // ta, nchip),
        in_specs=[pl.BlockSpec((None, ta, b), lambda i, k: (2 * k + lax.axis_index("c"), i, 0)),
                  pl.BlockSpec((None, ta, b), lambda i, k: (k, i, 0))],
        out_specs=(pl.BlockSpec((None, ta, b), lambda i, k: (k, i, 0)),
                   pl.BlockSpec((None, ta, b), lambda i, k: (my_chip(), i, 0))),
        compiler_params=_cp(("parallel", "arbitrary")), name=f"pair_sum_{a}x{b}",
    )(grad, got)


_HBM = pl.BlockSpec(memory_space=pltpu.HBM)
_SEM = pl.BlockSpec(memory_space=pltpu.SEMAPHORE)
_ANY = pl.BlockSpec(memory_space=pl.ANY)
_DATAFLOW = pltpu.SideEffectType.DATAFLOW_SIDE_EFFECTING


def _in_hbm(a):
    return pltpu.with_memory_space_constraint(a, pltpu.HBM)


def _exchange_begin(bufs, nw, route, after, copies_of, n_copies, name):
    nb = len(bufs)

    def body(*refs):
        send_sems, recv_sems = refs[nb + 2], refs[nb + 3]
        for w in range(nw):
            for k, (src, dst, to) in enumerate(copies_of(w, refs[:nb])):
                pltpu.make_async_remote_copy(src_ref=src, dst_ref=dst, send_sem=send_sems.at[k * nw + w],
                                             recv_sem=recv_sems.at[k * nw + w], device_id=to, device_id_type=MESH).start()

    out = _pallas_call(
        body, name=name,
        out_shape=(pltpu.SemaphoreType.DMA((n_copies * nw,)), pltpu.SemaphoreType.DMA((n_copies * nw,)),
                   *[pltpu.HBM(a.shape, a.dtype) for a in bufs], pltpu.HBM(route.shape, route.dtype)),
        in_specs=[_HBM] * (nb + 1) + [_ANY], out_specs=(_SEM, _SEM, *[_HBM] * (nb + 1)),
        input_output_aliases={i: 2 + i for i in range(nb + 1)},
        compiler_params=pltpu.CompilerParams(has_side_effects=_DATAFLOW),
    )(*[_in_hbm(a) for a in bufs], _in_hbm(route), after)
    return (out[0], out[1], out[2:2 + nb], nw), out[2 + nb]


def _exchange_end(handle, after, copies_of, n_copies, name):
    send_sems, recv_sems, thru, nw = handle
    nb = len(thru)

    def body(*refs):
        send_sems, recv_sems = refs[nb], refs[nb + 1]
        for w in range(nw):
            for k, (src, dst, to) in enumerate(copies_of(w, refs[:nb])):
                cp = pltpu.make_async_remote_copy(src_ref=src, dst_ref=dst, send_sem=send_sems.at[k * nw + w],
                                                  recv_sem=recv_sems.at[k * nw + w], device_id=to, device_id_type=MESH)
                cp.wait_send()
                cp.wait_recv()

    out = _pallas_call(
        body, name=name, out_shape=tuple(pltpu.HBM(a.shape, a.dtype) for a in thru),
        in_specs=[_HBM] * nb + [_SEM, _SEM, _ANY], out_specs=tuple([_HBM] * nb),
        input_output_aliases={i: i for i in range(nb)},
        compiler_params=pltpu.CompilerParams(has_side_effects=_DATAFLOW),
    )(*thru, send_sems, recv_sems, after)
    return list(out)


def _my_slot():
    return 4 * lax.axis_index("x") + 2 * lax.axis_index("y") + lax.axis_index("c")


def _shard_into_land(w_all, idx):
    _, a, b = w_all.shape
    ta = next(cc for cc in (1024, 704, 512, 352, 256, 128, 64, 32, 16) if a % cc == 0 and (cc * b * 4 <= 2 ** 21 or cc == 16))

    def body(w_ref, o_ref):
        o_ref[...] = w_ref[...].astype(bf16)

    return _pallas_call(
        body, out_shape=_sds((N_DEV, a, b), bf16), grid=(a // ta,),
        in_specs=[pl.BlockSpec((None, ta, b), lambda i: (idx, i, 0))],
        out_specs=pl.BlockSpec((None, ta, b), lambda i: (_my_slot(), i, 0)),
        compiler_params=_cp(("parallel",)), name=f"shard_into_land_{a}x{b}_{idx}",
    )(w_all)


def _gather_copies(w, land_refs):
    x, y, c = _place()
    blk = land_refs[w].at[4 * x + 2 * y + c]
    return [(blk, blk, to) for to in ((x, y, 1 - c), (1 - x, y, c), (x, 1 - y, c), (1 - x, 1 - y, c))]


def _gather_begin(lands, route, after, tag):
    return _exchange_begin(lands, len(lands), route, after, _gather_copies, 4, f"gather_begin_{tag}")


def _gather_end(handle, after, tag):
    return _exchange_end(handle, after, _gather_copies, 4, f"gather_end_{tag}")


def _gather_pass_on(lands):
    nw = len(lands)

    def body(*refs):
        l_refs = refs[nw:2 * nw]
        send_sems, recv_sems = refs[2 * nw:]
        x, y, c = _place()
        copies = []
        for w in range(nw):
            for j, (px, py) in enumerate([(1 - x, y), (x, 1 - y), (1 - x, 1 - y)]):
                blk = l_refs[w].at[4 * px + 2 * py + c]
                copies.append(pltpu.make_async_remote_copy(
                    src_ref=blk, dst_ref=blk, send_sem=send_sems.at[j, w], recv_sem=recv_sems.at[j, w],
                    device_id=(x, y, 1 - c), device_id_type=MESH))
        for cp in copies:
            cp.start()
        for cp in copies:
            cp.wait_send()
        for w in range(nw):
            for j, (px, py) in enumerate([(1 - x, y), (x, 1 - y), (1 - x, 1 - y)]):
                blk = l_refs[w].at[4 * px + 2 * py + (1 - c)]
                pltpu.make_async_remote_copy(src_ref=blk, dst_ref=blk, send_sem=send_sems.at[j, w], recv_sem=recv_sems.at[j, w],
                                             device_id=(x, y, 1 - c), device_id_type=MESH).wait_recv()

    return _pallas_call(
        body, out_shape=tuple(_sds(a.shape, a.dtype) for a in lands), in_specs=[_ANY] * nw, out_specs=tuple([_ANY] * nw),
        input_output_aliases={w: w for w in range(nw)},
        scratch_shapes=[pltpu.SemaphoreType.DMA((3, nw)), pltpu.SemaphoreType.DMA((3, nw))],
        name=f"gather_pass_on_{_tag(lands)}",
    )(*lands)


def _swap_copies(w, refs):
    x, y, c = _place()
    nw = len(refs) // 2
    return [(refs[w].at[2 * k + (1 - c)], refs[nw + w].at[k], (x, y, 1 - c)) for k in range(N_DEV // 2)]


def _swap_begin(grads, route, after, tag):
    lands = [lax.empty((N_DEV // 2,) + g.shape[1:], g.dtype) for g in grads]
    return _exchange_begin(list(grads) + lands, len(grads), route, after, _swap_copies, N_DEV // 2, f"swap_begin_{tag}")


def _swap_end(handle, after, tag):
    out = _exchange_end(handle, after, _swap_copies, N_DEV // 2, f"swap_end_{tag}")
    return out[:len(out) // 2], out[len(out) // 2:]


def _scatter_copies(w, refs):
    x, y, c = _place()
    nw = len(refs) // 2
    dst = refs[nw + w].at[2 * x + y]
    return [(refs[w].at[2 * px + py], dst, (px, py, c)) for px, py in ((1 - x, y), (x, 1 - y), (1 - x, 1 - y))]


def _scatter_begin(psums, lands, route, after, tag):
    return _exchange_begin(list(psums) + list(lands), len(psums), route, after, _scatter_copies, 3, f"scatter_begin_{tag}")


def _scatter_end(handle, after, tag):
    return _exchange_end(handle, after, _scatter_copies, 3, f"scatter_end_{tag}")


def _adamw(parts, w_all, m_all, v_all, l, carried):
    nparts, a, b = parts.shape
    nl = w_all.shape[0]
    ta = next(cc for cc in (1024, 704, 512, 352, 256, 128, 64, 32, 16, 8) if a % cc == 0 and (cc * b * 4 <= 2 ** 20 or cc == 8))
    c1 = 1.0 / (1.0 - ADAM_B1 ** ADAM_STEP)
    c2 = 1.0 / (1.0 - ADAM_B2 ** ADAM_STEP)

    def body(p_ref, w_ref, m_ref, v_ref, *rest):
        g_out, d_out, m_out, v_out = rest[-4:]
        g = p_ref[0].astype(f32)
        for k in range(1, nparts):
            g = g + p_ref[k].astype(f32)
        m_new = ADAM_B1 * m_ref[...] + (1.0 - ADAM_B1) * g
        v_new = ADAM_B2 * v_ref[...] + (1.0 - ADAM_B2) * (g * g)
        m_hat = m_new * c1
        v_hat = v_new * c2
        g_out[...] = g
        d_out[...] = -ADAM_LR * (m_hat / (jnp.sqrt(v_hat) + ADAM_EPS) + ADAM_WD * w_ref[...])
        m_out[...] = m_new
        v_out[...] = v_new

    one = pl.BlockSpec((None, ta, b), lambda i: (l, i, 0))
    keep = [] if carried is None else [pl.BlockSpec(memory_space=pl.ANY)] * 4
    return _pallas_call(
        body, out_shape=tuple(_sds((nl, a, b), f32) for _ in range(4)), grid=(a // ta,),
        in_specs=[pl.BlockSpec((nparts, ta, b), lambda i: (0, i, 0)), one, one, one] + keep, out_specs=(one, one, one, one),
        input_output_aliases=({} if carried is None else {4 + q: q for q in range(4)}),
        compiler_params=_cp(("parallel",)), name=f"adamw_{nparts}x{nl}x{a}x{b}_{l}{'' if carried is None else '_carried'}",
    )(parts, w_all, m_all, v_all, *(carried or ()))


def _to_flat(arrays):
    flat = jnp.concatenate([a.reshape(-1).astype(f32) for a in arrays])
    rows = -(-flat.shape[0] // (8 * LANES)) * 8
    return jnp.pad(flat, (0, rows * LANES - flat.shape[0])).reshape(rows, LANES)


def _from_flat(flat, shapes):
    flat = flat.reshape(-1)
    out, off = [], 0
    for shp in shapes:
        n = 1
        for dd in shp:
            n *= dd
        out.append(flat[off:off + n].reshape(shp))
        off += n
    return out


def kernel(x, mem, g_mix, g_ffn, w_in_a, g_v_a, w_spatial, b_spatial, w_in_b, g_q_b, g_k_b, g_mem, w_mem_kv, g_mq, g_mk, w_out, w_gate_up, w_down, loss_target, m_g_mix, m_g_ffn, m_w_in_a, m_g_v_a, m_w_spatial, m_b_spatial, m_w_in_b, m_g_q_b, m_g_k_b, m_g_mem, m_w_mem_kv, m_g_mq, m_g_mk, m_w_out, m_w_gate_up, m_w_down, v_g_mix, v_g_ffn, v_w_in_a, v_g_v_a, v_w_spatial, v_b_spatial, v_w_in_b, v_g_q_b, v_g_k_b, v_g_mem, v_w_mem_kv, v_g_mq, v_g_mk, v_w_out, v_w_gate_up, v_w_down):
    given = dict(locals())
    depth = g_mix.shape[0]
    s, d = x.shape[1], x.shape[2]
    nm = mem.shape[1]
    t = d - MEM_WIDTH
    ff = w_down.shape[1] * N_DEV
    x0 = x.reshape(s, d)
    mem0 = mem.reshape(nm, d)
    target = loss_target.reshape(s, d)
    tables = _rope_tables(s)

    big_names = ("w_in", "w_mem_kv", "w_out", "w_gate_up", "w_down")

    def stacked_key(name, l):
        if name == "w_in":
            return ("w_in_a" if l % 2 == 0 else "w_in_b"), l // 2
        return name, l

    n_mix = 3

    def own_shards(l):
        return [_shard_into_land(given[key], idx) for key, idx in (stacked_key(name, l) for name in big_names)]

    def gather_start(l, route, after):
        lands = cast_shards[l]
        h_mix, route = _gather_begin(lands[:n_mix], route, after, f"mix{l}")
        h_ffn, route = _gather_begin(lands[n_mix:], route, after, f"ffn{l}")
        return h_mix, h_ffn, route

    def gather_finish(handle, after, tag):
        return list(_gather_pass_on(_gather_end(handle, after, tag)))

    saved = []
    xc = x0
    cast_shards = {0: own_shards(0)}
    h_in, _ = _gather_begin(cast_shards[0][:1], g_mix[0].reshape(1, d), mem0, "in0")
    for l in range(1, depth):
        cast_shards[l] = own_shards(l)
    w_mix = gather_finish(h_in, cast_shards[depth - 1][0], "in0")
    w_ffn = None
    for l in range(depth):
        is_a = l % 2 == 0
        g_in = w_mix[0]
        qblk = (N_DEV * g_in.shape[2] - MEM_WIDTH) // MEM_WIDTH

        gm_row, gf_row, gmem_row = g_mix[l].reshape(1, d), g_ffn[l].reshape(1, d), g_mem[l].reshape(1, d)
        gmq_row, gmk_row = g_mq[l].reshape(1, HEAD), g_mk[l].reshape(1, HEAD)
        if l == 0:
            h_rest, gm_row = _gather_begin(cast_shards[0][1:n_mix], gm_row, g_in, "rest0")
            h_ffn, gm_row = _gather_begin(cast_shards[0][n_mix:], gm_row, g_in, "ffn0")
        h = _rms_fwd(xc, gm_row)
        z = _mm_cols_fwd(h, g_in, f32)
        if l == 0:
            w_mix = w_mix + gather_finish(h_rest, z, "rest0")
        g_kv, g_out = w_mix[1:]
        w_kv, w_o = (g.reshape(-1, g.shape[2]) for g in (g_kv, g_out))
        if is_a:
            ia = l // 2
            mix = dict(g_v=g_v_a[ia].reshape(1, t), w_s=w_spatial[ia], b_t=b_spatial[ia].T)
            cat = _gmlp_fwd(z, mix["g_v"], mix["w_s"], mix["b_t"])
        else:
            ib = l // 2
            mix = dict(g_q=g_q_b[ib].reshape(1, HEAD), g_k=g_k_b[ib].reshape(1, HEAD))
            q, k, v = _attn_prep_fwd(z, mix["g_q"], mix["g_k"], tables, t)
            cat, lse = _flash_fwd(q, k, v)
            mix.update(q=q, k=k, v=v, lse=lse)
        hm = _rms_fwd(mem0, gmem_row)
        kv = _matmul(hm, w_kv)
        cat = _mem_fwd(z, qblk, kv, gmq_row, gmk_row, cat)
        x1 = _matmul(cat, w_o, res=xc)
        w_ffn = gather_finish(h_ffn, x1, f"ffn{l}")
        if l + 1 < depth:
            next_mix, next_ffn, gf_row = gather_start(l + 1, gf_row, w_ffn[0])
        h2 = _rms_fwd(x1, gf_row)
        g_gu, g_dn = w_ffn
        w_dn = g_dn.reshape(-1, g_dn.shape[2])
        gate, up, act = _ffn_up(h2, g_gu)
        x2 = _matmul(act, w_dn, res=x1)
        saved.append(dict(x=xc, h=h, z=z, mix=mix, cat=cat, hm=hm, kv=kv, x1=x1, h2=h2, gate=gate, up=up, act=act, qblk=qblk,
                          w=(g_in, w_kv, w_o, g_gu, w_dn), rows=(gm_row, gf_row, gmem_row, gmq_row, gmk_row)))
        if l + 1 < depth:
            w_mix = gather_finish(next_mix, x2, f"mix{l + 1}")
            h_ffn = next_ffn
        xc = x2

    loss_row, dy, dy_b = _loss_head(xc, target)
    loss = lax.psum(loss_row[0, 0], ("x", "y", "c"))

    small = {n: [None] * given[n].shape[0] for n in ("g_mix", "g_ffn", "g_v_a", "w_spatial", "b_spatial", "g_q_b", "g_k_b", "g_mem", "g_mq", "g_mk")}
    big_out = {}

    def scatter_start(swap, route, after, tag):
        grads, got = _swap_end(swap, after, tag)
        sums = [_pair_sum(g, r) for g, r in zip(grads, got)]
        return _scatter_begin([p for p, _ in sums], [q for _, q in sums], route, after, tag)

    def scatter_finish(handle, names, l, after, tag):
        arrived = _scatter_end(handle, after, tag)[len(names):]
        for name, parts in zip(names, arrived):
            key, idx = stacked_key(name, l)
            big_out[key] = _adamw(parts, given[key], given["m_" + key], given["v_" + key], idx, big_out.get(key))

    pend_mix = None
    dx, dx_b = dy, dy_b
    for l in reversed(range(depth)):
        sv = saved[l]
        is_a = l % 2 == 0
        g_in, w_kv, w_o, g_gu, w_dn = sv["w"]
        gm_row, gf_row, gmem_row, gmq_row, gmk_row = sv["rows"]
        mix = sv["mix"]
        dw_dn = _matmul(sv["act"], dx_b, ta=True, out_dtype=bf16)
        dgu = _ffn_down_bwd(dx_b, w_dn, sv["gate"], sv["up"])
        dw_gu = _mm_cols_wgrad(sv["h2"], dgu, g_gu.shape[2])
        swap, dgate = _swap_begin([dw_gu, dw_dn.reshape(N_DEV, -1, d)], dgu[0], dx_b, f"ffn{l}")
        dh2 = _mm_cols_dgrad((dgate, dgu[1]), g_gu)
        dx1, dx1_b, dgf = _rms_bwd(sv["x1"], gf_row, dh2, dx)
        small["g_ffn"][l] = dgf.reshape(d)
        pend_ffn, dx1_b = scatter_start(swap, dx1_b, dx1, f"ffn{l}")
        if pend_mix is not None:
            scatter_finish(*pend_mix, dx1_b, f"mix{l + 1}")
        dw_o = _matmul(sv["cat"], dx1_b, ta=True, out_dtype=bf16)
        dcat = _matmul(dx1_b, w_o, tb=True, out_dtype=bf16)
        if is_a:
            dz, dws, dbt, dgv = _gmlp_bwd(sv["z"], mix["g_v"], mix["w_s"], mix["b_t"], dcat)
            small["w_spatial"][l // 2], small["b_spatial"][l // 2], small["g_v_a"][l // 2] = dws, dbt.T, dgv.reshape(t)
        else:
            dq, dk, dv = _flash_bwd(mix["q"], mix["k"], mix["v"], sv["cat"], dcat, mix["lse"])
            dz, dgq, dgk = _attn_prep_bwd(sv["z"], mix["g_q"], mix["g_k"], tables, dq, dk, dv, t)
            small["g_q_b"][l // 2], small["g_k_b"][l // 2] = dgq.reshape(HEAD), dgk.reshape(HEAD)
        dz, dkn, dvm, dgmq = _mem_bwd(sv["z"], sv["qblk"], sv["kv"], gmq_row, gmk_row, dcat, dz)
        dkv, dgmk = _mem_kv_bwd(sv["kv"], gmk_row, dkn, dvm)
        dw_kv = _matmul(sv["hm"], dkv, ta=True, out_dtype=bf16)
        dhm = _matmul(dkv, w_kv, tb=True)
        small["g_mem"][l] = _rms_bwd(mem0, gmem_row, dhm, None).reshape(d)
        small["g_mq"][l] = dgmq.reshape(HEAD)
        small["g_mk"][l] = dgmk.reshape(HEAD)
        dw_in = _mm_cols_wgrad(sv["h"], dz, g_in.shape[2])
        swap, dz = _swap_begin([dw_in] + [dw.reshape(N_DEV, -1, dw.shape[1]) for dw in (dw_kv, dw_o)], dz, dx1, f"mix{l}")
        dh = _mm_cols_dgrad(dz, g_in)
        dx, dx_b, dgm = _rms_bwd(sv["x"], gm_row, dh, dx1)
        small["g_mix"][l] = dgm.reshape(d)

        handle, dx_b = scatter_start(swap, dx_b, dx, f"mix{l}")
        pend_mix = (handle, big_names[:n_mix], l)
        scatter_finish(pend_ffn, big_names[n_mix:], l, dx_b, f"ffn{l}")
    small_names = tuple(small)
    small_grads = [jnp.stack(small[n]) for n in small_names]
    small_shapes = [g.shape for g in small_grads]
    (all_parts,) = _all_gather([_to_flat(small_grads)])
    souts = _adamw(all_parts, *[_to_flat([given[p + n] for n in small_names])[None] for p in ("", "m_", "v_")], 0, None)
    small_out = dict(zip(small_names, zip(*[_from_flat(flat, small_shapes) for flat in souts])))
    scatter_finish(*pend_mix, souts[0], "mix0")

    weights = ("g_mix", "g_ffn", "w_in_a", "g_v_a", "w_spatial", "b_spatial", "w_in_b", "g_q_b", "g_k_b", "g_mem", "w_mem_kv",
               "g_mq", "g_mk", "w_out", "w_gate_up", "w_down")
    results = {n: (small_out[n] if n in small_out else big_out[n]) for n in weights}
    grad_x = dx.reshape(1, s, d)
    return (loss, grad_x, *[results[n][kind] for kind in range(4) for n in weights])
```

```python
import functools

import jax
import jax.numpy as jnp
from jax import lax
from jax.experimental import pallas as pl
from jax.experimental.pallas import tpu as pltpu

f32 = jnp.float32
bf16 = jnp.bfloat16

HEAD = 128
CHUNK = 128
GRID_W = 64
MEM_HEADS = 4
KV_HEADS = 4
MEM_WIDTH = MEM_HEADS * HEAD
KV_WIDTH = KV_HEADS * HEAD
ROPE_THETA = 10000.0
ROPE_PAIRS = HEAD // 4
EPS = 1e-6
SCALE = HEAD ** -0.5
LOG2E = 1.4426950408889634
N_DEV = 8
LANES = 1024
VMEM_LIMIT = 56 * 1024 * 1024

ADAM_LR, ADAM_B1, ADAM_B2, ADAM_EPS, ADAM_WD, ADAM_STEP = 0.001, 0.9, 0.999, 1e-08, 0.01, 10

MESH = pl.DeviceIdType.MESH
_pallas_call = pl.pallas_call


def _pick(dim, cands):
    for c in cands:
        if dim % c == 0:
            return c
    return dim


def _cp(sem):
    return pltpu.CompilerParams(dimension_semantics=sem, vmem_limit_bytes=VMEM_LIMIT)


def _sds(shape, dtype):
    return jax.ShapeDtypeStruct(shape, dtype)


def _dot(a, b, ca, cb):
    return lax.dot_general(a, b, (((ca,), (cb,)), ((), ())), preferred_element_type=f32)


def _gelu(z):
    return 0.5 * z * (1.0 + lax.erf(z * 0.7071067811865476))


def _gelu_grad(z):
    return 0.5 * (1.0 + lax.erf(z * 0.7071067811865476)) + z * jnp.exp(-0.5 * z * z) * 0.3989422804014327


def _rot(x, sin_a, sin_b):
    return pltpu.roll(x, 96, 1) * sin_a + pltpu.roll(x, 32, 1) * sin_b


def _matmul(a, b, *, ta=False, tb=False, out_dtype=f32, res=None, tm=None, tn=None, tk=None):
    assert a.dtype == bf16 and b.dtype == bf16
    kdim, m = a.shape if ta else a.shape[::-1]
    n, k2 = b.shape if tb else b.shape[::-1]
    assert kdim == k2, (a.shape, b.shape, ta, tb)
    tm = tm or _pick(m, (1024, 1408, 512, 256, 128))
    tn = tn or _pick(n, (1024, 1408, 512, 256, 128))
    if tk is None:
        tk = kdim if kdim <= 2048 else _pick(kdim, (2816, 1024, 512, 256, 128))
    nk = kdim // tk
    ca, cb = (0 if ta else 1), (1 if tb else 0)
    has_res = res is not None

    def body(*refs):
        a_ref, b_ref = refs[0], refs[1]
        r_ref = refs[2] if has_res else None
        o_ref = refs[3] if has_res else refs[2]
        prod = _dot(a_ref[...], b_ref[...], ca, cb)
        if nk == 1:
            if has_res:
                prod = prod + r_ref[...]
            o_ref[...] = prod.astype(o_ref.dtype)
        else:
            acc = refs[-1]
            k = pl.program_id(2)

            @pl.when(k == 0)
            def _():
                acc[...] = prod

            @pl.when(k > 0)
            def _():
                acc[...] += prod

            @pl.when(k == nk - 1)
            def _():
                out = acc[...]
                if has_res:
                    out = out + r_ref[...]
                o_ref[...] = out.astype(o_ref.dtype)

    a_spec = pl.BlockSpec((tk, tm), lambda i, j, k: (k, i)) if ta else pl.BlockSpec((tm, tk), lambda i, j, k: (i, k))
    b_spec = pl.BlockSpec((tn, tk), lambda i, j, k: (j, k)) if tb else pl.BlockSpec((tk, tn), lambda i, j, k: (k, j))
    o_spec = pl.BlockSpec((tm, tn), lambda i, j, k: (i, j))
    in_specs = [a_spec, b_spec] + ([o_spec] if has_res else [])
    args = (a, b) + ((res,) if has_res else ())
    mode = ("t" if ta else "n") + ("t" if tb else "n")
    return _pallas_call(
        body, out_shape=_sds((m, n), out_dtype), grid=(m // tm, n // tn, nk),
        in_specs=in_specs, out_specs=o_spec,
        scratch_shapes=([pltpu.VMEM((tm, tn), f32)] if nk > 1 else []),
        compiler_params=_cp(("parallel", "parallel", "arbitrary")),
        name=f"mm_{mode}_{m}x{kdim}x{n}{'_res' if has_res else ''}_{jnp.dtype(out_dtype).name}",
    )(*args)


def _shards_per_step(n, pair_bytes=0):
    p = 2 if (n % 128 != 0 or 0 < 2 * pair_bytes <= VMEM_LIMIT // 2) else 1
    assert (p * n) % 128 == 0 and N_DEV % p == 0
    return p


def _lane_pieces(v, p, n):
    return [v] if p == 1 else [v[:, q * n:(q + 1) * n] for q in range(p)]


def _mm_cols_fwd(a, g, out_dtype):
    m, kdim = a.shape
    nd, k2, n = g.shape
    assert kdim == k2 and a.dtype == bf16 and g.dtype == bf16
    p = _shards_per_step(n)
    tm = _pick(m, (1024, 512, 256, 128))

    def body(a_ref, g_ref, o_ref):
        av = a_ref[...]
        parts = [_dot(av, g_ref[q], 1, 0) for q in range(p)]
        out = parts[0] if p == 1 else jnp.concatenate(parts, axis=1)
        o_ref[...] = out.astype(o_ref.dtype)

    return _pallas_call(
        body, out_shape=_sds((m, nd * n), out_dtype), grid=(m // tm, nd // p),
        in_specs=[pl.BlockSpec((tm, kdim), lambda i, j: (i, 0)), pl.BlockSpec((p, kdim, n), lambda i, j: (j, 0, 0))],
        out_specs=pl.BlockSpec((tm, p * n), lambda i, j: (i, j)),
        compiler_params=_cp(("parallel", "arbitrary")), name=f"mm_cols_fwd_{m}x{kdim}x{nd * n}_{jnp.dtype(out_dtype).name}",
    )(a, g)


def _pick_part(refs, nparts, step, per):
    val = refs[0][...]
    for hh in range(1, nparts):
        val = jnp.where(step >= hh * per, refs[hh][...], val)
    return val


def _part_step(step, hh, per):
    return jnp.clip(step - hh * per, 0, per - 1)


def _mm_cols_dgrad(dz, g):
    nd, kdim, n = g.shape
    p = _shards_per_step(n, pair_bytes=2 * kdim * n * 2)
    nj = nd // p
    parts = dz if isinstance(dz, tuple) else (dz,)
    m = parts[0].shape[0]
    nn = sum(part.shape[1] for part in parts)
    assert nn == nd * n and all(part.dtype == bf16 for part in parts) and g.dtype == bf16
    tm = _pick(m, (512, 256, 128))
    nparts = len(parts)
    per = nj // nparts

    def body(*refs):
        g_ref, o_ref, acc = refs[nparts:]
        j = pl.program_id(1)
        tot = None
        for q, piece in enumerate(_lane_pieces(_pick_part(refs, nparts, j, per), p, n)):
            dd = _dot(piece, g_ref[q], 1, 1)
            tot = dd if tot is None else tot + dd

        @pl.when(j == 0)
        def _():
            acc[...] = tot

        @pl.when(j > 0)
        def _():
            acc[...] += tot

        @pl.when(j == nj - 1)
        def _():
            o_ref[...] = acc[...]

    return _pallas_call(
        body, out_shape=_sds((m, kdim), f32), grid=(m // tm, nj),
        in_specs=[pl.BlockSpec((tm, p * n), (lambda hh: lambda i, j: (i, _part_step(j, hh, per)))(hh)) for hh in range(nparts)]
        + [pl.BlockSpec((p, kdim, n), lambda i, j: (j, 0, 0))],
        out_specs=pl.BlockSpec((tm, kdim), lambda i, j: (i, 0)),
        scratch_shapes=[pltpu.VMEM((tm, kdim), f32)],
        compiler_params=_cp(("parallel", "arbitrary")), name=f"mm_cols_dgrad_{m}x{nn}x{kdim}_{nparts}",
    )(*parts, g)


def _mm_cols_wgrad(a, dz, n):
    s, kdim = a.shape
    parts = dz if isinstance(dz, tuple) else (dz,)
    nparts = len(parts)
    nd = sum(part.shape[1] for part in parts) // n
    assert a.dtype == bf16 and all(part.dtype == bf16 for part in parts)
    p = _shards_per_step(n)
    per = nd // p // nparts
    tkw = _pick(kdim, (1024, 512, 256, 128))
    ts = _pick(s, (2048, 1024, 512, 256, 128))
    ns = s // ts

    def body(a_ref, *refs):
        o_ref, acc = refs[nparts:]
        si = pl.program_id(2)
        av = a_ref[...]
        prods = [_dot(av, piece, 0, 0) for piece in _lane_pieces(_pick_part(refs, nparts, pl.program_id(1), per), p, n)]

        @pl.when(si == 0)
        def _():
            for q in range(p):
                acc[q] = prods[q]

        @pl.when(si > 0)
        def _():
            for q in range(p):
                acc[q] += prods[q]

        @pl.when(si == ns - 1)
        def _():
            o_ref[...] = acc[...].astype(bf16)

    return _pallas_call(
        body, out_shape=_sds((nd, kdim, n), bf16), grid=(kdim // tkw, nd // p, ns),
        in_specs=[pl.BlockSpec((ts, tkw), lambda i, j, k: (k, i))]
        + [pl.BlockSpec((ts, p * n), (lambda hh: lambda i, j, k: (k, _part_step(j, hh, per)))(hh)) for hh in range(nparts)],
        out_specs=pl.BlockSpec((p, tkw, n), lambda i, j, k: (j, i, 0)),
        scratch_shapes=[pltpu.VMEM((p, tkw, n), f32)],
        compiler_params=_cp(("parallel", "parallel", "arbitrary")), name=f"mm_cols_wgrad_{kdim}x{s}x{nd * n}_{nparts}",
    )(a, *parts)


def _rms_fwd(x, g_row):
    s, d = x.shape
    tr = _pick(s, (512, 256, 128))

    def body(x_ref, g_ref, o_ref):
        xv = x_ref[...]
        r = lax.rsqrt(jnp.mean(xv * xv, axis=-1, keepdims=True) + EPS)
        o_ref[...] = (xv * r * g_ref[...]).astype(bf16)

    return _pallas_call(
        body, out_shape=_sds((s, d), bf16), grid=(s // tr,),
        in_specs=[pl.BlockSpec((tr, d), lambda i: (i, 0)), pl.BlockSpec((1, d), lambda i: (0, 0))],
        out_specs=pl.BlockSpec((tr, d), lambda i: (i, 0)),
        compiler_params=_cp(("parallel",)), name=f"rms_fwd_{s}x{d}",
    )(x, g_row)


def _rms_bwd(x, g_row, dh, dres):
    s, d = x.shape
    tr = _pick(s, (512, 256, 128))
    with_dx = dres is not None

    def body(*refs):
        if with_dx:
            x_ref, g_ref, dh_ref, dres_ref, dx_ref, dxb_ref, dg_ref = refs
        else:
            x_ref, g_ref, dh_ref, dg_ref = refs

        @pl.when(pl.program_id(0) == 0)
        def _():
            dg_ref[...] = jnp.zeros_like(dg_ref)

        xv = x_ref[...]
        r = lax.rsqrt(jnp.mean(xv * xv, axis=-1, keepdims=True) + EPS)
        xh = xv * r
        dy = dh_ref[...].astype(f32)
        dg_ref[...] += jnp.sum(dy * xh, axis=0, keepdims=True)
        if with_dx:
            gy = dy * g_ref[...]
            dx = dres_ref[...] + r * (gy - xh * jnp.mean(gy * xh, axis=-1, keepdims=True))
            dx_ref[...] = dx
            dxb_ref[...] = dx.astype(bf16)

    row = pl.BlockSpec((tr, d), lambda i: (i, 0))
    vec = pl.BlockSpec((1, d), lambda i: (0, 0))
    if with_dx:
        return _pallas_call(
            body, out_shape=(_sds((s, d), f32), _sds((s, d), bf16), _sds((1, d), f32)), grid=(s // tr,),
            in_specs=[row, vec, row, row], out_specs=(row, row, vec),
            compiler_params=_cp(("arbitrary",)), name=f"rms_bwd_{s}x{d}",
        )(x, g_row, dh, dres)
    return _pallas_call(
        body, out_shape=_sds((1, d), f32), grid=(s // tr,),
        in_specs=[row, vec, row], out_specs=vec,
        compiler_params=_cp(("arbitrary",)), name=f"rms_bwd_gain_{s}x{d}",
    )(x, g_row, dh)


def _gmlp_rows(s):
    return CHUNK * (2 if (s // CHUNK) % 2 == 0 else 1)


def _gmlp_fwd(z, g_v, w_s, b_t):
    s = z.shape[0]
    t = g_v.shape[1]
    ng = t // HEAD
    rb = _gmlp_rows(s)

    def body(z_ref, gv_ref, ws_ref, bt_ref, o_ref):
        for ci in range(rb // CHUNK):
            lo = ci * CHUNK
            a = _gelu(z_ref[lo:lo + CHUNK, :])
            u, vv = a[:, :t], a[:, t:]
            r = lax.rsqrt(jnp.mean(vv * vv, axis=-1, keepdims=True) + EPS)
            vn = (vv * r * gv_ref[...]).astype(bf16)
            for g in range(ng):
                cs = slice(g * HEAD, (g + 1) * HEAD)
                sg = _dot(ws_ref[g].astype(bf16), vn[:, cs], 1, 0) + bt_ref[:, g:g + 1]
                o_ref[lo:lo + CHUNK, cs] = (u[:, cs] * sg).astype(bf16)

    return _pallas_call(
        body, out_shape=_sds((s, t + MEM_WIDTH), bf16), grid=(s // rb,),
        in_specs=[pl.BlockSpec((rb, 2 * t), lambda i: (i, 0)), pl.BlockSpec((1, t), lambda i: (0, 0)),
                  pl.BlockSpec((ng, CHUNK, CHUNK), lambda i: (0, 0, 0)), pl.BlockSpec((CHUNK, ng), lambda i: (0, 0))],
        out_specs=pl.BlockSpec((rb, t), lambda i: (i, 0)),
        compiler_params=_cp(("parallel",)), name=f"gmlp_fwd_{s}",
    )(z, g_v, w_s, b_t)


def _gmlp_bwd(z, g_v, w_s, b_t, dtok):
    s = z.shape[0]
    t = g_v.shape[1]
    ng = t // HEAD
    rb = _gmlp_rows(s)
    nsteps = s // rb

    def body(z_ref, gv_ref, ws_ref, bt_ref, dt_ref, dz_ref, dws_ref, dbt_ref, dgv_ref, ds_acc):
        step = pl.program_id(0)

        @pl.when(step == 0)
        def _():
            dws_ref[...] = jnp.zeros_like(dws_ref)
            dgv_ref[...] = jnp.zeros_like(dgv_ref)
            ds_acc[...] = jnp.zeros_like(ds_acc)

        for ci in range(rb // CHUNK):
            lo = ci * CHUNK
            zz = z_ref[lo:lo + CHUNK, :]
            a = _gelu(zz)
            u, vv = a[:, :t], a[:, t:]
            r = lax.rsqrt(jnp.mean(vv * vv, axis=-1, keepdims=True) + EPS)
            vh = vv * r
            vn = (vh * gv_ref[...]).astype(bf16)
            dtok = dt_ref[lo:lo + CHUNK, :].astype(f32)
            ds = dtok * u
            ds_acc[...] += ds
            dsb = ds.astype(bf16)
            du_parts, dvn_parts = [], []
            for g in range(ng):
                cs = slice(g * HEAD, (g + 1) * HEAD)
                wg = ws_ref[g].astype(bf16)
                sg = _dot(wg, vn[:, cs], 1, 0) + bt_ref[:, g:g + 1]
                du_parts.append(dtok[:, cs] * sg)
                dws_ref[g] += _dot(dsb[:, cs], vn[:, cs], 1, 1)
                dvn_parts.append(_dot(wg, dsb[:, cs], 0, 0))
            dvn = jnp.concatenate(dvn_parts, axis=1)
            dgv_ref[...] += jnp.sum(dvn * vh, axis=0, keepdims=True)
            gy = dvn * gv_ref[...]
            dvv = r * (gy - vh * jnp.mean(gy * vh, axis=-1, keepdims=True))
            da = jnp.concatenate(du_parts + [dvv], axis=1)
            dz_ref[lo:lo + CHUNK, :] = (da * _gelu_grad(zz)).astype(bf16)

        @pl.when(step == nsteps - 1)
        def _():
            for g in range(ng):
                dbt_ref[:, g:g + 1] = jnp.sum(ds_acc[:, g * HEAD:(g + 1) * HEAD], axis=1, keepdims=True)

    return _pallas_call(
        body,
        out_shape=(_sds((s, z.shape[1]), bf16), _sds((ng, CHUNK, CHUNK), f32), _sds((CHUNK, ng), f32), _sds((1, t), f32)),
        grid=(nsteps,),
        in_specs=[pl.BlockSpec((rb, 2 * t), lambda i: (i, 0)), pl.BlockSpec((1, t), lambda i: (0, 0)),
                  pl.BlockSpec((ng, CHUNK, CHUNK), lambda i: (0, 0, 0)), pl.BlockSpec((CHUNK, ng), lambda i: (0, 0)),
                  pl.BlockSpec((rb, t), lambda i: (i, 0))],
        out_specs=(pl.BlockSpec((rb, 2 * t), lambda i: (i, 0)), pl.BlockSpec((ng, CHUNK, CHUNK), lambda i: (0, 0, 0)),
                   pl.BlockSpec((CHUNK, ng), lambda i: (0, 0)), pl.BlockSpec((1, t), lambda i: (0, 0))),
        scratch_shapes=[pltpu.VMEM((CHUNK, t), f32)],
        compiler_params=_cp(("arbitrary",)), name=f"gmlp_bwd_{s}",
    )(z, g_v, w_s, b_t, dtok)


def _rope_tables(s):
    n_rows = s // GRID_W
    rows = jnp.broadcast_to(jnp.arange(n_rows)[:, None], (n_rows, GRID_W)).reshape(s)
    cols = jnp.broadcast_to(jnp.arange(GRID_W)[None, :], (n_rows, GRID_W)).reshape(s)
    freqs = ROPE_THETA ** (-jnp.arange(ROPE_PAIRS, dtype=f32) / ROPE_PAIRS)
    ang_r = rows.astype(f32)[:, None] * freqs
    ang_c = cols.astype(f32)[:, None] * freqs
    ang = jnp.concatenate([ang_r, ang_r, ang_c, ang_c], axis=-1)
    cos, sin = jnp.cos(ang), jnp.sin(ang)
    first = (jnp.arange(HEAD) % (2 * ROPE_PAIRS)) < ROPE_PAIRS
    return cos, jnp.where(first, -sin, 0.0), jnp.where(first, 0.0, sin)


def _attn_prep_fwd(z, g_q, g_k, tables, t):
    s = z.shape[0]
    tr = _pick(s, (256, 128))
    nq = t // HEAD
    width = t + 2 * KV_WIDTH

    def body(z_ref, gq_ref, gk_ref, cos_ref, sa_ref, sb_ref, q_ref, k_ref, v_ref):
        cos, sa, sb = cos_ref[...], sa_ref[...], sb_ref[...]
        for h in range(nq + KV_HEADS):
            cs = slice(h * HEAD, (h + 1) * HEAD)
            xv = z_ref[:, cs]
            r = lax.rsqrt(jnp.mean(xv * xv, axis=-1, keepdims=True) + EPS)
            xn = xv * r * (gq_ref[...] if h < nq else gk_ref[...])
            y = xn * cos + _rot(xn, sa, sb)
            if h < nq:
                q_ref[:, cs] = (y * (SCALE * LOG2E)).astype(bf16)
            else:
                k_ref[:, (h - nq) * HEAD:(h - nq + 1) * HEAD] = y.astype(bf16)
        v_ref[...] = z_ref[:, t + KV_WIDTH:width].astype(bf16)

    row = lambda w: pl.BlockSpec((tr, w), lambda i: (i, 0))
    vec = pl.BlockSpec((1, HEAD), lambda i: (0, 0))
    return _pallas_call(
        body, out_shape=(_sds((s, t), bf16), _sds((s, KV_WIDTH), bf16), _sds((s, KV_WIDTH), bf16)), grid=(s // tr,),
        in_specs=[row(width), vec, vec, row(HEAD), row(HEAD), row(HEAD)],
        out_specs=(row(t), row(KV_WIDTH), row(KV_WIDTH)),
        compiler_params=_cp(("parallel",)), name=f"attn_prep_fwd_{s}",
    )(z, g_q, g_k, *tables)


def _attn_prep_bwd(z, g_q, g_k, tables, dq, dk, dv, t):
    s = z.shape[0]
    tr = _pick(s, (256, 128))
    nq = t // HEAD
    width = t + 2 * KV_WIDTH

    def body(z_ref, gq_ref, gk_ref, cos_ref, sa_ref, sb_ref, dq_ref, dk_ref, dv_ref, dz_ref, dgq_ref, dgk_ref):
        @pl.when(pl.program_id(0) == 0)
        def _():
            dgq_ref[...] = jnp.zeros_like(dgq_ref)
            dgk_ref[...] = jnp.zeros_like(dgk_ref)

        cos, sa, sb = cos_ref[...], sa_ref[...], sb_ref[...]
        for h in range(nq + KV_HEADS):
            cs = slice(h * HEAD, (h + 1) * HEAD)
            xv = z_ref[:, cs]
            r = lax.rsqrt(jnp.mean(xv * xv, axis=-1, keepdims=True) + EPS)
            xh = xv * r
            if h < nq:
                dy, g_ref, dg_ref = dq_ref[:, cs], gq_ref, dgq_ref
            else:
                dy, g_ref, dg_ref = dk_ref[:, (h - nq) * HEAD:(h - nq + 1) * HEAD], gk_ref, dgk_ref
            dy = dy.astype(f32)
            dxn = dy * cos - _rot(dy, sa, sb)
            dg_ref[...] += jnp.sum(dxn * xh, axis=0, keepdims=True)
            gy = dxn * g_ref[...]
            dz_ref[:, cs] = (r * (gy - xh * jnp.mean(gy * xh, axis=-1, keepdims=True))).astype(bf16)
        dz_ref[:, t + KV_WIDTH:width] = dv_ref[...].astype(bf16)

    row = lambda w: pl.BlockSpec((tr, w), lambda i: (i, 0))
    vec = pl.BlockSpec((1, HEAD), lambda i: (0, 0))
    return _pallas_call(
        body, out_shape=(_sds((s, z.shape[1]), bf16), _sds((1, HEAD), f32), _sds((1, HEAD), f32)), grid=(s // tr,),
        in_specs=[row(width), vec, vec, row(HEAD), row(HEAD), row(HEAD), row(t), row(KV_WIDTH), row(KV_WIDTH)],
        out_specs=(row(width), vec, vec),
        compiler_params=_cp(("arbitrary",)), name=f"attn_prep_bwd_{s}",
    )(z, g_q, g_k, *tables, dq, dk, dv)


def _flash_tiles(s):
    return _pick(s, (512, 256, 128)), _pick(s, (1024, 512, 256, 128))


def _flash_fwd(q, k, v):
    s, t = q.shape
    grp = t // KV_WIDTH
    tq, tk = _flash_tiles(s)
    nkv = s // tk
    rows = grp * tq

    def body(q_ref, k_ref, v_ref, o_ref, lse_ref, m_sc, acc_sc):
        ki = pl.program_id(2)

        @pl.when(ki == 0)
        def _():
            m_sc[...] = jnp.full(m_sc.shape, -jnp.inf, f32)
            acc_sc[...] = jnp.zeros_like(acc_sc)

        kk = k_ref[...]
        v1 = jnp.concatenate([v_ref[...], jnp.ones((tk, HEAD), bf16)], axis=1)
        sc = _dot(q_ref[:, 0:HEAD], kk, 1, 1)
        for g in range(grp):
            sc_next = _dot(q_ref[:, (g + 1) * HEAD:(g + 2) * HEAD], kk, 1, 1) if g + 1 < grp else None
            mine = slice(g * tq, (g + 1) * tq)
            m_prev = m_sc[mine, :]
            m_new = jnp.maximum(m_prev, jnp.max(sc, axis=-1, keepdims=True))
            alpha = jnp.exp2(m_prev - m_new)
            p = jnp.exp2((sc - m_new).astype(bf16))
            acc_sc[mine, :] = alpha * acc_sc[mine, :] + _dot(p, v1, 1, 0)
            m_sc[mine, :] = m_new
            sc = sc_next

        @pl.when(ki == nkv - 1)
        def _():
            acc = acc_sc[...]
            l = acc[:, HEAD:HEAD + 1]
            o = acc[:, :HEAD] / l
            for g in range(grp):
                o_ref[:, g * HEAD:(g + 1) * HEAD] = o[g * tq:(g + 1) * tq].astype(bf16)
            lse_ref[0] = jnp.broadcast_to(m_sc[...] + jnp.log(l) * LOG2E, (rows, HEAD))

    return _pallas_call(
        body, out_shape=(_sds((s, t + MEM_WIDTH), bf16), _sds((KV_HEADS, grp * s, HEAD), f32)), grid=(KV_HEADS, s // tq, nkv),
        in_specs=[pl.BlockSpec((tq, grp * HEAD), lambda h, i, j: (i, h)), pl.BlockSpec((tk, HEAD), lambda h, i, j: (j, h)),
                  pl.BlockSpec((tk, HEAD), lambda h, i, j: (j, h))],
        out_specs=(pl.BlockSpec((tq, grp * HEAD), lambda h, i, j: (i, h)), pl.BlockSpec((1, rows, HEAD), lambda h, i, j: (h, i, 0))),
        scratch_shapes=[pltpu.VMEM((rows, 1), f32), pltpu.VMEM((rows, 2 * HEAD), f32)],
        compiler_params=_cp(("parallel", "parallel", "arbitrary")), name=f"flash_fwd_{s}",
    )(q, k, v)


def _flash_delta(o, do, t):
    s = o.shape[0]
    grp = t // KV_WIDTH
    tq, _ = _flash_tiles(s)
    rows = grp * tq

    def body(o_ref, do_ref, d_ref):
        for g in range(grp):
            cs = slice(g * HEAD, (g + 1) * HEAD)
            dd = jnp.sum(o_ref[:, cs].astype(f32) * do_ref[:, cs].astype(f32), axis=-1, keepdims=True)
            d_ref[0, g * tq:(g + 1) * tq, :] = jnp.broadcast_to(dd, (tq, HEAD))

    qb = pl.BlockSpec((tq, grp * HEAD), lambda h, i: (i, h))
    return _pallas_call(
        body, out_shape=_sds((KV_HEADS, grp * s, HEAD), f32), grid=(KV_HEADS, s // tq),
        in_specs=[qb, qb], out_specs=pl.BlockSpec((1, rows, HEAD), lambda h, i: (h, i, 0)),
        compiler_params=_cp(("parallel", "parallel")), name=f"flash_delta_{s}",
    )(o, do)


def _flash_bwd(q, k, v, o, do, lse):
    s, t = q.shape
    grp = t // KV_WIDTH
    tq, tk = _flash_tiles(s)
    nq, nkv = s // tq, s // tk
    rows = grp * tq
    delta = _flash_delta(o, do, t)

    def body(q_ref, k_ref, v_ref, do_ref, lse_ref, delta_ref, dq_ref, dk_ref, dv_ref, dq_acc, dk_acc, dv_acc):
        kj, qi = pl.program_id(1), pl.program_id(2)

        @pl.when(qi == 0)
        def _():
            dk_acc[...] = jnp.zeros_like(dk_acc)
            dv_acc[...] = jnp.zeros_like(dv_acc)

        kk, vv = k_ref[...], v_ref[...]

        def products(g):
            qg, dog = q_ref[:, g * HEAD:(g + 1) * HEAD], do_ref[:, g * HEAD:(g + 1) * HEAD]
            return qg, dog, _dot(qg, kk, 1, 1), _dot(dog, vv, 1, 1)

        ahead = products(0)
        dv_sum = dk_sum = None
        terms = []
        for g in range(grp):
            qg, dog, sc, dp = ahead
            if g + 1 < grp:
                ahead = products(g + 1)
            head_rows = slice(g * tq, (g + 1) * tq)
            p = jnp.exp2((sc - lse_ref[0, head_rows, 0:1]).astype(bf16))
            ds = p * (dp - delta_ref[0, head_rows, 0:1]).astype(bf16)
            dv_g, dk_g = _dot(p, dog, 0, 0), _dot(ds, qg, 0, 0)
            dv_sum = dv_g if dv_sum is None else dv_sum + dv_g
            dk_sum = dk_g if dk_sum is None else dk_sum + dk_g
            terms.append(_dot(ds, kk, 1, 0))
        dv_acc[...] += dv_sum
        dk_acc[...] += dk_sum
        mine = pl.ds(pl.multiple_of(qi * rows, rows), rows)
        term = jnp.concatenate(terms, axis=0)

        @pl.when(kj == 0)
        def _():
            dq_acc[mine, :] = term

        @pl.when(kj > 0)
        def _():
            dq_acc[mine, :] += term

        @pl.when(kj == nkv - 1)
        def _():
            total = dq_acc[mine, :]
            for g in range(grp):
                dq_ref[:, g * HEAD:(g + 1) * HEAD] = total[g * tq:(g + 1) * tq] * SCALE

        @pl.when(qi == nq - 1)
        def _():
            dk_ref[...] = dk_acc[...] * (1.0 / LOG2E)
            dv_ref[...] = dv_acc[...]

    qb = pl.BlockSpec((tq, grp * HEAD), lambda h, j, i: (i, h))
    kb = pl.BlockSpec((tk, HEAD), lambda h, j, i: (j, h))
    lb = pl.BlockSpec((1, rows, HEAD), lambda h, j, i: (h, i, 0))
    dqb = pl.BlockSpec((tq, grp * HEAD), lambda h, j, i: (jnp.where(j == nkv - 1, i, 0), h))
    return _pallas_call(
        body, out_shape=(_sds((s, t), f32), _sds((s, KV_WIDTH), f32), _sds((s, KV_WIDTH), f32)), grid=(KV_HEADS, nkv, nq),
        in_specs=[qb, kb, kb, qb, lb, lb], out_specs=(dqb, kb, kb),
        scratch_shapes=[pltpu.VMEM((nq * rows, HEAD), f32), pltpu.VMEM((tk, HEAD), f32), pltpu.VMEM((tk, HEAD), f32)],
        compiler_params=_cp(("parallel", "arbitrary", "arbitrary")), name=f"flash_bwd_{s}",
    )(q, k, v, do, lse, delta)


def _mem_heads(z_ref, kv_ref, gq_ref, gk_ref, h):
    cs = slice(h * HEAD, (h + 1) * HEAD)
    xv = z_ref[:, cs]
    r = lax.rsqrt(jnp.mean(xv * xv, axis=-1, keepdims=True) + EPS)
    xh = xv * r
    kx = kv_ref[:, cs]
    rk = lax.rsqrt(jnp.mean(kx * kx, axis=-1, keepdims=True) + EPS)
    kn = (kx * rk * gk_ref[...]).astype(bf16)
    vv = kv_ref[:, MEM_WIDTH + h * HEAD:MEM_WIDTH + (h + 1) * HEAD].astype(bf16)
    qn = (xh * gq_ref[...]).astype(bf16)
    sc = _dot(qn, kn, 1, 1) * SCALE
    e = jnp.exp(sc - jnp.max(sc, axis=-1, keepdims=True))
    p = e / jnp.sum(e, axis=-1, keepdims=True)
    return cs, r, xh, qn, kn, vv, p


def _mem_fwd(z, qblk, kv, g_mq, g_mk, cat):
    s = z.shape[0]
    nm = kv.shape[0]
    tr = _pick(s, (512, 256, 128))
    oblk = cat.shape[1] // MEM_WIDTH - 1

    def body(z_ref, kv_ref, gq_ref, gk_ref, cat_ref, o_ref):
        for h in range(MEM_HEADS):
            cs, _, _, _, _, vv, p = _mem_heads(z_ref, kv_ref, gq_ref, gk_ref, h)
            o_ref[:, cs] = _dot(p.astype(bf16), vv, 1, 0).astype(bf16)

    vec = pl.BlockSpec((1, HEAD), lambda i: (0, 0))
    return _pallas_call(
        body, out_shape=_sds(cat.shape, bf16), grid=(s // tr,),
        in_specs=[pl.BlockSpec((tr, MEM_WIDTH), lambda i: (i, qblk)), pl.BlockSpec((nm, 2 * MEM_WIDTH), lambda i: (0, 0)), vec, vec,
                  pl.BlockSpec(memory_space=pl.ANY)],
        out_specs=pl.BlockSpec((tr, MEM_WIDTH), lambda i: (i, oblk)),
        input_output_aliases={4: 0},
        compiler_params=_cp(("parallel",)), name=f"mem_fwd_{s}_{qblk}",
    )(z, kv, g_mq, g_mk, cat)


def _mem_bwd(z, qblk, kv, g_mq, g_mk, dcat, dz):
    s = z.shape[0]
    nm = kv.shape[0]
    tr = _pick(s, (512, 256, 128))
    dblk = dcat.shape[1] // MEM_WIDTH - 1

    def body(z_ref, kv_ref, gq_ref, gk_ref, dm_ref, dzin_ref, dz_ref, dkn_ref, dv_ref, dgq_ref):
        @pl.when(pl.program_id(0) == 0)
        def _():
            dkn_ref[...] = jnp.zeros_like(dkn_ref)
            dv_ref[...] = jnp.zeros_like(dv_ref)
            dgq_ref[...] = jnp.zeros_like(dgq_ref)

        for h in range(MEM_HEADS):
            cs, r, xh, qn, kn, vv, p = _mem_heads(z_ref, kv_ref, gq_ref, gk_ref, h)
            dm = dm_ref[:, cs]
            dv_ref[:, cs] += _dot(p.astype(bf16), dm, 0, 0)
            dp = _dot(dm, vv, 1, 1)
            ds = (p * (dp - jnp.sum(dp * p, axis=-1, keepdims=True)) * SCALE).astype(bf16)
            dqn = _dot(ds, kn, 1, 0)
            dkn_ref[:, cs] += _dot(ds, qn, 0, 0)
            dgq_ref[...] += jnp.sum(dqn * xh, axis=0, keepdims=True)
            gy = dqn * gq_ref[...]
            dz_ref[:, cs] = (r * (gy - xh * jnp.mean(gy * xh, axis=-1, keepdims=True))).astype(bf16)

    vec = pl.BlockSpec((1, HEAD), lambda i: (0, 0))
    acc = pl.BlockSpec((nm, MEM_WIDTH), lambda i: (0, 0))
    return _pallas_call(
        body, out_shape=(_sds(dz.shape, bf16), _sds((nm, MEM_WIDTH), f32), _sds((nm, MEM_WIDTH), f32), _sds((1, HEAD), f32)),
        grid=(s // tr,),
        in_specs=[pl.BlockSpec((tr, MEM_WIDTH), lambda i: (i, qblk)), pl.BlockSpec((nm, 2 * MEM_WIDTH), lambda i: (0, 0)), vec, vec,
                  pl.BlockSpec((tr, MEM_WIDTH), lambda i: (i, dblk)), pl.BlockSpec(memory_space=pl.ANY)],
        out_specs=(pl.BlockSpec((tr, MEM_WIDTH), lambda i: (i, qblk)), acc, acc, vec),
        input_output_aliases={5: 0},
        compiler_params=_cp(("arbitrary",)), name=f"mem_bwd_{s}_{qblk}",
    )(z, kv, g_mq, g_mk, dcat, dz)


def _mem_kv_bwd(kv, g_mk, dkn, dv):
    nm = kv.shape[0]

    def body(kv_ref, gk_ref, dkn_ref, dv_ref, dkv_ref, dgk_ref):
        dgk = jnp.zeros((1, HEAD), f32)
        for h in range(MEM_HEADS):
            cs = slice(h * HEAD, (h + 1) * HEAD)
            kx = kv_ref[:, cs]
            rk = lax.rsqrt(jnp.mean(kx * kx, axis=-1, keepdims=True) + EPS)
            kh = kx * rk
            dkn_h = dkn_ref[:, cs]
            dgk = dgk + jnp.sum(dkn_h * kh, axis=0, keepdims=True)
            gy = dkn_h * gk_ref[...]
            dkv_ref[:, cs] = (rk * (gy - kh * jnp.mean(gy * kh, axis=-1, keepdims=True))).astype(bf16)
        dkv_ref[:, MEM_WIDTH:] = dv_ref[...].astype(bf16)
        dgk_ref[...] = dgk

    return _pallas_call(
        body, out_shape=(_sds((nm, 2 * MEM_WIDTH), bf16), _sds((1, HEAD), f32)),
        compiler_params=pltpu.CompilerParams(vmem_limit_bytes=VMEM_LIMIT), name=f"mem_kv_bwd_{nm}",
    )(kv, g_mk, dkn, dv)


def _ffn_up(h2, g):
    m, kdim = h2.shape
    nd, _, n = g.shape
    half = nd // 2
    assert n % 128 == 0 and h2.dtype == bf16 and g.dtype == bf16
    tm = _pick(m, (512, 256, 128))

    def body(a_ref, wg_ref, wu_ref, gate_ref, up_ref, act_ref):
        av = a_ref[...]
        gt = _dot(av, wg_ref[0], 1, 0)
        up = _dot(av, wu_ref[0], 1, 0)
        gate_ref[...] = gt.astype(bf16)
        up_ref[...] = up.astype(bf16)
        act_ref[...] = (gt * jax.nn.sigmoid(gt) * up).astype(bf16)

    out = pl.BlockSpec((tm, n), lambda i, j: (i, j))
    return _pallas_call(
        body, out_shape=tuple(_sds((m, half * n), bf16) for _ in range(3)), grid=(m // tm, half),
        in_specs=[pl.BlockSpec((tm, kdim), lambda i, j: (i, 0)), pl.BlockSpec((1, kdim, n), lambda i, j: (j, 0, 0)),
                  pl.BlockSpec((1, kdim, n), lambda i, j: (j + half, 0, 0))],
        out_specs=(out, out, out),
        compiler_params=_cp(("parallel", "arbitrary")), name=f"ffn_up_{m}x{kdim}x{half * n}",
    )(h2, g, g)


def _ffn_down_bwd(dx, w_dn, gate, up):
    m, d = dx.shape
    ff = w_dn.shape[0]
    tm = _pick(m, (512, 256, 128))
    tf = _pick(ff, (1408, 1024, 512, 256, 128))

    nsub = 2 if tm % 32 == 0 else 1
    rs = tm // nsub

    def body(dx_ref, w_ref, g_ref, u_ref, dg_ref, du_ref):
        wv = w_ref[...]
        da = _dot(dx_ref[0:rs, :], wv, 1, 1)
        for r in range(nsub):
            da_next = _dot(dx_ref[(r + 1) * rs:(r + 2) * rs, :], wv, 1, 1) if r + 1 < nsub else None
            mine = slice(r * rs, (r + 1) * rs)
            gt = g_ref[mine, :].astype(f32)
            sg = jax.nn.sigmoid(gt)
            dg_ref[mine, :] = (da * u_ref[mine, :].astype(f32) * sg * (1.0 + gt * (1.0 - sg))).astype(bf16)
            du_ref[mine, :] = (da * gt * sg).astype(bf16)
            da = da_next

    tile = pl.BlockSpec((tm, tf), lambda i, j: (i, j))
    return _pallas_call(
        body, out_shape=(_sds((m, ff), bf16), _sds((m, ff), bf16)), grid=(m // tm, ff // tf),
        in_specs=[pl.BlockSpec((tm, d), lambda i, j: (i, 0)), pl.BlockSpec((tf, d), lambda i, j: (j, 0)), tile, tile],
        out_specs=(tile, tile),
        compiler_params=_cp(("parallel", "arbitrary")), name=f"ffn_down_bwd_{m}x{d}x{ff}",
    )(dx, w_dn, gate, up)


def _loss_head(y, target):
    s, d = y.shape
    tr = _pick(s, (512, 256, 128))

    def body(y_ref, t_ref, l_ref, dy_ref, dyb_ref):
        @pl.when(pl.program_id(0) == 0)
        def _():
            l_ref[...] = jnp.zeros_like(l_ref)

        err = y_ref[...] - t_ref[...]
        l_ref[...] += 0.5 * jnp.sum(jnp.mean(err * err, axis=-1, keepdims=True), axis=0, keepdims=True)
        dy = err * (1.0 / d)
        dy_ref[...] = dy
        dyb_ref[...] = dy.astype(bf16)

    row = pl.BlockSpec((tr, d), lambda i: (i, 0))
    return _pallas_call(
        body, out_shape=(_sds((1, HEAD), f32), _sds((s, d), f32), _sds((s, d), bf16)), grid=(s // tr,),
        in_specs=[row, row], out_specs=(pl.BlockSpec((1, HEAD), lambda i: (0, 0)), row, row),
        compiler_params=_cp(("arbitrary",)), name=f"loss_{s}x{d}",
    )(y, target)


def _place():
    return lax.axis_index("x"), lax.axis_index("y"), lax.axis_index("c")


def _tag(arrays):
    return "_".join("x".join(str(dd) for dd in a.shape) for a in arrays)


def _all_gather(shards):
    nw = len(shards)
    hbm = pl.BlockSpec(memory_space=pl.ANY)

    def body(*refs):
        x_refs, out_refs = refs[:nw], refs[nw:2 * nw]
        send_sems, recv_sems, local_sems = refs[2 * nw:]
        x, y, c = _place()
        me, sibling = (x, y, c), (x, y, 1 - c)
        chips = [(1 - x, y), (x, 1 - y), (1 - x, 1 - y)]

        def slot(w, place):
            px, py, pc = place
            return out_refs[w].at[4 * px + 2 * py + pc]

        def copy(k, w, block_of, to, from_input=False):
            return pltpu.make_async_remote_copy(
                src_ref=x_refs[w] if from_input else slot(w, block_of), dst_ref=slot(w, block_of),
                send_sem=send_sems.at[k, w], recv_sem=recv_sems.at[k, w], device_id=to, device_id_type=MESH)

        mine = [pltpu.make_async_copy(x_refs[w], slot(w, me), local_sems.at[w]) for w in range(nw)]
        for cp in mine:
            cp.start()
        first = []
        for w in range(nw):
            first.append(copy(0, w, me, sibling, from_input=True))
            first += [copy(1 + j, w, me, (*chip, c), from_input=True) for j, chip in enumerate(chips)]
        for cp in first:
            cp.start()
        passed = []
        for w in range(nw):
            for j, chip in enumerate(chips):
                copy(1 + j, w, (*chip, c), me).wait_recv()
                fwd = copy(4 + j, w, (*chip, c), sibling)
                fwd.start()
                passed.append(fwd)
        for w in range(nw):
            copy(0, w, sibling, me).wait_recv()
            for j, chip in enumerate(chips):
                copy(4 + j, w, (*chip, 1 - c), me).wait_recv()
        for cp in first + passed:
            cp.wait_send()
        for cp in mine:
            cp.wait()

    return _pallas_call(
        body, out_shape=tuple(_sds((N_DEV,) + a.shape, a.dtype) for a in shards), in_specs=[hbm] * nw, out_specs=tuple([hbm] * nw),
        scratch_shapes=[pltpu.SemaphoreType.DMA((7, nw)), pltpu.SemaphoreType.DMA((7, nw)), pltpu.SemaphoreType.DMA((nw,))],
        name=f"all_gather_{_tag(shards)}_{jnp.dtype(shards[0].dtype).name}",
    )(*shards)


def _pair_sum(grad, got):
    nd, a, b = grad.shape
    nchip = nd // 2
    ta = _pick(a, (1024, 704, 512, 352, 256, 128, 64, 32, 16))

    def my_chip():
        return 2 * lax.axis_index("x") + lax.axis_index("y")

    def body(a_ref, b_ref, o_ref, land_ref):
        tot = (a_ref[...].astype(f32) + b_ref[...].astype(f32)).astype(o_ref.dtype)
        o_ref[...] = tot

        @pl.when(pl.program_id(1) == my_chip())
        def _():
            land_ref[...] = tot

    return _pallas_call(
        body, out_shape=(_sds(got.shape, grad.dtype), _sds(got.shape, grad.dtype)), grid=(a // ta, nchip),
        in_specs=[pl.BlockSpec((None, ta, b), lambda i, k: (2 * k + lax.axis_index("c"), i, 0)),
                  pl.BlockSpec((None, ta, b), lambda i, k: (k, i, 0))],
        out_specs=(pl.BlockSpec((None, ta, b), lambda i, k: (k, i, 0)),
                   pl.BlockSpec((None, ta, b), lambda i, k: (my_chip(), i, 0))),
        compiler_params=_cp(("parallel", "arbitrary")), name=f"pair_sum_{a}x{b}",
    )(grad, got)


_HBM = pl.BlockSpec(memory_space=pltpu.HBM)
_SEM = pl.BlockSpec(memory_space=pltpu.SEMAPHORE)
_ANY = pl.BlockSpec(memory_space=pl.ANY)
_DATAFLOW = pltpu.SideEffectType.DATAFLOW_SIDE_EFFECTING


def _in_hbm(a):
    return pltpu.with_memory_space_constraint(a, pltpu.HBM)


def _exchange_begin(bufs, nw, route, after, copies_of, n_copies, name):
    nb = len(bufs)

    def body(*refs):
        send_sems, recv_sems = refs[nb + 2], refs[nb + 3]
        for w in range(nw):
            for k, (src, dst, to) in enumerate(copies_of(w, refs[:nb])):
                pltpu.make_async_remote_copy(src_ref=src, dst_ref=dst, send_sem=send_sems.at[k * nw + w],
                                             recv_sem=recv_sems.at[k * nw + w], device_id=to, device_id_type=MESH).start()

    out = _pallas_call(
        body, name=name,
        out_shape=(pltpu.SemaphoreType.DMA((n_copies * nw,)), pltpu.SemaphoreType.DMA((n_copies * nw,)),
                   *[pltpu.HBM(a.shape, a.dtype) for a in bufs], pltpu.HBM(route.shape, route.dtype)),
        in_specs=[_HBM] * (nb + 1) + [_ANY], out_specs=(_SEM, _SEM, *[_HBM] * (nb + 1)),
        input_output_aliases={i: 2 + i for i in range(nb + 1)},
        compiler_params=pltpu.CompilerParams(has_side_effects=_DATAFLOW),
    )(*[_in_hbm(a) for a in bufs], _in_hbm(route), after)
    return (out[0], out[1], out[2:2 + nb], nw), out[2 + nb]


def _exchange_end(handle, after, copies_of, n_copies, name):
    send_sems, recv_sems, thru, nw = handle
    nb = len(thru)

    def body(*refs):
        send_sems, recv_sems = refs[nb], refs[nb + 1]
        for w in range(nw):
            for k, (src, dst, to) in enumerate(copies_of(w, refs[:nb])):
                cp = pltpu.make_async_remote_copy(src_ref=src, dst_ref=dst, send_sem=send_sems.at[k * nw + w],
                                                  recv_sem=recv_sems.at[k * nw + w], device_id=to, device_id_type=MESH)
                cp.wait_send()
                cp.wait_recv()

    out = _pallas_call(
        body, name=name, out_shape=tuple(pltpu.HBM(a.shape, a.dtype) for a in thru),
        in_specs=[_HBM] * nb + [_SEM, _SEM, _ANY], out_specs=tuple([_HBM] * nb),
        input_output_aliases={i: i for i in range(nb)},
        compiler_params=pltpu.CompilerParams(has_side_effects=_DATAFLOW),
    )(*thru, send_sems, recv_sems, after)
    return list(out)


def _my_slot():
    return 4 * lax.axis_index("x") + 2 * lax.axis_index("y") + lax.axis_index("c")


def _shard_into_land(w_all, idx):
    _, a, b = w_all.shape
    ta = next(cc for cc in (1024, 704, 512, 352, 256, 128, 64, 32, 16) if a % cc == 0 and (cc * b * 4 <= 2 ** 21 or cc == 16))

    def body(w_ref, o_ref):
        o_ref[...] = w_ref[...].astype(bf16)

    return _pallas_call(
        body, out_shape=_sds((N_DEV, a, b), bf16), grid=(a // ta,),
        in_specs=[pl.BlockSpec((None, ta, b), lambda i: (idx, i, 0))],
        out_specs=pl.BlockSpec((None, ta, b), lambda i: (_my_slot(), i, 0)),
        compiler_params=_cp(("parallel",)), name=f"shard_into_land_{a}x{b}_{idx}",
    )(w_all)


def _gather_copies(w, land_refs):
    x, y, c = _place()
    blk = land_refs[w].at[4 * x + 2 * y + c]
    return [(blk, blk, to) for to in ((x, y, 1 - c), (1 - x, y, c), (x, 1 - y, c), (1 - x, 1 - y, c))]


def _gather_begin(lands, route, after, tag):
    return _exchange_begin(lands, len(lands), route, after, _gather_copies, 4, f"gather_begin_{tag}")


def _gather_end(handle, after, tag):
    return _exchange_end(handle, after, _gather_copies, 4, f"gather_end_{tag}")


def _gather_pass_on(lands):
    nw = len(lands)

    def body(*refs):
        l_refs = refs[nw:2 * nw]
        send_sems, recv_sems = refs[2 * nw:]
        x, y, c = _place()
        copies = []
        for w in range(nw):
            for j, (px, py) in enumerate([(1 - x, y), (x, 1 - y), (1 - x, 1 - y)]):
                blk = l_refs[w].at[4 * px + 2 * py + c]
                copies.append(pltpu.make_async_remote_copy(
                    src_ref=blk, dst_ref=blk, send_sem=send_sems.at[j, w], recv_sem=recv_sems.at[j, w],
                    device_id=(x, y, 1 - c), device_id_type=MESH))
        for cp in copies:
            cp.start()
        for cp in copies:
            cp.wait_send()
        for w in range(nw):
            for j, (px, py) in enumerate([(1 - x, y), (x, 1 - y), (1 - x, 1 - y)]):
                blk = l_refs[w].at[4 * px + 2 * py + (1 - c)]
                pltpu.make_async_remote_copy(src_ref=blk, dst_ref=blk, send_sem=send_sems.at[j, w], recv_sem=recv_sems.at[j, w],
                                             device_id=(x, y, 1 - c), device_id_type=MESH).wait_recv()

    return _pallas_call(
        body, out_shape=tuple(_sds(a.shape, a.dtype) for a in lands), in_specs=[_ANY] * nw, out_specs=tuple([_ANY] * nw),
        input_output_aliases={w: w for w in range(nw)},
        scratch_shapes=[pltpu.SemaphoreType.DMA((3, nw)), pltpu.SemaphoreType.DMA((3, nw))],
        name=f"gather_pass_on_{_tag(lands)}",
    )(*lands)


def _pass_copies(w, land_refs):
    x, y, c = _place()
    blocks = [land_refs[w].at[4 * px + 2 * py + c] for px, py in ((1 - x, y), (x, 1 - y), (1 - x, 1 - y))]
    return [(blk, blk, (x, y, 1 - c)) for blk in blocks]


def _pass_begin(lands, route, after, tag):
    return _exchange_begin(lands, len(lands), route, after, _pass_copies, 3, f"pass_begin_{tag}")


def _pass_end(handle, after, tag):
    return _exchange_end(handle, after, _pass_copies, 3, f"pass_end_{tag}")


def _swap_copies(w, refs):
    x, y, c = _place()
    nw = len(refs) // 2
    return [(refs[w].at[2 * k + (1 - c)], refs[nw + w].at[k], (x, y, 1 - c)) for k in range(N_DEV // 2)]


def _swap_begin(grads, route, after, tag):
    lands = [lax.empty((N_DEV // 2,) + g.shape[1:], g.dtype) for g in grads]
    return _exchange_begin(list(grads) + lands, len(grads), route, after, _swap_copies, N_DEV // 2, f"swap_begin_{tag}")


def _swap_end(handle, after, tag):
    out = _exchange_end(handle, after, _swap_copies, N_DEV // 2, f"swap_end_{tag}")
    return out[:len(out) // 2], out[len(out) // 2:]


def _scatter_copies(w, refs):
    x, y, c = _place()
    nw = len(refs) // 2
    dst = refs[nw + w].at[2 * x + y]
    return [(refs[w].at[2 * px + py], dst, (px, py, c)) for px, py in ((1 - x, y), (x, 1 - y), (1 - x, 1 - y))]


def _scatter_begin(psums, lands, route, after, tag):
    return _exchange_begin(list(psums) + list(lands), len(psums), route, after, _scatter_copies, 3, f"scatter_begin_{tag}")


def _scatter_end(handle, after, tag):
    return _exchange_end(handle, after, _scatter_copies, 3, f"scatter_end_{tag}")


def _adamw(parts, w_all, m_all, v_all, l, carried):
    nparts, a, b = parts.shape
    nl = w_all.shape[0]
    ta = next(cc for cc in (1024, 704, 512, 352, 256, 128, 64, 32, 16, 8) if a % cc == 0 and (cc * b * 4 <= 2 ** 20 or cc == 8))
    c1 = 1.0 / (1.0 - ADAM_B1 ** ADAM_STEP)
    c2 = 1.0 / (1.0 - ADAM_B2 ** ADAM_STEP)

    def body(p_ref, w_ref, m_ref, v_ref, *rest):
        g_out, d_out, m_out, v_out = rest[-4:]
        g = p_ref[0].astype(f32)
        for k in range(1, nparts):
            g = g + p_ref[k].astype(f32)
        m_new = ADAM_B1 * m_ref[...] + (1.0 - ADAM_B1) * g
        v_new = ADAM_B2 * v_ref[...] + (1.0 - ADAM_B2) * (g * g)
        m_hat = m_new * c1
        v_hat = v_new * c2
        g_out[...] = g
        d_out[...] = -ADAM_LR * (m_hat / (jnp.sqrt(v_hat) + ADAM_EPS) + ADAM_WD * w_ref[...])
        m_out[...] = m_new
        v_out[...] = v_new

    one = pl.BlockSpec((None, ta, b), lambda i: (l, i, 0))
    keep = [] if carried is None else [pl.BlockSpec(memory_space=pl.ANY)] * 4
    return _pallas_call(
        body, out_shape=tuple(_sds((nl, a, b), f32) for _ in range(4)), grid=(a // ta,),
        in_specs=[pl.BlockSpec((nparts, ta, b), lambda i: (0, i, 0)), one, one, one] + keep, out_specs=(one, one, one, one),
        input_output_aliases=({} if carried is None else {4 + q: q for q in range(4)}),
        compiler_params=_cp(("parallel",)), name=f"adamw_{nparts}x{nl}x{a}x{b}_{l}{'' if carried is None else '_carried'}",
    )(parts, w_all, m_all, v_all, *(carried or ()))


def _to_flat(arrays):
    flat = jnp.concatenate([a.reshape(-1).astype(f32) for a in arrays])
    rows = -(-flat.shape[0] // (8 * LANES)) * 8
    return jnp.pad(flat, (0, rows * LANES - flat.shape[0])).reshape(rows, LANES)


def _from_flat(flat, shapes):
    flat = flat.reshape(-1)
    out, off = [], 0
    for shp in shapes:
        n = 1
        for dd in shp:
            n *= dd
        out.append(flat[off:off + n].reshape(shp))
        off += n
    return out


def kernel(x, mem, g_mix, g_ffn, w_in_a, g_v_a, w_spatial, b_spatial, w_in_b, g_q_b, g_k_b, g_mem, w_mem_kv, g_mq, g_mk, w_out, w_gate_up, w_down, loss_target, m_g_mix, m_g_ffn, m_w_in_a, m_g_v_a, m_w_spatial, m_b_spatial, m_w_in_b, m_g_q_b, m_g_k_b, m_g_mem, m_w_mem_kv, m_g_mq, m_g_mk, m_w_out, m_w_gate_up, m_w_down, v_g_mix, v_g_ffn, v_w_in_a, v_g_v_a, v_w_spatial, v_b_spatial, v_w_in_b, v_g_q_b, v_g_k_b, v_g_mem, v_w_mem_kv, v_g_mq, v_g_mk, v_w_out, v_w_gate_up, v_w_down):
    given = dict(locals())
    depth = g_mix.shape[0]
    s, d = x.shape[1], x.shape[2]
    nm = mem.shape[1]
    t = d - MEM_WIDTH
    ff = w_down.shape[1] * N_DEV
    x0 = x.reshape(s, d)
    mem0 = mem.reshape(nm, d)
    target = loss_target.reshape(s, d)
    tables = _rope_tables(s)

    big_names = ("w_in", "w_mem_kv", "w_out", "w_gate_up", "w_down")

    def stacked_key(name, l):
        if name == "w_in":
            return ("w_in_a" if l % 2 == 0 else "w_in_b"), l // 2
        return name, l

    n_mix = 3

    def own_shards(l):
        return [_shard_into_land(given[key], idx) for key, idx in (stacked_key(name, l) for name in big_names)]

    def gather_start(l, route, after):
        lands = cast_shards[l]
        h_mix, route = _gather_begin(lands[:n_mix], route, after, f"mix{l}")
        h_ffn, route = _gather_begin(lands[n_mix:], route, after, f"ffn{l}")
        return h_mix, h_ffn, route

    def gather_finish(handle, after, tag):
        return list(_gather_pass_on(_gather_end(handle, after, tag)))

    saved = []
    xc = x0
    cast_shards = {0: own_shards(0)}
    h_in, _ = _gather_begin(cast_shards[0][:1], g_mix[0].reshape(1, d), mem0, "in0")
    for l in range(1, depth):
        cast_shards[l] = own_shards(l)
    w_mix = gather_finish(h_in, cast_shards[depth - 1][0], "in0")
    w_ffn = None
    for l in range(depth):
        is_a = l % 2 == 0
        g_in = w_mix[0]
        qblk = (N_DEV * g_in.shape[2] - MEM_WIDTH) // MEM_WIDTH

        gm_row, gf_row, gmem_row = g_mix[l].reshape(1, d), g_ffn[l].reshape(1, d), g_mem[l].reshape(1, d)
        gmq_row, gmk_row = g_mq[l].reshape(1, HEAD), g_mk[l].reshape(1, HEAD)
        if l == 0:
            h_rest, gm_row = _gather_begin(cast_shards[0][1:n_mix], gm_row, g_in, "rest0")
            h_ffn, gm_row = _gather_begin(cast_shards[0][n_mix:], gm_row, g_in, "ffn0")
        h = _rms_fwd(xc, gm_row)
        z = _mm_cols_fwd(h, g_in, f32)
        if l == 0:
            w_mix = w_mix + gather_finish(h_rest, z, "rest0")
        g_kv, g_out = w_mix[1:]
        w_kv, w_o = (g.reshape(-1, g.shape[2]) for g in (g_kv, g_out))
        if is_a:
            ia = l // 2
            mix = dict(g_v=g_v_a[ia].reshape(1, t), w_s=w_spatial[ia], b_t=b_spatial[ia].T)
            cat = _gmlp_fwd(z, mix["g_v"], mix["w_s"], mix["b_t"])
        else:
            ib = l // 2
            mix = dict(g_q=g_q_b[ib].reshape(1, HEAD), g_k=g_k_b[ib].reshape(1, HEAD))
            q, k, v = _attn_prep_fwd(z, mix["g_q"], mix["g_k"], tables, t)
            cat, lse = _flash_fwd(q, k, v)
            mix.update(q=q, k=k, v=v, lse=lse)
        hm = _rms_fwd(mem0, gmem_row)
        kv = _matmul(hm, w_kv)
        cat = _mem_fwd(z, qblk, kv, gmq_row, gmk_row, cat)
        if l > 0:
            h_pass, cat = _pass_begin(_gather_end(h_ffn, cat, f"ffn{l}"), cat, z, f"ffn{l}")
        x1 = _matmul(cat, w_o, res=xc)
        w_ffn = gather_finish(h_ffn, x1, "ffn0") if l == 0 else _pass_end(h_pass, x1, f"ffn{l}")
        if l + 1 < depth:
            next_mix, next_ffn, gf_row = gather_start(l + 1, gf_row, w_ffn[0])
        h2 = _rms_fwd(x1, gf_row)
        g_gu, g_dn = w_ffn
        w_dn = g_dn.reshape(-1, g_dn.shape[2])
        gate, up, act = _ffn_up(h2, g_gu)
        x2 = _matmul(act, w_dn, res=x1)
        saved.append(dict(x=xc, h=h, z=z, mix=mix, cat=cat, hm=hm, kv=kv, x1=x1, h2=h2, gate=gate, up=up, act=act, qblk=qblk,
                          w=(g_in, w_kv, w_o, g_gu, w_dn), rows=(gm_row, gf_row, gmem_row, gmq_row, gmk_row)))
        if l + 1 < depth:
            w_mix = gather_finish(next_mix, x2, f"mix{l + 1}")
            h_ffn = next_ffn
        xc = x2

    loss_row, dy, dy_b = _loss_head(xc, target)
    loss = lax.psum(loss_row[0, 0], ("x", "y", "c"))

    small = {n: [None] * given[n].shape[0] for n in ("g_mix", "g_ffn", "g_v_a", "w_spatial", "b_spatial", "g_q_b", "g_k_b", "g_mem", "g_mq", "g_mk")}
    big_out = {}

    def scatter_start(swap, route, after, tag):
        grads, got = _swap_end(swap, after, tag)
        sums = [_pair_sum(g, r) for g, r in zip(grads, got)]
        return _scatter_begin([p for p, _ in sums], [q for _, q in sums], route, after, tag)

    def scatter_finish(handle, names, l, after, tag):
        arrived = _scatter_end(handle, after, tag)[len(names):]
        for name, parts in zip(names, arrived):
            key, idx = stacked_key(name, l)
            big_out[key] = _adamw(parts, given[key], given["m_" + key], given["v_" + key], idx, big_out.get(key))

    pend_mix = None
    dx, dx_b = dy, dy_b
    for l in reversed(range(depth)):
        sv = saved[l]
        is_a = l % 2 == 0
        g_in, w_kv, w_o, g_gu, w_dn = sv["w"]
        gm_row, gf_row, gmem_row, gmq_row, gmk_row = sv["rows"]
        mix = sv["mix"]
        dw_dn = _matmul(sv["act"], dx_b, ta=True, out_dtype=bf16)
        dgu = _ffn_down_bwd(dx_b, w_dn, sv["gate"], sv["up"])
        dw_gu = _mm_cols_wgrad(sv["h2"], dgu, g_gu.shape[2])
        swap, dgate = _swap_begin([dw_gu, dw_dn.reshape(N_DEV, -1, d)], dgu[0], dx_b, f"ffn{l}")
        dh2 = _mm_cols_dgrad((dgate, dgu[1]), g_gu)
        dx1, dx1_b, dgf = _rms_bwd(sv["x1"], gf_row, dh2, dx)
        small["g_ffn"][l] = dgf.reshape(d)
        pend_ffn, dx1_b = scatter_start(swap, dx1_b, dx1, f"ffn{l}")
        if pend_mix is not None:
            scatter_finish(*pend_mix, dx1_b, f"mix{l + 1}")
        dw_o = _matmul(sv["cat"], dx1_b, ta=True, out_dtype=bf16)
        dcat = _matmul(dx1_b, w_o, tb=True, out_dtype=bf16)
        if is_a:
            dz, dws, dbt, dgv = _gmlp_bwd(sv["z"], mix["g_v"], mix["w_s"], mix["b_t"], dcat)
            small["w_spatial"][l // 2], small["b_spatial"][l // 2], small["g_v_a"][l // 2] = dws, dbt.T, dgv.reshape(t)
        else:
            dq, dk, dv = _flash_bwd(mix["q"], mix["k"], mix["v"], sv["cat"], dcat, mix["lse"])
            dz, dgq, dgk = _attn_prep_bwd(sv["z"], mix["g_q"], mix["g_k"], tables, dq, dk, dv, t)
            small["g_q_b"][l // 2], small["g_k_b"][l // 2] = dgq.reshape(HEAD), dgk.reshape(HEAD)
        dz, dkn, dvm, dgmq = _mem_bwd(sv["z"], sv["qblk"], sv["kv"], gmq_row, gmk_row, dcat, dz)
        dkv, dgmk = _mem_kv_bwd(sv["kv"], gmk_row, dkn, dvm)
        dw_kv = _matmul(sv["hm"], dkv, ta=True, out_dtype=bf16)
        dhm = _matmul(dkv, w_kv, tb=True)
        small["g_mem"][l] = _rms_bwd(mem0, gmem_row, dhm, None).reshape(d)
        small["g_mq"][l] = dgmq.reshape(HEAD)
        small["g_mk"][l] = dgmk.reshape(HEAD)
        dw_in = _mm_cols_wgrad(sv["h"], dz, g_in.shape[2])
        swap, dz = _swap_begin([dw_in] + [dw.reshape(N_DEV, -1, dw.shape[1]) for dw in (dw_kv, dw_o)], dz, dx1, f"mix{l}")
        dh = _mm_cols_dgrad(dz, g_in)
        dx, dx_b, dgm = _rms_bwd(sv["x"], gm_row, dh, dx1)
        small["g_mix"][l] = dgm.reshape(d)

        handle, dx_b = scatter_start(swap, dx_b, dx, f"mix{l}")
        pend_mix = (handle, big_names[:n_mix], l)
        scatter_finish(pend_ffn, big_names[n_mix:], l, dx_b, f"ffn{l}")
    small_names = tuple(small)
    small_grads = [jnp.stack(small[n]) for n in small_names]
    small_shapes = [g.shape for g in small_grads]
    (all_parts,) = _all_gather([_to_flat(small_grads)])
    souts = _adamw(all_parts, *[_to_flat([given[p + n] for n in small_names])[None] for p in ("", "m_", "v_")], 0, None)
    small_out = dict(zip(small_names, zip(*[_from_flat(flat, small_shapes) for flat in souts])))
    scatter_finish(*pend_mix, souts[0], "mix0")

    weights = ("g_mix", "g_ffn", "w_in_a", "g_v_a", "w_spatial", "b_spatial", "w_in_b", "g_q_b", "g_k_b", "g_mem", "w_mem_kv",
               "g_mq", "g_mk", "w_out", "w_gate_up", "w_down")
    results = {n: (small_out[n] if n in small_out else big_out[n]) for n in weights}
    grad_x = dx.reshape(1, s, d)
    return (loss, grad_x, *[results[n][kind] for kind in range(4) for n in weights])
```

```python
import functools

import jax
import jax.numpy as jnp
from jax import lax
from jax.experimental import pallas as pl
from jax.experimental.pallas import tpu as pltpu

f32 = jnp.float32
bf16 = jnp.bfloat16

HEAD = 128
CHUNK = 128
GRID_W = 64
MEM_HEADS = 4
KV_HEADS = 4
MEM_WIDTH = MEM_HEADS * HEAD
KV_WIDTH = KV_HEADS * HEAD
ROPE_THETA = 10000.0
ROPE_PAIRS = HEAD // 4
EPS = 1e-6
SCALE = HEAD ** -0.5
LOG2E = 1.4426950408889634
N_DEV = 8
LANES = 1024
VMEM_LIMIT = 56 * 1024 * 1024

ADAM_LR, ADAM_B1, ADAM_B2, ADAM_EPS, ADAM_WD, ADAM_STEP = 0.001, 0.9, 0.999, 1e-08, 0.01, 10

MESH = pl.DeviceIdType.MESH
_pallas_call = pl.pallas_call


def _pick(dim, cands):
    for c in cands:
        if dim % c == 0:
            return c
    return dim


def _cp(sem):
    return pltpu.CompilerParams(dimension_semantics=sem, vmem_limit_bytes=VMEM_LIMIT)


def _sds(shape, dtype):
    return jax.ShapeDtypeStruct(shape, dtype)


def _dot(a, b, ca, cb):
    return lax.dot_general(a, b, (((ca,), (cb,)), ((), ())), preferred_element_type=f32)


def _gelu(z):
    return 0.5 * z * (1.0 + lax.erf(z * 0.7071067811865476))


def _gelu_grad(z):
    return 0.5 * (1.0 + lax.erf(z * 0.7071067811865476)) + z * jnp.exp(-0.5 * z * z) * 0.3989422804014327


def _rot(x, sin_a, sin_b):
    return pltpu.roll(x, 96, 1) * sin_a + pltpu.roll(x, 32, 1) * sin_b


def _matmul(a, b, *, ta=False, tb=False, out_dtype=f32, res=None, tm=None, tn=None, tk=None):
    assert a.dtype == bf16 and b.dtype == bf16
    kdim, m = a.shape if ta else a.shape[::-1]
    n, k2 = b.shape if tb else b.shape[::-1]
    assert kdim == k2, (a.shape, b.shape, ta, tb)
    tm = tm or _pick(m, (1024, 1408, 512, 256, 128))
    tn = tn or _pick(n, (1024, 1408, 512, 256, 128))
    if tk is None:
        tk = kdim if kdim <= 2048 else _pick(kdim, (2816, 1024, 512, 256, 128))
    nk = kdim // tk
    ca, cb = (0 if ta else 1), (1 if tb else 0)
    has_res = res is not None

    def body(*refs):
        a_ref, b_ref = refs[0], refs[1]
        r_ref = refs[2] if has_res else None
        o_ref = refs[3] if has_res else refs[2]
        prod = _dot(a_ref[...], b_ref[...], ca, cb)
        if nk == 1:
            if has_res:
                prod = prod + r_ref[...]
            o_ref[...] = prod.astype(o_ref.dtype)
        else:
            acc = refs[-1]
            k = pl.program_id(2)

            @pl.when(k == 0)
            def _():
                acc[...] = prod

            @pl.when(k > 0)
            def _():
                acc[...] += prod

            @pl.when(k == nk - 1)
            def _():
                out = acc[...]
                if has_res:
                    out = out + r_ref[...]
                o_ref[...] = out.astype(o_ref.dtype)

    a_spec = pl.BlockSpec((tk, tm), lambda i, j, k: (k, i)) if ta else pl.BlockSpec((tm, tk), lambda i, j, k: (i, k))
    b_spec = pl.BlockSpec((tn, tk), lambda i, j, k: (j, k)) if tb else pl.BlockSpec((tk, tn), lambda i, j, k: (k, j))
    o_spec = pl.BlockSpec((tm, tn), lambda i, j, k: (i, j))
    in_specs = [a_spec, b_spec] + ([o_spec] if has_res else [])
    args = (a, b) + ((res,) if has_res else ())
    mode = ("t" if ta else "n") + ("t" if tb else "n")
    return _pallas_call(
        body, out_shape=_sds((m, n), out_dtype), grid=(m // tm, n // tn, nk),
        in_specs=in_specs, out_specs=o_spec,
        scratch_shapes=([pltpu.VMEM((tm, tn), f32)] if nk > 1 else []),
        compiler_params=_cp(("parallel", "parallel", "arbitrary")),
        name=f"mm_{mode}_{m}x{kdim}x{n}{'_res' if has_res else ''}_{jnp.dtype(out_dtype).name}",
    )(*args)


def _shards_per_step(n, pair_bytes=0):
    p = 2 if (n % 128 != 0 or 0 < 2 * pair_bytes <= VMEM_LIMIT // 2) else 1
    if 0 < 4 * pair_bytes <= VMEM_LIMIT // 2:
        p = 4
    assert (p * n) % 128 == 0 and N_DEV % p == 0
    return p


def _lane_pieces(v, p, n):
    return [v] if p == 1 else [v[:, q * n:(q + 1) * n] for q in range(p)]


def _mm_cols_fwd(a, g, out_dtype):
    m, kdim = a.shape
    nd, k2, n = g.shape
    assert kdim == k2 and a.dtype == bf16 and g.dtype == bf16
    p = _shards_per_step(n)
    tm = _pick(m, (1024, 512, 256, 128))

    def body(a_ref, g_ref, o_ref):
        av = a_ref[...]
        parts = [_dot(av, g_ref[q], 1, 0) for q in range(p)]
        out = parts[0] if p == 1 else jnp.concatenate(parts, axis=1)
        o_ref[...] = out.astype(o_ref.dtype)

    return _pallas_call(
        body, out_shape=_sds((m, nd * n), out_dtype), grid=(m // tm, nd // p),
        in_specs=[pl.BlockSpec((tm, kdim), lambda i, j: (i, 0)), pl.BlockSpec((p, kdim, n), lambda i, j: (j, 0, 0))],
        out_specs=pl.BlockSpec((tm, p * n), lambda i, j: (i, j)),
        compiler_params=_cp(("parallel", "arbitrary")), name=f"mm_cols_fwd_{m}x{kdim}x{nd * n}_{jnp.dtype(out_dtype).name}",
    )(a, g)


def _pick_part(refs, nparts, step, per):
    val = refs[0][...]
    for hh in range(1, nparts):
        val = jnp.where(step >= hh * per, refs[hh][...], val)
    return val


def _part_step(step, hh, per):
    return jnp.clip(step - hh * per, 0, per - 1)


def _mm_cols_dgrad(dz, g):
    nd, kdim, n = g.shape
    p = _shards_per_step(n, pair_bytes=2 * kdim * n * 2)
    nj = nd // p
    parts = dz if isinstance(dz, tuple) else (dz,)
    m = parts[0].shape[0]
    nn = sum(part.shape[1] for part in parts)
    assert nn == nd * n and all(part.dtype == bf16 for part in parts) and g.dtype == bf16
    tm = _pick(m, (512, 256, 128))
    nparts = len(parts)
    per = nj // nparts

    def body(*refs):
        g_ref, o_ref, acc = refs[nparts:]
        j = pl.program_id(1)
        tot = None
        for q, piece in enumerate(_lane_pieces(_pick_part(refs, nparts, j, per), p, n)):
            dd = _dot(piece, g_ref[q], 1, 1)
            tot = dd if tot is None else tot + dd

        @pl.when(j == 0)
        def _():
            acc[...] = tot

        @pl.when(j > 0)
        def _():
            acc[...] += tot

        @pl.when(j == nj - 1)
        def _():
            o_ref[...] = acc[...]

    return _pallas_call(
        body, out_shape=_sds((m, kdim), f32), grid=(m // tm, nj),
        in_specs=[pl.BlockSpec((tm, p * n), (lambda hh: lambda i, j: (i, _part_step(j, hh, per)))(hh)) for hh in range(nparts)]
        + [pl.BlockSpec((p, kdim, n), lambda i, j: (j, 0, 0))],
        out_specs=pl.BlockSpec((tm, kdim), lambda i, j: (i, 0)),
        scratch_shapes=[pltpu.VMEM((tm, kdim), f32)],
        compiler_params=_cp(("parallel", "arbitrary")), name=f"mm_cols_dgrad_{m}x{nn}x{kdim}_{nparts}",
    )(*parts, g)


def _mm_cols_wgrad(a, dz, n):
    s, kdim = a.shape
    parts = dz if isinstance(dz, tuple) else (dz,)
    nparts = len(parts)
    nd = sum(part.shape[1] for part in parts) // n
    assert a.dtype == bf16 and all(part.dtype == bf16 for part in parts)
    p = _shards_per_step(n)
    per = nd // p // nparts
    tkw = _pick(kdim, (1024, 512, 256, 128))
    ts = _pick(s, (2048, 1024, 512, 256, 128))
    ns = s // ts

    def body(a_ref, *refs):
        o_ref, acc = refs[nparts:]
        si = pl.program_id(2)
        av = a_ref[...]
        prods = [_dot(av, piece, 0, 0) for piece in _lane_pieces(_pick_part(refs, nparts, pl.program_id(1), per), p, n)]

        @pl.when(si == 0)
        def _():
            for q in range(p):
                acc[q] = prods[q]

        @pl.when(si > 0)
        def _():
            for q in range(p):
                acc[q] += prods[q]

        @pl.when(si == ns - 1)
        def _():
            o_ref[...] = acc[...].astype(bf16)

    return _pallas_call(
        body, out_shape=_sds((nd, kdim, n), bf16), grid=(kdim // tkw, nd // p, ns),
        in_specs=[pl.BlockSpec((ts, tkw), lambda i, j, k: (k, i))]
        + [pl.BlockSpec((ts, p * n), (lambda hh: lambda i, j, k: (k, _part_step(j, hh, per)))(hh)) for hh in range(nparts)],
        out_specs=pl.BlockSpec((p, tkw, n), lambda i, j, k: (j, i, 0)),
        scratch_shapes=[pltpu.VMEM((p, tkw, n), f32)],
        compiler_params=_cp(("parallel", "parallel", "arbitrary")), name=f"mm_cols_wgrad_{kdim}x{s}x{nd * n}_{nparts}",
    )(a, *parts)


def _rms_fwd(x, g_row):
    s, d = x.shape
    tr = _pick(s, (512, 256, 128))

    def body(x_ref, g_ref, o_ref):
        xv = x_ref[...]
        r = lax.rsqrt(jnp.mean(xv * xv, axis=-1, keepdims=True) + EPS)
        o_ref[...] = (xv * r * g_ref[...]).astype(bf16)

    return _pallas_call(
        body, out_shape=_sds((s, d), bf16), grid=(s // tr,),
        in_specs=[pl.BlockSpec((tr, d), lambda i: (i, 0)), pl.BlockSpec((1, d), lambda i: (0, 0))],
        out_specs=pl.BlockSpec((tr, d), lambda i: (i, 0)),
        compiler_params=_cp(("parallel",)), name=f"rms_fwd_{s}x{d}",
    )(x, g_row)


def _rms_bwd(x, g_row, dh, dres):
    s, d = x.shape
    tr = _pick(s, (512, 256, 128))
    with_dx = dres is not None

    def body(*refs):
        if with_dx:
            x_ref, g_ref, dh_ref, dres_ref, dx_ref, dxb_ref, dg_ref = refs
        else:
            x_ref, g_ref, dh_ref, dg_ref = refs

        @pl.when(pl.program_id(0) == 0)
        def _():
            dg_ref[...] = jnp.zeros_like(dg_ref)

        xv = x_ref[...]
        r = lax.rsqrt(jnp.mean(xv * xv, axis=-1, keepdims=True) + EPS)
        xh = xv * r
        dy = dh_ref[...].astype(f32)
        dg_ref[...] += jnp.sum(dy * xh, axis=0, keepdims=True)
        if with_dx:
            gy = dy * g_ref[...]
            dx = dres_ref[...] + r * (gy - xh * jnp.mean(gy * xh, axis=-1, keepdims=True))
            dx_ref[...] = dx
            dxb_ref[...] = dx.astype(bf16)

    row = pl.BlockSpec((tr, d), lambda i: (i, 0))
    vec = pl.BlockSpec((1, d), lambda i: (0, 0))
    if with_dx:
        return _pallas_call(
            body, out_shape=(_sds((s, d), f32), _sds((s, d), bf16), _sds((1, d), f32)), grid=(s // tr,),
            in_specs=[row, vec, row, row], out_specs=(row, row, vec),
            compiler_params=_cp(("arbitrary",)), name=f"rms_bwd_{s}x{d}",
        )(x, g_row, dh, dres)
    return _pallas_call(
        body, out_shape=_sds((1, d), f32), grid=(s // tr,),
        in_specs=[row, vec, row], out_specs=vec,
        compiler_params=_cp(("arbitrary",)), name=f"rms_bwd_gain_{s}x{d}",
    )(x, g_row, dh)


def _gmlp_rows(s):
    return CHUNK * (2 if (s // CHUNK) % 2 == 0 else 1)


def _gmlp_fwd(z, g_v, w_s, b_t):
    s = z.shape[0]
    t = g_v.shape[1]
    ng = t // HEAD
    rb = _gmlp_rows(s)

    def body(z_ref, gv_ref, ws_ref, bt_ref, o_ref):
        for ci in range(rb // CHUNK):
            lo = ci * CHUNK
            a = _gelu(z_ref[lo:lo + CHUNK, :])
            u, vv = a[:, :t], a[:, t:]
            r = lax.rsqrt(jnp.mean(vv * vv, axis=-1, keepdims=True) + EPS)
            vn = (vv * r * gv_ref[...]).astype(bf16)
            for g in range(ng):
                cs = slice(g * HEAD, (g + 1) * HEAD)
                sg = _dot(ws_ref[g].astype(bf16), vn[:, cs], 1, 0) + bt_ref[:, g:g + 1]
                o_ref[lo:lo + CHUNK, cs] = (u[:, cs] * sg).astype(bf16)

    return _pallas_call(
        body, out_shape=_sds((s, t + MEM_WIDTH), bf16), grid=(s // rb,),
        in_specs=[pl.BlockSpec((rb, 2 * t), lambda i: (i, 0)), pl.BlockSpec((1, t), lambda i: (0, 0)),
                  pl.BlockSpec((ng, CHUNK, CHUNK), lambda i: (0, 0, 0)), pl.BlockSpec((CHUNK, ng), lambda i: (0, 0))],
        out_specs=pl.BlockSpec((rb, t), lambda i: (i, 0)),
        compiler_params=_cp(("parallel",)), name=f"gmlp_fwd_{s}",
    )(z, g_v, w_s, b_t)


def _gmlp_bwd(z, g_v, w_s, b_t, dtok):
    s = z.shape[0]
    t = g_v.shape[1]
    ng = t // HEAD
    rb = _gmlp_rows(s)
    nsteps = s // rb

    def body(z_ref, gv_ref, ws_ref, bt_ref, dt_ref, dz_ref, dws_ref, dbt_ref, dgv_ref, ds_acc):
        step = pl.program_id(0)

        @pl.when(step == 0)
        def _():
            dws_ref[...] = jnp.zeros_like(dws_ref)
            dgv_ref[...] = jnp.zeros_like(dgv_ref)
            ds_acc[...] = jnp.zeros_like(ds_acc)

        for ci in range(rb // CHUNK):
            lo = ci * CHUNK
            zz = z_ref[lo:lo + CHUNK, :]
            a = _gelu(zz)
            u, vv = a[:, :t], a[:, t:]
            r = lax.rsqrt(jnp.mean(vv * vv, axis=-1, keepdims=True) + EPS)
            vh = vv * r
            vn = (vh * gv_ref[...]).astype(bf16)
            dtok = dt_ref[lo:lo + CHUNK, :].astype(f32)
            ds = dtok * u
            ds_acc[...] += ds
            dsb = ds.astype(bf16)
            du_parts, dvn_parts = [], []
            for g in range(ng):
                cs = slice(g * HEAD, (g + 1) * HEAD)
                wg = ws_ref[g].astype(bf16)
                sg = _dot(wg, vn[:, cs], 1, 0) + bt_ref[:, g:g + 1]
                du_parts.append(dtok[:, cs] * sg)
                dws_ref[g] += _dot(dsb[:, cs], vn[:, cs], 1, 1)
                dvn_parts.append(_dot(wg, dsb[:, cs], 0, 0))
            dvn = jnp.concatenate(dvn_parts, axis=1)
            dgv_ref[...] += jnp.sum(dvn * vh, axis=0, keepdims=True)
            gy = dvn * gv_ref[...]
            dvv = r * (gy - vh * jnp.mean(gy * vh, axis=-1, keepdims=True))
            da = jnp.concatenate(du_parts + [dvv], axis=1)
            dz_ref[lo:lo + CHUNK, :] = (da * _gelu_grad(zz)).astype(bf16)

        @pl.when(step == nsteps - 1)
        def _():
            for g in range(ng):
                dbt_ref[:, g:g + 1] = jnp.sum(ds_acc[:, g * HEAD:(g + 1) * HEAD], axis=1, keepdims=True)

    return _pallas_call(
        body,
        out_shape=(_sds((s, z.shape[1]), bf16), _sds((ng, CHUNK, CHUNK), f32), _sds((CHUNK, ng), f32), _sds((1, t), f32)),
        grid=(nsteps,),
        in_specs=[pl.BlockSpec((rb, 2 * t), lambda i: (i, 0)), pl.BlockSpec((1, t), lambda i: (0, 0)),
                  pl.BlockSpec((ng, CHUNK, CHUNK), lambda i: (0, 0, 0)), pl.BlockSpec((CHUNK, ng), lambda i: (0, 0)),
                  pl.BlockSpec((rb, t), lambda i: (i, 0))],
        out_specs=(pl.BlockSpec((rb, 2 * t), lambda i: (i, 0)), pl.BlockSpec((ng, CHUNK, CHUNK), lambda i: (0, 0, 0)),
                   pl.BlockSpec((CHUNK, ng), lambda i: (0, 0)), pl.BlockSpec((1, t), lambda i: (0, 0))),
        scratch_shapes=[pltpu.VMEM((CHUNK, t), f32)],
        compiler_params=_cp(("arbitrary",)), name=f"gmlp_bwd_{s}",
    )(z, g_v, w_s, b_t, dtok)


def _rope_tables(s):
    n_rows = s // GRID_W
    rows = jnp.broadcast_to(jnp.arange(n_rows)[:, None], (n_rows, GRID_W)).reshape(s)
    cols = jnp.broadcast_to(jnp.arange(GRID_W)[None, :], (n_rows, GRID_W)).reshape(s)
    freqs = ROPE_THETA ** (-jnp.arange(ROPE_PAIRS, dtype=f32) / ROPE_PAIRS)
    ang_r = rows.astype(f32)[:, None] * freqs
    ang_c = cols.astype(f32)[:, None] * freqs
    ang = jnp.concatenate([ang_r, ang_r, ang_c, ang_c], axis=-1)
    cos, sin = jnp.cos(ang), jnp.sin(ang)
    first = (jnp.arange(HEAD) % (2 * ROPE_PAIRS)) < ROPE_PAIRS
    return cos, jnp.where(first, -sin, 0.0), jnp.where(first, 0.0, sin)


def _attn_prep_fwd(z, g_q, g_k, tables, t):
    s = z.shape[0]
    tr = _pick(s, (256, 128))
    nq = t // HEAD
    width = t + 2 * KV_WIDTH

    def body(z_ref, gq_ref, gk_ref, cos_ref, sa_ref, sb_ref, q_ref, k_ref, v_ref):
        cos, sa, sb = cos_ref[...], sa_ref[...], sb_ref[...]
        for h in range(nq + KV_HEADS):
            cs = slice(h * HEAD, (h + 1) * HEAD)
            xv = z_ref[:, cs]
            r = lax.rsqrt(jnp.mean(xv * xv, axis=-1, keepdims=True) + EPS)
            xn = xv * r * (gq_ref[...] if h < nq else gk_ref[...])
            y = xn * cos + _rot(xn, sa, sb)
            if h < nq:
                q_ref[:, cs] = (y * (SCALE * LOG2E)).astype(bf16)
            else:
                k_ref[:, (h - nq) * HEAD:(h - nq + 1) * HEAD] = y.astype(bf16)
        v_ref[...] = z_ref[:, t + KV_WIDTH:width].astype(bf16)

    row = lambda w: pl.BlockSpec((tr, w), lambda i: (i, 0))
    vec = pl.BlockSpec((1, HEAD), lambda i: (0, 0))
    return _pallas_call(
        body, out_shape=(_sds((s, t), bf16), _sds((s, KV_WIDTH), bf16), _sds((s, KV_WIDTH), bf16)), grid=(s // tr,),
        in_specs=[row(width), vec, vec, row(HEAD), row(HEAD), row(HEAD)],
        out_specs=(row(t), row(KV_WIDTH), row(KV_WIDTH)),
        compiler_params=_cp(("parallel",)), name=f"attn_prep_fwd_{s}",
    )(z, g_q, g_k, *tables)


def _attn_prep_bwd(z, g_q, g_k, tables, dq, dk, dv, t):
    s = z.shape[0]
    tr = _pick(s, (256, 128))
    nq = t // HEAD
    width = t + 2 * KV_WIDTH

    def body(z_ref, gq_ref, gk_ref, cos_ref, sa_ref, sb_ref, dq_ref, dk_ref, dv_ref, dz_ref, dgq_ref, dgk_ref):
        @pl.when(pl.program_id(0) == 0)
        def _():
            dgq_ref[...] = jnp.zeros_like(dgq_ref)
            dgk_ref[...] = jnp.zeros_like(dgk_ref)

        cos, sa, sb = cos_ref[...], sa_ref[...], sb_ref[...]
        for h in range(nq + KV_HEADS):
            cs = slice(h * HEAD, (h + 1) * HEAD)
            xv = z_ref[:, cs]
            r = lax.rsqrt(jnp.mean(xv * xv, axis=-1, keepdims=True) + EPS)
            xh = xv * r
            if h < nq:
                dy, g_ref, dg_ref = dq_ref[:, cs], gq_ref, dgq_ref
            else:
                dy, g_ref, dg_ref = dk_ref[:, (h - nq) * HEAD:(h - nq + 1) * HEAD], gk_ref, dgk_ref
            dy = dy.astype(f32)
            dxn = dy * cos - _rot(dy, sa, sb)
            dg_ref[...] += jnp.sum(dxn * xh, axis=0, keepdims=True)
            gy = dxn * g_ref[...]
            dz_ref[:, cs] = (r * (gy - xh * jnp.mean(gy * xh, axis=-1, keepdims=True))).astype(bf16)
        dz_ref[:, t + KV_WIDTH:width] = dv_ref[...].astype(bf16)

    row = lambda w: pl.BlockSpec((tr, w), lambda i: (i, 0))
    vec = pl.BlockSpec((1, HEAD), lambda i: (0, 0))
    return _pallas_call(
        body, out_shape=(_sds((s, z.shape[1]), bf16), _sds((1, HEAD), f32), _sds((1, HEAD), f32)), grid=(s // tr,),
        in_specs=[row(width), vec, vec, row(HEAD), row(HEAD), row(HEAD), row(t), row(KV_WIDTH), row(KV_WIDTH)],
        out_specs=(row(width), vec, vec),
        compiler_params=_cp(("arbitrary",)), name=f"attn_prep_bwd_{s}",
    )(z, g_q, g_k, *tables, dq, dk, dv)


def _flash_tiles(s):
    return _pick(s, (512, 256, 128)), _pick(s, (1024, 512, 256, 128))


def _flash_fwd(q, k, v):
    s, t = q.shape
    grp = t // KV_WIDTH
    tq, tk = _flash_tiles(s)
    nkv = s // tk
    rows = grp * tq

    def body(q_ref, k_ref, v_ref, o_ref, lse_ref, m_sc, acc_sc):
        ki = pl.program_id(2)

        @pl.when(ki == 0)
        def _():
            m_sc[...] = jnp.full(m_sc.shape, -jnp.inf, f32)
            acc_sc[...] = jnp.zeros_like(acc_sc)

        kk = k_ref[...]
        v1 = jnp.concatenate([v_ref[...], jnp.ones((tk, HEAD), bf16)], axis=1)
        sc = _dot(q_ref[:, 0:HEAD], kk, 1, 1)
        for g in range(grp):
            sc_next = _dot(q_ref[:, (g + 1) * HEAD:(g + 2) * HEAD], kk, 1, 1) if g + 1 < grp else None
            mine = slice(g * tq, (g + 1) * tq)
            m_prev = m_sc[mine, :]
            m_new = jnp.maximum(m_prev, jnp.max(sc, axis=-1, keepdims=True))
            alpha = jnp.exp2(m_prev - m_new)
            p = jnp.exp2((sc - m_new).astype(bf16))
            acc_sc[mine, :] = alpha * acc_sc[mine, :] + _dot(p, v1, 1, 0)
            m_sc[mine, :] = m_new
            sc = sc_next

        @pl.when(ki == nkv - 1)
        def _():
            acc = acc_sc[...]
            l = acc[:, HEAD:HEAD + 1]
            o = acc[:, :HEAD] / l
            for g in range(grp):
                o_ref[:, g * HEAD:(g + 1) * HEAD] = o[g * tq:(g + 1) * tq].astype(bf16)
            lse_ref[0] = jnp.broadcast_to(m_sc[...] + jnp.log(l) * LOG2E, (rows, HEAD))

    return _pallas_call(
        body, out_shape=(_sds((s, t + MEM_WIDTH), bf16), _sds((KV_HEADS, grp * s, HEAD), f32)), grid=(KV_HEADS, s // tq, nkv),
        in_specs=[pl.BlockSpec((tq, grp * HEAD), lambda h, i, j: (i, h)), pl.BlockSpec((tk, HEAD), lambda h, i, j: (j, h)),
                  pl.BlockSpec((tk, HEAD), lambda h, i, j: (j, h))],
        out_specs=(pl.BlockSpec((tq, grp * HEAD), lambda h, i, j: (i, h)), pl.BlockSpec((1, rows, HEAD), lambda h, i, j: (h, i, 0))),
        scratch_shapes=[pltpu.VMEM((rows, 1), f32), pltpu.VMEM((rows, 2 * HEAD), f32)],
        compiler_params=_cp(("parallel", "parallel", "arbitrary")), name=f"flash_fwd_{s}",
    )(q, k, v)


def _flash_delta(o, do, t):
    s = o.shape[0]
    grp = t // KV_WIDTH
    tq, _ = _flash_tiles(s)
    rows = grp * tq

    def body(o_ref, do_ref, d_ref):
        for g in range(grp):
            cs = slice(g * HEAD, (g + 1) * HEAD)
            dd = jnp.sum(o_ref[:, cs].astype(f32) * do_ref[:, cs].astype(f32), axis=-1, keepdims=True)
            d_ref[0, g * tq:(g + 1) * tq, :] = jnp.broadcast_to(dd, (tq, HEAD))

    qb = pl.BlockSpec((tq, grp * HEAD), lambda h, i: (i, h))
    return _pallas_call(
        body, out_shape=_sds((KV_HEADS, grp * s, HEAD), f32), grid=(KV_HEADS, s // tq),
        in_specs=[qb, qb], out_specs=pl.BlockSpec((1, rows, HEAD), lambda h, i: (h, i, 0)),
        compiler_params=_cp(("parallel", "parallel")), name=f"flash_delta_{s}",
    )(o, do)


def _flash_bwd(q, k, v, o, do, lse):
    s, t = q.shape
    grp = t // KV_WIDTH
    tq, tk = _flash_tiles(s)
    nq, nkv = s // tq, s // tk
    rows = grp * tq
    delta = _flash_delta(o, do, t)

    def body(q_ref, k_ref, v_ref, do_ref, lse_ref, delta_ref, dq_ref, dk_ref, dv_ref, dq_acc, dk_acc, dv_acc):
        kj, qi = pl.program_id(1), pl.program_id(2)

        @pl.when(qi == 0)
        def _():
            dk_acc[...] = jnp.zeros_like(dk_acc)
            dv_acc[...] = jnp.zeros_like(dv_acc)

        kk, vv = k_ref[...], v_ref[...]

        def products(g):
            qg, dog = q_ref[:, g * HEAD:(g + 1) * HEAD], do_ref[:, g * HEAD:(g + 1) * HEAD]
            return qg, dog, _dot(qg, kk, 1, 1), _dot(dog, vv, 1, 1)

        ahead = products(0)
        dv_sum = dk_sum = None
        terms = []
        for g in range(grp):
            qg, dog, sc, dp = ahead
            if g + 1 < grp:
                ahead = products(g + 1)
            head_rows = slice(g * tq, (g + 1) * tq)
            p = jnp.exp2((sc - lse_ref[0, head_rows, 0:1]).astype(bf16))
            ds = p * (dp - delta_ref[0, head_rows, 0:1]).astype(bf16)
            dv_g, dk_g = _dot(p, dog, 0, 0), _dot(ds, qg, 0, 0)
            dv_sum = dv_g if dv_sum is None else dv_sum + dv_g
            dk_sum = dk_g if dk_sum is None else dk_sum + dk_g
            terms.append(_dot(ds, kk, 1, 0))
        dv_acc[...] += dv_sum
        dk_acc[...] += dk_sum
        mine = pl.ds(pl.multiple_of(qi * rows, rows), rows)
        term = jnp.concatenate(terms, axis=0)

        @pl.when(kj == 0)
        def _():
            dq_acc[mine, :] = term

        @pl.when(kj > 0)
        def _():
            dq_acc[mine, :] += term

        @pl.when(kj == nkv - 1)
        def _():
            total = dq_acc[mine, :]
            for g in range(grp):
                dq_ref[:, g * HEAD:(g + 1) * HEAD] = total[g * tq:(g + 1) * tq] * SCALE

        @pl.when(qi == nq - 1)
        def _():
            dk_ref[...] = dk_acc[...] * (1.0 / LOG2E)
            dv_ref[...] = dv_acc[...]

    qb = pl.BlockSpec((tq, grp * HEAD), lambda h, j, i: (i, h))
    kb = pl.BlockSpec((tk, HEAD), lambda h, j, i: (j, h))
    lb = pl.BlockSpec((1, rows, HEAD), lambda h, j, i: (h, i, 0))
    dqb = pl.BlockSpec((tq, grp * HEAD), lambda h, j, i: (jnp.where(j == nkv - 1, i, 0), h))
    return _pallas_call(
        body, out_shape=(_sds((s, t), f32), _sds((s, KV_WIDTH), f32), _sds((s, KV_WIDTH), f32)), grid=(KV_HEADS, nkv, nq),
        in_specs=[qb, kb, kb, qb, lb, lb], out_specs=(dqb, kb, kb),
        scratch_shapes=[pltpu.VMEM((nq * rows, HEAD), f32), pltpu.VMEM((tk, HEAD), f32), pltpu.VMEM((tk, HEAD), f32)],
        compiler_params=_cp(("parallel", "arbitrary", "arbitrary")), name=f"flash_bwd_{s}",
    )(q, k, v, do, lse, delta)


def _mem_heads(z_ref, kv_ref, gq_ref, gk_ref, h):
    cs = slice(h * HEAD, (h + 1) * HEAD)
    xv = z_ref[:, cs]
    r = lax.rsqrt(jnp.mean(xv * xv, axis=-1, keepdims=True) + EPS)
    xh = xv * r
    kx = kv_ref[:, cs]
    rk = lax.rsqrt(jnp.mean(kx * kx, axis=-1, keepdims=True) + EPS)
    kn = (kx * rk * gk_ref[...]).astype(bf16)
    vv = kv_ref[:, MEM_WIDTH + h * HEAD:MEM_WIDTH + (h + 1) * HEAD].astype(bf16)
    qn = (xh * gq_ref[...]).astype(bf16)
    sc = _dot(qn, kn, 1, 1) * SCALE
    e = jnp.exp(sc - jnp.max(sc, axis=-1, keepdims=True))
    p = e / jnp.sum(e, axis=-1, keepdims=True)
    return cs, r, xh, qn, kn, vv, p


def _mem_fwd(z, qblk, kv, g_mq, g_mk, cat):
    s = z.shape[0]
    nm = kv.shape[0]
    tr = _pick(s, (512, 256, 128))
    oblk = cat.shape[1] // MEM_WIDTH - 1

    def body(z_ref, kv_ref, gq_ref, gk_ref, cat_ref, o_ref):
        for h in range(MEM_HEADS):
            cs, _, _, _, _, vv, p = _mem_heads(z_ref, kv_ref, gq_ref, gk_ref, h)
            o_ref[:, cs] = _dot(p.astype(bf16), vv, 1, 0).astype(bf16)

    vec = pl.BlockSpec((1, HEAD), lambda i: (0, 0))
    return _pallas_call(
        body, out_shape=_sds(cat.shape, bf16), grid=(s // tr,),
        in_specs=[pl.BlockSpec((tr, MEM_WIDTH), lambda i: (i, qblk)), pl.BlockSpec((nm, 2 * MEM_WIDTH), lambda i: (0, 0)), vec, vec,
                  pl.BlockSpec(memory_space=pl.ANY)],
        out_specs=pl.BlockSpec((tr, MEM_WIDTH), lambda i: (i, oblk)),
        input_output_aliases={4: 0},
        compiler_params=_cp(("parallel",)), name=f"mem_fwd_{s}_{qblk}",
    )(z, kv, g_mq, g_mk, cat)


def _mem_bwd(z, qblk, kv, g_mq, g_mk, dcat, dz):
    s = z.shape[0]
    nm = kv.shape[0]
    tr = _pick(s, (512, 256, 128))
    dblk = dcat.shape[1] // MEM_WIDTH - 1

    def body(z_ref, kv_ref, gq_ref, gk_ref, dm_ref, dzin_ref, dz_ref, dkn_ref, dv_ref, dgq_ref):
        @pl.when(pl.program_id(0) == 0)
        def _():
            dkn_ref[...] = jnp.zeros_like(dkn_ref)
            dv_ref[...] = jnp.zeros_like(dv_ref)
            dgq_ref[...] = jnp.zeros_like(dgq_ref)

        for h in range(MEM_HEADS):
            cs, r, xh, qn, kn, vv, p = _mem_heads(z_ref, kv_ref, gq_ref, gk_ref, h)
            dm = dm_ref[:, cs]
            dv_ref[:, cs] += _dot(p.astype(bf16), dm, 0, 0)
            dp = _dot(dm, vv, 1, 1)
            ds = (p * (dp - jnp.sum(dp * p, axis=-1, keepdims=True)) * SCALE).astype(bf16)
            dqn = _dot(ds, kn, 1, 0)
            dkn_ref[:, cs] += _dot(ds, qn, 0, 0)
            dgq_ref[...] += jnp.sum(dqn * xh, axis=0, keepdims=True)
            gy = dqn * gq_ref[...]
            dz_ref[:, cs] = (r * (gy - xh * jnp.mean(gy * xh, axis=-1, keepdims=True))).astype(bf16)

    vec = pl.BlockSpec((1, HEAD), lambda i: (0, 0))
    acc = pl.BlockSpec((nm, MEM_WIDTH), lambda i: (0, 0))
    return _pallas_call(
        body, out_shape=(_sds(dz.shape, bf16), _sds((nm, MEM_WIDTH), f32), _sds((nm, MEM_WIDTH), f32), _sds((1, HEAD), f32)),
        grid=(s // tr,),
        in_specs=[pl.BlockSpec((tr, MEM_WIDTH), lambda i: (i, qblk)), pl.BlockSpec((nm, 2 * MEM_WIDTH), lambda i: (0, 0)), vec, vec,
                  pl.BlockSpec((tr, MEM_WIDTH), lambda i: (i, dblk)), pl.BlockSpec(memory_space=pl.ANY)],
        out_specs=(pl.BlockSpec((tr, MEM_WIDTH), lambda i: (i, qblk)), acc, acc, vec),
        input_output_aliases={5: 0},
        compiler_params=_cp(("arbitrary",)), name=f"mem_bwd_{s}_{qblk}",
    )(z, kv, g_mq, g_mk, dcat, dz)


def _mem_kv_bwd(kv, g_mk, dkn, dv):
    nm = kv.shape[0]

    def body(kv_ref, gk_ref, dkn_ref, dv_ref, dkv_ref, dgk_ref):
        dgk = jnp.zeros((1, HEAD), f32)
        for h in range(MEM_HEADS):
            cs = slice(h * HEAD, (h + 1) * HEAD)
            kx = kv_ref[:, cs]
            rk = lax.rsqrt(jnp.mean(kx * kx, axis=-1, keepdims=True) + EPS)
            kh = kx * rk
            dkn_h = dkn_ref[:, cs]
            dgk = dgk + jnp.sum(dkn_h * kh, axis=0, keepdims=True)
            gy = dkn_h * gk_ref[...]
            dkv_ref[:, cs] = (rk * (gy - kh * jnp.mean(gy * kh, axis=-1, keepdims=True))).astype(bf16)
        dkv_ref[:, MEM_WIDTH:] = dv_ref[...].astype(bf16)
        dgk_ref[...] = dgk

    return _pallas_call(
        body, out_shape=(_sds((nm, 2 * MEM_WIDTH), bf16), _sds((1, HEAD), f32)),
        compiler_params=pltpu.CompilerParams(vmem_limit_bytes=VMEM_LIMIT), name=f"mem_kv_bwd_{nm}",
    )(kv, g_mk, dkn, dv)


def _ffn_up(h2, g):
    m, kdim = h2.shape
    nd, _, n = g.shape
    half = nd // 2
    assert n % 128 == 0 and h2.dtype == bf16 and g.dtype == bf16
    tm = _pick(m, (512, 256, 128))

    def body(a_ref, wg_ref, wu_ref, gate_ref, up_ref, act_ref):
        av = a_ref[...]
        gt = _dot(av, wg_ref[0], 1, 0)
        up = _dot(av, wu_ref[0], 1, 0)
        gate_ref[...] = gt.astype(bf16)
        up_ref[...] = up.astype(bf16)
        act_ref[...] = (gt * jax.nn.sigmoid(gt) * up).astype(bf16)

    out = pl.BlockSpec((tm, n), lambda i, j: (i, j))
    return _pallas_call(
        body, out_shape=tuple(_sds((m, half * n), bf16) for _ in range(3)), grid=(m // tm, half),
        in_specs=[pl.BlockSpec((tm, kdim), lambda i, j: (i, 0)), pl.BlockSpec((1, kdim, n), lambda i, j: (j, 0, 0)),
                  pl.BlockSpec((1, kdim, n), lambda i, j: (j + half, 0, 0))],
        out_specs=(out, out, out),
        compiler_params=_cp(("parallel", "arbitrary")), name=f"ffn_up_{m}x{kdim}x{half * n}",
    )(h2, g, g)


def _ffn_down_bwd(dx, w_dn, gate, up):
    m, d = dx.shape
    ff = w_dn.shape[0]
    tm = _pick(m, (512, 256, 128))
    tf = _pick(ff, (1408, 1024, 512, 256, 128))

    nsub = 2 if tm % 32 == 0 else 1
    rs = tm // nsub

    def body(dx_ref, w_ref, g_ref, u_ref, dg_ref, du_ref):
        wv = w_ref[...]
        da = _dot(dx_ref[0:rs, :], wv, 1, 1)
        for r in range(nsub):
            da_next = _dot(dx_ref[(r + 1) * rs:(r + 2) * rs, :], wv, 1, 1) if r + 1 < nsub else None
            mine = slice(r * rs, (r + 1) * rs)
            gt = g_ref[mine, :].astype(f32)
            sg = jax.nn.sigmoid(gt)
            dg_ref[mine, :] = (da * u_ref[mine, :].astype(f32) * sg * (1.0 + gt * (1.0 - sg))).astype(bf16)
            du_ref[mine, :] = (da * gt * sg).astype(bf16)
            da = da_next

    tile = pl.BlockSpec((tm, tf), lambda i, j: (i, j))
    return _pallas_call(
        body, out_shape=(_sds((m, ff), bf16), _sds((m, ff), bf16)), grid=(m // tm, ff // tf),
        in_specs=[pl.BlockSpec((tm, d), lambda i, j: (i, 0)), pl.BlockSpec((tf, d), lambda i, j: (j, 0)), tile, tile],
        out_specs=(tile, tile),
        compiler_params=_cp(("parallel", "arbitrary")), name=f"ffn_down_bwd_{m}x{d}x{ff}",
    )(dx, w_dn, gate, up)


def _loss_head(y, target):
    s, d = y.shape
    tr = _pick(s, (512, 256, 128))

    def body(y_ref, t_ref, l_ref, dy_ref, dyb_ref):
        @pl.when(pl.program_id(0) == 0)
        def _():
            l_ref[...] = jnp.zeros_like(l_ref)

        err = y_ref[...] - t_ref[...]
        l_ref[...] += 0.5 * jnp.sum(jnp.mean(err * err, axis=-1, keepdims=True), axis=0, keepdims=True)
        dy = err * (1.0 / d)
        dy_ref[...] = dy
        dyb_ref[...] = dy.astype(bf16)

    row = pl.BlockSpec((tr, d), lambda i: (i, 0))
    return _pallas_call(
        body, out_shape=(_sds((1, HEAD), f32), _sds((s, d), f32), _sds((s, d), bf16)), grid=(s // tr,),
        in_specs=[row, row], out_specs=(pl.BlockSpec((1, HEAD), lambda i: (0, 0)), row, row),
        compiler_params=_cp(("arbitrary",)), name=f"loss_{s}x{d}",
    )(y, target)


def _place():
    return lax.axis_index("x"), lax.axis_index("y"), lax.axis_index("c")


def _tag(arrays):
    return "_".join("x".join(str(dd) for dd in a.shape) for a in arrays)


def _all_gather(shards):
    nw = len(shards)
    hbm = pl.BlockSpec(memory_space=pl.ANY)

    def body(*refs):
        x_refs, out_refs = refs[:nw], refs[nw:2 * nw]
        send_sems, recv_sems, local_sems = refs[2 * nw:]
        x, y, c = _place()
        me, sibling = (x, y, c), (x, y, 1 - c)
        chips = [(1 - x, y), (x, 1 - y), (1 - x, 1 - y)]

        def slot(w, place):
            px, py, pc = place
            return out_refs[w].at[4 * px + 2 * py + pc]

        def copy(k, w, block_of, to, from_input=False):
            return pltpu.make_async_remote_copy(
                src_ref=x_refs[w] if from_input else slot(w, block_of), dst_ref=slot(w, block_of),
                send_sem=send_sems.at[k, w], recv_sem=recv_sems.at[k, w], device_id=to, device_id_type=MESH)

        mine = [pltpu.make_async_copy(x_refs[w], slot(w, me), local_sems.at[w]) for w in range(nw)]
        for cp in mine:
            cp.start()
        first = []
        for w in range(nw):
            first.append(copy(0, w, me, sibling, from_input=True))
            first += [copy(1 + j, w, me, (*chip, c), from_input=True) for j, chip in enumerate(chips)]
        for cp in first:
            cp.start()
        passed = []
        for w in range(nw):
            for j, chip in enumerate(chips):
                copy(1 + j, w, (*chip, c), me).wait_recv()
                fwd = copy(4 + j, w, (*chip, c), sibling)
                fwd.start()
                passed.append(fwd)
        for w in range(nw):
            copy(0, w, sibling, me).wait_recv()
            for j, chip in enumerate(chips):
                copy(4 + j, w, (*chip, 1 - c), me).wait_recv()
        for cp in first + passed:
            cp.wait_send()
        for cp in mine:
            cp.wait()

    return _pallas_call(
        body, out_shape=tuple(_sds((N_DEV,) + a.shape, a.dtype) for a in shards), in_specs=[hbm] * nw, out_specs=tuple([hbm] * nw),
        scratch_shapes=[pltpu.SemaphoreType.DMA((7, nw)), pltpu.SemaphoreType.DMA((7, nw)), pltpu.SemaphoreType.DMA((nw,))],
        name=f"all_gather_{_tag(shards)}_{jnp.dtype(shards[0].dtype).name}",
    )(*shards)


def _pair_sum(grad, got):
    nd, a, b = grad.shape
    nchip = nd // 2
    ta = _pick(a, (1024, 704, 512, 352, 256, 128, 64, 32, 16))

    def my_chip():
        return 2 * lax.axis_index("x") + lax.axis_index("y")

    def body(a_ref, b_ref, o_ref, land_ref):
        tot = (a_ref[...].astype(f32) + b_ref[...].astype(f32)).astype(o_ref.dtype)
        o_ref[...] = tot

        @pl.when(pl.program_id(1) == my_chip())
        def _():
            land_ref[...] = tot

    return _pallas_call(
        body, out_shape=(_sds(got.shape, grad.dtype), _sds(got.shape, grad.dtype)), grid=(a // ta, nchip),
        in_specs=[pl.BlockSpec((None, ta, b), lambda i, k: (2 * k + lax.axis_index("c"), i, 0)),
                  pl.BlockSpec((None, ta, b), lambda i, k: (k, i, 0))],
        out_specs=(pl.BlockSpec((None, ta, b), lambda i, k: (k, i, 0)),
                   pl.BlockSpec((None, ta, b), lambda i, k: (my_chip(), i, 0))),
        compiler_params=_cp(("parallel", "arbitrary")), name=f"pair_sum_{a}x{b}",
    )(grad, got)


_HBM = pl.BlockSpec(memory_space=pltpu.HBM)
_SEM = pl.BlockSpec(memory_space=pltpu.SEMAPHORE)
_ANY = pl.BlockSpec(memory_space=pl.ANY)
_DATAFLOW = pltpu.SideEffectType.DATAFLOW_SIDE_EFFECTING


def _in_hbm(a):
    return pltpu.with_memory_space_constraint(a, pltpu.HBM)


def _exchange_begin(bufs, nw, route, after, copies_of, n_copies, name):
    nb = len(bufs)

    def body(*refs):
        send_sems, recv_sems = refs[nb + 2], refs[nb + 3]
        for w in range(nw):
            for k, (src, dst, to) in enumerate(copies_of(w, refs[:nb])):
                pltpu.make_async_remote_copy(src_ref=src, dst_ref=dst, send_sem=send_sems.at[k * nw + w],
                                             recv_sem=recv_sems.at[k * nw + w], device_id=to, device_id_type=MESH).start()

    out = _pallas_call(
        body, name=name,
        out_shape=(pltpu.SemaphoreType.DMA((n_copies * nw,)), pltpu.SemaphoreType.DMA((n_copies * nw,)),
                   *[pltpu.HBM(a.shape, a.dtype) for a in bufs], pltpu.HBM(route.shape, route.dtype)),
        in_specs=[_HBM] * (nb + 1) + [_ANY], out_specs=(_SEM, _SEM, *[_HBM] * (nb + 1)),
        input_output_aliases={i: 2 + i for i in range(nb + 1)},
        compiler_params=pltpu.CompilerParams(has_side_effects=_DATAFLOW),
    )(*[_in_hbm(a) for a in bufs], _in_hbm(route), after)
    return (out[0], out[1], out[2:2 + nb], nw), out[2 + nb]


def _exchange_end(handle, after, copies_of, n_copies, name):
    send_sems, recv_sems, thru, nw = handle
    nb = len(thru)

    def body(*refs):
        send_sems, recv_sems = refs[nb], refs[nb + 1]
        for w in range(nw):
            for k, (src, dst, to) in enumerate(copies_of(w, refs[:nb])):
                cp = pltpu.make_async_remote_copy(src_ref=src, dst_ref=dst, send_sem=send_sems.at[k * nw + w],
                                                  recv_sem=recv_sems.at[k * nw + w], device_id=to, device_id_type=MESH)
                cp.wait_send()
                cp.wait_recv()

    out = _pallas_call(
        body, name=name, out_shape=tuple(pltpu.HBM(a.shape, a.dtype) for a in thru),
        in_specs=[_HBM] * nb + [_SEM, _SEM, _ANY], out_specs=tuple([_HBM] * nb),
        input_output_aliases={i: i for i in range(nb)},
        compiler_params=pltpu.CompilerParams(has_side_effects=_DATAFLOW),
    )(*thru, send_sems, recv_sems, after)
    return list(out)


def _my_slot():
    return 4 * lax.axis_index("x") + 2 * lax.axis_index("y") + lax.axis_index("c")


def _shard_into_land(w_all, idx):
    _, a, b = w_all.shape
    ta = next(cc for cc in (1024, 704, 512, 352, 256, 128, 64, 32, 16) if a % cc == 0 and (cc * b * 4 <= 2 ** 21 or cc == 16))

    def body(w_ref, o_ref):
        o_ref[...] = w_ref[...].astype(bf16)

    return _pallas_call(
        body, out_shape=_sds((N_DEV, a, b), bf16), grid=(a // ta,),
        in_specs=[pl.BlockSpec((None, ta, b), lambda i: (idx, i, 0))],
        out_specs=pl.BlockSpec((None, ta, b), lambda i: (_my_slot(), i, 0)),
        compiler_params=_cp(("parallel",)), name=f"shard_into_land_{a}x{b}_{idx}",
    )(w_all)


def _gather_copies(w, land_refs):
    x, y, c = _place()
    blk = land_refs[w].at[4 * x + 2 * y + c]
    return [(blk, blk, to) for to in ((x, y, 1 - c), (1 - x, y, c), (x, 1 - y, c), (1 - x, 1 - y, c))]


def _gather_begin(lands, route, after, tag):
    return _exchange_begin(lands, len(lands), route, after, _gather_copies, 4, f"gather_begin_{tag}")


def _gather_end(handle, after, tag):
    return _exchange_end(handle, after, _gather_copies, 4, f"gather_end_{tag}")


def _gather_pass_on(lands):
    nw = len(lands)

    def body(*refs):
        l_refs = refs[nw:2 * nw]
        send_sems, recv_sems = refs[2 * nw:]
        x, y, c = _place()
        copies = []
        for w in range(nw):
            for j, (px, py) in enumerate([(1 - x, y), (x, 1 - y), (1 - x, 1 - y)]):
                blk = l_refs[w].at[4 * px + 2 * py + c]
                copies.append(pltpu.make_async_remote_copy(
                    src_ref=blk, dst_ref=blk, send_sem=send_sems.at[j, w], recv_sem=recv_sems.at[j, w],
                    device_id=(x, y, 1 - c), device_id_type=MESH))
        for cp in copies:
            cp.start()
        for cp in copies:
            cp.wait_send()
        for w in range(nw):
            for j, (px, py) in enumerate([(1 - x, y), (x, 1 - y), (1 - x, 1 - y)]):
                blk = l_refs[w].at[4 * px + 2 * py + (1 - c)]
                pltpu.make_async_remote_copy(src_ref=blk, dst_ref=blk, send_sem=send_sems.at[j, w], recv_sem=recv_sems.at[j, w],
                                             device_id=(x, y, 1 - c), device_id_type=MESH).wait_recv()

    return _pallas_call(
        body, out_shape=tuple(_sds(a.shape, a.dtype) for a in lands), in_specs=[_ANY] * nw, out_specs=tuple([_ANY] * nw),
        input_output_aliases={w: w for w in range(nw)},
        scratch_shapes=[pltpu.SemaphoreType.DMA((3, nw)), pltpu.SemaphoreType.DMA((3, nw))],
        name=f"gather_pass_on_{_tag(lands)}",
    )(*lands)


def _pass_copies(w, land_refs):
    x, y, c = _place()
    blocks = [land_refs[w].at[4 * px + 2 * py + c] for px, py in ((1 - x, y), (x, 1 - y), (1 - x, 1 - y))]
    return [(blk, blk, (x, y, 1 - c)) for blk in blocks]


def _pass_begin(lands, route, after, tag):
    return _exchange_begin(lands, len(lands), route, after, _pass_copies, 3, f"pass_begin_{tag}")


def _pass_end(handle, after, tag):
    return _exchange_end(handle, after, _pass_copies, 3, f"pass_end_{tag}")


def _swap_copies(w, refs):
    x, y, c = _place()
    nw = len(refs) // 2
    return [(refs[w].at[2 * k + (1 - c)], refs[nw + w].at[k], (x, y, 1 - c)) for k in range(N_DEV // 2)]


def _swap_begin(grads, route, after, tag):
    lands = [lax.empty((N_DEV // 2,) + g.shape[1:], g.dtype) for g in grads]
    return _exchange_begin(list(grads) + lands, len(grads), route, after, _swap_copies, N_DEV // 2, f"swap_begin_{tag}")


def _swap_end(handle, after, tag):
    out = _exchange_end(handle, after, _swap_copies, N_DEV // 2, f"swap_end_{tag}")
    return out[:len(out) // 2], out[len(out) // 2:]


def _scatter_copies(w, refs):
    x, y, c = _place()
    nw = len(refs) // 2
    dst = refs[nw + w].at[2 * x + y]
    return [(refs[w].at[2 * px + py], dst, (px, py, c)) for px, py in ((1 - x, y), (x, 1 - y), (1 - x, 1 - y))]


def _scatter_begin(psums, lands, route, after, tag):
    return _exchange_begin(list(psums) + list(lands), len(psums), route, after, _scatter_copies, 3, f"scatter_begin_{tag}")


def _scatter_end(handle, after, tag):
    return _exchange_end(handle, after, _scatter_copies, 3, f"scatter_end_{tag}")


def _adamw(parts, w_all, m_all, v_all, l, carried):
    nparts, a, b = parts.shape
    nl = w_all.shape[0]
    ta = next(cc for cc in (1024, 704, 512, 352, 256, 128, 64, 32, 16, 8) if a % cc == 0 and (cc * b * 4 <= 2 ** 20 or cc == 8))
    c1 = 1.0 / (1.0 - ADAM_B1 ** ADAM_STEP)
    c2 = 1.0 / (1.0 - ADAM_B2 ** ADAM_STEP)

    def body(p_ref, w_ref, m_ref, v_ref, *rest):
        g_out, d_out, m_out, v_out = rest[-4:]
        g = p_ref[0].astype(f32)
        for k in range(1, nparts):
            g = g + p_ref[k].astype(f32)
        m_new = ADAM_B1 * m_ref[...] + (1.0 - ADAM_B1) * g
        v_new = ADAM_B2 * v_ref[...] + (1.0 - ADAM_B2) * (g * g)
        m_hat = m_new * c1
        v_hat = v_new * c2
        g_out[...] = g
        d_out[...] = -ADAM_LR * (m_hat / (jnp.sqrt(v_hat) + ADAM_EPS) + ADAM_WD * w_ref[...])
        m_out[...] = m_new
        v_out[...] = v_new

    one = pl.BlockSpec((None, ta, b), lambda i: (l, i, 0))
    keep = [] if carried is None else [pl.BlockSpec(memory_space=pl.ANY)] * 4
    return _pallas_call(
        body, out_shape=tuple(_sds((nl, a, b), f32) for _ in range(4)), grid=(a // ta,),
        in_specs=[pl.BlockSpec((nparts, ta, b), lambda i: (0, i, 0)), one, one, one] + keep, out_specs=(one, one, one, one),
        input_output_aliases=({} if carried is None else {4 + q: q for q in range(4)}),
        compiler_params=_cp(("parallel",)), name=f"adamw_{nparts}x{nl}x{a}x{b}_{l}{'' if carried is None else '_carried'}",
    )(parts, w_all, m_all, v_all, *(carried or ()))


def _to_flat(arrays):
    flat = jnp.concatenate([a.reshape(-1).astype(f32) for a in arrays])
    rows = -(-flat.shape[0] // (8 * LANES)) * 8
    return jnp.pad(flat, (0, rows * LANES - flat.shape[0])).reshape(rows, LANES)


def _from_flat(flat, shapes):
    flat = flat.reshape(-1)
    out, off = [], 0
    for shp in shapes:
        n = 1
        for dd in shp:
            n *= dd
        out.append(flat[off:off + n].reshape(shp))
        off += n
    return out


def kernel(x, mem, g_mix, g_ffn, w_in_a, g_v_a, w_spatial, b_spatial, w_in_b, g_q_b, g_k_b, g_mem, w_mem_kv, g_mq, g_mk, w_out, w_gate_up, w_down, loss_target, m_g_mix, m_g_ffn, m_w_in_a, m_g_v_a, m_w_spatial, m_b_spatial, m_w_in_b, m_g_q_b, m_g_k_b, m_g_mem, m_w_mem_kv, m_g_mq, m_g_mk, m_w_out, m_w_gate_up, m_w_down, v_g_mix, v_g_ffn, v_w_in_a, v_g_v_a, v_w_spatial, v_b_spatial, v_w_in_b, v_g_q_b, v_g_k_b, v_g_mem, v_w_mem_kv, v_g_mq, v_g_mk, v_w_out, v_w_gate_up, v_w_down):
    given = dict(locals())
    depth = g_mix.shape[0]
    s, d = x.shape[1], x.shape[2]
    nm = mem.shape[1]
    t = d - MEM_WIDTH
    ff = w_down.shape[1] * N_DEV
    x0 = x.reshape(s, d)
    mem0 = mem.reshape(nm, d)
    target = loss_target.reshape(s, d)
    tables = _rope_tables(s)

    big_names = ("w_in", "w_mem_kv", "w_out", "w_gate_up", "w_down")

    def stacked_key(name, l):
        if name == "w_in":
            return ("w_in_a" if l % 2 == 0 else "w_in_b"), l // 2
        return name, l

    n_mix = 3

    def own_shards(l):
        return [_shard_into_land(given[key], idx) for key, idx in (stacked_key(name, l) for name in big_names)]

    def gather_start(l, route, after):
        lands = cast_shards[l]
        h_mix, route = _gather_begin(lands[:n_mix], route, after, f"mix{l}")
        h_ffn, route = _gather_begin(lands[n_mix:], route, after, f"ffn{l}")
        return h_mix, h_ffn, route

    def gather_finish(handle, after, tag):
        return list(_gather_pass_on(_gather_end(handle, after, tag)))

    saved = []
    xc = x0
    cast_shards = {0: own_shards(0)}
    h_in, _ = _gather_begin(cast_shards[0][:1], g_mix[0].reshape(1, d), mem0, "in0")
    for l in range(1, depth):
        cast_shards[l] = own_shards(l)
    w_mix = gather_finish(h_in, cast_shards[depth - 1][0], "in0")
    w_ffn = None
    for l in range(depth):
        is_a = l % 2 == 0
        g_in = w_mix[0]
        qblk = (N_DEV * g_in.shape[2] - MEM_WIDTH) // MEM_WIDTH

        gm_row, gf_row, gmem_row = g_mix[l].reshape(1, d), g_ffn[l].reshape(1, d), g_mem[l].reshape(1, d)
        gmq_row, gmk_row = g_mq[l].reshape(1, HEAD), g_mk[l].reshape(1, HEAD)
        if l == 0:
            h_rest, gm_row = _gather_begin(cast_shards[0][1:n_mix], gm_row, g_in, "rest0")
            h_ffn, gm_row = _gather_begin(cast_shards[0][n_mix:], gm_row, g_in, "ffn0")
        h = _rms_fwd(xc, gm_row)
        z = _mm_cols_fwd(h, g_in, f32)
        if l == 0:
            w_mix = w_mix + gather_finish(h_rest, z, "rest0")
        g_kv, g_out = w_mix[1:]
        w_kv, w_o = (g.reshape(-1, g.shape[2]) for g in (g_kv, g_out))
        if is_a:
            ia = l // 2
            mix = dict(g_v=g_v_a[ia].reshape(1, t), w_s=w_spatial[ia], b_t=b_spatial[ia].T)
            cat = _gmlp_fwd(z, mix["g_v"], mix["w_s"], mix["b_t"])
        else:
            ib = l // 2
            mix = dict(g_q=g_q_b[ib].reshape(1, HEAD), g_k=g_k_b[ib].reshape(1, HEAD))
            q, k, v = _attn_prep_fwd(z, mix["g_q"], mix["g_k"], tables, t)
            cat, lse = _flash_fwd(q, k, v)
            mix.update(q=q, k=k, v=v, lse=lse)
        hm = _rms_fwd(mem0, gmem_row)
        kv = _matmul(hm, w_kv)
        cat = _mem_fwd(z, qblk, kv, gmq_row, gmk_row, cat)
        if l > 0:
            h_pass, cat = _pass_begin(_gather_end(h_ffn, cat, f"ffn{l}"), cat, z, f"ffn{l}")
        x1 = _matmul(cat, w_o, res=xc)
        w_ffn = gather_finish(h_ffn, x1, "ffn0") if l == 0 else _pass_end(h_pass, x1, f"ffn{l}")
        if l + 1 < depth:
            next_mix, next_ffn, gf_row = gather_start(l + 1, gf_row, w_ffn[0])
        h2 = _rms_fwd(x1, gf_row)
        g_gu, g_dn = w_ffn
        w_dn = g_dn.reshape(-1, g_dn.shape[2])
        gate, up, act = _ffn_up(h2, g_gu)
        x2 = _matmul(act, w_dn, res=x1)
        saved.append(dict(x=xc, h=h, z=z, mix=mix, cat=cat, hm=hm, kv=kv, x1=x1, h2=h2, gate=gate, up=up, act=act, qblk=qblk,
                          w=(g_in, w_kv, w_o, g_gu, w_dn), rows=(gm_row, gf_row, gmem_row, gmq_row, gmk_row)))
        if l + 1 < depth:
            w_mix = gather_finish(next_mix, x2, f"mix{l + 1}")
            h_ffn = next_ffn
        xc = x2

    loss_row, dy, dy_b = _loss_head(xc, target)
    loss = lax.psum(loss_row[0, 0], ("x", "y", "c"))

    small = {n: [None] * given[n].shape[0] for n in ("g_mix", "g_ffn", "g_v_a", "w_spatial", "b_spatial", "g_q_b", "g_k_b", "g_mem", "g_mq", "g_mk")}
    big_out = {}

    def scatter_start(swap, route, after, tag):
        grads, got = _swap_end(swap, after, tag)
        sums = [_pair_sum(g, r) for g, r in zip(grads, got)]
        return _scatter_begin([p for p, _ in sums], [q for _, q in sums], route, after, tag)

    def scatter_finish(handle, names, l, after, tag):
        arrived = _scatter_end(handle, after, tag)[len(names):]
        for name, parts in zip(names, arrived):
            key, idx = stacked_key(name, l)
            big_out[key] = _adamw(parts, given[key], given["m_" + key], given["v_" + key], idx, big_out.get(key))

    pend_mix = None
    dx, dx_b = dy, dy_b
    for l in reversed(range(depth)):
        sv = saved[l]
        is_a = l % 2 == 0
        g_in, w_kv, w_o, g_gu, w_dn = sv["w"]
        gm_row, gf_row, gmem_row, gmq_row, gmk_row = sv["rows"]
        mix = sv["mix"]
        dw_dn = _matmul(sv["act"], dx_b, ta=True, out_dtype=bf16)
        dgu = _ffn_down_bwd(dx_b, w_dn, sv["gate"], sv["up"])
        dw_gu = _mm_cols_wgrad(sv["h2"], dgu, g_gu.shape[2])
        swap, dgate = _swap_begin([dw_gu, dw_dn.reshape(N_DEV, -1, d)], dgu[0], dx_b, f"ffn{l}")
        dh2 = _mm_cols_dgrad((dgate, dgu[1]), g_gu)
        dx1, dx1_b, dgf = _rms_bwd(sv["x1"], gf_row, dh2, dx)
        small["g_ffn"][l] = dgf.reshape(d)
        pend_ffn, dx1_b = scatter_start(swap, dx1_b, dx1, f"ffn{l}")
        if pend_mix is not None:
            scatter_finish(*pend_mix, dx1_b, f"mix{l + 1}")
        dw_o = _matmul(sv["cat"], dx1_b, ta=True, out_dtype=bf16)
        dcat = _matmul(dx1_b, w_o, tb=True, out_dtype=bf16)
        if is_a:
            dz, dws, dbt, dgv = _gmlp_bwd(sv["z"], mix["g_v"], mix["w_s"], mix["b_t"], dcat)
            small["w_spatial"][l // 2], small["b_spatial"][l // 2], small["g_v_a"][l // 2] = dws, dbt.T, dgv.reshape(t)
        else:
            dq, dk, dv = _flash_bwd(mix["q"], mix["k"], mix["v"], sv["cat"], dcat, mix["lse"])
            dz, dgq, dgk = _attn_prep_bwd(sv["z"], mix["g_q"], mix["g_k"], tables, dq, dk, dv, t)
            small["g_q_b"][l // 2], small["g_k_b"][l // 2] = dgq.reshape(HEAD), dgk.reshape(HEAD)
        dz, dkn, dvm, dgmq = _mem_bwd(sv["z"], sv["qblk"], sv["kv"], gmq_row, gmk_row, dcat, dz)
        dkv, dgmk = _mem_kv_bwd(sv["kv"], gmk_row, dkn, dvm)
        dw_kv = _matmul(sv["hm"], dkv, ta=True, out_dtype=bf16)
        dhm = _matmul(dkv, w_kv, tb=True)
        small["g_mem"][l] = _rms_bwd(mem0, gmem_row, dhm, None).reshape(d)
        small["g_mq"][l] = dgmq.reshape(HEAD)
        small["g_mk"][l] = dgmk.reshape(HEAD)
        dw_in = _mm_cols_wgrad(sv["h"], dz, g_in.shape[2])
        swap, dz = _swap_begin([dw_in] + [dw.reshape(N_DEV, -1, dw.shape[1]) for dw in (dw_kv, dw_o)], dz, dx1, f"mix{l}")
        dh = _mm_cols_dgrad(dz, g_in)
        dx, dx_b, dgm = _rms_bwd(sv["x"], gm_row, dh, dx1)
        small["g_mix"][l] = dgm.reshape(d)

        handle, dx_b = scatter_start(swap, dx_b, dx, f"mix{l}")
        pend_mix = (handle, big_names[:n_mix], l)
        scatter_finish(pend_ffn, big_names[n_mix:], l, dx_b, f"ffn{l}")
    small_names = tuple(small)
    small_grads = [jnp.stack(small[n]) for n in small_names]
    small_shapes = [g.shape for g in small_grads]
    (all_parts,) = _all_gather([_to_flat(small_grads)])
    souts = _adamw(all_parts, *[_to_flat([given[p + n] for n in small_names])[None] for p in ("", "m_", "v_")], 0, None)
    small_out = dict(zip(small_names, zip(*[_from_flat(flat, small_shapes) for flat in souts])))
    scatter_finish(*pend_mix, souts[0], "mix0")

    weights = ("g_mix", "g_ffn", "w_in_a", "g_v_a", "w_spatial", "b_spatial", "w_in_b", "g_q_b", "g_k_b", "g_mem", "w_mem_kv",
               "g_mq", "g_mk", "w_out", "w_gate_up", "w_down")
    results = {n: (small_out[n] if n in small_out else big_out[n]) for n in weights}
    grad_x = dx.reshape(1, s, d)
    return (loss, grad_x, *[results[n][kind] for kind in range(4) for n in weights])
```
